```python
import math
import jax, jax.numpy as jnp
from jax import lax
import numpy as np

D_MODEL = 1024
BATCH = 2
SEQ = 8192
DEPTH = 1

GRID_W = 64
NA_HEADS = 8
NA_HEAD_DIM = 64
NA_WIDTH = NA_HEADS * NA_HEAD_DIM
NA_KH_MAX = 8
NA_KW = 16
HY_WIDTH = 512
HY_ORDER = 2
HY_SHORT_K = 3
HY_EMB_DIM = 33
HY_FILTER_HIDDEN = 64
HY_FAST_DECAY_PCT = 0.3
HY_SLOW_DECAY_PCT = 1.5
HY_DECAY_TARGET = 1e-2
N_DIR = 2
N_BRANCH = 2
D_FF = 2816
QKV_COLS = 3 * NA_WIDTH
HY_COLS = (HY_ORDER + 1) * HY_WIDTH
GATE_COLS = N_BRANCH * D_MODEL
IN_COLS = QKV_COLS + HY_COLS + GATE_COLS
DN_ALPHA = (2 * DEPTH) ** 0.25
DN_BETA = (8 * DEPTH) ** -0.25
LN_EPS = 1e-5

kernel_name = 'hybrid_na2d_hyena_macaron_deepnorm'


def layer_norm(x, g, b):
    xf = x.astype(jnp.float32)
    mu = jnp.mean(xf, axis=-1, keepdims=True)
    var = jnp.mean(jnp.square(xf - mu), axis=-1, keepdims=True)
    return ((xf - mu) * lax.rsqrt(var + LN_EPS)).astype(x.dtype) * g + b


def swiglu(x, w1, w3, w2):
    return (jax.nn.silu(x @ w1) * (x @ w3)) @ w2


def _window_starts(n, k):
    return np.clip(np.arange(n) - k // 2, 0, n - k)


def neighbourhood_attention_2d(q, k, v, rpb):
    b, L, h, dh = q.shape
    rows = L // GRID_W
    kh = min(NA_KH_MAX, rows)
    kw = NA_KW
    grid = lambda t: t.reshape(b, rows, GRID_W, h, dh).transpose(0, 3, 1, 2, 4)
    qg, kg, vg = grid(q), grid(k), grid(v)
    row_idx = _window_starts(rows, kh)[:, None] + np.arange(kh)[None, :]
    col_idx = _window_starts(GRID_W, kw)[:, None] + np.arange(kw)[None, :]
    k_band = kg[:, :, row_idx]
    v_band = vg[:, :, row_idx]
    col_sel = jax.nn.one_hot(col_idx, GRID_W, dtype=q.dtype)
    s_band = jnp.einsum('bhrqd,bhrakd->bhrqak', qg, k_band)
    s = jnp.einsum('bhrqak,qjk->bhrqaj', s_band, col_sel)
    dr = row_idx - np.arange(rows)[:, None] + (NA_KH_MAX - 1)
    dc = col_idx - np.arange(GRID_W)[:, None] + (NA_KW - 1)
    bias = rpb[:, dr[:, None, :, None], dc[None, :, None, :]]
    s = s.astype(jnp.float32) * (dh ** -0.5) + bias[None].astype(jnp.float32)
    p = jax.nn.softmax(s.reshape(b, h, rows, GRID_W, kh * kw), axis=-1)
    p = p.reshape(b, h, rows, GRID_W, kh, kw).astype(v.dtype)
    p_band = jnp.einsum('bhrqaj,qjk->bhrqak', p, col_sel)
    o = jnp.einsum('bhrqak,bhrakd->bhrqd', p_band, v_band)
    return o.transpose(0, 2, 3, 1, 4).reshape(b, L, h * dh)


def _short_conv(u, w, bias):
    L = u.shape[1]
    pad = HY_SHORT_K // 2
    up = jnp.pad(u, ((0, 0), (pad, pad), (0, 0)))
    out = bias
    for j in range(HY_SHORT_K):
        out = out + w[j] * up[:, j:j + L]
    return out


def _hyena_filters(L, fw1, fb1, fw2, fb2, fw3, fb3, freq):
    f32 = jnp.float32
    t = jnp.linspace(0.0, 1.0, L, dtype=f32)[:, None]
    bands = (HY_EMB_DIM - 1) // 2
    w = (2.0 * math.pi / L) * jnp.arange(L, dtype=f32)[:, None]
    f = jnp.linspace(1e-4, bands - 1, bands, dtype=f32)[None, :]
    z = jnp.concatenate([t, jnp.cos(f * w), -jnp.sin(f * w)], axis=-1)
    fr = freq.astype(f32)
    hdn = jnp.sin(fr * (z @ fw1.astype(f32) + fb1.astype(f32)))
    hdn = jnp.sin(fr * (hdn @ fw2.astype(f32) + fb2.astype(f32)))
    hf = hdn @ fw3.astype(f32) + fb3.astype(f32)
    hf = hf.reshape(L, N_DIR, HY_ORDER, HY_WIDTH)
    max_decay = math.log(HY_DECAY_TARGET) / HY_FAST_DECAY_PCT
    min_decay = math.log(HY_DECAY_TARGET) / HY_SLOW_DECAY_PCT
    deltas = jnp.abs(jnp.linspace(min_decay, max_decay, HY_WIDTH, dtype=f32))
    decay = jnp.exp(-t[:, :, None, None] * deltas)
    return hf * decay


def _two_sided_kernels(h):
    fwd, bwd = h[:, 0], h[:, 1]
    k = jnp.concatenate([fwd, jnp.zeros_like(fwd[:1]), bwd[1:][::-1]], axis=0)
    return k * lax.rsqrt(jnp.sum(jnp.square(k), axis=0, keepdims=True) + 1e-12)


def _fft_conv(z, k):
    L = z.shape[1]
    Z = jnp.fft.rfft(z.astype(jnp.float32), n=2 * L, axis=1)
    K = jnp.fft.rfft(k.astype(jnp.float32), n=2 * L, axis=0)
    return jnp.fft.irfft(Z * K[None], n=2 * L, axis=1)[:, :L].astype(z.dtype)


def hyena_mixer(u, short_w, short_b, fw1, fb1, fw2, fb2, fw3, fb3, freq, hy_bias):
    L = u.shape[1]
    parts = jnp.split(_short_conv(u, short_w, short_b), HY_ORDER + 1, axis=-1)
    kern = _two_sided_kernels(_hyena_filters(L, fw1, fb1, fw2, fb2, fw3, fb3, freq))
    z = parts[0]
    for n in range(HY_ORDER):
        z = parts[n + 1] * (_fft_conv(z, kern[:, n]) + hy_bias[n] * z)
    return z


def _layer(x, ln1_g, ln1_b, ffn1_w1, ffn1_w3, ffn1_w2, w_in, b_gate, na_rpb,
           hy_short_w, hy_short_b, hy_filt_w1, hy_filt_b1, hy_filt_w2, hy_filt_b2,
           hy_filt_w3, hy_filt_b3, hy_filt_freq, hy_bias, w_pa, w_pb, w_out,
           ln2_g, ln2_b, ffn2_w1, ffn2_w3, ffn2_w2, ln3_g, ln3_b):
    x = layer_norm(DN_ALPHA * x + 0.5 * swiglu(x, ffn1_w1, ffn1_w3, ffn1_w2), ln1_g, ln1_b)
    b, L, _ = x.shape
    proj = x @ w_in
    qkv, hy_u, gate_logits = jnp.split(proj, [QKV_COLS, QKV_COLS + HY_COLS], axis=-1)
    q, k, v = [t.reshape(b, L, NA_HEADS, NA_HEAD_DIM) for t in jnp.split(qkv, 3, axis=-1)]
    y_a = neighbourhood_attention_2d(q, k, v, na_rpb)
    y_b = hyena_mixer(hy_u, hy_short_w, hy_short_b, hy_filt_w1, hy_filt_b1, hy_filt_w2,
                      hy_filt_b2, hy_filt_w3, hy_filt_b3, hy_filt_freq, hy_bias)
    g_a, g_b = jnp.split(jax.nn.sigmoid(gate_logits + b_gate), N_BRANCH, axis=-1)
    mix = (g_a * (y_a @ w_pa) + g_b * (y_b @ w_pb)) @ w_out
    x = layer_norm(DN_ALPHA * x + mix, ln2_g, ln2_b)
    x = layer_norm(DN_ALPHA * x + 0.5 * swiglu(x, ffn2_w1, ffn2_w3, ffn2_w2), ln3_g, ln3_b)
    return x


def setup_inputs(seed: int = 0) -> dict:
    key = jax.random.key(seed)
    ks = jax.random.split(key, 32)
    f32 = jnp.float32

    def nrm(k, shape, scale):
        return jax.random.normal(k, shape, f32) * scale

    Dp = DEPTH
    return {
        'x': jax.random.normal(ks[0], (BATCH, SEQ, D_MODEL), f32),
        'ln1_g': 1.0 + nrm(ks[1], (Dp, D_MODEL), 0.02),
        'ln1_b': nrm(ks[2], (Dp, D_MODEL), 0.02),
        'ffn1_w1': nrm(ks[3], (Dp, D_MODEL, D_FF), D_MODEL ** -0.5),
        'ffn1_w3': nrm(ks[4], (Dp, D_MODEL, D_FF), D_MODEL ** -0.5),
        'ffn1_w2': nrm(ks[5], (Dp, D_FF, D_MODEL), DN_BETA * D_FF ** -0.5),
        'w_in': nrm(ks[6], (Dp, D_MODEL, IN_COLS), D_MODEL ** -0.5),
        'b_gate': nrm(ks[7], (Dp, GATE_COLS), 0.02),
        'na_rpb': nrm(ks[8], (Dp, NA_HEADS, 2 * NA_KH_MAX - 1, 2 * NA_KW - 1), 0.02),
        'hy_short_w': nrm(ks[9], (Dp, HY_SHORT_K, HY_COLS), HY_SHORT_K ** -0.5),
        'hy_short_b': nrm(ks[10], (Dp, HY_COLS), 0.02),
        'hy_filt_w1': nrm(ks[11], (Dp, HY_EMB_DIM, HY_FILTER_HIDDEN), HY_EMB_DIM ** -0.5),
        'hy_filt_b1': nrm(ks[12], (Dp, HY_FILTER_HIDDEN), 0.02),
        'hy_filt_w2': nrm(ks[13], (Dp, HY_FILTER_HIDDEN, HY_FILTER_HIDDEN), HY_FILTER_HIDDEN ** -0.5),
        'hy_filt_b2': nrm(ks[14], (Dp, HY_FILTER_HIDDEN), 0.02),
        'hy_filt_w3': nrm(ks[15], (Dp, HY_FILTER_HIDDEN, N_DIR * HY_ORDER * HY_WIDTH), HY_FILTER_HIDDEN ** -0.5),
        'hy_filt_b3': nrm(ks[16], (Dp, N_DIR * HY_ORDER * HY_WIDTH), 0.02),
        'hy_filt_freq': 1.0 + nrm(ks[17], (Dp, HY_FILTER_HIDDEN), 0.02),
        'hy_bias': nrm(ks[18], (Dp, HY_ORDER, HY_WIDTH), 1.0),
        'w_pa': nrm(ks[19], (Dp, NA_WIDTH, D_MODEL), DN_BETA * NA_WIDTH ** -0.5),
        'w_pb': nrm(ks[20], (Dp, HY_WIDTH, D_MODEL), DN_BETA * HY_WIDTH ** -0.5),
        'w_out': nrm(ks[21], (Dp, D_MODEL, D_MODEL), DN_BETA * D_MODEL ** -0.5),
        'ln2_g': 1.0 + nrm(ks[22], (Dp, D_MODEL), 0.02),
        'ln2_b': nrm(ks[23], (Dp, D_MODEL), 0.02),
        'ffn2_w1': nrm(ks[24], (Dp, D_MODEL, D_FF), D_MODEL ** -0.5),
        'ffn2_w3': nrm(ks[25], (Dp, D_MODEL, D_FF), D_MODEL ** -0.5),
        'ffn2_w2': nrm(ks[26], (Dp, D_FF, D_MODEL), DN_BETA * D_FF ** -0.5),
        'ln3_g': 1.0 + nrm(ks[27], (Dp, D_MODEL), 0.02),
        'ln3_b': nrm(ks[28], (Dp, D_MODEL), 0.02),
    }


def reference(x, ln1_g, ln1_b, ffn1_w1, ffn1_w3, ffn1_w2, w_in, b_gate, na_rpb,
              hy_short_w, hy_short_b, hy_filt_w1, hy_filt_b1, hy_filt_w2, hy_filt_b2,
              hy_filt_w3, hy_filt_b3, hy_filt_freq, hy_bias, w_pa, w_pb, w_out,
              ln2_g, ln2_b, ffn2_w1, ffn2_w3, ffn2_w2, ln3_g, ln3_b):
    for i in range(DEPTH):
        x = _layer(x, ln1_g[i], ln1_b[i], ffn1_w1[i], ffn1_w3[i], ffn1_w2[i], w_in[i],
                   b_gate[i], na_rpb[i], hy_short_w[i], hy_short_b[i], hy_filt_w1[i],
                   hy_filt_b1[i], hy_filt_w2[i], hy_filt_b2[i], hy_filt_w3[i], hy_filt_b3[i],
                   hy_filt_freq[i], hy_bias[i], w_pa[i], w_pb[i], w_out[i], ln2_g[i],
                   ln2_b[i], ffn2_w1[i], ffn2_w3[i], ffn2_w2[i], ln3_g[i], ln3_b[i])
    return x
```

```python
import functools
import math

import numpy as np
import jax
import jax.numpy as jnp
from jax import lax
from jax.experimental import pallas as pl
from jax.experimental.pallas import tpu as pltpu

F32 = jnp.float32
BF16 = jnp.bfloat16

GRID_W = 64
NA_HEADS = 8
NA_HEAD_DIM = 64
NA_KH = 8
NA_KW = 16
HY_ORDER = 2
HY_EMB_DIM = 33
HY_FAST_DECAY_PCT = 0.3
HY_SLOW_DECAY_PCT = 1.5
HY_DECAY_TARGET = 1e-2
DEPTH = 1
DN_ALPHA = (2 * DEPTH) ** 0.25
LN_EPS = 1e-5

LANES = 128
VMEM_LIMIT = 60 * 1024 * 1024

ROW_TILE = 512
FF_CHUNK = 256
PROJ_CHUNK = 512
NA_QROWS = 4
NA_KROWS = NA_QROWS + NA_KH
RADIX = 128
CONV_ROWS = 512
NEG = -1e30


def _cparams(sem):
    return pltpu.CompilerParams(dimension_semantics=sem, vmem_limit_bytes=VMEM_LIMIT)


def _resident(shape, index_map):
    return pl.BlockSpec(shape, index_map, pipeline_mode=pl.Buffered(1))


def _layer_norm(r, g, b):
    mu = jnp.mean(r, axis=-1, keepdims=True)
    d = r - mu
    var = jnp.mean(d * d, axis=-1, keepdims=True)
    return d * lax.rsqrt(var + LN_EPS) * g + b


def _swiglu_ln(x, xb_ref, acc_ref, w1_ref, w3_ref, w2_ref, g_ref, b_ref):
    d_ff = w1_ref.shape[1]
    xb_ref[...] = x.astype(BF16)
    for j in range(d_ff // FF_CHUNK):
        sl = slice(j * FF_CHUNK, (j + 1) * FF_CHUNK)
        a = jnp.dot(xb_ref[...], w1_ref[:, sl], preferred_element_type=F32)
        b = jnp.dot(xb_ref[...], w3_ref[:, sl], preferred_element_type=F32)
        h = (a * jax.nn.sigmoid(a) * b).astype(BF16)
        part = jnp.dot(h, w2_ref[sl, :], preferred_element_type=F32)
        if j == 0:
            acc_ref[...] = part
        else:
            acc_ref[...] += part
    return _layer_norm(DN_ALPHA * x + 0.5 * acc_ref[...], g_ref[...], b_ref[...])


def _dense_in_kernel(x_ref, w1_ref, w3_ref, w2_ref, g_ref, b_ref, win_ref, bg_ref,
                     x1_ref, qkv_ref, hyu_ref, gate_ref, xb_ref, acc_ref):
    x1 = _swiglu_ln(x_ref[...], xb_ref, acc_ref, w1_ref, w3_ref, w2_ref, g_ref, b_ref)
    x1_ref[...] = x1
    xb_ref[...] = x1.astype(BF16)
    n_qkv = qkv_ref.shape[1]
    n_hy = hyu_ref.shape[1]
    n_head = n_qkv // 3
    for j in range(win_ref.shape[1] // PROJ_CHUNK):
        c0 = j * PROJ_CHUNK
        p = jnp.dot(xb_ref[...], win_ref[:, c0:c0 + PROJ_CHUNK], preferred_element_type=F32)
        if c0 < n_qkv:
            if c0 < n_head:
                p = p * (NA_HEAD_DIM ** -0.5)
            qkv_ref[:, c0:c0 + PROJ_CHUNK] = p.astype(BF16)
        elif c0 < n_qkv + n_hy:
            hyu_ref[:, c0 - n_qkv:c0 - n_qkv + PROJ_CHUNK] = p.astype(BF16)
        else:
            g0 = c0 - n_qkv - n_hy
            gate_ref[:, g0:g0 + PROJ_CHUNK] = jax.nn.sigmoid(p + bg_ref[:, g0:g0 + PROJ_CHUNK]).astype(BF16)


def _dense_in(x2d, w1, w3, w2, g, b, w_in, b_gate, n_qkv, n_hy):
    n, d = x2d.shape
    d_ff = w1.shape[1]
    n_gate = w_in.shape[1] - n_qkv - n_hy
    assert n % ROW_TILE == 0 and d_ff % FF_CHUNK == 0
    assert n_qkv % (3 * PROJ_CHUNK) == 0 and n_hy % PROJ_CHUNK == 0 and n_gate % PROJ_CHUNK == 0
    row = lambda i: (i, 0)
    whole = lambda i: (0, 0)
    return pl.pallas_call(
        _dense_in_kernel,
        grid=(n // ROW_TILE,),
        in_specs=[
            pl.BlockSpec((ROW_TILE, d), row),
            _resident((d, d_ff), whole), _resident((d, d_ff), whole), _resident((d_ff, d), whole),
            _resident((1, d), whole), _resident((1, d), whole),
            _resident(w_in.shape, whole), _resident((1, n_gate), whole),
        ],
        out_specs=[
            pl.BlockSpec((ROW_TILE, d), row),
            pl.BlockSpec((ROW_TILE, n_qkv), row),
            pl.BlockSpec((ROW_TILE, n_hy), row),
            pl.BlockSpec((ROW_TILE, n_gate), row),
        ],
        out_shape=[
            jax.ShapeDtypeStruct((n, d), F32),
            jax.ShapeDtypeStruct((n, n_qkv), BF16),
            jax.ShapeDtypeStruct((n, n_hy), BF16),
            jax.ShapeDtypeStruct((n, n_gate), BF16),
        ],
        scratch_shapes=[pltpu.VMEM((ROW_TILE, d), BF16), pltpu.VMEM((ROW_TILE, d), F32)],
        compiler_params=_cparams(("arbitrary",)),
        name="dense_in",
    )(x2d, w1, w3, w2, g, b, w_in, b_gate)


def _dense_out_kernel(x1_ref, ya_ref, yb_ref, gate_ref, wpa_ref, wpb_ref, wout_ref, g2_ref, b2_ref,
                      w1_ref, w3_ref, w2_ref, g3_ref, b3_ref, o_ref, xb_ref, acc_ref):
    d = x1_ref.shape[1]
    pa = jnp.dot(ya_ref[...], wpa_ref[...], preferred_element_type=F32)
    pb = jnp.dot(yb_ref[...], wpb_ref[...], preferred_element_type=F32)
    m = gate_ref[:, :d].astype(F32) * pa + gate_ref[:, d:].astype(F32) * pb
    mix = jnp.dot(m.astype(BF16), wout_ref[...], preferred_element_type=F32)
    x2 = _layer_norm(DN_ALPHA * x1_ref[...] + mix, g2_ref[...], b2_ref[...])
    o_ref[...] = _swiglu_ln(x2, xb_ref, acc_ref, w1_ref, w3_ref, w2_ref, g3_ref, b3_ref)


def _dense_out(x1, ya, yb, gates, wpa, wpb, wout, g2, b2, w1, w3, w2, g3, b3):
    n, d = x1.shape
    d_ff = w1.shape[1]
    row = lambda i: (i, 0)
    whole = lambda i: (0, 0)
    return pl.pallas_call(
        _dense_out_kernel,
        grid=(n // ROW_TILE,),
        in_specs=[
            pl.BlockSpec((ROW_TILE, d), row),
            pl.BlockSpec((ROW_TILE, ya.shape[1]), row),
            pl.BlockSpec((ROW_TILE, yb.shape[1]), row),
            pl.BlockSpec((ROW_TILE, gates.shape[1]), row),
            _resident(wpa.shape, whole), _resident(wpb.shape, whole), _resident(wout.shape, whole),
            _resident((1, d), whole), _resident((1, d), whole),
            _resident((d, d_ff), whole), _resident((d, d_ff), whole), _resident((d_ff, d), whole),
            _resident((1, d), whole), _resident((1, d), whole),
        ],
        out_specs=pl.BlockSpec((ROW_TILE, d), row),
        out_shape=jax.ShapeDtypeStruct((n, d), F32),
        scratch_shapes=[pltpu.VMEM((ROW_TILE, d), BF16), pltpu.VMEM((ROW_TILE, d), F32)],
        compiler_params=_cparams(("arbitrary",)),
        name="dense_out",
    )(x1, ya, yb, gates, wpa, wpb, wout, g2, b2, w1, w3, w2, g3, b3)


def _window_start(n, k):
    return np.clip(np.arange(n) - k // 2, 0, n - k)


def _na_group_types(rows):
    last = rows - NA_QROWS
    return ((0, 0), (NA_QROWS * 2, NA_QROWS), (last, rows - NA_KROWS))


def _na_tables(rows):
    w = GRID_W
    shift = np.zeros((2, 2 * NA_KW - 1, w, 2 * w), np.float32)
    q = np.arange(w)[:, None]
    kc = np.arange(w)[None, :]
    for dc in range(2 * NA_KW - 1):
        hit = (kc - q + NA_KW - 1 == dc).astype(np.float32)
        shift[0, dc, :, :w] = hit
        shift[1, dc, :, w:] = hit
    cs = _window_start(w, NA_KW)
    col_ok = (kc >= cs[:, None]) & (kc < cs[:, None] + NA_KW)
    rs_all = _window_start(rows, NA_KH)
    negmask = np.full((3, NA_QROWS, w, NA_KROWS, w), NEG, np.float32)
    for t, (r0, kb) in enumerate(_na_group_types(rows)):
        for i in range(NA_QROWS):
            rs = rs_all[r0 + i]
            for j in range(NA_KROWS):
                if rs <= kb + j < rs + NA_KH:
                    negmask[t, i, :, j, :] = np.where(col_ok, 0.0, NEG)
    return shift, negmask.reshape(3, NA_QROWS * w, NA_KROWS * w)


def _na_bias_kernel(rpb_ref, shift_ref, neg_ref, o_ref, u_ref, *, rows):
    h = pl.program_id(0)
    n_dr = 2 * NA_KH - 1
    n_dc = 2 * NA_KW - 1
    w = GRID_W

    def dr_body(dr, carry):
        def dc_body(dc, acc):
            s = rpb_ref[(h * n_dr + dr) * n_dc + dc]
            return acc[0] + s * shift_ref[0, dc], acc[1] + s * shift_ref[1, dc]

        zero = jnp.zeros((w, 2 * w), F32)
        left, right = lax.fori_loop(0, n_dc, dc_body, (zero, zero))
        u_ref[0, dr] = left
        u_ref[1, dr] = right
        return carry

    lax.fori_loop(0, n_dr, dr_body, 0)
    rs_all = _window_start(rows, NA_KH)
    for t, (r0, kb) in enumerate(_na_group_types(rows)):
        for i in range(NA_QROWS):
            rs = rs_all[r0 + i]
            for jj in range(NA_KROWS // 2):
                val = neg_ref[t, i * w:(i + 1) * w, jj * 2 * w:(jj + 1) * 2 * w]
                for s in range(2):
                    kr = kb + 2 * jj + s
                    if rs <= kr < rs + NA_KH:
                        val = val + u_ref[s, kr - (r0 + i) + NA_KH - 1]
                o_ref[t, 0, i * w:(i + 1) * w, jj * 2 * w:(jj + 1) * 2 * w] = val


def _na_bias(rpb, rows):
    shift, negmask = _na_tables(rows)
    nq, nk = NA_QROWS * GRID_W, NA_KROWS * GRID_W
    return pl.pallas_call(
        functools.partial(_na_bias_kernel, rows=rows),
        grid=(NA_HEADS,),
        in_specs=[
            pl.BlockSpec(memory_space=pltpu.SMEM),
            _resident(shift.shape, lambda h: (0, 0, 0, 0)),
            _resident(negmask.shape, lambda h: (0, 0, 0)),
        ],
        out_specs=pl.BlockSpec((3, 1, nq, nk), lambda h: (0, h, 0, 0)),
        out_shape=jax.ShapeDtypeStruct((3, NA_HEADS, nq, nk), F32),
        scratch_shapes=[pltpu.VMEM((2, 2 * NA_KH - 1, GRID_W, 2 * GRID_W), F32)],
        compiler_params=_cparams(("arbitrary",)),
        name="na_bias",
    )(rpb.reshape(-1), jnp.asarray(shift), jnp.asarray(negmask))


def _na_attn_kernel(q_ref, k_ref, v_ref, bias_ref, o_ref, *, rows):
    g = pl.program_id(1)
    kb = jnp.clip(g * NA_QROWS - NA_KH // 2, 0, rows - NA_KROWS)
    k0 = pl.multiple_of(kb * GRID_W, GRID_W)
    nk = NA_KROWS * GRID_W
    dh = NA_HEAD_DIM
    outs = []
    for h in range(NA_HEADS):
        q = q_ref[0, :, h * dh:(h + 1) * dh]
        k = k_ref[0, pl.ds(k0, nk), h * dh:(h + 1) * dh]
        v = v_ref[0, pl.ds(k0, nk), h * dh:(h + 1) * dh]
        s = lax.dot_general(q, k, (((1,), (1,)), ((), ())), preferred_element_type=F32)
        s = s + bias_ref[0, h]
        p = jnp.exp(s - jnp.max(s, axis=-1, keepdims=True))
        den = jnp.sum(p, axis=-1, keepdims=True)
        o = jnp.dot(p.astype(BF16), v, preferred_element_type=F32)
        outs.append(o / den)
    o_ref[0] = jnp.concatenate(outs, axis=-1).astype(BF16)


def _na_attn(qkv, bias, rows):
    b, seq, three_w = qkv.shape
    width = three_w // 3
    nq, nk = NA_QROWS * GRID_W, NA_KROWS * GRID_W
    groups = rows // NA_QROWS

    def bias_idx(bi, g):
        return (jnp.where(g == 0, 0, jnp.where(g == groups - 1, 2, 1)), 0, 0, 0)

    return pl.pallas_call(
        functools.partial(_na_attn_kernel, rows=rows),
        grid=(b, groups),
        in_specs=[
            pl.BlockSpec((1, nq, width), lambda bi, g: (bi, g, 0)),
            _resident((1, seq, width), lambda bi, g: (bi, 0, 1)),
            _resident((1, seq, width), lambda bi, g: (bi, 0, 2)),
            pl.BlockSpec((1, NA_HEADS, nq, nk), bias_idx),
        ],
        out_specs=pl.BlockSpec((1, nq, width), lambda bi, g: (bi, g, 0)),
        out_shape=jax.ShapeDtypeStruct((b, seq, width), BF16),
        compiler_params=_cparams(("arbitrary", "arbitrary")),
        name="na_attn",
    )(qkv, qkv, qkv, bias)


def _dft_tables():
    k = np.arange(RADIX)
    ang = 2.0 * np.pi * np.outer(k, k) / RADIX
    fr, fi = np.cos(ang), -np.sin(ang)
    ang_t = 2.0 * np.pi * np.outer(k, k) / (RADIX * RADIX)
    twr, twi = np.cos(ang_t), -np.sin(ang_t)
    f32 = lambda a: jnp.asarray(a, F32)
    return f32(fr), f32(fi), f32(twr), f32(twi)


def _embed(cr, ci):
    return jnp.concatenate([jnp.concatenate([cr, -ci], axis=1), jnp.concatenate([ci, cr], axis=1)], axis=0)


def _position_features(seq):
    t = np.linspace(0.0, 1.0, seq)[:, None]
    bands = (HY_EMB_DIM - 1) // 2
    w = (2.0 * math.pi / seq) * np.arange(seq)[:, None]
    f = np.linspace(1e-4, bands - 1, bands)[None, :]
    z = np.concatenate([t, np.cos(f * w), -np.sin(f * w)], axis=-1)
    pos = np.concatenate([np.arange(seq), seq - np.arange(seq)])
    pos[seq] = 0
    zz = np.zeros((2 * seq, LANES), np.float32)
    zz[:, :HY_EMB_DIM] = z[pos]
    return zz


def _filt_mlp_kernel(z_ref, w1_ref, b1_ref, w2_ref, b2_ref, fr_ref, o_ref):
    hp = lax.Precision.HIGHEST
    fr = fr_ref[...]
    h = jnp.sin(fr * (jnp.dot(z_ref[...], w1_ref[...], precision=hp, preferred_element_type=F32) + b1_ref[...]))
    o_ref[...] = jnp.sin(fr * (jnp.dot(h, w2_ref[...], precision=hp, preferred_element_type=F32) + b2_ref[...]))


def _filt_mlp(zz, w1p, b1, w2, b2, freq):
    n2l = zz.shape[0]
    hid = w2.shape[0]
    rows = 2048
    whole = lambda i: (0, 0)
    return pl.pallas_call(
        _filt_mlp_kernel,
        grid=(n2l // rows,),
        in_specs=[pl.BlockSpec((rows, LANES), lambda i: (i, 0)),
                  pl.BlockSpec(w1p.shape, whole), pl.BlockSpec((1, hid), whole),
                  pl.BlockSpec(w2.shape, whole), pl.BlockSpec((1, hid), whole), pl.BlockSpec((1, hid), whole)],
        out_specs=pl.BlockSpec((rows, hid), lambda i: (i, 0)),
        out_shape=jax.ShapeDtypeStruct((n2l, hid), F32),
        compiler_params=_cparams(("arbitrary",)),
        name="filt_mlp",
    )(zz, w1p, b1, w2, b2, freq)


def _filt_time_kernel(h_ref, w3_ref, b3_ref, delta_ref, k_ref, ss_ref, *, seq):
    i = pl.program_id(0)
    rows = h_ref.shape[0]
    hf = jnp.dot(h_ref[...], w3_ref[0], precision=lax.Precision.HIGHEST, preferred_element_type=F32) + b3_ref[0]
    m = i * rows + lax.broadcasted_iota(jnp.int32, (rows, 1), 0)
    pos = jnp.where(m < seq, m, 2 * seq - m).astype(F32)
    t = pos / (seq - 1)
    k = jnp.where(m == seq, 0.0, hf * jnp.exp(-t * delta_ref[...]))
    k_ref[...] = k
    part = jnp.sum(k * k, axis=0, keepdims=True)

    @pl.when(i == 0)
    def _():
        ss_ref[...] = part

    @pl.when(i != 0)
    def _():
        ss_ref[...] += part


def _filt_time(hdn, w3d, b3d, delta2, seq):
    n2l, hid = hdn.shape
    cols = w3d.shape[2]
    rows = 1024
    half = seq // rows
    return pl.pallas_call(
        functools.partial(_filt_time_kernel, seq=seq),
        grid=(n2l // rows,),
        in_specs=[pl.BlockSpec((rows, hid), lambda i: (i, 0)),
                  pl.BlockSpec((1, hid, cols), lambda i: (i // half, 0, 0)),
                  pl.BlockSpec((1, 1, cols), lambda i: (i // half, 0, 0)),
                  pl.BlockSpec((1, cols), lambda i: (0, 0))],
        out_specs=[pl.BlockSpec((rows, cols), lambda i: (i, 0)),
                   pl.BlockSpec((1, cols), lambda i: (0, 0))],
        out_shape=[jax.ShapeDtypeStruct((n2l, cols), F32), jax.ShapeDtypeStruct((1, cols), F32)],
        compiler_params=_cparams(("arbitrary",)),
        name="filt_time",
    )(hdn, w3d, b3d, delta2)


def _filt_spec_kernel(k_ref, ss_ref, fa_ref, fr_ref, fi_ref, twr_ref, twi_ref, o_ref, a_ref):
    r = RADIX
    scale = lax.rsqrt(ss_ref[...] + 1e-12)

    def a_body(n2, c):
        rhs = k_ref[pl.ds(n2, r, stride=r), :].astype(BF16)
        out = jnp.dot(fa_ref[...], rhs, preferred_element_type=F32)
        a_ref[0, pl.ds(n2, r, stride=r), :] = out[:r]
        a_ref[1, pl.ds(n2, r, stride=r), :] = out[r:]
        return c

    lax.fori_loop(0, r, a_body, 0)

    def b_body(k1, c):
        r0 = pl.multiple_of(k1 * r, r)
        blk = jnp.concatenate([a_ref[0, pl.ds(r0, r), :], a_ref[1, pl.ds(r0, r), :]], axis=0).astype(BF16)
        twr, twi = twr_ref[pl.ds(k1, 1), :], twi_ref[pl.ds(k1, 1), :]
        fr, fi = fr_ref[...], fi_ref[...]
        hemb = _embed(fr * twr - fi * twi, fr * twi + fi * twr).astype(BF16)
        s = jnp.dot(hemb, blk, preferred_element_type=F32) * scale
        o_ref[0, pl.ds(r0, r), :] = s[:r].astype(BF16)
        o_ref[1, pl.ds(r0, r), :] = s[r:].astype(BF16)
        return c

    lax.fori_loop(0, r, b_body, 0)


def _filt_spec(ktime, ss, fa_real, fr, fi, twr, twi):
    n2l, cols = ktime.shape
    whole = lambda j: (0, 0)
    return pl.pallas_call(
        _filt_spec_kernel,
        grid=(cols // LANES,),
        in_specs=[pl.BlockSpec((n2l, LANES), lambda j: (0, j)),
                  pl.BlockSpec((1, LANES), lambda j: (0, j)),
                  _resident(fa_real.shape, whole), _resident(fr.shape, whole), _resident(fi.shape, whole),
                  _resident(twr.shape, whole), _resident(twi.shape, whole)],
        out_specs=pl.BlockSpec((2, n2l, LANES), lambda j: (0, 0, j)),
        out_shape=jax.ShapeDtypeStruct((2, n2l, cols), BF16),
        scratch_shapes=[pltpu.VMEM((2, n2l, LANES), F32)],
        compiler_params=_cparams(("arbitrary",)),
        name="filt_spec",
    )(ktime, ss, fa_real, fr, fi, twr, twi)


def _short_conv_rows(u_ref, bi, r0, nrows, seq, w_ref, b_ref):
    edge = 16
    cur = u_ref[bi, pl.ds(r0, nrows), :].astype(F32)
    before = u_ref[bi, pl.ds(pl.multiple_of(jnp.maximum(r0 - edge, 0), edge), edge), :].astype(F32)
    after = u_ref[bi, pl.ds(pl.multiple_of(jnp.minimum(r0 + nrows, seq - edge), edge), edge), :].astype(F32)
    prev_edge = jnp.where(r0 > 0, before[edge - 1:edge, :], 0.0)
    next_edge = jnp.where(r0 + nrows < seq, after[0:1, :], 0.0)
    row = lax.broadcasted_iota(jnp.int32, cur.shape, 0)
    prev = jnp.where(row == 0, prev_edge, pltpu.roll(cur, 1, 0))
    nxt = jnp.where(row == nrows - 1, next_edge, pltpu.roll(cur, nrows - 1, 0))
    return b_ref[...] + w_ref[0:1, :] * prev + w_ref[1:2, :] * cur + w_ref[2:3, :] * nxt


def _hy_conv_kernel(z_ref, zw_ref, zb_ref, u_ref, uw_ref, ub_ref, ks_ref, hb_ref,
                    fa_ref, fc_ref, fr_ref, fi_ref, twr_ref, twi_ref, o_ref, a_ref, v_ref, *, conv_in):
    r = RADIX
    half = r // 2
    nb, seq, _ = z_ref.shape
    nchunk = seq // CONV_ROWS

    def load_body(c, carry):
        r0 = pl.multiple_of(c * CONV_ROWS, CONV_ROWS)
        for bi in range(nb):
            if conv_in:
                v_ref[bi, pl.ds(r0, CONV_ROWS), :] = _short_conv_rows(z_ref, bi, r0, CONV_ROWS, seq, zw_ref, zb_ref)
            else:
                v_ref[bi, pl.ds(r0, CONV_ROWS), :] = z_ref[bi, pl.ds(r0, CONV_ROWS), :].astype(F32)
        return carry

    lax.fori_loop(0, nchunk, load_body, 0)

    def a_body(n2, carry):
        rhs = jnp.concatenate([v_ref[0, pl.ds(n2, half, stride=r), :],
                               v_ref[1, pl.ds(n2, half, stride=r), :]], axis=0).astype(BF16)
        out = jnp.dot(fa_ref[...], rhs, preferred_element_type=F32)
        a_ref[0, pl.ds(n2, r, stride=r), :] = out[:r]
        a_ref[1, pl.ds(n2, r, stride=r), :] = out[r:]
        return carry

    lax.fori_loop(0, r, a_body, 0)

    def b_body(k1, carry):
        r0 = pl.multiple_of(k1 * r, r)
        blk = jnp.concatenate([a_ref[0, pl.ds(r0, r), :], a_ref[1, pl.ds(r0, r), :]], axis=0).astype(BF16)
        twr, twi = twr_ref[pl.ds(k1, 1), :], twi_ref[pl.ds(k1, 1), :]
        fr, fi = fr_ref[...], fi_ref[...]
        hemb = _embed(fr * twr - fi * twi, fr * twi + fi * twr).astype(BF16)
        s = jnp.dot(hemb, blk, preferred_element_type=F32)
        sr, si = s[:r], s[r:]
        kr = ks_ref[0, pl.ds(r0, r), :].astype(F32)
        ki = ks_ref[1, pl.ds(r0, r), :].astype(F32)
        p = jnp.concatenate([sr * kr - si * ki, sr * ki + si * kr], axis=0).astype(BF16)
        t = jnp.dot(fc_ref[...], p, preferred_element_type=F32)
        a_ref[0, pl.ds(r0, r), :] = t[:r]
        a_ref[1, pl.ds(r0, r), :] = t[r:]
        return carry

    lax.fori_loop(0, r, b_body, 0)

    inv_n = 1.0 / (r * r)

    def c_body(n2, carry):
        rhs = jnp.concatenate([a_ref[0, pl.ds(n2, r, stride=r), :],
                               a_ref[1, pl.ds(n2, r, stride=r), :]], axis=0).astype(BF16)
        twr, twi = twr_ref[pl.ds(n2, 1), :], twi_ref[pl.ds(n2, 1), :]
        fr, fi = fr_ref[0:half, :], fi_ref[0:half, :]
        gemb = _embed((fr * twr - fi * twi) * inv_n, (fr * twi + fi * twr) * (-inv_n)).astype(BF16)
        y = jnp.dot(gemb, rhs, preferred_element_type=F32)
        a_ref[0, pl.ds(n2, half, stride=r), :] = y[:half]
        a_ref[1, pl.ds(n2, half, stride=r), :] = y[half:]
        return carry

    lax.fori_loop(0, r, c_body, 0)

    def out_body(c, carry):
        r0 = pl.multiple_of(c * CONV_ROWS, CONV_ROWS)
        for bi in range(nb):
            gate = _short_conv_rows(u_ref, bi, r0, CONV_ROWS, seq, uw_ref, ub_ref)
            y = a_ref[bi, pl.ds(r0, CONV_ROWS), :] + hb_ref[...] * v_ref[bi, pl.ds(r0, CONV_ROWS), :]
            o_ref[bi, pl.ds(r0, CONV_ROWS), :] = (gate * y).astype(BF16)
        return carry

    lax.fori_loop(0, nchunk, out_body, 0)


def _hy_conv(z, z_part, u, u_part, order, short_w, short_b, kspec, hy_bias, tables, conv_in):
    nb, seq, _ = u.shape
    width = hy_bias.shape[1]
    tiles = width // LANES
    fa, fc, fr, fi, twr, twi = tables
    whole = lambda j: (0, 0)
    zcol = lambda j: (0, 0, z_part * tiles + j)
    ucol = lambda j: (0, 0, u_part * tiles + j)
    return pl.pallas_call(
        functools.partial(_hy_conv_kernel, conv_in=conv_in),
        grid=(tiles,),
        in_specs=[
            _resident((nb, seq, LANES), zcol),
            pl.BlockSpec((3, LANES), lambda j: (0, z_part * tiles + j)),
            pl.BlockSpec((1, LANES), lambda j: (0, z_part * tiles + j)),
            _resident((nb, seq, LANES), ucol),
            pl.BlockSpec((3, LANES), lambda j: (0, u_part * tiles + j)),
            pl.BlockSpec((1, LANES), lambda j: (0, u_part * tiles + j)),
            _resident((2, 2 * seq, LANES), lambda j: (0, 0, order * tiles + j)),
            pl.BlockSpec((1, LANES), lambda j: (0, j)),
            _resident(fa.shape, whole), _resident(fc.shape, whole), _resident(fr.shape, whole),
            _resident(fi.shape, whole), _resident(twr.shape, whole), _resident(twi.shape, whole),
        ],
        out_specs=pl.BlockSpec((nb, seq, LANES), lambda j: (0, 0, j)),
        out_shape=jax.ShapeDtypeStruct((nb, seq, width), BF16),
        scratch_shapes=[pltpu.VMEM((2, 2 * seq, LANES), F32), pltpu.VMEM((nb, seq, LANES), F32)],
        compiler_params=_cparams(("arbitrary",)),
        name=f"hy_conv{order}",
    )(z, short_w, short_b, u, short_w, short_b, kspec, hy_bias[order:order + 1], fa, fc, fr, fi, twr, twi)


def _hyena(hyu, short_w, short_b, fw1, fb1, fw2, fb2, fw3, fb3, freq, hy_bias):
    nb, seq, _ = hyu.shape
    assert nb == 2 and 2 * seq == RADIX * RADIX
    width = hy_bias.shape[1]
    hid = fw2.shape[0]
    fr, fi, twr, twi = _dft_tables()
    half = RADIX // 2
    fa_real = jnp.concatenate([fr, fi], axis=0).astype(BF16)
    fa_half = _embed(fr[:, :half], fi[:, :half]).astype(BF16)
    fc = _embed(fr, -fi).astype(BF16)

    zz = jnp.asarray(_position_features(seq))
    w1p = jnp.zeros((LANES, hid), F32).at[:HY_EMB_DIM].set(fw1)
    hdn = _filt_mlp(zz, w1p, fb1.reshape(1, hid), fw2, fb2.reshape(1, hid), freq.reshape(1, hid))
    w3d = fw3.reshape(hid, 2, HY_ORDER * width).transpose(1, 0, 2)
    b3d = fb3.reshape(2, 1, HY_ORDER * width)
    max_decay = math.log(HY_DECAY_TARGET) / HY_FAST_DECAY_PCT
    min_decay = math.log(HY_DECAY_TARGET) / HY_SLOW_DECAY_PCT
    delta = np.abs(np.linspace(min_decay, max_decay, width)).astype(np.float32)
    delta2 = jnp.asarray(np.tile(delta, HY_ORDER)[None, :])
    ktime, ss = _filt_time(hdn, w3d, b3d, delta2, seq)
    kspec = _filt_spec(ktime, ss, fa_real, fr, fi, twr, twi)

    tables = (fa_half, fc, fr, fi, twr, twi)
    sb = short_b.reshape(1, -1)
    z1 = _hy_conv(hyu, 0, hyu, 1, 0, short_w, sb, kspec, hy_bias, tables, conv_in=True)
    return _hy_conv(z1, 0, hyu, 2, 1, short_w, sb, kspec, hy_bias, tables, conv_in=False)


def kernel(x, ln1_g, ln1_b, ffn1_w1, ffn1_w3, ffn1_w2, w_in, b_gate, na_rpb, hy_short_w, hy_short_b,
           hy_filt_w1, hy_filt_b1, hy_filt_w2, hy_filt_b2, hy_filt_w3, hy_filt_b3, hy_filt_freq, hy_bias,
           w_pa, w_pb, w_out, ln2_g, ln2_b, ffn2_w1, ffn2_w3, ffn2_w2, ln3_g, ln3_b):
    assert ln1_g.shape[0] == DEPTH
    b, seq, d = x.shape
    rows = seq // GRID_W
    na_width = NA_HEADS * NA_HEAD_DIM
    n_qkv = 3 * na_width
    n_hy = hy_short_w.shape[2]
    bf = lambda a: a[0].astype(BF16)
    vec = lambda a: a[0].reshape(1, -1)

    x1, qkv, hyu, gates = _dense_in(
        x.reshape(b * seq, d), bf(ffn1_w1), bf(ffn1_w3), bf(ffn1_w2), vec(ln1_g), vec(ln1_b),
        bf(w_in), vec(b_gate), n_qkv, n_hy)

    bias = _na_bias(na_rpb[0], rows)
    ya = _na_attn(qkv.reshape(b, seq, n_qkv), bias, rows)

    yb = _hyena(hyu.reshape(b, seq, n_hy), hy_short_w[0], hy_short_b[0], hy_filt_w1[0], hy_filt_b1[0],
                hy_filt_w2[0], hy_filt_b2[0], hy_filt_w3[0], hy_filt_b3[0], hy_filt_freq[0], hy_bias[0])

    out = _dense_out(
        x1, ya.reshape(b * seq, na_width), yb.reshape(b * seq, -1), gates,
        bf(w_pa), bf(w_pb), bf(w_out), vec(ln2_g), vec(ln2_b),
        bf(ffn2_w1), bf(ffn2_w3), bf(ffn2_w2), vec(ln3_g), vec(ln3_b))
    return out.reshape(b, seq, d)
```

```python
import functools
import math

import numpy as np
import jax
import jax.numpy as jnp
from jax import lax
from jax.experimental import pallas as pl
from jax.experimental.pallas import tpu as pltpu

F32 = jnp.float32
BF16 = jnp.bfloat16

GRID_W = 64
NA_HEADS = 8
NA_HEAD_DIM = 64
NA_KH = 8
NA_KW = 16
HY_ORDER = 2
HY_EMB_DIM = 33
HY_FAST_DECAY_PCT = 0.3
HY_SLOW_DECAY_PCT = 1.5
HY_DECAY_TARGET = 1e-2
DEPTH = 1
DN_ALPHA = (2 * DEPTH) ** 0.25
LN_EPS = 1e-5

LANES = 128
VMEM_LIMIT = 60 * 1024 * 1024

ROW_TILE = 512
FF_CHUNK = 256
PROJ_CHUNK = 512
NA_QROWS = 4
NA_KROWS = NA_QROWS + NA_KH
RADIX = 128
PITCH = RADIX + 8
DFT_UNROLL = 4
CONV_ROWS = 512
NEG = -1e30


def _cparams(sem):
    return pltpu.CompilerParams(dimension_semantics=sem, vmem_limit_bytes=VMEM_LIMIT)


def _resident(shape, index_map):
    return pl.BlockSpec(shape, index_map, pipeline_mode=pl.Buffered(1))


def _layer_norm(r, g, b):
    mu = jnp.mean(r, axis=-1, keepdims=True)
    d = r - mu
    var = jnp.mean(d * d, axis=-1, keepdims=True)
    return d * lax.rsqrt(var + LN_EPS) * g + b


def _swiglu_ln(x, xb_ref, acc_ref, w1_ref, w3_ref, w2_ref, g_ref, b_ref):
    d_ff = w1_ref.shape[1]
    xb_ref[...] = x.astype(BF16)
    for j in range(d_ff // FF_CHUNK):
        sl = slice(j * FF_CHUNK, (j + 1) * FF_CHUNK)
        a = jnp.dot(xb_ref[...], w1_ref[:, sl], preferred_element_type=F32)
        b = jnp.dot(xb_ref[...], w3_ref[:, sl], preferred_element_type=F32)
        h = (a * jax.nn.sigmoid(a) * b).astype(BF16)
        part = jnp.dot(h, w2_ref[sl, :], preferred_element_type=F32)
        if j == 0:
            acc_ref[...] = part
        else:
            acc_ref[...] += part
    return _layer_norm(DN_ALPHA * x + 0.5 * acc_ref[...], g_ref[...], b_ref[...])


def _dense_in_kernel(x_ref, w1_ref, w3_ref, w2_ref, g_ref, b_ref, win_ref, bg_ref,
                     x1_ref, qkv_ref, hyu_ref, gate_ref, xb_ref, acc_ref):
    x1 = _swiglu_ln(x_ref[...], xb_ref, acc_ref, w1_ref, w3_ref, w2_ref, g_ref, b_ref)
    x1_ref[...] = x1
    xb_ref[...] = x1.astype(BF16)
    n_qkv = qkv_ref.shape[1]
    n_hy = hyu_ref.shape[1]
    n_head = n_qkv // 3
    for j in range(win_ref.shape[1] // PROJ_CHUNK):
        c0 = j * PROJ_CHUNK
        p = jnp.dot(xb_ref[...], win_ref[:, c0:c0 + PROJ_CHUNK], preferred_element_type=F32)
        if c0 < n_qkv:
            if c0 < n_head:
                p = p * (NA_HEAD_DIM ** -0.5)
            qkv_ref[:, c0:c0 + PROJ_CHUNK] = p.astype(BF16)
        elif c0 < n_qkv + n_hy:
            hyu_ref[:, c0 - n_qkv:c0 - n_qkv + PROJ_CHUNK] = p.astype(BF16)
        else:
            g0 = c0 - n_qkv - n_hy
            gate_ref[:, g0:g0 + PROJ_CHUNK] = jax.nn.sigmoid(p + bg_ref[:, g0:g0 + PROJ_CHUNK]).astype(BF16)


def _dense_in(x2d, w1, w3, w2, g, b, w_in, b_gate, n_qkv, n_hy):
    n, d = x2d.shape
    d_ff = w1.shape[1]
    n_gate = w_in.shape[1] - n_qkv - n_hy
    assert n % ROW_TILE == 0 and d_ff % FF_CHUNK == 0
    assert n_qkv % (3 * PROJ_CHUNK) == 0 and n_hy % PROJ_CHUNK == 0 and n_gate % PROJ_CHUNK == 0
    row = lambda i: (i, 0)
    whole = lambda i: (0, 0)
    return pl.pallas_call(
        _dense_in_kernel,
        grid=(n // ROW_TILE,),
        in_specs=[
            pl.BlockSpec((ROW_TILE, d), row),
            _resident((d, d_ff), whole), _resident((d, d_ff), whole), _resident((d_ff, d), whole),
            _resident((1, d), whole), _resident((1, d), whole),
            _resident(w_in.shape, whole), _resident((1, n_gate), whole),
        ],
        out_specs=[
            pl.BlockSpec((ROW_TILE, d), row),
            pl.BlockSpec((ROW_TILE, n_qkv), row),
            pl.BlockSpec((ROW_TILE, n_hy), row),
            pl.BlockSpec((ROW_TILE, n_gate), row),
        ],
        out_shape=[
            jax.ShapeDtypeStruct((n, d), F32),
            jax.ShapeDtypeStruct((n, n_qkv), BF16),
            jax.ShapeDtypeStruct((n, n_hy), BF16),
            jax.ShapeDtypeStruct((n, n_gate), BF16),
        ],
        scratch_shapes=[pltpu.VMEM((ROW_TILE, d), BF16), pltpu.VMEM((ROW_TILE, d), F32)],
        compiler_params=_cparams(("arbitrary",)),
        name="dense_in",
    )(x2d, w1, w3, w2, g, b, w_in, b_gate)


def _dense_out_kernel(x1_ref, ya_ref, yb_ref, gate_ref, wpa_ref, wpb_ref, wout_ref, g2_ref, b2_ref,
                      w1_ref, w3_ref, w2_ref, g3_ref, b3_ref, o_ref, xb_ref, acc_ref):
    d = x1_ref.shape[1]
    pa = jnp.dot(ya_ref[...], wpa_ref[...], preferred_element_type=F32)
    pb = jnp.dot(yb_ref[...], wpb_ref[...], preferred_element_type=F32)
    m = gate_ref[:, :d].astype(F32) * pa + gate_ref[:, d:].astype(F32) * pb
    mix = jnp.dot(m.astype(BF16), wout_ref[...], preferred_element_type=F32)
    x2 = _layer_norm(DN_ALPHA * x1_ref[...] + mix, g2_ref[...], b2_ref[...])
    o_ref[...] = _swiglu_ln(x2, xb_ref, acc_ref, w1_ref, w3_ref, w2_ref, g3_ref, b3_ref)


def _dense_out(x1, ya, yb, gates, wpa, wpb, wout, g2, b2, w1, w3, w2, g3, b3):
    n, d = x1.shape
    d_ff = w1.shape[1]
    row = lambda i: (i, 0)
    whole = lambda i: (0, 0)
    return pl.pallas_call(
        _dense_out_kernel,
        grid=(n // ROW_TILE,),
        in_specs=[
            pl.BlockSpec((ROW_TILE, d), row),
            pl.BlockSpec((ROW_TILE, ya.shape[1]), row),
            pl.BlockSpec((ROW_TILE, yb.shape[1]), row),
            pl.BlockSpec((ROW_TILE, gates.shape[1]), row),
            _resident(wpa.shape, whole), _resident(wpb.shape, whole), _resident(wout.shape, whole),
            _resident((1, d), whole), _resident((1, d), whole),
            _resident((d, d_ff), whole), _resident((d, d_ff), whole), _resident((d_ff, d), whole),
            _resident((1, d), whole), _resident((1, d), whole),
        ],
        out_specs=pl.BlockSpec((ROW_TILE, d), row),
        out_shape=jax.ShapeDtypeStruct((n, d), F32),
        scratch_shapes=[pltpu.VMEM((ROW_TILE, d), BF16), pltpu.VMEM((ROW_TILE, d), F32)],
        compiler_params=_cparams(("arbitrary",)),
        name="dense_out",
    )(x1, ya, yb, gates, wpa, wpb, wout, g2, b2, w1, w3, w2, g3, b3)


def _window_start(n, k):
    return np.clip(np.arange(n) - k // 2, 0, n - k)


def _na_group_types(rows):
    last = rows - NA_QROWS
    return ((0, 0), (NA_QROWS * 2, NA_QROWS), (last, rows - NA_KROWS))


def _na_tables(rows):
    w = GRID_W
    shift = np.zeros((2, 2 * NA_KW - 1, w, 2 * w), np.float32)
    q = np.arange(w)[:, None]
    kc = np.arange(w)[None, :]
    for dc in range(2 * NA_KW - 1):
        hit = (kc - q + NA_KW - 1 == dc).astype(np.float32)
        shift[0, dc, :, :w] = hit
        shift[1, dc, :, w:] = hit
    cs = _window_start(w, NA_KW)
    col_ok = (kc >= cs[:, None]) & (kc < cs[:, None] + NA_KW)
    rs_all = _window_start(rows, NA_KH)
    negmask = np.full((3, NA_QROWS, w, NA_KROWS, w), NEG, np.float32)
    for t, (r0, kb) in enumerate(_na_group_types(rows)):
        for i in range(NA_QROWS):
            rs = rs_all[r0 + i]
            for j in range(NA_KROWS):
                if rs <= kb + j < rs + NA_KH:
                    negmask[t, i, :, j, :] = np.where(col_ok, 0.0, NEG)
    return shift, negmask.reshape(3, NA_QROWS * w, NA_KROWS * w)


def _na_bias_kernel(rpb_ref, shift_ref, neg_ref, o_ref, u_ref, *, rows):
    h = pl.program_id(0)
    n_dr = 2 * NA_KH - 1
    n_dc = 2 * NA_KW - 1
    w = GRID_W

    def dr_body(dr, carry):
        def dc_body(dc, acc):
            s = rpb_ref[(h * n_dr + dr) * n_dc + dc]
            return acc[0] + s * shift_ref[0, dc], acc[1] + s * shift_ref[1, dc]

        zero = jnp.zeros((w, 2 * w), F32)
        left, right = lax.fori_loop(0, n_dc, dc_body, (zero, zero))
        u_ref[0, dr] = left
        u_ref[1, dr] = right
        return carry

    lax.fori_loop(0, n_dr, dr_body, 0)
    rs_all = _window_start(rows, NA_KH)
    for t, (r0, kb) in enumerate(_na_group_types(rows)):
        for i in range(NA_QROWS):
            rs = rs_all[r0 + i]
            for jj in range(NA_KROWS // 2):
                val = neg_ref[t, i * w:(i + 1) * w, jj * 2 * w:(jj + 1) * 2 * w]
                for s in range(2):
                    kr = kb + 2 * jj + s
                    if rs <= kr < rs + NA_KH:
                        val = val + u_ref[s, kr - (r0 + i) + NA_KH - 1]
                o_ref[t, 0, i * w:(i + 1) * w, jj * 2 * w:(jj + 1) * 2 * w] = val


def _na_bias(rpb, rows):
    shift, negmask = _na_tables(rows)
    nq, nk = NA_QROWS * GRID_W, NA_KROWS * GRID_W
    return pl.pallas_call(
        functools.partial(_na_bias_kernel, rows=rows),
        grid=(NA_HEADS,),
        in_specs=[
            pl.BlockSpec(memory_space=pltpu.SMEM),
            _resident(shift.shape, lambda h: (0, 0, 0, 0)),
            _resident(negmask.shape, lambda h: (0, 0, 0)),
        ],
        out_specs=pl.BlockSpec((3, 1, nq, nk), lambda h: (0, h, 0, 0)),
        out_shape=jax.ShapeDtypeStruct((3, NA_HEADS, nq, nk), F32),
        scratch_shapes=[pltpu.VMEM((2, 2 * NA_KH - 1, GRID_W, 2 * GRID_W), F32)],
        compiler_params=_cparams(("arbitrary",)),
        name="na_bias",
    )(rpb.reshape(-1), jnp.asarray(shift), jnp.asarray(negmask))


def _na_attn_kernel(q_ref, k_ref, v_ref, bias_ref, o_ref, *, rows):
    g = pl.program_id(1)
    kb = jnp.clip(g * NA_QROWS - NA_KH // 2, 0, rows - NA_KROWS)
    k0 = pl.multiple_of(kb * GRID_W, GRID_W)
    nk = NA_KROWS * GRID_W
    dh = NA_HEAD_DIM
    outs = []
    for h in range(NA_HEADS):
        q = q_ref[0, :, h * dh:(h + 1) * dh]
        k = k_ref[0, pl.ds(k0, nk), h * dh:(h + 1) * dh]
        v = v_ref[0, pl.ds(k0, nk), h * dh:(h + 1) * dh]
        s = lax.dot_general(q, k, (((1,), (1,)), ((), ())), preferred_element_type=F32)
        s = s + bias_ref[0, h]
        p = jnp.exp(s - jnp.max(s, axis=-1, keepdims=True))
        den = jnp.sum(p, axis=-1, keepdims=True)
        o = jnp.dot(p.astype(BF16), v, preferred_element_type=F32)
        outs.append(o / den)
    o_ref[0] = jnp.concatenate(outs, axis=-1).astype(BF16)


def _na_attn(qkv, bias, rows):
    b, seq, three_w = qkv.shape
    width = three_w // 3
    nq, nk = NA_QROWS * GRID_W, NA_KROWS * GRID_W
    groups = rows // NA_QROWS

    def bias_idx(bi, g):
        return (jnp.where(g == 0, 0, jnp.where(g == groups - 1, 2, 1)), 0, 0, 0)

    return pl.pallas_call(
        functools.partial(_na_attn_kernel, rows=rows),
        grid=(b, groups),
        in_specs=[
            pl.BlockSpec((1, nq, width), lambda bi, g: (bi, g, 0)),
            _resident((1, seq, width), lambda bi, g: (bi, 0, 1)),
            _resident((1, seq, width), lambda bi, g: (bi, 0, 2)),
            pl.BlockSpec((1, NA_HEADS, nq, nk), bias_idx),
        ],
        out_specs=pl.BlockSpec((1, nq, width), lambda bi, g: (bi, g, 0)),
        out_shape=jax.ShapeDtypeStruct((b, seq, width), BF16),
        compiler_params=_cparams(("arbitrary", "arbitrary")),
        name="na_attn",
    )(qkv, qkv, qkv, bias)


def _dft_tables():
    k = np.arange(RADIX)
    ang = 2.0 * np.pi * np.outer(k, k) / RADIX
    fr, fi = np.cos(ang), -np.sin(ang)
    ang_t = 2.0 * np.pi * np.outer(k, k) / (RADIX * RADIX)
    twr, twi = np.cos(ang_t), -np.sin(ang_t)
    f32 = lambda a: jnp.asarray(a, F32)
    return f32(fr), f32(fi), f32(twr), f32(twi)


def _embed(cr, ci):
    return jnp.concatenate([jnp.concatenate([cr, -ci], axis=1), jnp.concatenate([ci, cr], axis=1)], axis=0)


def _position_features(seq):
    t = np.linspace(0.0, 1.0, seq)[:, None]
    bands = (HY_EMB_DIM - 1) // 2
    w = (2.0 * math.pi / seq) * np.arange(seq)[:, None]
    f = np.linspace(1e-4, bands - 1, bands)[None, :]
    z = np.concatenate([t, np.cos(f * w), -np.sin(f * w)], axis=-1)
    pos = np.concatenate([np.arange(seq), seq - np.arange(seq)])
    pos[seq] = 0
    zz = np.zeros((2 * seq, LANES), np.float32)
    zz[:, :HY_EMB_DIM] = z[pos]
    return zz


def _filt_mlp_kernel(z_ref, w1_ref, b1_ref, w2_ref, b2_ref, fr_ref, o_ref):
    hp = lax.Precision.HIGHEST
    fr = fr_ref[...]
    h = jnp.sin(fr * (jnp.dot(z_ref[...], w1_ref[...], precision=hp, preferred_element_type=F32) + b1_ref[...]))
    o_ref[...] = jnp.sin(fr * (jnp.dot(h, w2_ref[...], precision=hp, preferred_element_type=F32) + b2_ref[...]))


def _filt_mlp(zz, w1p, b1, w2, b2, freq):
    n2l = zz.shape[0]
    hid = w2.shape[0]
    rows = 2048
    whole = lambda i: (0, 0)
    return pl.pallas_call(
        _filt_mlp_kernel,
        grid=(n2l // rows,),
        in_specs=[pl.BlockSpec((rows, LANES), lambda i: (i, 0)),
                  pl.BlockSpec(w1p.shape, whole), pl.BlockSpec((1, hid), whole),
                  pl.BlockSpec(w2.shape, whole), pl.BlockSpec((1, hid), whole), pl.BlockSpec((1, hid), whole)],
        out_specs=pl.BlockSpec((rows, hid), lambda i: (i, 0)),
        out_shape=jax.ShapeDtypeStruct((n2l, hid), F32),
        compiler_params=_cparams(("arbitrary",)),
        name="filt_mlp",
    )(zz, w1p, b1, w2, b2, freq)


def _filt_time_kernel(h_ref, w3_ref, b3_ref, delta_ref, k_ref, ss_ref, *, seq):
    i = pl.program_id(0)
    rows = h_ref.shape[0]
    hf = jnp.dot(h_ref[...], w3_ref[0], precision=lax.Precision.HIGHEST, preferred_element_type=F32) + b3_ref[0]
    m = i * rows + lax.broadcasted_iota(jnp.int32, (rows, 1), 0)
    pos = jnp.where(m < seq, m, 2 * seq - m).astype(F32)
    t = pos / (seq - 1)
    k = jnp.where(m == seq, 0.0, hf * jnp.exp(-t * delta_ref[...]))
    for j in range(rows // RADIX):
        k_ref[j * PITCH:j * PITCH + RADIX, :] = k[j * RADIX:(j + 1) * RADIX]
        k_ref[j * PITCH + RADIX:(j + 1) * PITCH, :] = jnp.zeros((PITCH - RADIX, k.shape[1]), F32)
    part = jnp.sum(k * k, axis=0, keepdims=True)

    @pl.when(i == 0)
    def _():
        ss_ref[...] = part

    @pl.when(i != 0)
    def _():
        ss_ref[...] += part


def _filt_time(hdn, w3d, b3d, delta2, seq):
    n2l, hid = hdn.shape
    cols = w3d.shape[2]
    rows = 1024
    half = seq // rows
    prows = rows // RADIX * PITCH
    return pl.pallas_call(
        functools.partial(_filt_time_kernel, seq=seq),
        grid=(n2l // rows,),
        in_specs=[pl.BlockSpec((rows, hid), lambda i: (i, 0)),
                  pl.BlockSpec((1, hid, cols), lambda i: (i // half, 0, 0)),
                  pl.BlockSpec((1, 1, cols), lambda i: (i // half, 0, 0)),
                  pl.BlockSpec((1, cols), lambda i: (0, 0))],
        out_specs=[pl.BlockSpec((prows, cols), lambda i: (i, 0)),
                   pl.BlockSpec((1, cols), lambda i: (0, 0))],
        out_shape=[jax.ShapeDtypeStruct((n2l // RADIX * PITCH, cols), F32), jax.ShapeDtypeStruct((1, cols), F32)],
        compiler_params=_cparams(("arbitrary",)),
        name="filt_time",
    )(hdn, w3d, b3d, delta2)


def _filt_spec_kernel(k_ref, ss_ref, fa_ref, fr_ref, fi_ref, twr_ref, twi_ref, o_ref, a_ref):
    r = RADIX
    scale = lax.rsqrt(ss_ref[...] + 1e-12)

    def a_body(n2, c):
        rhs = k_ref[pl.ds(n2, r, stride=PITCH), :].astype(BF16)
        out = jnp.dot(fa_ref[...], rhs, preferred_element_type=F32)
        a_ref[0, pl.ds(n2, r, stride=PITCH), :] = out[:r]
        a_ref[1, pl.ds(n2, r, stride=PITCH), :] = out[r:]
        return c

    lax.fori_loop(0, r, a_body, 0, unroll=DFT_UNROLL)

    def b_body(k1, c):
        r0 = pl.multiple_of(k1 * PITCH, 8)
        o0 = pl.multiple_of(k1 * r, r)
        blk = jnp.concatenate([a_ref[0, pl.ds(r0, r), :], a_ref[1, pl.ds(r0, r), :]], axis=0).astype(BF16)
        twr, twi = twr_ref[pl.ds(k1, 1), :], twi_ref[pl.ds(k1, 1), :]
        fr, fi = fr_ref[...], fi_ref[...]
        hemb = _embed(fr * twr - fi * twi, fr * twi + fi * twr).astype(BF16)
        s = jnp.dot(hemb, blk, preferred_element_type=F32) * scale
        o_ref[0, pl.ds(o0, r), :] = s[:r].astype(BF16)
        o_ref[1, pl.ds(o0, r), :] = s[r:].astype(BF16)
        return c

    lax.fori_loop(0, r, b_body, 0, unroll=DFT_UNROLL)


def _filt_spec(ktime, ss, fa_real, fr, fi, twr, twi):
    prows, cols = ktime.shape
    n2l = prows // PITCH * RADIX
    whole = lambda j: (0, 0)
    return pl.pallas_call(
        _filt_spec_kernel,
        grid=(cols // LANES,),
        in_specs=[pl.BlockSpec((prows, LANES), lambda j: (0, j)),
                  pl.BlockSpec((1, LANES), lambda j: (0, j)),
                  _resident(fa_real.shape, whole), _resident(fr.shape, whole), _resident(fi.shape, whole),
                  _resident(twr.shape, whole), _resident(twi.shape, whole)],
        out_specs=pl.BlockSpec((2, n2l, LANES), lambda j: (0, 0, j)),
        out_shape=jax.ShapeDtypeStruct((2, n2l, cols), BF16),
        scratch_shapes=[pltpu.VMEM((2, prows, LANES), F32)],
        compiler_params=_cparams(("arbitrary",)),
        name="filt_spec",
    )(ktime, ss, fa_real, fr, fi, twr, twi)


def _short_conv_rows(u_ref, bi, r0, nrows, seq, w_ref, b_ref):
    edge = 16
    cur = u_ref[bi, pl.ds(r0, nrows), :].astype(F32)
    before = u_ref[bi, pl.ds(pl.multiple_of(jnp.maximum(r0 - edge, 0), edge), edge), :].astype(F32)
    after = u_ref[bi, pl.ds(pl.multiple_of(jnp.minimum(r0 + nrows, seq - edge), edge), edge), :].astype(F32)
    prev_edge = jnp.where(r0 > 0, before[edge - 1:edge, :], 0.0)
    next_edge = jnp.where(r0 + nrows < seq, after[0:1, :], 0.0)
    row = lax.broadcasted_iota(jnp.int32, cur.shape, 0)
    prev = jnp.where(row == 0, prev_edge, pltpu.roll(cur, 1, 0))
    nxt = jnp.where(row == nrows - 1, next_edge, pltpu.roll(cur, nrows - 1, 0))
    return b_ref[...] + w_ref[0:1, :] * prev + w_ref[1:2, :] * cur + w_ref[2:3, :] * nxt


def _hy_conv_kernel(z_ref, zw_ref, zb_ref, u_ref, uw_ref, ub_ref, ks_ref, hb_ref,
                    fa_ref, fc_ref, fr_ref, fi_ref, twr_ref, twi_ref, o_ref, a_ref, v_ref, *, conv_in):
    r = RADIX
    half = r // 2
    nb, seq, _ = z_ref.shape
    nchunk = seq // CONV_ROWS
    blocks = CONV_ROWS // r

    def z_rows(bi, r0):
        if conv_in:
            return _short_conv_rows(z_ref, bi, r0, CONV_ROWS, seq, zw_ref, zb_ref)
        return z_ref[bi, pl.ds(r0, CONV_ROWS), :].astype(F32)

    def load_body(c, carry):
        r0 = pl.multiple_of(c * CONV_ROWS, CONV_ROWS)
        for bi in range(nb):
            v = z_rows(bi, r0)
            for j in range(blocks):
                p0 = pl.multiple_of((c * blocks + j) * PITCH, 8)
                v_ref[bi, pl.ds(p0, r), :] = v[j * r:(j + 1) * r]
        return carry

    lax.fori_loop(0, nchunk, load_body, 0)

    def a_body(n2, carry):
        rhs = jnp.concatenate([v_ref[0, pl.ds(n2, half, stride=PITCH), :],
                               v_ref[1, pl.ds(n2, half, stride=PITCH), :]], axis=0).astype(BF16)
        out = jnp.dot(fa_ref[...], rhs, preferred_element_type=F32)
        a_ref[0, pl.ds(n2, r, stride=PITCH), :] = out[:r]
        a_ref[1, pl.ds(n2, r, stride=PITCH), :] = out[r:]
        return carry

    lax.fori_loop(0, r, a_body, 0, unroll=DFT_UNROLL)

    def b_body(k1, carry):
        r0 = pl.multiple_of(k1 * PITCH, 8)
        s0 = pl.multiple_of(k1 * r, r)
        blk = jnp.concatenate([a_ref[0, pl.ds(r0, r), :], a_ref[1, pl.ds(r0, r), :]], axis=0).astype(BF16)
        twr, twi = twr_ref[pl.ds(k1, 1), :], twi_ref[pl.ds(k1, 1), :]
        fr, fi = fr_ref[...], fi_ref[...]
        hemb = _embed(fr * twr - fi * twi, fr * twi + fi * twr).astype(BF16)
        s = jnp.dot(hemb, blk, preferred_element_type=F32)
        sr, si = s[:r], s[r:]
        kr = ks_ref[0, pl.ds(s0, r), :].astype(F32)
        ki = ks_ref[1, pl.ds(s0, r), :].astype(F32)
        p = jnp.concatenate([sr * kr - si * ki, sr * ki + si * kr], axis=0).astype(BF16)
        t = jnp.dot(fc_ref[...], p, preferred_element_type=F32)
        a_ref[0, pl.ds(r0, r), :] = t[:r]
        a_ref[1, pl.ds(r0, r), :] = t[r:]
        return carry

    lax.fori_loop(0, r, b_body, 0, unroll=DFT_UNROLL)

    inv_n = 1.0 / (r * r)

    def c_body(n2, carry):
        rhs = jnp.concatenate([a_ref[0, pl.ds(n2, r, stride=PITCH), :],
                               a_ref[1, pl.ds(n2, r, stride=PITCH), :]], axis=0).astype(BF16)
        twr, twi = twr_ref[pl.ds(n2, 1), :], twi_ref[pl.ds(n2, 1), :]
        fr, fi = fr_ref[0:half, :], fi_ref[0:half, :]
        gemb = _embed((fr * twr - fi * twi) * inv_n, (fr * twi + fi * twr) * (-inv_n)).astype(BF16)
        y = jnp.dot(gemb, rhs, preferred_element_type=F32)
        v_ref[0, pl.ds(n2, half, stride=PITCH), :] = y[:half]
        v_ref[1, pl.ds(n2, half, stride=PITCH), :] = y[half:]
        return carry

    lax.fori_loop(0, r, c_body, 0, unroll=DFT_UNROLL)

    def out_body(c, carry):
        r0 = pl.multiple_of(c * CONV_ROWS, CONV_ROWS)
        for bi in range(nb):
            gate = _short_conv_rows(u_ref, bi, r0, CONV_ROWS, seq, uw_ref, ub_ref)
            conv = jnp.concatenate(
                [v_ref[bi, pl.ds(pl.multiple_of((c * blocks + j) * PITCH, 8), r), :] for j in range(blocks)], axis=0)
            o_ref[bi, pl.ds(r0, CONV_ROWS), :] = (gate * (conv + hb_ref[...] * z_rows(bi, r0))).astype(BF16)
        return carry

    lax.fori_loop(0, nchunk, out_body, 0)


def _hy_conv(z, z_part, u, u_part, order, short_w, short_b, kspec, hy_bias, tables, conv_in):
    nb, seq, _ = u.shape
    width = hy_bias.shape[1]
    tiles = width // LANES
    fa, fc, fr, fi, twr, twi = tables
    whole = lambda j: (0, 0)
    zcol = lambda j: (0, 0, z_part * tiles + j)
    ucol = lambda j: (0, 0, u_part * tiles + j)
    return pl.pallas_call(
        functools.partial(_hy_conv_kernel, conv_in=conv_in),
        grid=(tiles,),
        in_specs=[
            _resident((nb, seq, LANES), zcol),
            pl.BlockSpec((3, LANES), lambda j: (0, z_part * tiles + j)),
            pl.BlockSpec((1, LANES), lambda j: (0, z_part * tiles + j)),
            _resident((nb, seq, LANES), ucol),
            pl.BlockSpec((3, LANES), lambda j: (0, u_part * tiles + j)),
            pl.BlockSpec((1, LANES), lambda j: (0, u_part * tiles + j)),
            _resident((2, 2 * seq, LANES), lambda j: (0, 0, order * tiles + j)),
            pl.BlockSpec((1, LANES), lambda j: (0, j)),
            _resident(fa.shape, whole), _resident(fc.shape, whole), _resident(fr.shape, whole),
            _resident(fi.shape, whole), _resident(twr.shape, whole), _resident(twi.shape, whole),
        ],
        out_specs=pl.BlockSpec((nb, seq, LANES), lambda j: (0, 0, j)),
        out_shape=jax.ShapeDtypeStruct((nb, seq, width), BF16),
        scratch_shapes=[pltpu.VMEM((2, 2 * seq // RADIX * PITCH, LANES), F32),
                        pltpu.VMEM((nb, seq // RADIX * PITCH, LANES), F32)],
        compiler_params=_cparams(("arbitrary",)),
        name=f"hy_conv{order}",
    )(z, short_w, short_b, u, short_w, short_b, kspec, hy_bias[order:order + 1], fa, fc, fr, fi, twr, twi)


def _hyena(hyu, short_w, short_b, fw1, fb1, fw2, fb2, fw3, fb3, freq, hy_bias):
    nb, seq, _ = hyu.shape
    assert nb == 2 and 2 * seq == RADIX * RADIX
    width = hy_bias.shape[1]
    hid = fw2.shape[0]
    fr, fi, twr, twi = _dft_tables()
    half = RADIX // 2
    fa_real = jnp.concatenate([fr, fi], axis=0).astype(BF16)
    fa_half = _embed(fr[:, :half], fi[:, :half]).astype(BF16)
    fc = _embed(fr, -fi).astype(BF16)

    zz = jnp.asarray(_position_features(seq))
    w1p = jnp.zeros((LANES, hid), F32).at[:HY_EMB_DIM].set(fw1)
    hdn = _filt_mlp(zz, w1p, fb1.reshape(1, hid), fw2, fb2.reshape(1, hid), freq.reshape(1, hid))
    w3d = fw3.reshape(hid, 2, HY_ORDER * width).transpose(1, 0, 2)
    b3d = fb3.reshape(2, 1, HY_ORDER * width)
    max_decay = math.log(HY_DECAY_TARGET) / HY_FAST_DECAY_PCT
    min_decay = math.log(HY_DECAY_TARGET) / HY_SLOW_DECAY_PCT
    delta = np.abs(np.linspace(min_decay, max_decay, width)).astype(np.float32)
    delta2 = jnp.asarray(np.tile(delta, HY_ORDER)[None, :])
    ktime, ss = _filt_time(hdn, w3d, b3d, delta2, seq)
    kspec = _filt_spec(ktime, ss, fa_real, fr, fi, twr, twi)

    tables = (fa_half, fc, fr, fi, twr, twi)
    sb = short_b.reshape(1, -1)
    z1 = _hy_conv(hyu, 0, hyu, 1, 0, short_w, sb, kspec, hy_bias, tables, conv_in=True)
    return _hy_conv(z1, 0, hyu, 2, 1, short_w, sb, kspec, hy_bias, tables, conv_in=False)


def kernel(x, ln1_g, ln1_b, ffn1_w1, ffn1_w3, ffn1_w2, w_in, b_gate, na_rpb, hy_short_w, hy_short_b,
           hy_filt_w1, hy_filt_b1, hy_filt_w2, hy_filt_b2, hy_filt_w3, hy_filt_b3, hy_filt_freq, hy_bias,
           w_pa, w_pb, w_out, ln2_g, ln2_b, ffn2_w1, ffn2_w3, ffn2_w2, ln3_g, ln3_b):
    assert ln1_g.shape[0] == DEPTH
    b, seq, d = x.shape
    rows = seq // GRID_W
    na_width = NA_HEADS * NA_HEAD_DIM
    n_qkv = 3 * na_width
    n_hy = hy_short_w.shape[2]
    bf = lambda a: a[0].astype(BF16)
    vec = lambda a: a[0].reshape(1, -1)

    x1, qkv, hyu, gates = _dense_in(
        x.reshape(b * seq, d), bf(ffn1_w1), bf(ffn1_w3), bf(ffn1_w2), vec(ln1_g), vec(ln1_b),
        bf(w_in), vec(b_gate), n_qkv, n_hy)

    bias = _na_bias(na_rpb[0], rows)
    ya = _na_attn(qkv.reshape(b, seq, n_qkv), bias, rows)

    yb = _hyena(hyu.reshape(b, seq, n_hy), hy_short_w[0], hy_short_b[0], hy_filt_w1[0], hy_filt_b1[0],
                hy_filt_w2[0], hy_filt_b2[0], hy_filt_w3[0], hy_filt_b3[0], hy_filt_freq[0], hy_bias[0])

    out = _dense_out(
        x1, ya.reshape(b * seq, na_width), yb.reshape(b * seq, -1), gates,
        bf(w_pa), bf(w_pb), bf(w_out), vec(ln2_g), vec(ln2_b),
        bf(ffn2_w1), bf(ffn2_w3), bf(ffn2_w2), vec(ln3_g), vec(ln3_b))
    return out.reshape(b, seq, d)
```

```python
import functools
import math

import numpy as np
import jax
import jax.numpy as jnp
from jax import lax
from jax.experimental import pallas as pl
from jax.experimental.pallas import tpu as pltpu

F32 = jnp.float32
BF16 = jnp.bfloat16

GRID_W = 64
NA_HEADS = 8
NA_HEAD_DIM = 64
NA_KH = 8
NA_KW = 16
HY_ORDER = 2
HY_EMB_DIM = 33
HY_FAST_DECAY_PCT = 0.3
HY_SLOW_DECAY_PCT = 1.5
HY_DECAY_TARGET = 1e-2
DEPTH = 1
DN_ALPHA = (2 * DEPTH) ** 0.25
LN_EPS = 1e-5

LANES = 128
VMEM_LIMIT = 60 * 1024 * 1024

ROW_TILE = 512
FF_CHUNK = 256
PROJ_CHUNK = 512
NA_QROWS = 4
NA_KROWS = NA_QROWS + NA_KH
RADIX = 128
PITCH = RADIX + 8
DFT_UNROLL = 4
CONV_ROWS = 512
NEG = -1e30


def _cparams(sem):
    return pltpu.CompilerParams(dimension_semantics=sem, vmem_limit_bytes=VMEM_LIMIT)


def _resident(shape, index_map):
    return pl.BlockSpec(shape, index_map, pipeline_mode=pl.Buffered(1))


def _layer_norm(r, g, b):
    mu = jnp.mean(r, axis=-1, keepdims=True)
    d = r - mu
    var = jnp.mean(d * d, axis=-1, keepdims=True)
    return d * lax.rsqrt(var + LN_EPS) * g + b


def _swiglu_ln(x, xb_ref, acc_ref, w1_ref, w3_ref, w2_ref, g_ref, b_ref):
    d_ff = w1_ref.shape[1]
    xb_ref[...] = x.astype(BF16)
    for j in range(d_ff // FF_CHUNK):
        sl = slice(j * FF_CHUNK, (j + 1) * FF_CHUNK)
        a = jnp.dot(xb_ref[...], w1_ref[:, sl], preferred_element_type=F32)
        b = jnp.dot(xb_ref[...], w3_ref[:, sl], preferred_element_type=F32)
        h = (a * jax.nn.sigmoid(a) * b).astype(BF16)
        part = jnp.dot(h, w2_ref[sl, :], preferred_element_type=F32)
        if j == 0:
            acc_ref[...] = part
        else:
            acc_ref[...] += part
    return _layer_norm(DN_ALPHA * x + 0.5 * acc_ref[...], g_ref[...], b_ref[...])


def _dense_in_kernel(x_ref, w1_ref, w3_ref, w2_ref, g_ref, b_ref, win_ref, bg_ref,
                     x1_ref, qkv_ref, hyu_ref, gate_ref, xb_ref, acc_ref):
    x1 = _swiglu_ln(x_ref[...], xb_ref, acc_ref, w1_ref, w3_ref, w2_ref, g_ref, b_ref)
    x1_ref[...] = x1
    xb_ref[...] = x1.astype(BF16)
    n_qkv = qkv_ref.shape[1]
    n_hy = hyu_ref.shape[1]
    n_head = n_qkv // 3
    for j in range(win_ref.shape[1] // PROJ_CHUNK):
        c0 = j * PROJ_CHUNK
        p = jnp.dot(xb_ref[...], win_ref[:, c0:c0 + PROJ_CHUNK], preferred_element_type=F32)
        if c0 < n_qkv:
            if c0 < n_head:
                p = p * (NA_HEAD_DIM ** -0.5)
            qkv_ref[:, c0:c0 + PROJ_CHUNK] = p.astype(BF16)
        elif c0 < n_qkv + n_hy:
            hyu_ref[:, c0 - n_qkv:c0 - n_qkv + PROJ_CHUNK] = p.astype(BF16)
        else:
            g0 = c0 - n_qkv - n_hy
            gate_ref[:, g0:g0 + PROJ_CHUNK] = jax.nn.sigmoid(p + bg_ref[:, g0:g0 + PROJ_CHUNK]).astype(BF16)


def _dense_in(x2d, w1, w3, w2, g, b, w_in, b_gate, n_qkv, n_hy):
    n, d = x2d.shape
    d_ff = w1.shape[1]
    n_gate = w_in.shape[1] - n_qkv - n_hy
    assert n % ROW_TILE == 0 and d_ff % FF_CHUNK == 0
    assert n_qkv % (3 * PROJ_CHUNK) == 0 and n_hy % PROJ_CHUNK == 0 and n_gate % PROJ_CHUNK == 0
    row = lambda i: (i, 0)
    whole = lambda i: (0, 0)
    return pl.pallas_call(
        _dense_in_kernel,
        grid=(n // ROW_TILE,),
        in_specs=[
            pl.BlockSpec((ROW_TILE, d), row),
            _resident((d, d_ff), whole), _resident((d, d_ff), whole), _resident((d_ff, d), whole),
            _resident((1, d), whole), _resident((1, d), whole),
            _resident(w_in.shape, whole), _resident((1, n_gate), whole),
        ],
        out_specs=[
            pl.BlockSpec((ROW_TILE, d), row),
            pl.BlockSpec((ROW_TILE, n_qkv), row),
            pl.BlockSpec((ROW_TILE, n_hy), row),
            pl.BlockSpec((ROW_TILE, n_gate), row),
        ],
        out_shape=[
            jax.ShapeDtypeStruct((n, d), F32),
            jax.ShapeDtypeStruct((n, n_qkv), BF16),
            jax.ShapeDtypeStruct((n, n_hy), BF16),
            jax.ShapeDtypeStruct((n, n_gate), BF16),
        ],
        scratch_shapes=[pltpu.VMEM((ROW_TILE, d), BF16), pltpu.VMEM((ROW_TILE, d), F32)],
        compiler_params=_cparams(("arbitrary",)),
        name="dense_in",
    )(x2d, w1, w3, w2, g, b, w_in, b_gate)


def _dense_out_kernel(x1_ref, ya_ref, yb_ref, gate_ref, wpa_ref, wpb_ref, wout_ref, g2_ref, b2_ref,
                      w1_ref, w3_ref, w2_ref, g3_ref, b3_ref, o_ref, xb_ref, acc_ref):
    d = x1_ref.shape[1]
    pa = jnp.dot(ya_ref[...], wpa_ref[...], preferred_element_type=F32)
    pb = jnp.dot(yb_ref[...], wpb_ref[...], preferred_element_type=F32)
    m = gate_ref[:, :d].astype(F32) * pa + gate_ref[:, d:].astype(F32) * pb
    mix = jnp.dot(m.astype(BF16), wout_ref[...], preferred_element_type=F32)
    x2 = _layer_norm(DN_ALPHA * x1_ref[...] + mix, g2_ref[...], b2_ref[...])
    o_ref[...] = _swiglu_ln(x2, xb_ref, acc_ref, w1_ref, w3_ref, w2_ref, g3_ref, b3_ref)


def _dense_out(x1, ya, yb, gates, wpa, wpb, wout, g2, b2, w1, w3, w2, g3, b3):
    n, d = x1.shape
    d_ff = w1.shape[1]
    row = lambda i: (i, 0)
    whole = lambda i: (0, 0)
    return pl.pallas_call(
        _dense_out_kernel,
        grid=(n // ROW_TILE,),
        in_specs=[
            pl.BlockSpec((ROW_TILE, d), row),
            pl.BlockSpec((ROW_TILE, ya.shape[1]), row),
            pl.BlockSpec((ROW_TILE, yb.shape[1]), row),
            pl.BlockSpec((ROW_TILE, gates.shape[1]), row),
            _resident(wpa.shape, whole), _resident(wpb.shape, whole), _resident(wout.shape, whole),
            _resident((1, d), whole), _resident((1, d), whole),
            _resident((d, d_ff), whole), _resident((d, d_ff), whole), _resident((d_ff, d), whole),
            _resident((1, d), whole), _resident((1, d), whole),
        ],
        out_specs=pl.BlockSpec((ROW_TILE, d), row),
        out_shape=jax.ShapeDtypeStruct((n, d), F32),
        scratch_shapes=[pltpu.VMEM((ROW_TILE, d), BF16), pltpu.VMEM((ROW_TILE, d), F32)],
        compiler_params=_cparams(("arbitrary",)),
        name="dense_out",
    )(x1, ya, yb, gates, wpa, wpb, wout, g2, b2, w1, w3, w2, g3, b3)


def _window_start(n, k):
    return np.clip(np.arange(n) - k // 2, 0, n - k)


def _na_group_types(rows):
    last = rows - NA_QROWS
    return ((0, 0), (NA_QROWS * 2, NA_QROWS), (last, rows - NA_KROWS))


def _na_tables(rows):
    w = GRID_W
    shift = np.zeros((2 * NA_KW - 1, w, 2 * w), np.float32)
    q = np.arange(w)[:, None]
    kc = np.arange(w)[None, :]
    for dc in range(2 * NA_KW - 1):
        shift[dc, :, :w] = (kc - q + NA_KW - 1 == dc).astype(np.float32)
    cs = _window_start(w, NA_KW)
    col_ok = (kc >= cs[:, None]) & (kc < cs[:, None] + NA_KW)
    rs_all = _window_start(rows, NA_KH)
    negmask = np.full((3, NA_QROWS, w, NA_KROWS, w), NEG, np.float32)
    for t, (r0, kb) in enumerate(_na_group_types(rows)):
        for i in range(NA_QROWS):
            rs = rs_all[r0 + i]
            for j in range(NA_KROWS):
                if rs <= kb + j < rs + NA_KH:
                    negmask[t, i, :, j, :] = np.where(col_ok, 0.0, NEG)
    return shift, negmask.reshape(3, NA_QROWS * w, NA_KROWS * w)


def _na_bias_kernel(rpb_ref, shift_ref, neg_ref, o_ref, u_ref, *, rows):
    h = pl.program_id(0)
    n_dr = 2 * NA_KH - 1
    n_dc = 2 * NA_KW - 1
    w = GRID_W

    def dr_body(dr, carry):
        base = (h * n_dr + dr) * n_dc
        left = rpb_ref[base] * shift_ref[0]
        for dc in range(1, n_dc):
            left = left + rpb_ref[base + dc] * shift_ref[dc]
        u_ref[0, dr] = left
        u_ref[1, dr] = pltpu.roll(left, w, 1)
        return carry

    lax.fori_loop(0, n_dr, dr_body, 0)
    rs_all = _window_start(rows, NA_KH)
    for t, (r0, kb) in enumerate(_na_group_types(rows)):
        for i in range(NA_QROWS):
            rs = rs_all[r0 + i]
            for jj in range(NA_KROWS // 2):
                val = neg_ref[t, i * w:(i + 1) * w, jj * 2 * w:(jj + 1) * 2 * w]
                for s in range(2):
                    kr = kb + 2 * jj + s
                    if rs <= kr < rs + NA_KH:
                        val = val + u_ref[s, kr - (r0 + i) + NA_KH - 1]
                o_ref[t, 0, i * w:(i + 1) * w, jj * 2 * w:(jj + 1) * 2 * w] = val


def _na_bias(rpb, rows):
    shift, negmask = _na_tables(rows)
    nq, nk = NA_QROWS * GRID_W, NA_KROWS * GRID_W
    return pl.pallas_call(
        functools.partial(_na_bias_kernel, rows=rows),
        grid=(NA_HEADS,),
        in_specs=[
            pl.BlockSpec(memory_space=pltpu.SMEM),
            _resident(shift.shape, lambda h: (0, 0, 0)),
            _resident(negmask.shape, lambda h: (0, 0, 0)),
        ],
        out_specs=pl.BlockSpec((3, 1, nq, nk), lambda h: (0, h, 0, 0)),
        out_shape=jax.ShapeDtypeStruct((3, NA_HEADS, nq, nk), F32),
        scratch_shapes=[pltpu.VMEM((2, 2 * NA_KH - 1, GRID_W, 2 * GRID_W), F32)],
        compiler_params=_cparams(("arbitrary",)),
        name="na_bias",
    )(rpb.reshape(-1), jnp.asarray(shift), jnp.asarray(negmask))


def _na_attn_kernel(q_ref, k_ref, v_ref, bias_ref, o_ref, *, rows):
    g = pl.program_id(1)
    kb = jnp.clip(g * NA_QROWS - NA_KH // 2, 0, rows - NA_KROWS)
    k0 = pl.multiple_of(kb * GRID_W, GRID_W)
    nk = NA_KROWS * GRID_W
    dh = NA_HEAD_DIM
    lane = lax.broadcasted_iota(jnp.int32, (1, 2 * dh), 1)
    for hp in range(NA_HEADS // 2):
        cols = slice(hp * 2 * dh, (hp + 1) * 2 * dh)
        q2 = q_ref[0, :, cols]
        k2 = k_ref[0, pl.ds(k0, nk), cols]
        v2 = v_ref[0, pl.ds(k0, nk), cols]
        pair = []
        for s in range(2):
            own = (lane < dh) if s == 0 else (lane >= dh)
            sc = lax.dot_general(q2, jnp.where(own, k2, jnp.zeros_like(k2)), (((1,), (1,)), ((), ())),
                                 preferred_element_type=F32)
            sc = sc + bias_ref[0, 2 * hp + s]
            e = jnp.exp(sc - jnp.max(sc, axis=-1, keepdims=True)).astype(BF16)
            r = jnp.dot(e, jnp.where(own, v2, jnp.ones_like(v2)), preferred_element_type=F32)
            den = r[:, dh:dh + 1] if s == 0 else r[:, 0:1]
            pair.append(jnp.where(own, r / den, 0.0))
        o_ref[0, :, cols] = (pair[0] + pair[1]).astype(BF16)


def _na_attn(qkv, bias, rows):
    b, seq, three_w = qkv.shape
    width = three_w // 3
    nq, nk = NA_QROWS * GRID_W, NA_KROWS * GRID_W
    groups = rows // NA_QROWS

    def bias_idx(bi, g):
        return (jnp.where(g == 0, 0, jnp.where(g == groups - 1, 2, 1)), 0, 0, 0)

    return pl.pallas_call(
        functools.partial(_na_attn_kernel, rows=rows),
        grid=(b, groups),
        in_specs=[
            pl.BlockSpec((1, nq, width), lambda bi, g: (bi, g, 0)),
            _resident((1, seq, width), lambda bi, g: (bi, 0, 1)),
            _resident((1, seq, width), lambda bi, g: (bi, 0, 2)),
            pl.BlockSpec((1, NA_HEADS, nq, nk), bias_idx),
        ],
        out_specs=pl.BlockSpec((1, nq, width), lambda bi, g: (bi, g, 0)),
        out_shape=jax.ShapeDtypeStruct((b, seq, width), BF16),
        compiler_params=_cparams(("arbitrary", "arbitrary")),
        name="na_attn",
    )(qkv, qkv, qkv, bias)


def _dft_tables():
    k = np.arange(RADIX)
    ang = 2.0 * np.pi * np.outer(k, k) / RADIX
    fr, fi = np.cos(ang), -np.sin(ang)
    ang_t = 2.0 * np.pi * np.outer(k, k) / (RADIX * RADIX)
    twr, twi = np.cos(ang_t), -np.sin(ang_t)
    f32 = lambda a: jnp.asarray(a, F32)
    return f32(fr), f32(fi), f32(twr), f32(twi)


def _embed(cr, ci):
    return jnp.concatenate([jnp.concatenate([cr, -ci], axis=1), jnp.concatenate([ci, cr], axis=1)], axis=0)


def _position_features(seq):
    t = np.linspace(0.0, 1.0, seq)[:, None]
    bands = (HY_EMB_DIM - 1) // 2
    w = (2.0 * math.pi / seq) * np.arange(seq)[:, None]
    f = np.linspace(1e-4, bands - 1, bands)[None, :]
    z = np.concatenate([t, np.cos(f * w), -np.sin(f * w)], axis=-1)
    rev = seq - np.arange(seq)
    rev[0] = 0
    zz = np.zeros((seq, 2 * LANES), np.float32)
    zz[:, :HY_EMB_DIM] = z
    zz[:, LANES:LANES + HY_EMB_DIM] = z[rev]
    return zz


def _filt_mlp_kernel(z_ref, w1_ref, b1_ref, w2_ref, b2_ref, fr_ref, o_ref):
    hp = lax.Precision.HIGHEST
    fr = fr_ref[...]
    hid = o_ref.shape[2]
    h = jnp.sin(fr * (jnp.dot(z_ref[...], w1_ref[...], precision=hp, preferred_element_type=F32) + b1_ref[...]))
    h = jnp.sin(fr * (jnp.dot(h, w2_ref[...], precision=hp, preferred_element_type=F32) + b2_ref[...]))
    o_ref[0] = h[:, :hid]
    o_ref[1] = h[:, hid:]


def _filt_mlp(zz, fw1, fb1, fw2, fb2, freq):
    seq = zz.shape[0]
    hid = fw2.shape[0]
    assert 2 * hid == LANES
    w1 = jnp.zeros((2 * LANES, LANES), F32)
    w1 = w1.at[:HY_EMB_DIM, :hid].set(fw1).at[LANES:LANES + HY_EMB_DIM, hid:].set(fw1)
    w2 = jnp.zeros((LANES, LANES), F32).at[:hid, :hid].set(fw2).at[hid:, hid:].set(fw2)
    twice = lambda v: jnp.concatenate([v, v]).reshape(1, LANES)
    rows = 2048
    whole = lambda i: (0, 0)
    return pl.pallas_call(
        _filt_mlp_kernel,
        grid=(seq // rows,),
        in_specs=[pl.BlockSpec((rows, 2 * LANES), lambda i: (i, 0)),
                  pl.BlockSpec(w1.shape, whole), pl.BlockSpec((1, LANES), whole),
                  pl.BlockSpec(w2.shape, whole), pl.BlockSpec((1, LANES), whole), pl.BlockSpec((1, LANES), whole)],
        out_specs=pl.BlockSpec((2, rows, hid), lambda i: (0, i, 0)),
        out_shape=jax.ShapeDtypeStruct((2, seq, hid), F32),
        compiler_params=_cparams(("arbitrary",)),
        name="filt_mlp",
    )(zz, w1, twice(fb1), w2, twice(fb2), twice(freq))


def _split_bf16(a):
    hi = a.astype(BF16)
    return hi, (a - hi.astype(F32)).astype(BF16)


def _filt_time_kernel(h_ref, w3_ref, b3_ref, delta_ref, k_ref, ss_ref, *, seq):
    i = pl.program_id(0)
    rows = h_ref.shape[0]
    h_hi, h_lo = _split_bf16(h_ref[...])
    w_hi, w_lo = _split_bf16(w3_ref[0])
    hf = (jnp.dot(h_hi, w_hi, preferred_element_type=F32) + jnp.dot(h_lo, w_hi, preferred_element_type=F32)
          + jnp.dot(h_hi, w_lo, preferred_element_type=F32)) + b3_ref[0]
    m = i * rows + lax.broadcasted_iota(jnp.int32, (rows, 1), 0)
    pos = jnp.where(m < seq, m, 2 * seq - m).astype(F32)
    t = pos / (seq - 1)
    k = jnp.where(m == seq, 0.0, hf * jnp.exp(-t * delta_ref[...]))
    for j in range(rows // RADIX):
        k_ref[j * PITCH:j * PITCH + RADIX, :] = k[j * RADIX:(j + 1) * RADIX]
        k_ref[j * PITCH + RADIX:(j + 1) * PITCH, :] = jnp.zeros((PITCH - RADIX, k.shape[1]), F32)
    part = jnp.sum(k * k, axis=0, keepdims=True)

    @pl.when(i == 0)
    def _():
        ss_ref[...] = part

    @pl.when(i != 0)
    def _():
        ss_ref[...] += part


def _filt_time(hdn, w3d, b3d, delta2, seq):
    n2l, hid = hdn.shape
    cols = w3d.shape[2]
    rows = 1024
    half = seq // rows
    prows = rows // RADIX * PITCH
    return pl.pallas_call(
        functools.partial(_filt_time_kernel, seq=seq),
        grid=(n2l // rows,),
        in_specs=[pl.BlockSpec((rows, hid), lambda i: (i, 0)),
                  pl.BlockSpec((1, hid, cols), lambda i: (i // half, 0, 0)),
                  pl.BlockSpec((1, 1, cols), lambda i: (i // half, 0, 0)),
                  pl.BlockSpec((1, cols), lambda i: (0, 0))],
        out_specs=[pl.BlockSpec((prows, cols), lambda i: (i, 0)),
                   pl.BlockSpec((1, cols), lambda i: (0, 0))],
        out_shape=[jax.ShapeDtypeStruct((n2l // RADIX * PITCH, cols), F32), jax.ShapeDtypeStruct((1, cols), F32)],
        compiler_params=_cparams(("arbitrary",)),
        name="filt_time",
    )(hdn, w3d, b3d, delta2)


def _filt_spec_kernel(k_ref, ss_ref, fa_ref, fr_ref, fi_ref, twr_ref, twi_ref, o_ref, a_ref):
    r = RADIX
    scale = lax.rsqrt(ss_ref[...] + 1e-12)

    def a_body(n2, c):
        rhs = k_ref[pl.ds(n2, r, stride=PITCH), :].astype(BF16)
        out = jnp.dot(fa_ref[...], rhs, preferred_element_type=F32)
        a_ref[0, pl.ds(n2, r, stride=PITCH), :] = out[:r]
        a_ref[1, pl.ds(n2, r, stride=PITCH), :] = out[r:]
        return c

    lax.fori_loop(0, r, a_body, 0, unroll=DFT_UNROLL)

    def b_body(k1, c):
        r0 = pl.multiple_of(k1 * PITCH, 8)
        o0 = pl.multiple_of(k1 * r, r)
        blk = jnp.concatenate([a_ref[0, pl.ds(r0, r), :], a_ref[1, pl.ds(r0, r), :]], axis=0).astype(BF16)
        twr, twi = twr_ref[pl.ds(k1, 1), :], twi_ref[pl.ds(k1, 1), :]
        fr, fi = fr_ref[...], fi_ref[...]
        hemb = _embed(fr * twr - fi * twi, fr * twi + fi * twr).astype(BF16)
        s = jnp.dot(hemb, blk, preferred_element_type=F32) * scale
        o_ref[0, pl.ds(o0, r), :] = s[:r].astype(BF16)
        o_ref[1, pl.ds(o0, r), :] = s[r:].astype(BF16)
        return c

    lax.fori_loop(0, r, b_body, 0, unroll=DFT_UNROLL)


def _filt_spec(ktime, ss, fa_real, fr, fi, twr, twi):
    prows, cols = ktime.shape
    n2l = prows // PITCH * RADIX
    whole = lambda j: (0, 0)
    return pl.pallas_call(
        _filt_spec_kernel,
        grid=(cols // LANES,),
        in_specs=[pl.BlockSpec((prows, LANES), lambda j: (0, j)),
                  pl.BlockSpec((1, LANES), lambda j: (0, j)),
                  _resident(fa_real.shape, whole), _resident(fr.shape, whole), _resident(fi.shape, whole),
                  _resident(twr.shape, whole), _resident(twi.shape, whole)],
        out_specs=pl.BlockSpec((2, n2l, LANES), lambda j: (0, 0, j)),
        out_shape=jax.ShapeDtypeStruct((2, n2l, cols), BF16),
        scratch_shapes=[pltpu.VMEM((2, prows, LANES), F32)],
        compiler_params=_cparams(("arbitrary",)),
        name="filt_spec",
    )(ktime, ss, fa_real, fr, fi, twr, twi)


def _short_conv_rows(u_ref, bi, r0, nrows, seq, w_ref, b_ref):
    edge = 16
    cur = u_ref[bi, pl.ds(r0, nrows), :].astype(F32)
    before = u_ref[bi, pl.ds(pl.multiple_of(jnp.maximum(r0 - edge, 0), edge), edge), :].astype(F32)
    after = u_ref[bi, pl.ds(pl.multiple_of(jnp.minimum(r0 + nrows, seq - edge), edge), edge), :].astype(F32)
    prev_edge = jnp.where(r0 > 0, before[edge - 1:edge, :], 0.0)
    next_edge = jnp.where(r0 + nrows < seq, after[0:1, :], 0.0)
    row = lax.broadcasted_iota(jnp.int32, cur.shape, 0)
    prev = jnp.where(row == 0, prev_edge, pltpu.roll(cur, 1, 0))
    nxt = jnp.where(row == nrows - 1, next_edge, pltpu.roll(cur, nrows - 1, 0))
    return b_ref[...] + w_ref[0:1, :] * prev + w_ref[1:2, :] * cur + w_ref[2:3, :] * nxt


def _hy_conv_kernel(z_ref, zw_ref, zb_ref, u_ref, uw_ref, ub_ref, ks_ref, hb_ref,
                    fa_ref, fc_ref, fr_ref, fi_ref, twr_ref, twi_ref, o_ref, a_ref, v_ref, *, conv_in):
    r = RADIX
    half = r // 2
    nb, seq, _ = z_ref.shape
    nchunk = seq // CONV_ROWS
    blocks = CONV_ROWS // r

    def z_rows(bi, r0):
        if conv_in:
            return _short_conv_rows(z_ref, bi, r0, CONV_ROWS, seq, zw_ref, zb_ref)
        return z_ref[bi, pl.ds(r0, CONV_ROWS), :].astype(F32)

    def load_body(c, carry):
        r0 = pl.multiple_of(c * CONV_ROWS, CONV_ROWS)
        for bi in range(nb):
            v = z_rows(bi, r0)
            for j in range(blocks):
                p0 = pl.multiple_of((c * blocks + j) * PITCH, 8)
                v_ref[bi, pl.ds(p0, r), :] = v[j * r:(j + 1) * r]
        return carry

    lax.fori_loop(0, nchunk, load_body, 0)

    def a_body(n2, carry):
        rhs = jnp.concatenate([v_ref[0, pl.ds(n2, half, stride=PITCH), :],
                               v_ref[1, pl.ds(n2, half, stride=PITCH), :]], axis=0).astype(BF16)
        out = jnp.dot(fa_ref[...], rhs, preferred_element_type=F32)
        a_ref[0, pl.ds(n2, r, stride=PITCH), :] = out[:r]
        a_ref[1, pl.ds(n2, r, stride=PITCH), :] = out[r:]
        return carry

    lax.fori_loop(0, r, a_body, 0, unroll=DFT_UNROLL)

    def b_body(k1, carry):
        r0 = pl.multiple_of(k1 * PITCH, 8)
        s0 = pl.multiple_of(k1 * r, r)
        blk = jnp.concatenate([a_ref[0, pl.ds(r0, r), :], a_ref[1, pl.ds(r0, r), :]], axis=0).astype(BF16)
        twr, twi = twr_ref[pl.ds(k1, 1), :], twi_ref[pl.ds(k1, 1), :]
        fr, fi = fr_ref[...], fi_ref[...]
        hemb = _embed(fr * twr - fi * twi, fr * twi + fi * twr).astype(BF16)
        s = jnp.dot(hemb, blk, preferred_element_type=F32)
        sr, si = s[:r], s[r:]
        kr = ks_ref[0, pl.ds(s0, r), :].astype(F32)
        ki = ks_ref[1, pl.ds(s0, r), :].astype(F32)
        p = jnp.concatenate([sr * kr - si * ki, sr * ki + si * kr], axis=0).astype(BF16)
        t = jnp.dot(fc_ref[...], p, preferred_element_type=F32)
        a_ref[0, pl.ds(r0, r), :] = t[:r]
        a_ref[1, pl.ds(r0, r), :] = t[r:]
        return carry

    lax.fori_loop(0, r, b_body, 0, unroll=DFT_UNROLL)

    inv_n = 1.0 / (r * r)

    def c_body(n2, carry):
        rhs = jnp.concatenate([a_ref[0, pl.ds(n2, r, stride=PITCH), :],
                               a_ref[1, pl.ds(n2, r, stride=PITCH), :]], axis=0).astype(BF16)
        twr, twi = twr_ref[pl.ds(n2, 1), :], twi_ref[pl.ds(n2, 1), :]
        fr, fi = fr_ref[0:half, :], fi_ref[0:half, :]
        gemb = _embed((fr * twr - fi * twi) * inv_n, (fr * twi + fi * twr) * (-inv_n)).astype(BF16)
        y = jnp.dot(gemb, rhs, preferred_element_type=F32)
        v_ref[0, pl.ds(n2, half, stride=PITCH), :] = y[:half]
        v_ref[1, pl.ds(n2, half, stride=PITCH), :] = y[half:]
        return carry

    lax.fori_loop(0, r, c_body, 0, unroll=DFT_UNROLL)

    def out_body(c, carry):
        r0 = pl.multiple_of(c * CONV_ROWS, CONV_ROWS)
        for bi in range(nb):
            gate = _short_conv_rows(u_ref, bi, r0, CONV_ROWS, seq, uw_ref, ub_ref)
            conv = jnp.concatenate(
                [v_ref[bi, pl.ds(pl.multiple_of((c * blocks + j) * PITCH, 8), r), :] for j in range(blocks)], axis=0)
            o_ref[bi, pl.ds(r0, CONV_ROWS), :] = (gate * (conv + hb_ref[...] * z_rows(bi, r0))).astype(BF16)
        return carry

    lax.fori_loop(0, nchunk, out_body, 0)


def _hy_conv(z, z_part, u, u_part, order, short_w, short_b, kspec, hy_bias, tables, conv_in):
    nb, seq, _ = u.shape
    width = hy_bias.shape[1]
    tiles = width // LANES
    fa, fc, fr, fi, twr, twi = tables
    whole = lambda j: (0, 0)
    zcol = lambda j: (0, 0, z_part * tiles + j)
    ucol = lambda j: (0, 0, u_part * tiles + j)
    return pl.pallas_call(
        functools.partial(_hy_conv_kernel, conv_in=conv_in),
        grid=(tiles,),
        in_specs=[
            _resident((nb, seq, LANES), zcol),
            pl.BlockSpec((3, LANES), lambda j: (0, z_part * tiles + j)),
            pl.BlockSpec((1, LANES), lambda j: (0, z_part * tiles + j)),
            _resident((nb, seq, LANES), ucol),
            pl.BlockSpec((3, LANES), lambda j: (0, u_part * tiles + j)),
            pl.BlockSpec((1, LANES), lambda j: (0, u_part * tiles + j)),
            _resident((2, 2 * seq, LANES), lambda j: (0, 0, order * tiles + j)),
            pl.BlockSpec((1, LANES), lambda j: (0, j)),
            _resident(fa.shape, whole), _resident(fc.shape, whole), _resident(fr.shape, whole),
            _resident(fi.shape, whole), _resident(twr.shape, whole), _resident(twi.shape, whole),
        ],
        out_specs=pl.BlockSpec((nb, seq, LANES), lambda j: (0, 0, j)),
        out_shape=jax.ShapeDtypeStruct((nb, seq, width), BF16),
        scratch_shapes=[pltpu.VMEM((2, 2 * seq // RADIX * PITCH, LANES), F32),
                        pltpu.VMEM((nb, seq // RADIX * PITCH, LANES), F32)],
        compiler_params=_cparams(("arbitrary",)),
        name=f"hy_conv{order}",
    )(z, short_w, short_b, u, short_w, short_b, kspec, hy_bias[order:order + 1], fa, fc, fr, fi, twr, twi)


def _hyena(hyu, short_w, short_b, fw1, fb1, fw2, fb2, fw3, fb3, freq, hy_bias):
    nb, seq, _ = hyu.shape
    assert nb == 2 and 2 * seq == RADIX * RADIX
    width = hy_bias.shape[1]
    hid = fw2.shape[0]
    fr, fi, twr, twi = _dft_tables()
    half = RADIX // 2
    fa_real = jnp.concatenate([fr, fi], axis=0).astype(BF16)
    fa_half = _embed(fr[:, :half], fi[:, :half]).astype(BF16)
    fc = _embed(fr, -fi).astype(BF16)

    zz = jnp.asarray(_position_features(seq))
    hdn = _filt_mlp(zz, fw1, fb1, fw2, fb2, freq).reshape(2 * seq, hid)
    w3d = fw3.reshape(hid, 2, HY_ORDER * width).transpose(1, 0, 2)
    b3d = fb3.reshape(2, 1, HY_ORDER * width)
    max_decay = math.log(HY_DECAY_TARGET) / HY_FAST_DECAY_PCT
    min_decay = math.log(HY_DECAY_TARGET) / HY_SLOW_DECAY_PCT
    delta = np.abs(np.linspace(min_decay, max_decay, width)).astype(np.float32)
    delta2 = jnp.asarray(np.tile(delta, HY_ORDER)[None, :])
    ktime, ss = _filt_time(hdn, w3d, b3d, delta2, seq)
    kspec = _filt_spec(ktime, ss, fa_real, fr, fi, twr, twi)

    tables = (fa_half, fc, fr, fi, twr, twi)
    sb = short_b.reshape(1, -1)
    z1 = _hy_conv(hyu, 0, hyu, 1, 0, short_w, sb, kspec, hy_bias, tables, conv_in=True)
    return _hy_conv(z1, 0, hyu, 2, 1, short_w, sb, kspec, hy_bias, tables, conv_in=False)


def kernel(x, ln1_g, ln1_b, ffn1_w1, ffn1_w3, ffn1_w2, w_in, b_gate, na_rpb, hy_short_w, hy_short_b,
           hy_filt_w1, hy_filt_b1, hy_filt_w2, hy_filt_b2, hy_filt_w3, hy_filt_b3, hy_filt_freq, hy_bias,
           w_pa, w_pb, w_out, ln2_g, ln2_b, ffn2_w1, ffn2_w3, ffn2_w2, ln3_g, ln3_b):
    assert ln1_g.shape[0] == DEPTH
    b, seq, d = x.shape
    rows = seq // GRID_W
    na_width = NA_HEADS * NA_HEAD_DIM
    n_qkv = 3 * na_width
    n_hy = hy_short_w.shape[2]
    bf = lambda a: a[0].astype(BF16)
    vec = lambda a: a[0].reshape(1, -1)

    x1, qkv, hyu, gates = _dense_in(
        x.reshape(b * seq, d), bf(ffn1_w1), bf(ffn1_w3), bf(ffn1_w2), vec(ln1_g), vec(ln1_b),
        bf(w_in), vec(b_gate), n_qkv, n_hy)

    bias = _na_bias(na_rpb[0], rows)
    ya = _na_attn(qkv.reshape(b, seq, n_qkv), bias, rows)

    yb = _hyena(hyu.reshape(b, seq, n_hy), hy_short_w[0], hy_short_b[0], hy_filt_w1[0], hy_filt_b1[0],
                hy_filt_w2[0], hy_filt_b2[0], hy_filt_w3[0], hy_filt_b3[0], hy_filt_freq[0], hy_bias[0])

    out = _dense_out(
        x1, ya.reshape(b * seq, na_width), yb.reshape(b * seq, -1), gates,
        bf(w_pa), bf(w_pb), bf(w_out), vec(ln2_g), vec(ln2_b),
        bf(ffn2_w1), bf(ffn2_w3), bf(ffn2_w2), vec(ln3_g), vec(ln3_b))
    return out.reshape(b, seq, d)
```

```python
import functools
import math

import numpy as np
import jax
import jax.numpy as jnp
from jax import lax
from jax.experimental import pallas as pl
from jax.experimental.pallas import tpu as pltpu

F32 = jnp.float32
BF16 = jnp.bfloat16

GRID_W = 64
NA_HEADS = 8
NA_HEAD_DIM = 64
NA_KH = 8
NA_KW = 16
HY_ORDER = 2
HY_EMB_DIM = 33
HY_FAST_DECAY_PCT = 0.3
HY_SLOW_DECAY_PCT = 1.5
HY_DECAY_TARGET = 1e-2
DEPTH = 1
DN_ALPHA = (2 * DEPTH) ** 0.25
LN_EPS = 1e-5

LANES = 128
VMEM_LIMIT = 60 * 1024 * 1024

ROW_TILE = 512
FF_CHUNK = 256
PROJ_CHUNK = 512
NA_QROWS = 4
NA_KROWS = NA_QROWS + NA_KH
RADIX = 128
PITCH = RADIX + 8
DFT_UNROLL = 4
DFT_UNROLL_STRIDED = 8
CONV_ROWS = 512
NEG = -1e30


def _cparams(sem):
    return pltpu.CompilerParams(dimension_semantics=sem, vmem_limit_bytes=VMEM_LIMIT)


def _resident(shape, index_map):
    return pl.BlockSpec(shape, index_map, pipeline_mode=pl.Buffered(1))


def _layer_norm(r, g, b):
    mu = jnp.mean(r, axis=-1, keepdims=True)
    d = r - mu
    var = jnp.mean(d * d, axis=-1, keepdims=True)
    return d * lax.rsqrt(var + LN_EPS) * g + b


def _swiglu_ln(x, xb_ref, acc_ref, w1_ref, w3_ref, w2_ref, g_ref, b_ref):
    d_ff = w1_ref.shape[1]
    xb_ref[...] = x.astype(BF16)
    for j in range(d_ff // FF_CHUNK):
        sl = slice(j * FF_CHUNK, (j + 1) * FF_CHUNK)
        a = jnp.dot(xb_ref[...], w1_ref[:, sl], preferred_element_type=F32)
        b = jnp.dot(xb_ref[...], w3_ref[:, sl], preferred_element_type=F32)
        h = (a * jax.nn.sigmoid(a) * b).astype(BF16)
        part = jnp.dot(h, w2_ref[sl, :], preferred_element_type=F32)
        if j == 0:
            acc_ref[...] = part
        else:
            acc_ref[...] += part
    return _layer_norm(DN_ALPHA * x + 0.5 * acc_ref[...], g_ref[...], b_ref[...])


def _dense_in_kernel(x_ref, w1_ref, w3_ref, w2_ref, g_ref, b_ref, win_ref, bg_ref,
                     x1_ref, qkv_ref, hyu_ref, gate_ref, xb_ref, acc_ref):
    x1 = _swiglu_ln(x_ref[...], xb_ref, acc_ref, w1_ref, w3_ref, w2_ref, g_ref, b_ref)
    x1_ref[...] = x1
    xb_ref[...] = x1.astype(BF16)
    n_qkv = qkv_ref.shape[1]
    n_hy = hyu_ref.shape[1]
    n_head = n_qkv // 3
    for j in range(win_ref.shape[1] // PROJ_CHUNK):
        c0 = j * PROJ_CHUNK
        p = jnp.dot(xb_ref[...], win_ref[:, c0:c0 + PROJ_CHUNK], preferred_element_type=F32)
        if c0 < n_qkv:
            if c0 < n_head:
                p = p * (NA_HEAD_DIM ** -0.5)
            qkv_ref[:, c0:c0 + PROJ_CHUNK] = p.astype(BF16)
        elif c0 < n_qkv + n_hy:
            hyu_ref[:, c0 - n_qkv:c0 - n_qkv + PROJ_CHUNK] = p.astype(BF16)
        else:
            g0 = c0 - n_qkv - n_hy
            gate_ref[:, g0:g0 + PROJ_CHUNK] = jax.nn.sigmoid(p + bg_ref[:, g0:g0 + PROJ_CHUNK]).astype(BF16)


def _dense_in(x2d, w1, w3, w2, g, b, w_in, b_gate, n_qkv, n_hy):
    n, d = x2d.shape
    d_ff = w1.shape[1]
    n_gate = w_in.shape[1] - n_qkv - n_hy
    assert n % ROW_TILE == 0 and d_ff % FF_CHUNK == 0
    assert n_qkv % (3 * PROJ_CHUNK) == 0 and n_hy % PROJ_CHUNK == 0 and n_gate % PROJ_CHUNK == 0
    row = lambda i: (i, 0)
    whole = lambda i: (0, 0)
    return pl.pallas_call(
        _dense_in_kernel,
        grid=(n // ROW_TILE,),
        in_specs=[
            pl.BlockSpec((ROW_TILE, d), row),
            _resident((d, d_ff), whole), _resident((d, d_ff), whole), _resident((d_ff, d), whole),
            _resident((1, d), whole), _resident((1, d), whole),
            _resident(w_in.shape, whole), _resident((1, n_gate), whole),
        ],
        out_specs=[
            pl.BlockSpec((ROW_TILE, d), row),
            pl.BlockSpec((ROW_TILE, n_qkv), row),
            pl.BlockSpec((ROW_TILE, n_hy), row),
            pl.BlockSpec((ROW_TILE, n_gate), row),
        ],
        out_shape=[
            jax.ShapeDtypeStruct((n, d), F32),
            jax.ShapeDtypeStruct((n, n_qkv), BF16),
            jax.ShapeDtypeStruct((n, n_hy), BF16),
            jax.ShapeDtypeStruct((n, n_gate), BF16),
        ],
        scratch_shapes=[pltpu.VMEM((ROW_TILE, d), BF16), pltpu.VMEM((ROW_TILE, d), F32)],
        compiler_params=_cparams(("arbitrary",)),
        name="dense_in",
    )(x2d, w1, w3, w2, g, b, w_in, b_gate)


def _dense_out_kernel(x1_ref, ya_ref, yb_ref, gate_ref, wpa_ref, wpb_ref, wout_ref, g2_ref, b2_ref,
                      w1_ref, w3_ref, w2_ref, g3_ref, b3_ref, o_ref, xb_ref, acc_ref):
    d = x1_ref.shape[1]
    pa = jnp.dot(ya_ref[...], wpa_ref[...], preferred_element_type=F32)
    pb = jnp.dot(yb_ref[...], wpb_ref[...], preferred_element_type=F32)
    m = gate_ref[:, :d].astype(F32) * pa + gate_ref[:, d:].astype(F32) * pb
    mix = jnp.dot(m.astype(BF16), wout_ref[...], preferred_element_type=F32)
    x2 = _layer_norm(DN_ALPHA * x1_ref[...] + mix, g2_ref[...], b2_ref[...])
    o_ref[...] = _swiglu_ln(x2, xb_ref, acc_ref, w1_ref, w3_ref, w2_ref, g3_ref, b3_ref)


def _dense_out(x1, ya, yb, gates, wpa, wpb, wout, g2, b2, w1, w3, w2, g3, b3):
    n, d = x1.shape
    d_ff = w1.shape[1]
    row = lambda i: (i, 0)
    whole = lambda i: (0, 0)
    return pl.pallas_call(
        _dense_out_kernel,
        grid=(n // ROW_TILE,),
        in_specs=[
            pl.BlockSpec((ROW_TILE, d), row),
            pl.BlockSpec((ROW_TILE, ya.shape[1]), row),
            pl.BlockSpec((ROW_TILE, yb.shape[1]), row),
            pl.BlockSpec((ROW_TILE, gates.shape[1]), row),
            _resident(wpa.shape, whole), _resident(wpb.shape, whole), _resident(wout.shape, whole),
            _resident((1, d), whole), _resident((1, d), whole),
            _resident((d, d_ff), whole), _resident((d, d_ff), whole), _resident((d_ff, d), whole),
            _resident((1, d), whole), _resident((1, d), whole),
        ],
        out_specs=pl.BlockSpec((ROW_TILE, d), row),
        out_shape=jax.ShapeDtypeStruct((n, d), F32),
        scratch_shapes=[pltpu.VMEM((ROW_TILE, d), BF16), pltpu.VMEM((ROW_TILE, d), F32)],
        compiler_params=_cparams(("arbitrary",)),
        name="dense_out",
    )(x1, ya, yb, gates, wpa, wpb, wout, g2, b2, w1, w3, w2, g3, b3)


def _window_start(n, k):
    return np.clip(np.arange(n) - k // 2, 0, n - k)


def _na_group_types(rows):
    last = rows - NA_QROWS
    return ((0, 0), (NA_QROWS * 2, NA_QROWS), (last, rows - NA_KROWS))


def _na_tables(rows):
    w = GRID_W
    shift = np.zeros((2 * NA_KW - 1, w, 2 * w), np.float32)
    q = np.arange(w)[:, None]
    kc = np.arange(w)[None, :]
    for dc in range(2 * NA_KW - 1):
        shift[dc, :, :w] = (kc - q + NA_KW - 1 == dc).astype(np.float32)
    cs = _window_start(w, NA_KW)
    col_ok = (kc >= cs[:, None]) & (kc < cs[:, None] + NA_KW)
    rs_all = _window_start(rows, NA_KH)
    negmask = np.full((3, NA_QROWS, w, NA_KROWS, w), NEG, np.float32)
    for t, (r0, kb) in enumerate(_na_group_types(rows)):
        for i in range(NA_QROWS):
            rs = rs_all[r0 + i]
            for j in range(NA_KROWS):
                if rs <= kb + j < rs + NA_KH:
                    negmask[t, i, :, j, :] = np.where(col_ok, 0.0, NEG)
    return shift, negmask.reshape(3, NA_QROWS * w, NA_KROWS * w)


def _na_bias_kernel(rpb_ref, shift_ref, neg_ref, o_ref, u_ref, *, rows):
    h = pl.program_id(0)
    n_dr = 2 * NA_KH - 1
    n_dc = 2 * NA_KW - 1
    w = GRID_W

    def dr_body(dr, carry):
        base = (h * n_dr + dr) * n_dc
        left = rpb_ref[base] * shift_ref[0]
        for dc in range(1, n_dc):
            left = left + rpb_ref[base + dc] * shift_ref[dc]
        u_ref[0, dr] = left
        u_ref[1, dr] = pltpu.roll(left, w, 1)
        return carry

    lax.fori_loop(0, n_dr, dr_body, 0)
    rs_all = _window_start(rows, NA_KH)
    for t, (r0, kb) in enumerate(_na_group_types(rows)):
        for i in range(NA_QROWS):
            rs = rs_all[r0 + i]
            for jj in range(NA_KROWS // 2):
                val = neg_ref[t, i * w:(i + 1) * w, jj * 2 * w:(jj + 1) * 2 * w]
                for s in range(2):
                    kr = kb + 2 * jj + s
                    if rs <= kr < rs + NA_KH:
                        val = val + u_ref[s, kr - (r0 + i) + NA_KH - 1]
                o_ref[t, 0, i * w:(i + 1) * w, jj * 2 * w:(jj + 1) * 2 * w] = val


def _na_bias(rpb, rows):
    shift, negmask = _na_tables(rows)
    nq, nk = NA_QROWS * GRID_W, NA_KROWS * GRID_W
    return pl.pallas_call(
        functools.partial(_na_bias_kernel, rows=rows),
        grid=(NA_HEADS,),
        in_specs=[
            pl.BlockSpec(memory_space=pltpu.SMEM),
            _resident(shift.shape, lambda h: (0, 0, 0)),
            _resident(negmask.shape, lambda h: (0, 0, 0)),
        ],
        out_specs=pl.BlockSpec((3, 1, nq, nk), lambda h: (0, h, 0, 0)),
        out_shape=jax.ShapeDtypeStruct((3, NA_HEADS, nq, nk), F32),
        scratch_shapes=[pltpu.VMEM((2, 2 * NA_KH - 1, GRID_W, 2 * GRID_W), F32)],
        compiler_params=_cparams(("arbitrary",)),
        name="na_bias",
    )(rpb.reshape(-1), jnp.asarray(shift), jnp.asarray(negmask))


def _na_attn_kernel(q_ref, k_ref, v_ref, bias_ref, o_ref, *, rows):
    g = pl.program_id(1)
    kb = jnp.clip(g * NA_QROWS - NA_KH // 2, 0, rows - NA_KROWS)
    k0 = pl.multiple_of(kb * GRID_W, GRID_W)
    nk = NA_KROWS * GRID_W
    dh = NA_HEAD_DIM
    lane = lax.broadcasted_iota(jnp.int32, (1, 2 * dh), 1)
    for hp in range(NA_HEADS // 2):
        cols = slice(hp * 2 * dh, (hp + 1) * 2 * dh)
        q2 = q_ref[0, :, cols]
        k2 = k_ref[0, pl.ds(k0, nk), cols]
        v2 = v_ref[0, pl.ds(k0, nk), cols]
        pair = []
        for s in range(2):
            own = (lane < dh) if s == 0 else (lane >= dh)
            sc = lax.dot_general(q2, jnp.where(own, k2, jnp.zeros_like(k2)), (((1,), (1,)), ((), ())),
                                 preferred_element_type=F32)
            sc = sc + bias_ref[0, 2 * hp + s]
            e = jnp.exp(sc - jnp.max(sc, axis=-1, keepdims=True)).astype(BF16)
            r = jnp.dot(e, jnp.where(own, v2, jnp.ones_like(v2)), preferred_element_type=F32)
            den = r[:, dh:dh + 1] if s == 0 else r[:, 0:1]
            pair.append(jnp.where(own, r / den, 0.0))
        o_ref[0, :, cols] = (pair[0] + pair[1]).astype(BF16)


def _na_attn(qkv, bias, rows):
    b, seq, three_w = qkv.shape
    width = three_w // 3
    nq, nk = NA_QROWS * GRID_W, NA_KROWS * GRID_W
    groups = rows // NA_QROWS

    def bias_idx(bi, g):
        return (jnp.where(g == 0, 0, jnp.where(g == groups - 1, 2, 1)), 0, 0, 0)

    return pl.pallas_call(
        functools.partial(_na_attn_kernel, rows=rows),
        grid=(b, groups),
        in_specs=[
            pl.BlockSpec((1, nq, width), lambda bi, g: (bi, g, 0)),
            _resident((1, seq, width), lambda bi, g: (bi, 0, 1)),
            _resident((1, seq, width), lambda bi, g: (bi, 0, 2)),
            pl.BlockSpec((1, NA_HEADS, nq, nk), bias_idx),
        ],
        out_specs=pl.BlockSpec((1, nq, width), lambda bi, g: (bi, g, 0)),
        out_shape=jax.ShapeDtypeStruct((b, seq, width), BF16),
        compiler_params=_cparams(("arbitrary", "arbitrary")),
        name="na_attn",
    )(qkv, qkv, qkv, bias)


def _dft_tables():
    k = np.arange(RADIX)
    ang = 2.0 * np.pi * np.outer(k, k) / RADIX
    fr, fi = np.cos(ang), -np.sin(ang)
    ang_t = 2.0 * np.pi * np.outer(k, k) / (RADIX * RADIX)
    twr, twi = np.cos(ang_t), -np.sin(ang_t)
    f32 = lambda a: jnp.asarray(a, F32)
    return f32(fr), f32(fi), f32(twr), f32(twi)


def _cmatmul(c_rows, x_lanes):
    m = c_rows.shape[0] // 2
    r = jnp.dot(c_rows, x_lanes, preferred_element_type=F32)
    return r[:m, :LANES] - r[m:, LANES:], r[:m, LANES:] + r[m:, :LANES]


def _load_pair(ref, rows):
    return jnp.concatenate([ref[0, rows, :], ref[1, rows, :]], axis=1).astype(BF16)


def _store_pair(ref, rows, re, im):
    ref[0, rows, :] = re
    ref[1, rows, :] = im


def _twiddled(fr, fi, twr, twi, scale_re=1.0, scale_im=1.0):
    return jnp.concatenate([(fr * twr - fi * twi) * scale_re, (fr * twi + fi * twr) * scale_im], axis=0).astype(BF16)


def _position_features(seq):
    t = np.linspace(0.0, 1.0, seq)[:, None]
    bands = (HY_EMB_DIM - 1) // 2
    w = (2.0 * math.pi / seq) * np.arange(seq)[:, None]
    f = np.linspace(1e-4, bands - 1, bands)[None, :]
    z = np.concatenate([t, np.cos(f * w), -np.sin(f * w)], axis=-1)
    rev = seq - np.arange(seq)
    rev[0] = 0
    zz = np.zeros((seq, 2 * LANES), np.float32)
    zz[:, :HY_EMB_DIM] = z
    zz[:, LANES:LANES + HY_EMB_DIM] = z[rev]
    return zz


def _filt_mlp_kernel(z_ref, w1_ref, b1_ref, w2_ref, b2_ref, fr_ref, o_ref):
    hp = lax.Precision.HIGHEST
    fr = fr_ref[...]
    hid = LANES // 2
    h = jnp.sin(fr * (jnp.dot(z_ref[...], w1_ref[...], precision=hp, preferred_element_type=F32) + b1_ref[...]))
    h = jnp.sin(fr * (jnp.dot(h, w2_ref[...], precision=hp, preferred_element_type=F32) + b2_ref[...]))
    hi = h.astype(BF16).astype(F32)
    lo = h - hi
    hi_sw, lo_sw = pltpu.roll(hi, hid, 1), pltpu.roll(lo, hid, 1)
    low = lax.broadcasted_iota(jnp.int32, (1, LANES), 1) < hid
    o_ref[0, :, :LANES] = jnp.where(low, hi, lo_sw).astype(BF16)
    o_ref[0, :, LANES:] = jnp.where(low, hi, 0.0).astype(BF16)
    o_ref[1, :, :LANES] = jnp.where(low, hi_sw, lo).astype(BF16)
    o_ref[1, :, LANES:] = jnp.where(low, hi_sw, 0.0).astype(BF16)


def _filt_mlp(zz, fw1, fb1, fw2, fb2, freq):
    seq = zz.shape[0]
    hid = fw2.shape[0]
    assert 2 * hid == LANES
    w1 = jnp.zeros((2 * LANES, LANES), F32)
    w1 = w1.at[:HY_EMB_DIM, :hid].set(fw1).at[LANES:LANES + HY_EMB_DIM, hid:].set(fw1)
    w2 = jnp.zeros((LANES, LANES), F32).at[:hid, :hid].set(fw2).at[hid:, hid:].set(fw2)
    twice = lambda v: jnp.concatenate([v, v]).reshape(1, LANES)
    rows = 2048
    whole = lambda i: (0, 0)
    return pl.pallas_call(
        _filt_mlp_kernel,
        grid=(seq // rows,),
        in_specs=[pl.BlockSpec((rows, 2 * LANES), lambda i: (i, 0)),
                  pl.BlockSpec(w1.shape, whole), pl.BlockSpec((1, LANES), whole),
                  pl.BlockSpec(w2.shape, whole), pl.BlockSpec((1, LANES), whole), pl.BlockSpec((1, LANES), whole)],
        out_specs=pl.BlockSpec((2, rows, 2 * LANES), lambda i: (0, i, 0)),
        out_shape=jax.ShapeDtypeStruct((2, seq, 2 * LANES), BF16),
        compiler_params=_cparams(("arbitrary",)),
        name="filt_mlp",
    )(zz, w1, twice(fb1), w2, twice(fb2), twice(freq))


def _filt_time_kernel(h_ref, w3_ref, b3_ref, delta_ref, k_ref, ss_ref, *, seq):
    i = pl.program_id(0)
    rows = h_ref.shape[0]
    w = w3_ref[0]
    w_hi = w.astype(BF16)
    w_lo = (w - w_hi.astype(F32)).astype(BF16)
    wcat = jnp.concatenate([w_hi, w_hi, w_lo, jnp.zeros_like(w_lo)], axis=0)
    hf = jnp.dot(h_ref[...], wcat, preferred_element_type=F32) + b3_ref[0]
    m = i * rows + lax.broadcasted_iota(jnp.int32, (rows, 1), 0)
    pos = jnp.where(m < seq, m, 2 * seq - m).astype(F32)
    t = pos / (seq - 1)
    k = jnp.where(m == seq, 0.0, hf * jnp.exp(-t * delta_ref[...]))
    for j in range(rows // RADIX):
        k_ref[j * PITCH:j * PITCH + RADIX, :] = k[j * RADIX:(j + 1) * RADIX]
        k_ref[j * PITCH + RADIX:(j + 1) * PITCH, :] = jnp.zeros((PITCH - RADIX, k.shape[1]), F32)
    part = jnp.sum(k * k, axis=0, keepdims=True)

    @pl.when(i == 0)
    def _():
        ss_ref[...] = part

    @pl.when(i != 0)
    def _():
        ss_ref[...] += part


def _filt_time(hdn, w3d, b3d, delta2, seq):
    n2l, packed = hdn.shape
    _, hid, cols = w3d.shape
    assert packed == 4 * hid
    rows = 1024
    half = seq // rows
    prows = rows // RADIX * PITCH
    return pl.pallas_call(
        functools.partial(_filt_time_kernel, seq=seq),
        grid=(n2l // rows,),
        in_specs=[pl.BlockSpec((rows, packed), lambda i: (i, 0)),
                  pl.BlockSpec((1, hid, cols), lambda i: (i // half, 0, 0)),
                  pl.BlockSpec((1, 1, cols), lambda i: (i // half, 0, 0)),
                  pl.BlockSpec((1, cols), lambda i: (0, 0))],
        out_specs=[pl.BlockSpec((prows, cols), lambda i: (i, 0)),
                   pl.BlockSpec((1, cols), lambda i: (0, 0))],
        out_shape=[jax.ShapeDtypeStruct((n2l // RADIX * PITCH, cols), F32), jax.ShapeDtypeStruct((1, cols), F32)],
        compiler_params=_cparams(("arbitrary",)),
        name="filt_time",
    )(hdn, w3d, b3d, delta2)


def _filt_spec_kernel(k_ref, ss_ref, fa_ref, fr_ref, fi_ref, twr_ref, twi_ref, o_ref, a_ref):
    r = RADIX
    scale = lax.rsqrt(ss_ref[...] + 1e-12)

    def a_body(n2, c):
        rhs = k_ref[pl.ds(n2, r, stride=PITCH), :].astype(BF16)
        out = jnp.dot(fa_ref[...], rhs, preferred_element_type=F32)
        _store_pair(a_ref, pl.ds(n2, r, stride=PITCH), out[:r], out[r:])
        return c

    lax.fori_loop(0, r, a_body, 0, unroll=DFT_UNROLL_STRIDED)

    def b_body(k1, c):
        r0 = pl.multiple_of(k1 * PITCH, 8)
        o0 = pl.multiple_of(k1 * r, r)
        blk = _load_pair(a_ref, pl.ds(r0, r))
        sr, si = _cmatmul(_twiddled(fr_ref[...], fi_ref[...], twr_ref[pl.ds(k1, 1), :], twi_ref[pl.ds(k1, 1), :]), blk)
        o_ref[0, pl.ds(o0, r), :] = (sr * scale).astype(BF16)
        o_ref[1, pl.ds(o0, r), :] = (si * scale).astype(BF16)
        return c

    lax.fori_loop(0, r, b_body, 0, unroll=DFT_UNROLL_STRIDED)


def _filt_spec(ktime, ss, fa_real, fr, fi, twr, twi):
    prows, cols = ktime.shape
    n2l = prows // PITCH * RADIX
    whole = lambda j: (0, 0)
    return pl.pallas_call(
        _filt_spec_kernel,
        grid=(cols // LANES,),
        in_specs=[pl.BlockSpec((prows, LANES), lambda j: (0, j)),
                  pl.BlockSpec((1, LANES), lambda j: (0, j)),
                  _resident(fa_real.shape, whole), _resident(fr.shape, whole), _resident(fi.shape, whole),
                  _resident(twr.shape, whole), _resident(twi.shape, whole)],
        out_specs=pl.BlockSpec((2, n2l, LANES), lambda j: (0, 0, j)),
        out_shape=jax.ShapeDtypeStruct((2, n2l, cols), BF16),
        scratch_shapes=[pltpu.VMEM((2, prows, LANES), F32)],
        compiler_params=_cparams(("arbitrary",)),
        name="filt_spec",
    )(ktime, ss, fa_real, fr, fi, twr, twi)


def _short_conv_rows(u_ref, bi, r0, nrows, seq, w_ref, b_ref):
    edge = 16
    cur = u_ref[bi, pl.ds(r0, nrows), :].astype(F32)
    before = u_ref[bi, pl.ds(pl.multiple_of(jnp.maximum(r0 - edge, 0), edge), edge), :].astype(F32)
    after = u_ref[bi, pl.ds(pl.multiple_of(jnp.minimum(r0 + nrows, seq - edge), edge), edge), :].astype(F32)
    prev_edge = jnp.where(r0 > 0, before[edge - 1:edge, :], 0.0)
    next_edge = jnp.where(r0 + nrows < seq, after[0:1, :], 0.0)
    row = lax.broadcasted_iota(jnp.int32, cur.shape, 0)
    prev = jnp.where(row == 0, prev_edge, pltpu.roll(cur, 1, 0))
    nxt = jnp.where(row == nrows - 1, next_edge, pltpu.roll(cur, nrows - 1, 0))
    return b_ref[...] + w_ref[0:1, :] * prev + w_ref[1:2, :] * cur + w_ref[2:3, :] * nxt


def _hy_conv_kernel(z_ref, zw_ref, zb_ref, u_ref, uw_ref, ub_ref, ks_ref, hb_ref,
                    fa_ref, fc_ref, fr_ref, fi_ref, twr_ref, twi_ref, o_ref, a_ref, v_ref, *, conv_in):
    r = RADIX
    half = r // 2
    nb, seq, _ = z_ref.shape
    nchunk = seq // CONV_ROWS
    blocks = CONV_ROWS // r

    def load_body(c, carry):
        r0 = pl.multiple_of(c * CONV_ROWS, CONV_ROWS)
        for bi in range(nb):
            if conv_in:
                v = _short_conv_rows(z_ref, bi, r0, CONV_ROWS, seq, zw_ref, zb_ref)
                o_ref[bi, pl.ds(r0, CONV_ROWS), :] = v.astype(BF16)
            else:
                v = z_ref[bi, pl.ds(r0, CONV_ROWS), :].astype(F32)
            for j in range(blocks):
                p0 = pl.multiple_of((c * blocks + j) * PITCH, 8)
                v_ref[bi, pl.ds(p0, r), :] = v[j * r:(j + 1) * r]
        return carry

    lax.fori_loop(0, nchunk, load_body, 0)

    def a_body(n2, carry):
        re, im = _cmatmul(fa_ref[...], _load_pair(v_ref, pl.ds(n2, half, stride=PITCH)))
        _store_pair(a_ref, pl.ds(n2, r, stride=PITCH), re, im)
        return carry

    lax.fori_loop(0, r, a_body, 0, unroll=DFT_UNROLL_STRIDED)

    def b_forward(k1):
        r0 = pl.multiple_of(k1 * PITCH, 8)
        s0 = pl.multiple_of(k1 * r, r)
        blk = _load_pair(a_ref, pl.ds(r0, r))
        sr, si = _cmatmul(_twiddled(fr_ref[...], fi_ref[...], twr_ref[pl.ds(k1, 1), :], twi_ref[pl.ds(k1, 1), :]), blk)
        kr = ks_ref[0, pl.ds(s0, r), :].astype(F32)
        ki = ks_ref[1, pl.ds(s0, r), :].astype(F32)
        return jnp.concatenate([sr * kr - si * ki, sr * ki + si * kr], axis=1).astype(BF16)

    def b_inverse(k1, p):
        tr, ti = _cmatmul(fc_ref[...], p)
        _store_pair(a_ref, pl.ds(pl.multiple_of(k1 * PITCH, 8), r), tr, ti)

    def b_body(k1, p):
        p_next = b_forward(k1 + 1)
        b_inverse(k1, p)
        return p_next

    b_inverse(r - 1, lax.fori_loop(0, r - 1, b_body, b_forward(0), unroll=DFT_UNROLL))

    inv_n = 1.0 / (r * r)

    def c_body(n2, carry):
        g = _twiddled(fr_ref[0:half, :], fi_ref[0:half, :], twr_ref[pl.ds(n2, 1), :], twi_ref[pl.ds(n2, 1), :],
                      inv_n, -inv_n)
        yr, yi = _cmatmul(g, _load_pair(a_ref, pl.ds(n2, r, stride=PITCH)))
        _store_pair(v_ref, pl.ds(n2, half, stride=PITCH), yr, yi)
        return carry

    lax.fori_loop(0, r, c_body, 0, unroll=DFT_UNROLL_STRIDED)

    def out_body(c, carry):
        r0 = pl.multiple_of(c * CONV_ROWS, CONV_ROWS)
        for bi in range(nb):
            gate = _short_conv_rows(u_ref, bi, r0, CONV_ROWS, seq, uw_ref, ub_ref)
            conv = jnp.concatenate(
                [v_ref[bi, pl.ds(pl.multiple_of((c * blocks + j) * PITCH, 8), r), :] for j in range(blocks)], axis=0)
            zp = (o_ref if conv_in else z_ref)[bi, pl.ds(r0, CONV_ROWS), :].astype(F32)
            o_ref[bi, pl.ds(r0, CONV_ROWS), :] = (gate * (conv + hb_ref[...] * zp)).astype(BF16)
        return carry

    lax.fori_loop(0, nchunk, out_body, 0)


def _hy_conv(z, z_part, u, u_part, order, short_w, short_b, kspec, hy_bias, tables, conv_in):
    nb, seq, _ = u.shape
    width = hy_bias.shape[1]
    tiles = width // LANES
    fa, fc, fr, fi, twr, twi = tables
    whole = lambda j: (0, 0)
    zcol = lambda j: (0, 0, z_part * tiles + j)
    ucol = lambda j: (0, 0, u_part * tiles + j)
    return pl.pallas_call(
        functools.partial(_hy_conv_kernel, conv_in=conv_in),
        grid=(tiles,),
        in_specs=[
            _resident((nb, seq, LANES), zcol),
            pl.BlockSpec((3, LANES), lambda j: (0, z_part * tiles + j)),
            pl.BlockSpec((1, LANES), lambda j: (0, z_part * tiles + j)),
            _resident((nb, seq, LANES), ucol),
            pl.BlockSpec((3, LANES), lambda j: (0, u_part * tiles + j)),
            pl.BlockSpec((1, LANES), lambda j: (0, u_part * tiles + j)),
            _resident((2, 2 * seq, LANES), lambda j: (0, 0, order * tiles + j)),
            pl.BlockSpec((1, LANES), lambda j: (0, j)),
            _resident(fa.shape, whole), _resident(fc.shape, whole), _resident(fr.shape, whole),
            _resident(fi.shape, whole), _resident(twr.shape, whole), _resident(twi.shape, whole),
        ],
        out_specs=pl.BlockSpec((nb, seq, LANES), lambda j: (0, 0, j)),
        out_shape=jax.ShapeDtypeStruct((nb, seq, width), BF16),
        scratch_shapes=[pltpu.VMEM((2, 2 * seq // RADIX * PITCH, LANES), F32),
                        pltpu.VMEM((nb, seq // RADIX * PITCH, LANES), F32)],
        compiler_params=_cparams(("arbitrary",)),
        name=f"hy_conv{order}",
    )(z, short_w, short_b, u, short_w, short_b, kspec, hy_bias[order:order + 1], fa, fc, fr, fi, twr, twi)


def _hyena(hyu, short_w, short_b, fw1, fb1, fw2, fb2, fw3, fb3, freq, hy_bias):
    nb, seq, _ = hyu.shape
    assert nb == 2 and 2 * seq == RADIX * RADIX
    width = hy_bias.shape[1]
    hid = fw2.shape[0]
    fr, fi, twr, twi = _dft_tables()
    half = RADIX // 2
    fa_real = jnp.concatenate([fr, fi], axis=0).astype(BF16)
    fa_half = jnp.concatenate([fr[:, :half], fi[:, :half]], axis=0).astype(BF16)
    fc = jnp.concatenate([fr, -fi], axis=0).astype(BF16)

    zz = jnp.asarray(_position_features(seq))
    hdn = _filt_mlp(zz, fw1, fb1, fw2, fb2, freq).reshape(2 * seq, 4 * hid)
    w3d = fw3.reshape(hid, 2, HY_ORDER * width).transpose(1, 0, 2)
    b3d = fb3.reshape(2, 1, HY_ORDER * width)
    max_decay = math.log(HY_DECAY_TARGET) / HY_FAST_DECAY_PCT
    min_decay = math.log(HY_DECAY_TARGET) / HY_SLOW_DECAY_PCT
    delta = np.abs(np.linspace(min_decay, max_decay, width)).astype(np.float32)
    delta2 = jnp.asarray(np.tile(delta, HY_ORDER)[None, :])
    ktime, ss = _filt_time(hdn, w3d, b3d, delta2, seq)
    kspec = _filt_spec(ktime, ss, fa_real, fr, fi, twr, twi)

    tables = (fa_half, fc, fr, fi, twr, twi)
    sb = short_b.reshape(1, -1)
    z1 = _hy_conv(hyu, 0, hyu, 1, 0, short_w, sb, kspec, hy_bias, tables, conv_in=True)
    return _hy_conv(z1, 0, hyu, 2, 1, short_w, sb, kspec, hy_bias, tables, conv_in=False)


def kernel(x, ln1_g, ln1_b, ffn1_w1, ffn1_w3, ffn1_w2, w_in, b_gate, na_rpb, hy_short_w, hy_short_b,
           hy_filt_w1, hy_filt_b1, hy_filt_w2, hy_filt_b2, hy_filt_w3, hy_filt_b3, hy_filt_freq, hy_bias,
           w_pa, w_pb, w_out, ln2_g, ln2_b, ffn2_w1, ffn2_w3, ffn2_w2, ln3_g, ln3_b):
    assert ln1_g.shape[0] == DEPTH
    b, seq, d = x.shape
    rows = seq // GRID_W
    na_width = NA_HEADS * NA_HEAD_DIM
    n_qkv = 3 * na_width
    n_hy = hy_short_w.shape[2]
    bf = lambda a: a[0].astype(BF16)
    vec = lambda a: a[0].reshape(1, -1)

    x1, qkv, hyu, gates = _dense_in(
        x.reshape(b * seq, d), bf(ffn1_w1), bf(ffn1_w3), bf(ffn1_w2), vec(ln1_g), vec(ln1_b),
        bf(w_in), vec(b_gate), n_qkv, n_hy)

    bias = _na_bias(na_rpb[0], rows)
    ya = _na_attn(qkv.reshape(b, seq, n_qkv), bias, rows)

    yb = _hyena(hyu.reshape(b, seq, n_hy), hy_short_w[0], hy_short_b[0], hy_filt_w1[0], hy_filt_b1[0],
                hy_filt_w2[0], hy_filt_b2[0], hy_filt_w3[0], hy_filt_b3[0], hy_filt_freq[0], hy_bias[0])

    out = _dense_out(
        x1, ya.reshape(b * seq, na_width), yb.reshape(b * seq, -1), gates,
        bf(w_pa), bf(w_pb), bf(w_out), vec(ln2_g), vec(ln2_b),
        bf(ffn2_w1), bf(ffn2_w3), bf(ffn2_w2), vec(ln3_g), vec(ln3_b))
    return out.reshape(b, seq, d)
```

```python
import functools
import math

import numpy as np
import jax
import jax.numpy as jnp
from jax import lax
from jax.experimental import pallas as pl
from jax.experimental.pallas import tpu as pltpu

F32 = jnp.float32
BF16 = jnp.bfloat16

GRID_W = 64
NA_HEADS = 8
NA_HEAD_DIM = 64
NA_KH = 8
NA_KW = 16
HY_ORDER = 2
HY_EMB_DIM = 33
HY_FAST_DECAY_PCT = 0.3
HY_SLOW_DECAY_PCT = 1.5
HY_DECAY_TARGET = 1e-2
DEPTH = 1
DN_ALPHA = (2 * DEPTH) ** 0.25
LN_EPS = 1e-5

LANES = 128
VMEM_LIMIT = 60 * 1024 * 1024

ROW_TILE = 512
FF_CHUNK = 256
PROJ_CHUNK = 512
NA_QROWS = 4
NA_KROWS = NA_QROWS + NA_KH
RADIX = 128
PITCH = RADIX + 8
DFT_UNROLL = 8
DFT_UNROLL_STRIDED = 16
CONV_ROWS = 512
NEG = -1e30


def _cparams(sem):
    return pltpu.CompilerParams(dimension_semantics=sem, vmem_limit_bytes=VMEM_LIMIT)


def _resident(shape, index_map):
    return pl.BlockSpec(shape, index_map, pipeline_mode=pl.Buffered(1))


def _layer_norm(r, g, b):
    mu = jnp.mean(r, axis=-1, keepdims=True)
    d = r - mu
    var = jnp.mean(d * d, axis=-1, keepdims=True)
    return d * lax.rsqrt(var + LN_EPS) * g + b


def _swiglu_ln(x, xb_ref, acc_ref, w1_ref, w3_ref, w2_ref, g_ref, b_ref):
    d_ff = w1_ref.shape[1]
    xb_ref[...] = x.astype(BF16)
    for j in range(d_ff // FF_CHUNK):
        sl = slice(j * FF_CHUNK, (j + 1) * FF_CHUNK)
        a = jnp.dot(xb_ref[...], w1_ref[:, sl], preferred_element_type=F32)
        b = jnp.dot(xb_ref[...], w3_ref[:, sl], preferred_element_type=F32)
        h = (a * jax.nn.sigmoid(a) * b).astype(BF16)
        part = jnp.dot(h, w2_ref[sl, :], preferred_element_type=F32)
        if j == 0:
            acc_ref[...] = part
        else:
            acc_ref[...] += part
    return _layer_norm(DN_ALPHA * x + 0.5 * acc_ref[...], g_ref[...], b_ref[...])


def _dense_in_kernel(x_ref, w1_ref, w3_ref, w2_ref, g_ref, b_ref, win_ref, bg_ref,
                     x1_ref, qkv_ref, hyu_ref, gate_ref, xb_ref, acc_ref):
    x1 = _swiglu_ln(x_ref[...], xb_ref, acc_ref, w1_ref, w3_ref, w2_ref, g_ref, b_ref)
    x1_ref[...] = x1
    xb_ref[...] = x1.astype(BF16)
    n_qkv = qkv_ref.shape[1]
    n_hy = hyu_ref.shape[1]
    n_head = n_qkv // 3
    for j in range(win_ref.shape[1] // PROJ_CHUNK):
        c0 = j * PROJ_CHUNK
        p = jnp.dot(xb_ref[...], win_ref[:, c0:c0 + PROJ_CHUNK], preferred_element_type=F32)
        if c0 < n_qkv:
            if c0 < n_head:
                p = p * (NA_HEAD_DIM ** -0.5)
            qkv_ref[:, c0:c0 + PROJ_CHUNK] = p.astype(BF16)
        elif c0 < n_qkv + n_hy:
            hyu_ref[:, c0 - n_qkv:c0 - n_qkv + PROJ_CHUNK] = p.astype(BF16)
        else:
            g0 = c0 - n_qkv - n_hy
            gate_ref[:, g0:g0 + PROJ_CHUNK] = jax.nn.sigmoid(p + bg_ref[:, g0:g0 + PROJ_CHUNK]).astype(BF16)


def _dense_in(x2d, w1, w3, w2, g, b, w_in, b_gate, n_qkv, n_hy):
    n, d = x2d.shape
    d_ff = w1.shape[1]
    n_gate = w_in.shape[1] - n_qkv - n_hy
    assert n % ROW_TILE == 0 and d_ff % FF_CHUNK == 0
    assert n_qkv % (3 * PROJ_CHUNK) == 0 and n_hy % PROJ_CHUNK == 0 and n_gate % PROJ_CHUNK == 0
    row = lambda i: (i, 0)
    whole = lambda i: (0, 0)
    return pl.pallas_call(
        _dense_in_kernel,
        grid=(n // ROW_TILE,),
        in_specs=[
            pl.BlockSpec((ROW_TILE, d), row),
            _resident((d, d_ff), whole), _resident((d, d_ff), whole), _resident((d_ff, d), whole),
            _resident((1, d), whole), _resident((1, d), whole),
            _resident(w_in.shape, whole), _resident((1, n_gate), whole),
        ],
        out_specs=[
            pl.BlockSpec((ROW_TILE, d), row),
            pl.BlockSpec((ROW_TILE, n_qkv), row),
            pl.BlockSpec((ROW_TILE, n_hy), row),
            pl.BlockSpec((ROW_TILE, n_gate), row),
        ],
        out_shape=[
            jax.ShapeDtypeStruct((n, d), F32),
            jax.ShapeDtypeStruct((n, n_qkv), BF16),
            jax.ShapeDtypeStruct((n, n_hy), BF16),
            jax.ShapeDtypeStruct((n, n_gate), BF16),
        ],
        scratch_shapes=[pltpu.VMEM((ROW_TILE, d), BF16), pltpu.VMEM((ROW_TILE, d), F32)],
        compiler_params=_cparams(("arbitrary",)),
        name="dense_in",
    )(x2d, w1, w3, w2, g, b, w_in, b_gate)


def _dense_out_kernel(x1_ref, ya_ref, yb_ref, gate_ref, wpa_ref, wpb_ref, wout_ref, g2_ref, b2_ref,
                      w1_ref, w3_ref, w2_ref, g3_ref, b3_ref, o_ref, xb_ref, acc_ref):
    d = x1_ref.shape[1]
    pa = jnp.dot(ya_ref[...], wpa_ref[...], preferred_element_type=F32)
    pb = jnp.dot(yb_ref[...], wpb_ref[...], preferred_element_type=F32)
    m = gate_ref[:, :d].astype(F32) * pa + gate_ref[:, d:].astype(F32) * pb
    mix = jnp.dot(m.astype(BF16), wout_ref[...], preferred_element_type=F32)
    x2 = _layer_norm(DN_ALPHA * x1_ref[...] + mix, g2_ref[...], b2_ref[...])
    o_ref[...] = _swiglu_ln(x2, xb_ref, acc_ref, w1_ref, w3_ref, w2_ref, g3_ref, b3_ref)


def _dense_out(x1, ya, yb, gates, wpa, wpb, wout, g2, b2, w1, w3, w2, g3, b3):
    n, d = x1.shape
    d_ff = w1.shape[1]
    row = lambda i: (i, 0)
    whole = lambda i: (0, 0)
    return pl.pallas_call(
        _dense_out_kernel,
        grid=(n // ROW_TILE,),
        in_specs=[
            pl.BlockSpec((ROW_TILE, d), row),
            pl.BlockSpec((ROW_TILE, ya.shape[1]), row),
            pl.BlockSpec((ROW_TILE, yb.shape[1]), row),
            pl.BlockSpec((ROW_TILE, gates.shape[1]), row),
            _resident(wpa.shape, whole), _resident(wpb.shape, whole), _resident(wout.shape, whole),
            _resident((1, d), whole), _resident((1, d), whole),
            _resident((d, d_ff), whole), _resident((d, d_ff), whole), _resident((d_ff, d), whole),
            _resident((1, d), whole), _resident((1, d), whole),
        ],
        out_specs=pl.BlockSpec((ROW_TILE, d), row),
        out_shape=jax.ShapeDtypeStruct((n, d), F32),
        scratch_shapes=[pltpu.VMEM((ROW_TILE, d), BF16), pltpu.VMEM((ROW_TILE, d), F32)],
        compiler_params=_cparams(("arbitrary",)),
        name="dense_out",
    )(x1, ya, yb, gates, wpa, wpb, wout, g2, b2, w1, w3, w2, g3, b3)


def _window_start(n, k):
    return np.clip(np.arange(n) - k // 2, 0, n - k)


def _na_group_types(rows):
    last = rows - NA_QROWS
    return ((0, 0), (NA_QROWS * 2, NA_QROWS), (last, rows - NA_KROWS))


def _na_tables(rows):
    w = GRID_W
    shift = np.zeros((2 * NA_KW - 1, w, 2 * w), np.float32)
    q = np.arange(w)[:, None]
    kc = np.arange(w)[None, :]
    for dc in range(2 * NA_KW - 1):
        shift[dc, :, :w] = (kc - q + NA_KW - 1 == dc).astype(np.float32)
    cs = _window_start(w, NA_KW)
    col_ok = (kc >= cs[:, None]) & (kc < cs[:, None] + NA_KW)
    rs_all = _window_start(rows, NA_KH)
    negmask = np.full((3, NA_QROWS, w, NA_KROWS, w), NEG, np.float32)
    for t, (r0, kb) in enumerate(_na_group_types(rows)):
        for i in range(NA_QROWS):
            rs = rs_all[r0 + i]
            for j in range(NA_KROWS):
                if rs <= kb + j < rs + NA_KH:
                    negmask[t, i, :, j, :] = np.where(col_ok, 0.0, NEG)
    return shift, negmask.reshape(3, NA_QROWS * w, NA_KROWS * w)


def _na_bias_kernel(rpb_ref, shift_ref, neg_ref, o_ref, u_ref, *, rows):
    h = pl.program_id(0)
    n_dr = 2 * NA_KH - 1
    n_dc = 2 * NA_KW - 1
    w = GRID_W

    def dr_body(dr, carry):
        base = (h * n_dr + dr) * n_dc
        left = rpb_ref[base] * shift_ref[0]
        for dc in range(1, n_dc):
            left = left + rpb_ref[base + dc] * shift_ref[dc]
        u_ref[0, dr] = left
        u_ref[1, dr] = pltpu.roll(left, w, 1)
        return carry

    lax.fori_loop(0, n_dr, dr_body, 0)
    rs_all = _window_start(rows, NA_KH)
    for t, (r0, kb) in enumerate(_na_group_types(rows)):
        for i in range(NA_QROWS):
            rs = rs_all[r0 + i]
            for jj in range(NA_KROWS // 2):
                val = neg_ref[t, i * w:(i + 1) * w, jj * 2 * w:(jj + 1) * 2 * w]
                for s in range(2):
                    kr = kb + 2 * jj + s
                    if rs <= kr < rs + NA_KH:
                        val = val + u_ref[s, kr - (r0 + i) + NA_KH - 1]
                o_ref[t, 0, i * w:(i + 1) * w, jj * 2 * w:(jj + 1) * 2 * w] = val


def _na_bias(rpb, rows):
    shift, negmask = _na_tables(rows)
    nq, nk = NA_QROWS * GRID_W, NA_KROWS * GRID_W
    return pl.pallas_call(
        functools.partial(_na_bias_kernel, rows=rows),
        grid=(NA_HEADS,),
        in_specs=[
            pl.BlockSpec(memory_space=pltpu.SMEM),
            _resident(shift.shape, lambda h: (0, 0, 0)),
            _resident(negmask.shape, lambda h: (0, 0, 0)),
        ],
        out_specs=pl.BlockSpec((3, 1, nq, nk), lambda h: (0, h, 0, 0)),
        out_shape=jax.ShapeDtypeStruct((3, NA_HEADS, nq, nk), F32),
        scratch_shapes=[pltpu.VMEM((2, 2 * NA_KH - 1, GRID_W, 2 * GRID_W), F32)],
        compiler_params=_cparams(("arbitrary",)),
        name="na_bias",
    )(rpb.reshape(-1), jnp.asarray(shift), jnp.asarray(negmask))


def _na_attn_kernel(q_ref, k_ref, v_ref, bias_ref, o_ref, *, rows):
    g = pl.program_id(1)
    kb = jnp.clip(g * NA_QROWS - NA_KH // 2, 0, rows - NA_KROWS)
    k0 = pl.multiple_of(kb * GRID_W, GRID_W)
    nk = NA_KROWS * GRID_W
    dh = NA_HEAD_DIM
    lane = lax.broadcasted_iota(jnp.int32, (1, 2 * dh), 1)
    for hp in range(NA_HEADS // 2):
        cols = slice(hp * 2 * dh, (hp + 1) * 2 * dh)
        q2 = q_ref[0, :, cols]
        k2 = k_ref[0, pl.ds(k0, nk), cols]
        v2 = v_ref[0, pl.ds(k0, nk), cols]
        pair = []
        for s in range(2):
            own = (lane < dh) if s == 0 else (lane >= dh)
            sc = lax.dot_general(q2, jnp.where(own, k2, jnp.zeros_like(k2)), (((1,), (1,)), ((), ())),
                                 preferred_element_type=F32)
            sc = sc + bias_ref[0, 2 * hp + s]
            e = jnp.exp(sc - jnp.max(sc, axis=-1, keepdims=True)).astype(BF16)
            r = jnp.dot(e, jnp.where(own, v2, jnp.ones_like(v2)), preferred_element_type=F32)
            den = r[:, dh:dh + 1] if s == 0 else r[:, 0:1]
            pair.append(jnp.where(own, r / den, 0.0))
        o_ref[0, :, cols] = (pair[0] + pair[1]).astype(BF16)


def _na_attn(qkv, bias, rows):
    b, seq, three_w = qkv.shape
    width = three_w // 3
    nq, nk = NA_QROWS * GRID_W, NA_KROWS * GRID_W
    groups = rows // NA_QROWS

    def bias_idx(bi, g):
        return (jnp.where(g == 0, 0, jnp.where(g == groups - 1, 2, 1)), 0, 0, 0)

    return pl.pallas_call(
        functools.partial(_na_attn_kernel, rows=rows),
        grid=(b, groups),
        in_specs=[
            pl.BlockSpec((1, nq, width), lambda bi, g: (bi, g, 0)),
            _resident((1, seq, width), lambda bi, g: (bi, 0, 1)),
            _resident((1, seq, width), lambda bi, g: (bi, 0, 2)),
            pl.BlockSpec((1, NA_HEADS, nq, nk), bias_idx),
        ],
        out_specs=pl.BlockSpec((1, nq, width), lambda bi, g: (bi, g, 0)),
        out_shape=jax.ShapeDtypeStruct((b, seq, width), BF16),
        compiler_params=_cparams(("arbitrary", "arbitrary")),
        name="na_attn",
    )(qkv, qkv, qkv, bias)


def _dft_tables():
    k = np.arange(RADIX)
    ang = 2.0 * np.pi * np.outer(k, k) / RADIX
    fr, fi = np.cos(ang), -np.sin(ang)
    ang_t = 2.0 * np.pi * np.outer(k, k) / (RADIX * RADIX)
    twr, twi = np.cos(ang_t), -np.sin(ang_t)
    f32 = lambda a: jnp.asarray(a, F32)
    return f32(fr), f32(fi), f32(twr), f32(twi)


def _cmatmul(c_rows, x_lanes):
    m = c_rows.shape[0] // 2
    r = jnp.dot(c_rows, x_lanes, preferred_element_type=F32)
    return r[:m, :LANES] - r[m:, LANES:], r[:m, LANES:] + r[m:, :LANES]


def _load_pair(ref, rows):
    return jnp.concatenate([ref[0, rows, :], ref[1, rows, :]], axis=1).astype(BF16)


def _store_pair(ref, rows, re, im):
    ref[0, rows, :] = re
    ref[1, rows, :] = im


def _twiddled(fr, fi, twr, twi, scale_re=1.0, scale_im=1.0):
    return jnp.concatenate([(fr * twr - fi * twi) * scale_re, (fr * twi + fi * twr) * scale_im], axis=0).astype(BF16)


def _position_features(seq):
    t = np.linspace(0.0, 1.0, seq)[:, None]
    bands = (HY_EMB_DIM - 1) // 2
    w = (2.0 * math.pi / seq) * np.arange(seq)[:, None]
    f = np.linspace(1e-4, bands - 1, bands)[None, :]
    z = np.concatenate([t, np.cos(f * w), -np.sin(f * w)], axis=-1)
    rev = seq - np.arange(seq)
    rev[0] = 0
    zz = np.zeros((seq, 2 * LANES), np.float32)
    zz[:, :HY_EMB_DIM] = z
    zz[:, LANES:LANES + HY_EMB_DIM] = z[rev]
    return zz


def _filt_mlp_kernel(z_ref, w1_ref, b1_ref, w2_ref, b2_ref, fr_ref, o_ref):
    hp = lax.Precision.HIGHEST
    fr = fr_ref[...]
    hid = LANES // 2
    h = jnp.sin(fr * (jnp.dot(z_ref[...], w1_ref[...], precision=hp, preferred_element_type=F32) + b1_ref[...]))
    h = jnp.sin(fr * (jnp.dot(h, w2_ref[...], precision=hp, preferred_element_type=F32) + b2_ref[...]))
    hi = h.astype(BF16).astype(F32)
    lo = h - hi
    hi_sw, lo_sw = pltpu.roll(hi, hid, 1), pltpu.roll(lo, hid, 1)
    low = lax.broadcasted_iota(jnp.int32, (1, LANES), 1) < hid
    o_ref[0, :, :LANES] = jnp.where(low, hi, lo_sw).astype(BF16)
    o_ref[0, :, LANES:] = jnp.where(low, hi, 0.0).astype(BF16)
    o_ref[1, :, :LANES] = jnp.where(low, hi_sw, lo).astype(BF16)
    o_ref[1, :, LANES:] = jnp.where(low, hi_sw, 0.0).astype(BF16)


def _filt_mlp(zz, fw1, fb1, fw2, fb2, freq):
    seq = zz.shape[0]
    hid = fw2.shape[0]
    assert 2 * hid == LANES
    w1 = jnp.zeros((2 * LANES, LANES), F32)
    w1 = w1.at[:HY_EMB_DIM, :hid].set(fw1).at[LANES:LANES + HY_EMB_DIM, hid:].set(fw1)
    w2 = jnp.zeros((LANES, LANES), F32).at[:hid, :hid].set(fw2).at[hid:, hid:].set(fw2)
    twice = lambda v: jnp.concatenate([v, v]).reshape(1, LANES)
    rows = 2048
    whole = lambda i: (0, 0)
    return pl.pallas_call(
        _filt_mlp_kernel,
        grid=(seq // rows,),
        in_specs=[pl.BlockSpec((rows, 2 * LANES), lambda i: (i, 0)),
                  pl.BlockSpec(w1.shape, whole), pl.BlockSpec((1, LANES), whole),
                  pl.BlockSpec(w2.shape, whole), pl.BlockSpec((1, LANES), whole), pl.BlockSpec((1, LANES), whole)],
        out_specs=pl.BlockSpec((2, rows, 2 * LANES), lambda i: (0, i, 0)),
        out_shape=jax.ShapeDtypeStruct((2, seq, 2 * LANES), BF16),
        compiler_params=_cparams(("arbitrary",)),
        name="filt_mlp",
    )(zz, w1, twice(fb1), w2, twice(fb2), twice(freq))


def _filt_time_kernel(h_ref, w3_ref, b3_ref, delta_ref, k_ref, ss_ref, *, seq):
    i = pl.program_id(0)
    rows = h_ref.shape[0]
    w = w3_ref[0]
    w_hi = w.astype(BF16)
    w_lo = (w - w_hi.astype(F32)).astype(BF16)
    wcat = jnp.concatenate([w_hi, w_hi, w_lo, jnp.zeros_like(w_lo)], axis=0)
    hf = jnp.dot(h_ref[...], wcat, preferred_element_type=F32) + b3_ref[0]
    m = i * rows + lax.broadcasted_iota(jnp.int32, (rows, 1), 0)
    pos = jnp.where(m < seq, m, 2 * seq - m).astype(F32)
    t = pos / (seq - 1)
    k = jnp.where(m == seq, 0.0, hf * jnp.exp(-t * delta_ref[...]))
    for j in range(rows // RADIX):
        k_ref[j * PITCH:j * PITCH + RADIX, :] = k[j * RADIX:(j + 1) * RADIX]
        k_ref[j * PITCH + RADIX:(j + 1) * PITCH, :] = jnp.zeros((PITCH - RADIX, k.shape[1]), F32)
    part = jnp.sum(k * k, axis=0, keepdims=True)

    @pl.when(i == 0)
    def _():
        ss_ref[...] = part

    @pl.when(i != 0)
    def _():
        ss_ref[...] += part


def _filt_time(hdn, w3d, b3d, delta2, seq):
    n2l, packed = hdn.shape
    _, hid, cols = w3d.shape
    assert packed == 4 * hid
    rows = 1024
    half = seq // rows
    prows = rows // RADIX * PITCH
    return pl.pallas_call(
        functools.partial(_filt_time_kernel, seq=seq),
        grid=(n2l // rows,),
        in_specs=[pl.BlockSpec((rows, packed), lambda i: (i, 0)),
                  pl.BlockSpec((1, hid, cols), lambda i: (i // half, 0, 0)),
                  pl.BlockSpec((1, 1, cols), lambda i: (i // half, 0, 0)),
                  pl.BlockSpec((1, cols), lambda i: (0, 0))],
        out_specs=[pl.BlockSpec((prows, cols), lambda i: (i, 0)),
                   pl.BlockSpec((1, cols), lambda i: (0, 0))],
        out_shape=[jax.ShapeDtypeStruct((n2l // RADIX * PITCH, cols), F32), jax.ShapeDtypeStruct((1, cols), F32)],
        compiler_params=_cparams(("arbitrary",)),
        name="filt_time",
    )(hdn, w3d, b3d, delta2)


def _filt_spec_kernel(k_ref, ss_ref, fa_ref, fr_ref, fi_ref, twr_ref, twi_ref, o_ref, a_ref):
    r = RADIX
    scale = lax.rsqrt(ss_ref[...] + 1e-12)

    def a_body(i, c):
        n2 = 2 * i
        rhs = jnp.concatenate([k_ref[pl.ds(n2, r, stride=PITCH), :], k_ref[pl.ds(n2 + 1, r, stride=PITCH), :]],
                              axis=1).astype(BF16)
        out = jnp.dot(fa_ref[...], rhs, preferred_element_type=F32)
        _store_pair(a_ref, pl.ds(n2, r, stride=PITCH), out[:r, :LANES], out[r:, :LANES])
        _store_pair(a_ref, pl.ds(n2 + 1, r, stride=PITCH), out[:r, LANES:], out[r:, LANES:])
        return c

    lax.fori_loop(0, r // 2, a_body, 0, unroll=DFT_UNROLL_STRIDED // 2)

    def b_body(k1, c):
        r0 = pl.multiple_of(k1 * PITCH, 8)
        o0 = pl.multiple_of(k1 * r, r)
        blk = _load_pair(a_ref, pl.ds(r0, r))
        sr, si = _cmatmul(_twiddled(fr_ref[...], fi_ref[...], twr_ref[pl.ds(k1, 1), :], twi_ref[pl.ds(k1, 1), :]), blk)
        o_ref[0, pl.ds(o0, r), :] = (sr * scale).astype(BF16)
        o_ref[1, pl.ds(o0, r), :] = (si * scale).astype(BF16)
        return c

    lax.fori_loop(0, r, b_body, 0, unroll=DFT_UNROLL_STRIDED)


def _filt_spec(ktime, ss, fa_real, fr, fi, twr, twi):
    prows, cols = ktime.shape
    n2l = prows // PITCH * RADIX
    whole = lambda j: (0, 0)
    return pl.pallas_call(
        _filt_spec_kernel,
        grid=(cols // LANES,),
        in_specs=[pl.BlockSpec((prows, LANES), lambda j: (0, j)),
                  pl.BlockSpec((1, LANES), lambda j: (0, j)),
                  _resident(fa_real.shape, whole), _resident(fr.shape, whole), _resident(fi.shape, whole),
                  _resident(twr.shape, whole), _resident(twi.shape, whole)],
        out_specs=pl.BlockSpec((2, n2l, LANES), lambda j: (0, 0, j)),
        out_shape=jax.ShapeDtypeStruct((2, n2l, cols), BF16),
        scratch_shapes=[pltpu.VMEM((2, prows, LANES), F32)],
        compiler_params=_cparams(("arbitrary",)),
        name="filt_spec",
    )(ktime, ss, fa_real, fr, fi, twr, twi)


def _short_conv_rows(u_ref, bi, r0, nrows, seq, w_ref, b_ref):
    edge = 16
    cur = u_ref[bi, pl.ds(r0, nrows), :].astype(F32)
    before = u_ref[bi, pl.ds(pl.multiple_of(jnp.maximum(r0 - edge, 0), edge), edge), :].astype(F32)
    after = u_ref[bi, pl.ds(pl.multiple_of(jnp.minimum(r0 + nrows, seq - edge), edge), edge), :].astype(F32)
    prev_edge = jnp.where(r0 > 0, before[edge - 1:edge, :], 0.0)
    next_edge = jnp.where(r0 + nrows < seq, after[0:1, :], 0.0)
    w0, w1, w2 = w_ref[0:1, :], w_ref[1:2, :], w_ref[2:3, :]
    out = b_ref[...] + w0 * pltpu.roll(cur, 1, 0) + w1 * cur + w2 * pltpu.roll(cur, nrows - 1, 0)
    row = lax.broadcasted_iota(jnp.int32, (8, cur.shape[1]), 0)
    first = out[0:8] + jnp.where(row == 0, w0 * (prev_edge - cur[nrows - 1:nrows]), 0.0)
    last = out[nrows - 8:] + jnp.where(row == 7, w2 * (next_edge - cur[0:1]), 0.0)
    return jnp.concatenate([first, out[8:nrows - 8], last], axis=0)


def _hy_conv_kernel(z_ref, zw_ref, zb_ref, u_ref, uw_ref, ub_ref, ks_ref, hb_ref,
                    fa_ref, fc_ref, fr_ref, fi_ref, twr_ref, twi_ref, o_ref, a_ref, v_ref, *, conv_in):
    r = RADIX
    half = r // 2
    nb, seq, _ = z_ref.shape
    nchunk = seq // CONV_ROWS
    blocks = CONV_ROWS // r

    def load_body(c, carry):
        r0 = pl.multiple_of(c * CONV_ROWS, CONV_ROWS)
        for bi in range(nb):
            if conv_in:
                v = _short_conv_rows(z_ref, bi, r0, CONV_ROWS, seq, zw_ref, zb_ref)
                o_ref[bi, pl.ds(r0, CONV_ROWS), :] = v.astype(BF16)
            else:
                v = z_ref[bi, pl.ds(r0, CONV_ROWS), :].astype(F32)
            for j in range(blocks):
                p0 = pl.multiple_of((c * blocks + j) * PITCH, 8)
                v_ref[bi, pl.ds(p0, r), :] = v[j * r:(j + 1) * r]
        return carry

    lax.fori_loop(0, nchunk, load_body, 0)

    def a_body(n2, carry):
        re, im = _cmatmul(fa_ref[...], _load_pair(v_ref, pl.ds(n2, half, stride=PITCH)))
        _store_pair(a_ref, pl.ds(n2, r, stride=PITCH), re, im)
        return carry

    lax.fori_loop(0, r, a_body, 0, unroll=DFT_UNROLL_STRIDED)

    def b_forward(k1):
        r0 = pl.multiple_of(k1 * PITCH, 8)
        s0 = pl.multiple_of(k1 * r, r)
        blk = _load_pair(a_ref, pl.ds(r0, r))
        sr, si = _cmatmul(_twiddled(fr_ref[...], fi_ref[...], twr_ref[pl.ds(k1, 1), :], twi_ref[pl.ds(k1, 1), :]), blk)
        kr = ks_ref[0, pl.ds(s0, r), :].astype(F32)
        ki = ks_ref[1, pl.ds(s0, r), :].astype(F32)
        return jnp.concatenate([sr * kr - si * ki, sr * ki + si * kr], axis=1).astype(BF16)

    def b_inverse(k1, p):
        tr, ti = _cmatmul(fc_ref[...], p)
        _store_pair(a_ref, pl.ds(pl.multiple_of(k1 * PITCH, 8), r), tr, ti)

    def b_body(k1, p):
        p_next = b_forward(k1 + 1)
        b_inverse(k1, p)
        return p_next

    b_inverse(r - 1, lax.fori_loop(0, r - 1, b_body, b_forward(0), unroll=DFT_UNROLL))

    inv_n = 1.0 / (r * r)

    def c_body(n2, carry):
        g = _twiddled(fr_ref[0:half, :], fi_ref[0:half, :], twr_ref[pl.ds(n2, 1), :], twi_ref[pl.ds(n2, 1), :],
                      inv_n, -inv_n)
        yr, yi = _cmatmul(g, _load_pair(a_ref, pl.ds(n2, r, stride=PITCH)))
        _store_pair(v_ref, pl.ds(n2, half, stride=PITCH), yr, yi)
        return carry

    lax.fori_loop(0, r, c_body, 0, unroll=DFT_UNROLL_STRIDED)

    def out_body(c, carry):
        r0 = pl.multiple_of(c * CONV_ROWS, CONV_ROWS)
        for bi in range(nb):
            gate = _short_conv_rows(u_ref, bi, r0, CONV_ROWS, seq, uw_ref, ub_ref)
            conv = jnp.concatenate(
                [v_ref[bi, pl.ds(pl.multiple_of((c * blocks + j) * PITCH, 8), r), :] for j in range(blocks)], axis=0)
            zp = (o_ref if conv_in else z_ref)[bi, pl.ds(r0, CONV_ROWS), :].astype(F32)
            o_ref[bi, pl.ds(r0, CONV_ROWS), :] = (gate * (conv + hb_ref[...] * zp)).astype(BF16)
        return carry

    lax.fori_loop(0, nchunk, out_body, 0)


def _hy_conv(z, z_part, u, u_part, order, short_w, short_b, kspec, hy_bias, tables, conv_in):
    nb, seq, _ = u.shape
    width = hy_bias.shape[1]
    tiles = width // LANES
    fa, fc, fr, fi, twr, twi = tables
    whole = lambda j: (0, 0)
    zcol = lambda j: (0, 0, z_part * tiles + j)
    ucol = lambda j: (0, 0, u_part * tiles + j)
    return pl.pallas_call(
        functools.partial(_hy_conv_kernel, conv_in=conv_in),
        grid=(tiles,),
        in_specs=[
            pl.BlockSpec((nb, seq, LANES), zcol),
            pl.BlockSpec((3, LANES), lambda j: (0, z_part * tiles + j)),
            pl.BlockSpec((1, LANES), lambda j: (0, z_part * tiles + j)),
            pl.BlockSpec((nb, seq, LANES), ucol),
            pl.BlockSpec((3, LANES), lambda j: (0, u_part * tiles + j)),
            pl.BlockSpec((1, LANES), lambda j: (0, u_part * tiles + j)),
            _resident((2, 2 * seq, LANES), lambda j: (0, 0, order * tiles + j)),
            pl.BlockSpec((1, LANES), lambda j: (0, j)),
            _resident(fa.shape, whole), _resident(fc.shape, whole), _resident(fr.shape, whole),
            _resident(fi.shape, whole), _resident(twr.shape, whole), _resident(twi.shape, whole),
        ],
        out_specs=_resident((nb, seq, LANES), lambda j: (0, 0, j)),
        out_shape=jax.ShapeDtypeStruct((nb, seq, width), BF16),
        scratch_shapes=[pltpu.VMEM((2, 2 * seq // RADIX * PITCH, LANES), F32),
                        pltpu.VMEM((nb, seq // RADIX * PITCH, LANES), F32)],
        compiler_params=_cparams(("arbitrary",)),
        name=f"hy_conv{order}",
    )(z, short_w, short_b, u, short_w, short_b, kspec, hy_bias[order:order + 1], fa, fc, fr, fi, twr, twi)


def _hyena(hyu, short_w, short_b, fw1, fb1, fw2, fb2, fw3, fb3, freq, hy_bias):
    nb, seq, _ = hyu.shape
    assert nb == 2 and 2 * seq == RADIX * RADIX
    width = hy_bias.shape[1]
    hid = fw2.shape[0]
    fr, fi, twr, twi = _dft_tables()
    half = RADIX // 2
    fa_real = jnp.concatenate([fr, fi], axis=0).astype(BF16)
    fa_half = jnp.concatenate([fr[:, :half], fi[:, :half]], axis=0).astype(BF16)
    fc = jnp.concatenate([fr, -fi], axis=0).astype(BF16)

    zz = jnp.asarray(_position_features(seq))
    hdn = _filt_mlp(zz, fw1, fb1, fw2, fb2, freq).reshape(2 * seq, 4 * hid)
    w3d = fw3.reshape(hid, 2, HY_ORDER * width).transpose(1, 0, 2)
    b3d = fb3.reshape(2, 1, HY_ORDER * width)
    max_decay = math.log(HY_DECAY_TARGET) / HY_FAST_DECAY_PCT
    min_decay = math.log(HY_DECAY_TARGET) / HY_SLOW_DECAY_PCT
    delta = np.abs(np.linspace(min_decay, max_decay, width)).astype(np.float32)
    delta2 = jnp.asarray(np.tile(delta, HY_ORDER)[None, :])
    ktime, ss = _filt_time(hdn, w3d, b3d, delta2, seq)
    kspec = _filt_spec(ktime, ss, fa_real, fr, fi, twr, twi)

    tables = (fa_half, fc, fr, fi, twr, twi)
    sb = short_b.reshape(1, -1)
    z1 = _hy_conv(hyu, 0, hyu, 1, 0, short_w, sb, kspec, hy_bias, tables, conv_in=True)
    return _hy_conv(z1, 0, hyu, 2, 1, short_w, sb, kspec, hy_bias, tables, conv_in=False)


def kernel(x, ln1_g, ln1_b, ffn1_w1, ffn1_w3, ffn1_w2, w_in, b_gate, na_rpb, hy_short_w, hy_short_b,
           hy_filt_w1, hy_filt_b1, hy_filt_w2, hy_filt_b2, hy_filt_w3, hy_filt_b3, hy_filt_freq, hy_bias,
           w_pa, w_pb, w_out, ln2_g, ln2_b, ffn2_w1, ffn2_w3, ffn2_w2, ln3_g, ln3_b):
    assert ln1_g.shape[0] == DEPTH
    b, seq, d = x.shape
    rows = seq // GRID_W
    na_width = NA_HEADS * NA_HEAD_DIM
    n_qkv = 3 * na_width
    n_hy = hy_short_w.shape[2]
    bf = lambda a: a[0].astype(BF16)
    vec = lambda a: a[0].reshape(1, -1)

    x1, qkv, hyu, gates = _dense_in(
        x.reshape(b * seq, d), bf(ffn1_w1), bf(ffn1_w3), bf(ffn1_w2), vec(ln1_g), vec(ln1_b),
        bf(w_in), vec(b_gate), n_qkv, n_hy)

    bias = _na_bias(na_rpb[0], rows)
    ya = _na_attn(qkv.reshape(b, seq, n_qkv), bias, rows)

    yb = _hyena(hyu.reshape(b, seq, n_hy), hy_short_w[0], hy_short_b[0], hy_filt_w1[0], hy_filt_b1[0],
                hy_filt_w2[0], hy_filt_b2[0], hy_filt_w3[0], hy_filt_b3[0], hy_filt_freq[0], hy_bias[0])

    out = _dense_out(
        x1, ya.reshape(b * seq, na_width), yb.reshape(b * seq, -1), gates,
        bf(w_pa), bf(w_pb), bf(w_out), vec(ln2_g), vec(ln2_b),
        bf(ffn2_w1), bf(ffn2_w3), bf(ffn2_w2), vec(ln3_g), vec(ln3_b))
    return out.reshape(b, seq, d)
```

```python
import functools
import math

import numpy as np
import jax
import jax.numpy as jnp
from jax import lax
from jax.experimental import pallas as pl
from jax.experimental.pallas import tpu as pltpu

F32 = jnp.float32
BF16 = jnp.bfloat16

GRID_W = 64
NA_HEADS = 8
NA_HEAD_DIM = 64
NA_KH = 8
NA_KW = 16
HY_ORDER = 2
HY_EMB_DIM = 33
HY_FAST_DECAY_PCT = 0.3
HY_SLOW_DECAY_PCT = 1.5
HY_DECAY_TARGET = 1e-2
DEPTH = 1
DN_ALPHA = (2 * DEPTH) ** 0.25
LN_EPS = 1e-5

LANES = 128
VMEM_LIMIT = 60 * 1024 * 1024

ROW_TILE = 512
FF_CHUNK = 256
PROJ_CHUNK = 512
NA_QROWS = 4
NA_KROWS = NA_QROWS + NA_KH
RADIX = 128
PITCH = RADIX + 8
DFT_UNROLL = 8
DFT_UNROLL_STRIDED = 16
CONV_ROWS = 512
NEG = -1e30


def _cparams(sem):
    return pltpu.CompilerParams(dimension_semantics=sem, vmem_limit_bytes=VMEM_LIMIT)


def _resident(shape, index_map):
    return pl.BlockSpec(shape, index_map, pipeline_mode=pl.Buffered(1))


def _layer_norm(r, g, b):
    mu = jnp.mean(r, axis=-1, keepdims=True)
    d = r - mu
    var = jnp.mean(d * d, axis=-1, keepdims=True)
    return d * lax.rsqrt(var + LN_EPS) * g + b


def _swiglu_ln(x, xb_ref, acc_ref, w1_ref, w3_ref, w2_ref, g_ref, b_ref):
    d_ff = w1_ref.shape[1]
    xb_ref[...] = x.astype(BF16)
    for j in range(d_ff // FF_CHUNK):
        sl = slice(j * FF_CHUNK, (j + 1) * FF_CHUNK)
        a = jnp.dot(xb_ref[...], w1_ref[:, sl], preferred_element_type=F32)
        b = jnp.dot(xb_ref[...], w3_ref[:, sl], preferred_element_type=F32)
        h = (a * jax.nn.sigmoid(a) * b).astype(BF16)
        part = jnp.dot(h, w2_ref[sl, :], preferred_element_type=F32)
        if j == 0:
            acc_ref[...] = part
        else:
            acc_ref[...] += part
    return _layer_norm(DN_ALPHA * x + 0.5 * acc_ref[...], g_ref[...], b_ref[...])


def _dense_in_kernel(x_ref, w1_ref, w3_ref, w2_ref, g_ref, b_ref, win_ref, bg_ref,
                     x1_ref, qkv_ref, hyu_ref, gate_ref, xb_ref, acc_ref):
    x1 = _swiglu_ln(x_ref[...], xb_ref, acc_ref, w1_ref, w3_ref, w2_ref, g_ref, b_ref)
    x1_ref[...] = x1
    xb_ref[...] = x1.astype(BF16)
    n_qkv = qkv_ref.shape[1]
    n_hy = hyu_ref.shape[1]
    n_head = n_qkv // 3
    for j in range(win_ref.shape[1] // PROJ_CHUNK):
        c0 = j * PROJ_CHUNK
        p = jnp.dot(xb_ref[...], win_ref[:, c0:c0 + PROJ_CHUNK], preferred_element_type=F32)
        if c0 < n_qkv:
            if c0 < n_head:
                p = p * (NA_HEAD_DIM ** -0.5)
            qkv_ref[:, c0:c0 + PROJ_CHUNK] = p.astype(BF16)
        elif c0 < n_qkv + n_hy:
            hyu_ref[:, c0 - n_qkv:c0 - n_qkv + PROJ_CHUNK] = p.astype(BF16)
        else:
            g0 = c0 - n_qkv - n_hy
            gate_ref[:, g0:g0 + PROJ_CHUNK] = jax.nn.sigmoid(p + bg_ref[:, g0:g0 + PROJ_CHUNK]).astype(BF16)


def _dense_in(x2d, w1, w3, w2, g, b, w_in, b_gate, n_qkv, n_hy):
    n, d = x2d.shape
    d_ff = w1.shape[1]
    n_gate = w_in.shape[1] - n_qkv - n_hy
    assert n % ROW_TILE == 0 and d_ff % FF_CHUNK == 0
    assert n_qkv % (3 * PROJ_CHUNK) == 0 and n_hy % PROJ_CHUNK == 0 and n_gate % PROJ_CHUNK == 0
    row = lambda i: (i, 0)
    whole = lambda i: (0, 0)
    return pl.pallas_call(
        _dense_in_kernel,
        grid=(n // ROW_TILE,),
        in_specs=[
            pl.BlockSpec((ROW_TILE, d), row),
            _resident((d, d_ff), whole), _resident((d, d_ff), whole), _resident((d_ff, d), whole),
            _resident((1, d), whole), _resident((1, d), whole),
            _resident(w_in.shape, whole), _resident((1, n_gate), whole),
        ],
        out_specs=[
            pl.BlockSpec((ROW_TILE, d), row),
            pl.BlockSpec((ROW_TILE, n_qkv), row),
            pl.BlockSpec((ROW_TILE, n_hy), row),
            pl.BlockSpec((ROW_TILE, n_gate), row),
        ],
        out_shape=[
            jax.ShapeDtypeStruct((n, d), F32),
            jax.ShapeDtypeStruct((n, n_qkv), BF16),
            jax.ShapeDtypeStruct((n, n_hy), BF16),
            jax.ShapeDtypeStruct((n, n_gate), BF16),
        ],
        scratch_shapes=[pltpu.VMEM((ROW_TILE, d), BF16), pltpu.VMEM((ROW_TILE, d), F32)],
        compiler_params=_cparams(("arbitrary",)),
        name="dense_in",
    )(x2d, w1, w3, w2, g, b, w_in, b_gate)


def _dense_out_kernel(x1_ref, ya_ref, yb_ref, gate_ref, wpa_ref, wpb_ref, wout_ref, g2_ref, b2_ref,
                      w1_ref, w3_ref, w2_ref, g3_ref, b3_ref, o_ref, xb_ref, acc_ref):
    d = x1_ref.shape[1]
    pa = jnp.dot(ya_ref[...], wpa_ref[...], preferred_element_type=F32)
    pb = jnp.dot(yb_ref[...], wpb_ref[...], preferred_element_type=F32)
    m = gate_ref[:, :d].astype(F32) * pa + gate_ref[:, d:].astype(F32) * pb
    mix = jnp.dot(m.astype(BF16), wout_ref[...], preferred_element_type=F32)
    x2 = _layer_norm(DN_ALPHA * x1_ref[...] + mix, g2_ref[...], b2_ref[...])
    o_ref[...] = _swiglu_ln(x2, xb_ref, acc_ref, w1_ref, w3_ref, w2_ref, g3_ref, b3_ref)


def _dense_out(x1, ya, yb, gates, wpa, wpb, wout, g2, b2, w1, w3, w2, g3, b3):
    n, d = x1.shape
    d_ff = w1.shape[1]
    row = lambda i: (i, 0)
    whole = lambda i: (0, 0)
    return pl.pallas_call(
        _dense_out_kernel,
        grid=(n // ROW_TILE,),
        in_specs=[
            pl.BlockSpec((ROW_TILE, d), row),
            pl.BlockSpec((ROW_TILE, ya.shape[1]), row),
            pl.BlockSpec((ROW_TILE, yb.shape[1]), row),
            pl.BlockSpec((ROW_TILE, gates.shape[1]), row),
            _resident(wpa.shape, whole), _resident(wpb.shape, whole), _resident(wout.shape, whole),
            _resident((1, d), whole), _resident((1, d), whole),
            _resident((d, d_ff), whole), _resident((d, d_ff), whole), _resident((d_ff, d), whole),
            _resident((1, d), whole), _resident((1, d), whole),
        ],
        out_specs=pl.BlockSpec((ROW_TILE, d), row),
        out_shape=jax.ShapeDtypeStruct((n, d), F32),
        scratch_shapes=[pltpu.VMEM((ROW_TILE, d), BF16), pltpu.VMEM((ROW_TILE, d), F32)],
        compiler_params=_cparams(("arbitrary",)),
        name="dense_out",
    )(x1, ya, yb, gates, wpa, wpb, wout, g2, b2, w1, w3, w2, g3, b3)


def _window_start(n, k):
    return np.clip(np.arange(n) - k // 2, 0, n - k)


def _na_group_types(rows):
    last = rows - NA_QROWS
    return ((0, 0), (NA_QROWS * 2, NA_QROWS), (last, rows - NA_KROWS))


def _na_tables(rows):
    w = GRID_W
    shift = np.zeros((2 * NA_KW - 1, w, 2 * w), np.float32)
    q = np.arange(w)[:, None]
    kc = np.arange(w)[None, :]
    for dc in range(2 * NA_KW - 1):
        shift[dc, :, :w] = (kc - q + NA_KW - 1 == dc).astype(np.float32)
    cs = _window_start(w, NA_KW)
    col_ok = (kc >= cs[:, None]) & (kc < cs[:, None] + NA_KW)
    rs_all = _window_start(rows, NA_KH)
    negmask = np.full((3, NA_QROWS, w, NA_KROWS, w), NEG, np.float32)
    for t, (r0, kb) in enumerate(_na_group_types(rows)):
        for i in range(NA_QROWS):
            rs = rs_all[r0 + i]
            for j in range(NA_KROWS):
                if rs <= kb + j < rs + NA_KH:
                    negmask[t, i, :, j, :] = np.where(col_ok, 0.0, NEG)
    return shift, negmask.reshape(3, NA_QROWS * w, NA_KROWS * w)


def _na_bias_kernel(rpb_ref, shift_ref, neg_ref, o_ref, u_ref, *, rows):
    h = pl.program_id(0)
    n_dr = 2 * NA_KH - 1
    n_dc = 2 * NA_KW - 1
    w = GRID_W

    def dr_body(dr, carry):
        base = (h * n_dr + dr) * n_dc
        left = rpb_ref[base] * shift_ref[0]
        for dc in range(1, n_dc):
            left = left + rpb_ref[base + dc] * shift_ref[dc]
        u_ref[0, dr] = left
        u_ref[1, dr] = pltpu.roll(left, w, 1)
        return carry

    lax.fori_loop(0, n_dr, dr_body, 0)
    rs_all = _window_start(rows, NA_KH)
    for t, (r0, kb) in enumerate(_na_group_types(rows)):
        for i in range(NA_QROWS):
            rs = rs_all[r0 + i]
            for jj in range(NA_KROWS // 2):
                val = neg_ref[t, i * w:(i + 1) * w, jj * 2 * w:(jj + 1) * 2 * w]
                for s in range(2):
                    kr = kb + 2 * jj + s
                    if rs <= kr < rs + NA_KH:
                        val = val + u_ref[s, kr - (r0 + i) + NA_KH - 1]
                o_ref[t, 0, i * w:(i + 1) * w, jj * 2 * w:(jj + 1) * 2 * w] = val


def _na_bias(rpb, rows):
    shift, negmask = _na_tables(rows)
    nq, nk = NA_QROWS * GRID_W, NA_KROWS * GRID_W
    return pl.pallas_call(
        functools.partial(_na_bias_kernel, rows=rows),
        grid=(NA_HEADS,),
        in_specs=[
            pl.BlockSpec(memory_space=pltpu.SMEM),
            _resident(shift.shape, lambda h: (0, 0, 0)),
            _resident(negmask.shape, lambda h: (0, 0, 0)),
        ],
        out_specs=pl.BlockSpec((3, 1, nq, nk), lambda h: (0, h, 0, 0)),
        out_shape=jax.ShapeDtypeStruct((3, NA_HEADS, nq, nk), F32),
        scratch_shapes=[pltpu.VMEM((2, 2 * NA_KH - 1, GRID_W, 2 * GRID_W), F32)],
        compiler_params=_cparams(("arbitrary",)),
        name="na_bias",
    )(rpb.reshape(-1), jnp.asarray(shift), jnp.asarray(negmask))


def _na_attn_kernel(q_ref, k_ref, v_ref, bias_ref, o_ref, *, rows):
    g = pl.program_id(1)
    kb = jnp.clip(g * NA_QROWS - NA_KH // 2, 0, rows - NA_KROWS)
    k0 = pl.multiple_of(kb * GRID_W, GRID_W)
    nk = NA_KROWS * GRID_W
    dh = NA_HEAD_DIM
    lane = lax.broadcasted_iota(jnp.int32, (1, 2 * dh), 1)
    for hp in range(NA_HEADS // 2):
        cols = slice(hp * 2 * dh, (hp + 1) * 2 * dh)
        q2 = q_ref[0, :, cols]
        k2 = k_ref[0, pl.ds(k0, nk), cols]
        v2 = v_ref[0, pl.ds(k0, nk), cols]
        pair = []
        for s in range(2):
            own = (lane < dh) if s == 0 else (lane >= dh)
            sc = lax.dot_general(q2, jnp.where(own, k2, jnp.zeros_like(k2)), (((1,), (1,)), ((), ())),
                                 preferred_element_type=F32)
            sc = sc + bias_ref[0, 2 * hp + s]
            e = jnp.exp(sc - jnp.max(sc, axis=-1, keepdims=True)).astype(BF16)
            r = jnp.dot(e, jnp.where(own, v2, jnp.ones_like(v2)), preferred_element_type=F32)
            den = r[:, dh:dh + 1] if s == 0 else r[:, 0:1]
            pair.append(jnp.where(own, r / den, 0.0))
        o_ref[0, :, cols] = (pair[0] + pair[1]).astype(BF16)


def _na_attn(qkv, bias, rows):
    b, seq, three_w = qkv.shape
    width = three_w // 3
    nq, nk = NA_QROWS * GRID_W, NA_KROWS * GRID_W
    groups = rows // NA_QROWS

    def bias_idx(bi, g):
        return (jnp.where(g == 0, 0, jnp.where(g == groups - 1, 2, 1)), 0, 0, 0)

    return pl.pallas_call(
        functools.partial(_na_attn_kernel, rows=rows),
        grid=(b, groups),
        in_specs=[
            pl.BlockSpec((1, nq, width), lambda bi, g: (bi, g, 0)),
            _resident((1, seq, width), lambda bi, g: (bi, 0, 1)),
            _resident((1, seq, width), lambda bi, g: (bi, 0, 2)),
            pl.BlockSpec((1, NA_HEADS, nq, nk), bias_idx),
        ],
        out_specs=pl.BlockSpec((1, nq, width), lambda bi, g: (bi, g, 0)),
        out_shape=jax.ShapeDtypeStruct((b, seq, width), BF16),
        compiler_params=_cparams(("arbitrary", "arbitrary")),
        name="na_attn",
    )(qkv, qkv, qkv, bias)


def _dft_tables():
    k = np.arange(RADIX)
    ang = 2.0 * np.pi * np.outer(k, k) / RADIX
    fr, fi = np.cos(ang), -np.sin(ang)
    ang_t = 2.0 * np.pi * np.outer(k, k) / (RADIX * RADIX)
    twr, twi = np.cos(ang_t), -np.sin(ang_t)
    f32 = lambda a: jnp.asarray(a, F32)
    return f32(fr), f32(fi), f32(twr), f32(twi)


def _cmatmul(c_rows, x_lanes):
    m = c_rows.shape[0] // 2
    r = jnp.dot(c_rows, x_lanes, preferred_element_type=F32)
    return r[:m, :LANES] - r[m:, LANES:], r[:m, LANES:] + r[m:, :LANES]


def _load_pair(ref, rows):
    return jnp.concatenate([ref[0, rows, :], ref[1, rows, :]], axis=1).astype(BF16)


def _store_pair(ref, rows, re, im):
    ref[0, rows, :] = re
    ref[1, rows, :] = im


def _twiddled(fr, fi, twr, twi, scale_re=1.0, scale_im=1.0):
    return jnp.concatenate([(fr * twr - fi * twi) * scale_re, (fr * twi + fi * twr) * scale_im], axis=0).astype(BF16)


def _position_features(seq):
    t = np.linspace(0.0, 1.0, seq)[:, None]
    bands = (HY_EMB_DIM - 1) // 2
    w = (2.0 * math.pi / seq) * np.arange(seq)[:, None]
    f = np.linspace(1e-4, bands - 1, bands)[None, :]
    z = np.concatenate([t, np.cos(f * w), -np.sin(f * w)], axis=-1)
    rev = seq - np.arange(seq)
    rev[0] = 0
    zz = np.zeros((seq, 2 * LANES), np.float32)
    zz[:, :HY_EMB_DIM] = z
    zz[:, LANES:LANES + HY_EMB_DIM] = z[rev]
    return zz


def _filt_mlp_kernel(z_ref, w1_ref, b1_ref, w2_ref, b2_ref, fr_ref, o_ref):
    hp = lax.Precision.HIGHEST
    fr = fr_ref[...]
    hid = LANES // 2
    h = jnp.sin(fr * (jnp.dot(z_ref[...], w1_ref[...], precision=hp, preferred_element_type=F32) + b1_ref[...]))
    h = jnp.sin(fr * (jnp.dot(h, w2_ref[...], precision=hp, preferred_element_type=F32) + b2_ref[...]))
    hi = h.astype(BF16).astype(F32)
    lo = h - hi
    hi_sw, lo_sw = pltpu.roll(hi, hid, 1), pltpu.roll(lo, hid, 1)
    low = lax.broadcasted_iota(jnp.int32, (1, LANES), 1) < hid
    o_ref[0, :, :LANES] = jnp.where(low, hi, lo_sw).astype(BF16)
    o_ref[0, :, LANES:] = jnp.where(low, hi, 0.0).astype(BF16)
    o_ref[1, :, :LANES] = jnp.where(low, hi_sw, lo).astype(BF16)
    o_ref[1, :, LANES:] = jnp.where(low, hi_sw, 0.0).astype(BF16)


def _filt_mlp(zz, fw1, fb1, fw2, fb2, freq):
    seq = zz.shape[0]
    hid = fw2.shape[0]
    assert 2 * hid == LANES
    w1 = jnp.zeros((2 * LANES, LANES), F32)
    w1 = w1.at[:HY_EMB_DIM, :hid].set(fw1).at[LANES:LANES + HY_EMB_DIM, hid:].set(fw1)
    w2 = jnp.zeros((LANES, LANES), F32).at[:hid, :hid].set(fw2).at[hid:, hid:].set(fw2)
    twice = lambda v: jnp.concatenate([v, v]).reshape(1, LANES)
    rows = 2048
    whole = lambda i: (0, 0)
    return pl.pallas_call(
        _filt_mlp_kernel,
        grid=(seq // rows,),
        in_specs=[pl.BlockSpec((rows, 2 * LANES), lambda i: (i, 0)),
                  pl.BlockSpec(w1.shape, whole), pl.BlockSpec((1, LANES), whole),
                  pl.BlockSpec(w2.shape, whole), pl.BlockSpec((1, LANES), whole), pl.BlockSpec((1, LANES), whole)],
        out_specs=pl.BlockSpec((2, rows, 2 * LANES), lambda i: (0, i, 0)),
        out_shape=jax.ShapeDtypeStruct((2, seq, 2 * LANES), BF16),
        compiler_params=_cparams(("arbitrary",)),
        name="filt_mlp",
    )(zz, w1, twice(fb1), w2, twice(fb2), twice(freq))


def _filt_time_kernel(h_ref, w3_ref, b3_ref, delta_ref, k_ref, ss_ref, *, seq):
    i = pl.program_id(0)
    rows = h_ref.shape[0]
    w = w3_ref[0]
    w_hi = w.astype(BF16)
    w_lo = (w - w_hi.astype(F32)).astype(BF16)
    wcat = jnp.concatenate([w_hi, w_hi, w_lo, jnp.zeros_like(w_lo)], axis=0)
    hf = jnp.dot(h_ref[...], wcat, preferred_element_type=F32) + b3_ref[0]
    m = i * rows + lax.broadcasted_iota(jnp.int32, (rows, 1), 0)
    pos = jnp.where(m < seq, m, 2 * seq - m).astype(F32)
    t = pos / (seq - 1)
    k = jnp.where(m == seq, 0.0, hf * jnp.exp(-t * delta_ref[...]))
    for j in range(rows // RADIX):
        k_ref[j * PITCH:j * PITCH + RADIX, :] = k[j * RADIX:(j + 1) * RADIX]
        k_ref[j * PITCH + RADIX:(j + 1) * PITCH, :] = jnp.zeros((PITCH - RADIX, k.shape[1]), F32)
    part = jnp.sum(k * k, axis=0, keepdims=True)

    @pl.when(i == 0)
    def _():
        ss_ref[...] = part

    @pl.when(i != 0)
    def _():
        ss_ref[...] += part


def _filt_time(hdn, w3d, b3d, delta2, seq):
    n2l, packed = hdn.shape
    _, hid, cols = w3d.shape
    assert packed == 4 * hid
    rows = 1024
    half = seq // rows
    prows = rows // RADIX * PITCH
    return pl.pallas_call(
        functools.partial(_filt_time_kernel, seq=seq),
        grid=(n2l // rows,),
        in_specs=[pl.BlockSpec((rows, packed), lambda i: (i, 0)),
                  pl.BlockSpec((1, hid, cols), lambda i: (i // half, 0, 0)),
                  pl.BlockSpec((1, 1, cols), lambda i: (i // half, 0, 0)),
                  pl.BlockSpec((1, cols), lambda i: (0, 0))],
        out_specs=[pl.BlockSpec((prows, cols), lambda i: (i, 0)),
                   pl.BlockSpec((1, cols), lambda i: (0, 0))],
        out_shape=[jax.ShapeDtypeStruct((n2l // RADIX * PITCH, cols), F32), jax.ShapeDtypeStruct((1, cols), F32)],
        compiler_params=_cparams(("arbitrary",)),
        name="filt_time",
    )(hdn, w3d, b3d, delta2)


def _filt_spec_kernel(k_ref, ss_ref, fa_ref, fr_ref, fi_ref, twr_ref, twi_ref, o_ref, a_ref):
    r = RADIX
    scale = lax.rsqrt(ss_ref[...] + 1e-12)

    def a_body(i, c):
        n2 = 2 * i
        rhs = jnp.concatenate([k_ref[pl.ds(n2, r, stride=PITCH), :], k_ref[pl.ds(n2 + 1, r, stride=PITCH), :]],
                              axis=1).astype(BF16)
        out = jnp.dot(fa_ref[...], rhs, preferred_element_type=F32)
        _store_pair(a_ref, pl.ds(n2, r, stride=PITCH), out[:r, :LANES], out[r:, :LANES])
        _store_pair(a_ref, pl.ds(n2 + 1, r, stride=PITCH), out[:r, LANES:], out[r:, LANES:])
        return c

    lax.fori_loop(0, r // 2, a_body, 0, unroll=DFT_UNROLL_STRIDED // 2)

    def b_body(k1, c):
        r0 = pl.multiple_of(k1 * PITCH, 8)
        o0 = pl.multiple_of(k1 * r, r)
        blk = _load_pair(a_ref, pl.ds(r0, r))
        sr, si = _cmatmul(_twiddled(fr_ref[...], fi_ref[...], twr_ref[pl.ds(k1, 1), :], twi_ref[pl.ds(k1, 1), :]), blk)
        o_ref[0, pl.ds(o0, r), :] = (sr * scale).astype(BF16)
        o_ref[1, pl.ds(o0, r), :] = (si * scale).astype(BF16)
        return c

    lax.fori_loop(0, r, b_body, 0, unroll=DFT_UNROLL_STRIDED)


def _filt_spec(ktime, ss, fa_real, fr, fi, twr, twi):
    prows, cols = ktime.shape
    n2l = prows // PITCH * RADIX
    whole = lambda j: (0, 0)
    return pl.pallas_call(
        _filt_spec_kernel,
        grid=(cols // LANES,),
        in_specs=[pl.BlockSpec((prows, LANES), lambda j: (0, j)),
                  pl.BlockSpec((1, LANES), lambda j: (0, j)),
                  _resident(fa_real.shape, whole), _resident(fr.shape, whole), _resident(fi.shape, whole),
                  _resident(twr.shape, whole), _resident(twi.shape, whole)],
        out_specs=pl.BlockSpec((2, n2l, LANES), lambda j: (0, 0, j)),
        out_shape=jax.ShapeDtypeStruct((2, n2l, cols), BF16),
        scratch_shapes=[pltpu.VMEM((2, prows, LANES), F32)],
        compiler_params=_cparams(("arbitrary",)),
        name="filt_spec",
    )(ktime, ss, fa_real, fr, fi, twr, twi)


def _short_conv_rows(u_ref, bi, r0, nrows, seq, w_ref, b_ref):
    edge = 16
    cur = u_ref[bi, pl.ds(r0, nrows), :].astype(F32)
    before = u_ref[bi, pl.ds(pl.multiple_of(jnp.maximum(r0 - edge, 0), edge), edge), :].astype(F32)
    after = u_ref[bi, pl.ds(pl.multiple_of(jnp.minimum(r0 + nrows, seq - edge), edge), edge), :].astype(F32)
    prev_edge = jnp.where(r0 > 0, before[edge - 1:edge, :], 0.0)
    next_edge = jnp.where(r0 + nrows < seq, after[0:1, :], 0.0)
    w0, w1, w2 = w_ref[0:1, :], w_ref[1:2, :], w_ref[2:3, :]
    out = b_ref[...] + w0 * pltpu.roll(cur, 1, 0) + w1 * cur + w2 * pltpu.roll(cur, nrows - 1, 0)
    row = lax.broadcasted_iota(jnp.int32, (8, cur.shape[1]), 0)
    first = out[0:8] + jnp.where(row == 0, w0 * (prev_edge - cur[nrows - 1:nrows]), 0.0)
    last = out[nrows - 8:] + jnp.where(row == 7, w2 * (next_edge - cur[0:1]), 0.0)
    return jnp.concatenate([first, out[8:nrows - 8], last], axis=0)


def _hy_conv_kernel(z_ref, zw_ref, zb_ref, u_ref, uw_ref, ub_ref, ks_hbm, hb_ref,
                    fa_ref, fc_ref, fr_ref, fi_ref, twr_ref, twi_ref, o_ref, a_ref, v_ref, ks_ref, ks_sem,
                    *, conv_in, ks_tile0):
    r = RADIX
    half = r // 2
    nb, seq, _ = z_ref.shape
    nchunk = seq // CONV_ROWS
    blocks = CONV_ROWS // r

    ks_col = pl.multiple_of((ks_tile0 + pl.program_id(0)) * LANES, LANES)
    ks_copy = pltpu.make_async_copy(ks_hbm.at[:, :, pl.ds(ks_col, LANES)], ks_ref, ks_sem)
    ks_copy.start()

    def load_body(c, carry):
        r0 = pl.multiple_of(c * CONV_ROWS, CONV_ROWS)
        for bi in range(nb):
            if conv_in:
                v = _short_conv_rows(z_ref, bi, r0, CONV_ROWS, seq, zw_ref, zb_ref)
                o_ref[bi, pl.ds(r0, CONV_ROWS), :] = v.astype(BF16)
            else:
                v = z_ref[bi, pl.ds(r0, CONV_ROWS), :].astype(F32)
            for j in range(blocks):
                p0 = pl.multiple_of((c * blocks + j) * PITCH, 8)
                v_ref[bi, pl.ds(p0, r), :] = v[j * r:(j + 1) * r]
        return carry

    lax.fori_loop(0, nchunk, load_body, 0)

    def a_body(n2, carry):
        re, im = _cmatmul(fa_ref[...], _load_pair(v_ref, pl.ds(n2, half, stride=PITCH)))
        _store_pair(a_ref, pl.ds(n2, r, stride=PITCH), re, im)
        return carry

    lax.fori_loop(0, r, a_body, 0, unroll=DFT_UNROLL_STRIDED)

    def b_forward(k1):
        r0 = pl.multiple_of(k1 * PITCH, 8)
        s0 = pl.multiple_of(k1 * r, r)
        blk = _load_pair(a_ref, pl.ds(r0, r))
        sr, si = _cmatmul(_twiddled(fr_ref[...], fi_ref[...], twr_ref[pl.ds(k1, 1), :], twi_ref[pl.ds(k1, 1), :]), blk)
        kr = ks_ref[0, pl.ds(s0, r), :].astype(F32)
        ki = ks_ref[1, pl.ds(s0, r), :].astype(F32)
        return jnp.concatenate([sr * kr - si * ki, sr * ki + si * kr], axis=1).astype(BF16)

    def b_inverse(k1, p):
        tr, ti = _cmatmul(fc_ref[...], p)
        _store_pair(a_ref, pl.ds(pl.multiple_of(k1 * PITCH, 8), r), tr, ti)

    def b_body(k1, p):
        p_next = b_forward(k1 + 1)
        b_inverse(k1, p)
        return p_next

    ks_copy.wait()
    b_inverse(r - 1, lax.fori_loop(0, r - 1, b_body, b_forward(0), unroll=DFT_UNROLL))

    inv_n = 1.0 / (r * r)

    def c_body(n2, carry):
        g = _twiddled(fr_ref[0:half, :], fi_ref[0:half, :], twr_ref[pl.ds(n2, 1), :], twi_ref[pl.ds(n2, 1), :],
                      inv_n, -inv_n)
        yr, yi = _cmatmul(g, _load_pair(a_ref, pl.ds(n2, r, stride=PITCH)))
        _store_pair(v_ref, pl.ds(n2, half, stride=PITCH), yr, yi)
        return carry

    lax.fori_loop(0, r, c_body, 0, unroll=DFT_UNROLL_STRIDED)

    def out_body(c, carry):
        r0 = pl.multiple_of(c * CONV_ROWS, CONV_ROWS)
        for bi in range(nb):
            gate = _short_conv_rows(u_ref, bi, r0, CONV_ROWS, seq, uw_ref, ub_ref)
            conv = jnp.concatenate(
                [v_ref[bi, pl.ds(pl.multiple_of((c * blocks + j) * PITCH, 8), r), :] for j in range(blocks)], axis=0)
            zp = (o_ref if conv_in else z_ref)[bi, pl.ds(r0, CONV_ROWS), :].astype(F32)
            o_ref[bi, pl.ds(r0, CONV_ROWS), :] = (gate * (conv + hb_ref[...] * zp)).astype(BF16)
        return carry

    lax.fori_loop(0, nchunk, out_body, 0)


def _hy_conv(z, z_part, u, u_part, order, short_w, short_b, kspec, hy_bias, tables, conv_in):
    nb, seq, _ = u.shape
    width = hy_bias.shape[1]
    tiles = width // LANES
    fa, fc, fr, fi, twr, twi = tables
    whole = lambda j: (0, 0)
    zcol = lambda j: (0, 0, z_part * tiles + j)
    ucol = lambda j: (0, 0, u_part * tiles + j)
    return pl.pallas_call(
        functools.partial(_hy_conv_kernel, conv_in=conv_in, ks_tile0=order * tiles),
        grid=(tiles,),
        in_specs=[
            pl.BlockSpec((nb, seq, LANES), zcol),
            pl.BlockSpec((3, LANES), lambda j: (0, z_part * tiles + j)),
            pl.BlockSpec((1, LANES), lambda j: (0, z_part * tiles + j)),
            pl.BlockSpec((nb, seq, LANES), ucol),
            pl.BlockSpec((3, LANES), lambda j: (0, u_part * tiles + j)),
            pl.BlockSpec((1, LANES), lambda j: (0, u_part * tiles + j)),
            pl.BlockSpec(memory_space=pl.ANY),
            pl.BlockSpec((1, LANES), lambda j: (0, j)),
            _resident(fa.shape, whole), _resident(fc.shape, whole), _resident(fr.shape, whole),
            _resident(fi.shape, whole), _resident(twr.shape, whole), _resident(twi.shape, whole),
        ],
        out_specs=_resident((nb, seq, LANES), lambda j: (0, 0, j)),
        out_shape=jax.ShapeDtypeStruct((nb, seq, width), BF16),
        scratch_shapes=[pltpu.VMEM((2, 2 * seq // RADIX * PITCH, LANES), F32),
                        pltpu.VMEM((nb, seq // RADIX * PITCH, LANES), F32),
                        pltpu.VMEM((2, 2 * seq, LANES), BF16),
                        pltpu.SemaphoreType.DMA(())],
        compiler_params=_cparams(("arbitrary",)),
        name=f"hy_conv{order}",
    )(z, short_w, short_b, u, short_w, short_b, kspec, hy_bias[order:order + 1], fa, fc, fr, fi, twr, twi)


def _hyena(hyu, short_w, short_b, fw1, fb1, fw2, fb2, fw3, fb3, freq, hy_bias):
    nb, seq, _ = hyu.shape
    assert nb == 2 and 2 * seq == RADIX * RADIX
    width = hy_bias.shape[1]
    hid = fw2.shape[0]
    fr, fi, twr, twi = _dft_tables()
    half = RADIX // 2
    fa_real = jnp.concatenate([fr, fi], axis=0).astype(BF16)
    fa_half = jnp.concatenate([fr[:, :half], fi[:, :half]], axis=0).astype(BF16)
    fc = jnp.concatenate([fr, -fi], axis=0).astype(BF16)

    zz = jnp.asarray(_position_features(seq))
    hdn = _filt_mlp(zz, fw1, fb1, fw2, fb2, freq).reshape(2 * seq, 4 * hid)
    w3d = fw3.reshape(hid, 2, HY_ORDER * width).transpose(1, 0, 2)
    b3d = fb3.reshape(2, 1, HY_ORDER * width)
    max_decay = math.log(HY_DECAY_TARGET) / HY_FAST_DECAY_PCT
    min_decay = math.log(HY_DECAY_TARGET) / HY_SLOW_DECAY_PCT
    delta = np.abs(np.linspace(min_decay, max_decay, width)).astype(np.float32)
    delta2 = jnp.asarray(np.tile(delta, HY_ORDER)[None, :])
    ktime, ss = _filt_time(hdn, w3d, b3d, delta2, seq)
    kspec = _filt_spec(ktime, ss, fa_real, fr, fi, twr, twi)

    tables = (fa_half, fc, fr, fi, twr, twi)
    sb = short_b.reshape(1, -1)
    z1 = _hy_conv(hyu, 0, hyu, 1, 0, short_w, sb, kspec, hy_bias, tables, conv_in=True)
    return _hy_conv(z1, 0, hyu, 2, 1, short_w, sb, kspec, hy_bias, tables, conv_in=False)


def kernel(x, ln1_g, ln1_b, ffn1_w1, ffn1_w3, ffn1_w2, w_in, b_gate, na_rpb, hy_short_w, hy_short_b,
           hy_filt_w1, hy_filt_b1, hy_filt_w2, hy_filt_b2, hy_filt_w3, hy_filt_b3, hy_filt_freq, hy_bias,
           w_pa, w_pb, w_out, ln2_g, ln2_b, ffn2_w1, ffn2_w3, ffn2_w2, ln3_g, ln3_b):
    assert ln1_g.shape[0] == DEPTH
    b, seq, d = x.shape
    rows = seq // GRID_W
    na_width = NA_HEADS * NA_HEAD_DIM
    n_qkv = 3 * na_width
    n_hy = hy_short_w.shape[2]
    bf = lambda a: a[0].astype(BF16)
    vec = lambda a: a[0].reshape(1, -1)

    x1, qkv, hyu, gates = _dense_in(
        x.reshape(b * seq, d), bf(ffn1_w1), bf(ffn1_w3), bf(ffn1_w2), vec(ln1_g), vec(ln1_b),
        bf(w_in), vec(b_gate), n_qkv, n_hy)

    bias = _na_bias(na_rpb[0], rows)
    ya = _na_attn(qkv.reshape(b, seq, n_qkv), bias, rows)

    yb = _hyena(hyu.reshape(b, seq, n_hy), hy_short_w[0], hy_short_b[0], hy_filt_w1[0], hy_filt_b1[0],
                hy_filt_w2[0], hy_filt_b2[0], hy_filt_w3[0], hy_filt_b3[0], hy_filt_freq[0], hy_bias[0])

    out = _dense_out(
        x1, ya.reshape(b * seq, na_width), yb.reshape(b * seq, -1), gates,
        bf(w_pa), bf(w_pb), bf(w_out), vec(ln2_g), vec(ln2_b),
        bf(ffn2_w1), bf(ffn2_w3), bf(ffn2_w2), vec(ln3_g), vec(ln3_b))
    return out.reshape(b, seq, d)
```

```python
import functools
import math

import numpy as np
import jax
import jax.numpy as jnp
from jax import lax
from jax.experimental import pallas as pl
from jax.experimental.pallas import tpu as pltpu

F32 = jnp.float32
BF16 = jnp.bfloat16

GRID_W = 64
NA_HEADS = 8
NA_HEAD_DIM = 64
NA_KH = 8
NA_KW = 16
HY_ORDER = 2
HY_EMB_DIM = 33
HY_FAST_DECAY_PCT = 0.3
HY_SLOW_DECAY_PCT = 1.5
HY_DECAY_TARGET = 1e-2
DEPTH = 1
DN_ALPHA = (2 * DEPTH) ** 0.25
LN_EPS = 1e-5

LANES = 128
VMEM_LIMIT = 60 * 1024 * 1024

ROW_TILE = 512
FF_CHUNK = 256
PROJ_CHUNK = 512
CAST_ROWS = 128
NA_QROWS = 4
NA_KROWS = NA_QROWS + NA_KH
RADIX = 128
PITCH = RADIX + 8
DFT_UNROLL = 8
DFT_UNROLL_STRIDED = 16
CONV_ROWS = 512
NEG = -1e30


def _cparams(sem):
    return pltpu.CompilerParams(dimension_semantics=sem, vmem_limit_bytes=VMEM_LIMIT)


def _resident(shape, index_map):
    return pl.BlockSpec(shape, index_map, pipeline_mode=pl.Buffered(1))


def _cast_weights_in(pairs, stage_ref, sem):
    for src, dst in pairs:
        rows, cols = src.shape
        n = rows // CAST_ROWS

        def copy(c, slot, src=src, cols=cols):
            return pltpu.make_async_copy(src.at[pl.ds(c * CAST_ROWS, CAST_ROWS), :],
                                         stage_ref.at[slot, :, pl.ds(0, cols)], sem.at[slot])

        copy(0, 0).start()

        def body(c, carry, copy=copy, dst=dst, cols=cols, n=n):
            slot = c % 2

            @pl.when(c + 1 < n)
            def _():
                copy(c + 1, 1 - slot).start()

            copy(c, slot).wait()
            r0 = pl.multiple_of(c * CAST_ROWS, CAST_ROWS)
            dst[pl.ds(r0, CAST_ROWS), :] = stage_ref[slot, :, 0:cols].astype(BF16)
            return carry

        lax.fori_loop(0, n, body, 0)


def _layer_norm(r, g, b):
    mu = jnp.mean(r, axis=-1, keepdims=True)
    d = r - mu
    var = jnp.mean(d * d, axis=-1, keepdims=True)
    return d * lax.rsqrt(var + LN_EPS) * g + b


def _swiglu_ln(x, xb_ref, acc_ref, w1_ref, w3_ref, w2_ref, g_ref, b_ref):
    d_ff = w1_ref.shape[1]
    xb_ref[...] = x.astype(BF16)
    for j in range(d_ff // FF_CHUNK):
        sl = slice(j * FF_CHUNK, (j + 1) * FF_CHUNK)
        a = jnp.dot(xb_ref[...], w1_ref[:, sl], preferred_element_type=F32)
        b = jnp.dot(xb_ref[...], w3_ref[:, sl], preferred_element_type=F32)
        h = (a * jax.nn.sigmoid(a) * b).astype(BF16)
        part = jnp.dot(h, w2_ref[sl, :], preferred_element_type=F32)
        if j == 0:
            acc_ref[...] = part
        else:
            acc_ref[...] += part
    return _layer_norm(DN_ALPHA * x + 0.5 * acc_ref[...], g_ref[...], b_ref[...])


def _dense_in_kernel(x_ref, w1_hbm, w3_hbm, w2_hbm, g_ref, b_ref, win_hbm, bg_ref,
                     x1_ref, qkv_ref, hyu_ref, gate_ref,
                     xb_ref, acc_ref, w1_ref, w3_ref, w2_ref, win_ref, stage_ref, sem):
    @pl.when(pl.program_id(0) == 0)
    def _():
        _cast_weights_in(((w1_hbm, w1_ref), (w3_hbm, w3_ref), (w2_hbm, w2_ref), (win_hbm, win_ref)), stage_ref, sem)

    x1 = _swiglu_ln(x_ref[...], xb_ref, acc_ref, w1_ref, w3_ref, w2_ref, g_ref, b_ref)
    x1_ref[...] = x1
    xb_ref[...] = x1.astype(BF16)
    n_qkv = qkv_ref.shape[1]
    n_hy = hyu_ref.shape[1]
    n_head = n_qkv // 3
    for j in range(win_ref.shape[1] // PROJ_CHUNK):
        c0 = j * PROJ_CHUNK
        p = jnp.dot(xb_ref[...], win_ref[:, c0:c0 + PROJ_CHUNK], preferred_element_type=F32)
        if c0 < n_qkv:
            if c0 < n_head:
                p = p * (NA_HEAD_DIM ** -0.5)
            qkv_ref[:, c0:c0 + PROJ_CHUNK] = p.astype(BF16)
        elif c0 < n_qkv + n_hy:
            hyu_ref[:, c0 - n_qkv:c0 - n_qkv + PROJ_CHUNK] = p.astype(BF16)
        else:
            g0 = c0 - n_qkv - n_hy
            gate_ref[:, g0:g0 + PROJ_CHUNK] = jax.nn.sigmoid(p + bg_ref[:, g0:g0 + PROJ_CHUNK]).astype(BF16)


def _dense_in(x2d, w1, w3, w2, g, b, w_in, b_gate, n_qkv, n_hy):
    n, d = x2d.shape
    d_ff = w1.shape[1]
    n_gate = w_in.shape[1] - n_qkv - n_hy
    assert n % ROW_TILE == 0 and d_ff % FF_CHUNK == 0
    assert n_qkv % (3 * PROJ_CHUNK) == 0 and n_hy % PROJ_CHUNK == 0 and n_gate % PROJ_CHUNK == 0
    weights = (w1, w3, w2, w_in)
    assert all(w.dtype == F32 and w.shape[0] % CAST_ROWS == 0 for w in weights)
    row = lambda i: (i, 0)
    whole = lambda i: (0, 0)
    hbm = pl.BlockSpec(memory_space=pl.ANY)
    return pl.pallas_call(
        _dense_in_kernel,
        grid=(n // ROW_TILE,),
        in_specs=[
            pl.BlockSpec((ROW_TILE, d), row),
            hbm, hbm, hbm,
            _resident((1, d), whole), _resident((1, d), whole),
            hbm, _resident((1, n_gate), whole),
        ],
        out_specs=[
            pl.BlockSpec((ROW_TILE, d), row),
            pl.BlockSpec((ROW_TILE, n_qkv), row),
            pl.BlockSpec((ROW_TILE, n_hy), row),
            pl.BlockSpec((ROW_TILE, n_gate), row),
        ],
        out_shape=[
            jax.ShapeDtypeStruct((n, d), F32),
            jax.ShapeDtypeStruct((n, n_qkv), BF16),
            jax.ShapeDtypeStruct((n, n_hy), BF16),
            jax.ShapeDtypeStruct((n, n_gate), BF16),
        ],
        scratch_shapes=[pltpu.VMEM((ROW_TILE, d), BF16), pltpu.VMEM((ROW_TILE, d), F32)]
        + [pltpu.VMEM(w.shape, BF16) for w in weights]
        + [pltpu.VMEM((2, CAST_ROWS, max(w.shape[1] for w in weights)), F32), pltpu.SemaphoreType.DMA((2,))],
        compiler_params=_cparams(("arbitrary",)),
        name="dense_in",
    )(x2d, w1, w3, w2, g, b, w_in, b_gate)


def _dense_out_kernel(x1_ref, ya_ref, yb_ref, gate_ref, wpa_hbm, wpb_hbm, wout_hbm, g2_ref, b2_ref,
                      w1_hbm, w3_hbm, w2_hbm, g3_ref, b3_ref, o_ref,
                      xb_ref, acc_ref, wpa_ref, wpb_ref, wout_ref, w1_ref, w3_ref, w2_ref, stage_ref, sem):
    @pl.when(pl.program_id(0) == 0)
    def _():
        _cast_weights_in(((wpa_hbm, wpa_ref), (wpb_hbm, wpb_ref), (wout_hbm, wout_ref),
                          (w1_hbm, w1_ref), (w3_hbm, w3_ref), (w2_hbm, w2_ref)), stage_ref, sem)

    d = x1_ref.shape[1]
    pa = jnp.dot(ya_ref[...], wpa_ref[...], preferred_element_type=F32)
    pb = jnp.dot(yb_ref[...], wpb_ref[...], preferred_element_type=F32)
    m = gate_ref[:, :d].astype(F32) * pa + gate_ref[:, d:].astype(F32) * pb
    mix = jnp.dot(m.astype(BF16), wout_ref[...], preferred_element_type=F32)
    x2 = _layer_norm(DN_ALPHA * x1_ref[...] + mix, g2_ref[...], b2_ref[...])
    o_ref[...] = _swiglu_ln(x2, xb_ref, acc_ref, w1_ref, w3_ref, w2_ref, g3_ref, b3_ref)


def _dense_out(x1, ya, yb, gates, wpa, wpb, wout, g2, b2, w1, w3, w2, g3, b3):
    n, d = x1.shape
    weights = (wpa, wpb, wout, w1, w3, w2)
    assert all(w.dtype == F32 and w.shape[0] % CAST_ROWS == 0 for w in weights)
    row = lambda i: (i, 0)
    whole = lambda i: (0, 0)
    hbm = pl.BlockSpec(memory_space=pl.ANY)
    return pl.pallas_call(
        _dense_out_kernel,
        grid=(n // ROW_TILE,),
        in_specs=[
            pl.BlockSpec((ROW_TILE, d), row),
            pl.BlockSpec((ROW_TILE, ya.shape[1]), row),
            pl.BlockSpec((ROW_TILE, yb.shape[1]), row),
            pl.BlockSpec((ROW_TILE, gates.shape[1]), row),
            hbm, hbm, hbm,
            _resident((1, d), whole), _resident((1, d), whole),
            hbm, hbm, hbm,
            _resident((1, d), whole), _resident((1, d), whole),
        ],
        out_specs=pl.BlockSpec((ROW_TILE, d), row),
        out_shape=jax.ShapeDtypeStruct((n, d), F32),
        scratch_shapes=[pltpu.VMEM((ROW_TILE, d), BF16), pltpu.VMEM((ROW_TILE, d), F32)]
        + [pltpu.VMEM(w.shape, BF16) for w in weights]
        + [pltpu.VMEM((2, CAST_ROWS, max(w.shape[1] for w in weights)), F32), pltpu.SemaphoreType.DMA((2,))],
        compiler_params=_cparams(("arbitrary",)),
        name="dense_out",
    )(x1, ya, yb, gates, wpa, wpb, wout, g2, b2, w1, w3, w2, g3, b3)


def _window_start(n, k):
    return np.clip(np.arange(n) - k // 2, 0, n - k)


def _na_group_types(rows):
    last = rows - NA_QROWS
    return ((0, 0), (NA_QROWS * 2, NA_QROWS), (last, rows - NA_KROWS))


def _na_tables(rows):
    w = GRID_W
    shift = np.zeros((2 * NA_KW - 1, w, 2 * w), np.float32)
    q = np.arange(w)[:, None]
    kc = np.arange(w)[None, :]
    for dc in range(2 * NA_KW - 1):
        shift[dc, :, :w] = (kc - q + NA_KW - 1 == dc).astype(np.float32)
    cs = _window_start(w, NA_KW)
    col_ok = (kc >= cs[:, None]) & (kc < cs[:, None] + NA_KW)
    rs_all = _window_start(rows, NA_KH)
    negmask = np.full((3, NA_QROWS, w, NA_KROWS, w), NEG, np.float32)
    for t, (r0, kb) in enumerate(_na_group_types(rows)):
        for i in range(NA_QROWS):
            rs = rs_all[r0 + i]
            for j in range(NA_KROWS):
                if rs <= kb + j < rs + NA_KH:
                    negmask[t, i, :, j, :] = np.where(col_ok, 0.0, NEG)
    return shift, negmask.reshape(3, NA_QROWS * w, NA_KROWS * w)


def _na_bias_kernel(rpb_ref, shift_ref, neg_ref, o_ref, u_ref, *, rows):
    h = pl.program_id(0)
    n_dr = 2 * NA_KH - 1
    n_dc = 2 * NA_KW - 1
    w = GRID_W

    def dr_body(dr, carry):
        base = (h * n_dr + dr) * n_dc
        left = rpb_ref[base] * shift_ref[0]
        for dc in range(1, n_dc):
            left = left + rpb_ref[base + dc] * shift_ref[dc]
        u_ref[0, dr] = left
        u_ref[1, dr] = pltpu.roll(left, w, 1)
        return carry

    lax.fori_loop(0, n_dr, dr_body, 0)
    rs_all = _window_start(rows, NA_KH)
    for t, (r0, kb) in enumerate(_na_group_types(rows)):
        for i in range(NA_QROWS):
            rs = rs_all[r0 + i]
            for jj in range(NA_KROWS // 2):
                val = neg_ref[t, i * w:(i + 1) * w, jj * 2 * w:(jj + 1) * 2 * w]
                for s in range(2):
                    kr = kb + 2 * jj + s
                    if rs <= kr < rs + NA_KH:
                        val = val + u_ref[s, kr - (r0 + i) + NA_KH - 1]
                o_ref[t, 0, i * w:(i + 1) * w, jj * 2 * w:(jj + 1) * 2 * w] = val


def _na_bias(rpb, rows):
    shift, negmask = _na_tables(rows)
    nq, nk = NA_QROWS * GRID_W, NA_KROWS * GRID_W
    return pl.pallas_call(
        functools.partial(_na_bias_kernel, rows=rows),
        grid=(NA_HEADS,),
        in_specs=[
            pl.BlockSpec(memory_space=pltpu.SMEM),
            _resident(shift.shape, lambda h: (0, 0, 0)),
            _resident(negmask.shape, lambda h: (0, 0, 0)),
        ],
        out_specs=pl.BlockSpec((3, 1, nq, nk), lambda h: (0, h, 0, 0)),
        out_shape=jax.ShapeDtypeStruct((3, NA_HEADS, nq, nk), F32),
        scratch_shapes=[pltpu.VMEM((2, 2 * NA_KH - 1, GRID_W, 2 * GRID_W), F32)],
        compiler_params=_cparams(("arbitrary",)),
        name="na_bias",
    )(rpb.reshape(-1), jnp.asarray(shift), jnp.asarray(negmask))


def _na_attn_kernel(q_ref, k_ref, v_ref, bias_ref, o_ref, *, rows):
    g = pl.program_id(1)
    kb = jnp.clip(g * NA_QROWS - NA_KH // 2, 0, rows - NA_KROWS)
    k0 = pl.multiple_of(kb * GRID_W, GRID_W)
    nk = NA_KROWS * GRID_W
    dh = NA_HEAD_DIM
    lane = lax.broadcasted_iota(jnp.int32, (1, 2 * dh), 1)
    for hp in range(NA_HEADS // 2):
        cols = slice(hp * 2 * dh, (hp + 1) * 2 * dh)
        q2 = q_ref[0, :, cols]
        k2 = k_ref[0, pl.ds(k0, nk), cols]
        v2 = v_ref[0, pl.ds(k0, nk), cols]
        pair = []
        for s in range(2):
            own = (lane < dh) if s == 0 else (lane >= dh)
            sc = lax.dot_general(q2, jnp.where(own, k2, jnp.zeros_like(k2)), (((1,), (1,)), ((), ())),
                                 preferred_element_type=F32)
            sc = sc + bias_ref[0, 2 * hp + s]
            e = jnp.exp(sc - jnp.max(sc, axis=-1, keepdims=True)).astype(BF16)
            r = jnp.dot(e, jnp.where(own, v2, jnp.ones_like(v2)), preferred_element_type=F32)
            den = r[:, dh:dh + 1] if s == 0 else r[:, 0:1]
            pair.append(jnp.where(own, r / den, 0.0))
        o_ref[0, :, cols] = (pair[0] + pair[1]).astype(BF16)


def _na_attn(qkv, bias, rows):
    b, seq, three_w = qkv.shape
    width = three_w // 3
    nq, nk = NA_QROWS * GRID_W, NA_KROWS * GRID_W
    groups = rows // NA_QROWS

    def bias_idx(bi, g):
        return (jnp.where(g == 0, 0, jnp.where(g == groups - 1, 2, 1)), 0, 0, 0)

    return pl.pallas_call(
        functools.partial(_na_attn_kernel, rows=rows),
        grid=(b, groups),
        in_specs=[
            pl.BlockSpec((1, nq, width), lambda bi, g: (bi, g, 0)),
            pl.BlockSpec((1, seq, width), lambda bi, g: (bi, 0, 1)),
            pl.BlockSpec((1, seq, width), lambda bi, g: (bi, 0, 2)),
            pl.BlockSpec((1, NA_HEADS, nq, nk), bias_idx),
        ],
        out_specs=pl.BlockSpec((1, nq, width), lambda bi, g: (bi, g, 0)),
        out_shape=jax.ShapeDtypeStruct((b, seq, width), BF16),
        compiler_params=_cparams(("arbitrary", "arbitrary")),
        name="na_attn",
    )(qkv, qkv, qkv, bias)


def _dft_tables():
    k = np.arange(RADIX)
    ang = 2.0 * np.pi * np.outer(k, k) / RADIX
    fr, fi = np.cos(ang), -np.sin(ang)
    ang_t = 2.0 * np.pi * np.outer(k, k) / (RADIX * RADIX)
    twr, twi = np.cos(ang_t), -np.sin(ang_t)
    f32 = lambda a: jnp.asarray(a, F32)
    return f32(fr), f32(fi), f32(twr), f32(twi)


def _cmatmul(c_rows, x_lanes):
    m = c_rows.shape[0] // 2
    r = jnp.dot(c_rows, x_lanes, preferred_element_type=F32)
    return r[:m, :LANES] - r[m:, LANES:], r[:m, LANES:] + r[m:, :LANES]


def _load_pair(ref, rows):
    return jnp.concatenate([ref[0, rows, :], ref[1, rows, :]], axis=1).astype(BF16)


def _store_pair(ref, rows, re, im):
    ref[0, rows, :] = re
    ref[1, rows, :] = im


def _twiddled(fr, fi, twr, twi, scale_re=1.0, scale_im=1.0):
    return jnp.concatenate([(fr * twr - fi * twi) * scale_re, (fr * twi + fi * twr) * scale_im], axis=0).astype(BF16)


def _position_features(seq):
    t = np.linspace(0.0, 1.0, seq)[:, None]
    bands = (HY_EMB_DIM - 1) // 2
    w = (2.0 * math.pi / seq) * np.arange(seq)[:, None]
    f = np.linspace(1e-4, bands - 1, bands)[None, :]
    z = np.concatenate([t, np.cos(f * w), -np.sin(f * w)], axis=-1)
    rev = seq - np.arange(seq)
    rev[0] = 0
    zz = np.zeros((seq, 2 * LANES), np.float32)
    zz[:, :HY_EMB_DIM] = z
    zz[:, LANES:LANES + HY_EMB_DIM] = z[rev]
    return zz


def _filt_mlp_kernel(z_ref, w1_ref, b1_ref, w2_ref, b2_ref, fr_ref, o_ref):
    hp = lax.Precision.HIGHEST
    fr = fr_ref[...]
    hid = LANES // 2
    h = jnp.sin(fr * (jnp.dot(z_ref[...], w1_ref[...], precision=hp, preferred_element_type=F32) + b1_ref[...]))
    h = jnp.sin(fr * (jnp.dot(h, w2_ref[...], precision=hp, preferred_element_type=F32) + b2_ref[...]))
    hi = h.astype(BF16).astype(F32)
    lo = h - hi
    hi_sw, lo_sw = pltpu.roll(hi, hid, 1), pltpu.roll(lo, hid, 1)
    low = lax.broadcasted_iota(jnp.int32, (1, LANES), 1) < hid
    o_ref[0, :, :LANES] = jnp.where(low, hi, lo_sw).astype(BF16)
    o_ref[0, :, LANES:] = jnp.where(low, hi, 0.0).astype(BF16)
    o_ref[1, :, :LANES] = jnp.where(low, hi_sw, lo).astype(BF16)
    o_ref[1, :, LANES:] = jnp.where(low, hi_sw, 0.0).astype(BF16)


def _filt_mlp(zz, fw1, fb1, fw2, fb2, freq):
    seq = zz.shape[0]
    hid = fw2.shape[0]
    assert 2 * hid == LANES
    w1 = jnp.zeros((2 * LANES, LANES), F32)
    w1 = w1.at[:HY_EMB_DIM, :hid].set(fw1).at[LANES:LANES + HY_EMB_DIM, hid:].set(fw1)
    w2 = jnp.zeros((LANES, LANES), F32).at[:hid, :hid].set(fw2).at[hid:, hid:].set(fw2)
    twice = lambda v: jnp.concatenate([v, v]).reshape(1, LANES)
    rows = 2048
    whole = lambda i: (0, 0)
    return pl.pallas_call(
        _filt_mlp_kernel,
        grid=(seq // rows,),
        in_specs=[pl.BlockSpec((rows, 2 * LANES), lambda i: (i, 0)),
                  pl.BlockSpec(w1.shape, whole), pl.BlockSpec((1, LANES), whole),
                  pl.BlockSpec(w2.shape, whole), pl.BlockSpec((1, LANES), whole), pl.BlockSpec((1, LANES), whole)],
        out_specs=pl.BlockSpec((2, rows, 2 * LANES), lambda i: (0, i, 0)),
        out_shape=jax.ShapeDtypeStruct((2, seq, 2 * LANES), BF16),
        compiler_params=_cparams(("arbitrary",)),
        name="filt_mlp",
    )(zz, w1, twice(fb1), w2, twice(fb2), twice(freq))


def _filt_time_kernel(h_ref, w3_ref, b3_ref, delta_ref, k_ref, ss_ref, *, seq):
    i = pl.program_id(0)
    rows = h_ref.shape[0]
    w = w3_ref[0]
    w_hi = w.astype(BF16)
    w_lo = (w - w_hi.astype(F32)).astype(BF16)
    wcat = jnp.concatenate([w_hi, w_hi, w_lo, jnp.zeros_like(w_lo)], axis=0)
    hf = jnp.dot(h_ref[...], wcat, preferred_element_type=F32) + b3_ref[0]
    m = i * rows + lax.broadcasted_iota(jnp.int32, (rows, 1), 0)
    pos = jnp.where(m < seq, m, 2 * seq - m).astype(F32)
    t = pos / (seq - 1)
    k = jnp.where(m == seq, 0.0, hf * jnp.exp(-t * delta_ref[...]))
    for j in range(rows // RADIX):
        k_ref[j * PITCH:j * PITCH + RADIX, :] = k[j * RADIX:(j + 1) * RADIX]
        k_ref[j * PITCH + RADIX:(j + 1) * PITCH, :] = jnp.zeros((PITCH - RADIX, k.shape[1]), F32)
    part = jnp.sum(k * k, axis=0, keepdims=True)

    @pl.when(i == 0)
    def _():
        ss_ref[...] = part

    @pl.when(i != 0)
    def _():
        ss_ref[...] += part


def _filt_time(hdn, w3d, b3d, delta2, seq):
    n2l, packed = hdn.shape
    _, hid, cols = w3d.shape
    assert packed == 4 * hid
    rows = 1024
    half = seq // rows
    prows = rows // RADIX * PITCH
    return pl.pallas_call(
        functools.partial(_filt_time_kernel, seq=seq),
        grid=(n2l // rows,),
        in_specs=[pl.BlockSpec((rows, packed), lambda i: (i, 0)),
                  pl.BlockSpec((1, hid, cols), lambda i: (i // half, 0, 0)),
                  pl.BlockSpec((1, 1, cols), lambda i: (i // half, 0, 0)),
                  pl.BlockSpec((1, cols), lambda i: (0, 0))],
        out_specs=[pl.BlockSpec((prows, cols), lambda i: (i, 0)),
                   pl.BlockSpec((1, cols), lambda i: (0, 0))],
        out_shape=[jax.ShapeDtypeStruct((n2l // RADIX * PITCH, cols), F32), jax.ShapeDtypeStruct((1, cols), F32)],
        compiler_params=_cparams(("arbitrary",)),
        name="filt_time",
    )(hdn, w3d, b3d, delta2)


def _filt_spec_kernel(k_ref, ss_ref, fa_ref, fr_ref, fi_ref, twr_ref, twi_ref, o_ref, a_ref):
    r = RADIX
    scale = lax.rsqrt(ss_ref[...] + 1e-12)

    def a_body(i, c):
        n2 = 2 * i
        rhs = jnp.concatenate([k_ref[pl.ds(n2, r, stride=PITCH), :], k_ref[pl.ds(n2 + 1, r, stride=PITCH), :]],
                              axis=1).astype(BF16)
        out = jnp.dot(fa_ref[...], rhs, preferred_element_type=F32)
        _store_pair(a_ref, pl.ds(n2, r, stride=PITCH), out[:r, :LANES], out[r:, :LANES])
        _store_pair(a_ref, pl.ds(n2 + 1, r, stride=PITCH), out[:r, LANES:], out[r:, LANES:])
        return c

    lax.fori_loop(0, r // 2, a_body, 0, unroll=DFT_UNROLL_STRIDED // 2)

    def b_body(k1, c):
        r0 = pl.multiple_of(k1 * PITCH, 8)
        o0 = pl.multiple_of(k1 * r, r)
        blk = _load_pair(a_ref, pl.ds(r0, r))
        sr, si = _cmatmul(_twiddled(fr_ref[...], fi_ref[...], twr_ref[pl.ds(k1, 1), :], twi_ref[pl.ds(k1, 1), :]), blk)
        o_ref[0, pl.ds(o0, r), :] = (sr * scale).astype(BF16)
        o_ref[1, pl.ds(o0, r), :] = (si * scale).astype(BF16)
        return c

    lax.fori_loop(0, r, b_body, 0, unroll=DFT_UNROLL_STRIDED)


def _filt_spec(ktime, ss, fa_real, fr, fi, twr, twi):
    prows, cols = ktime.shape
    n2l = prows // PITCH * RADIX
    whole = lambda j: (0, 0)
    return pl.pallas_call(
        _filt_spec_kernel,
        grid=(cols // LANES,),
        in_specs=[pl.BlockSpec((prows, LANES), lambda j: (0, j)),
                  pl.BlockSpec((1, LANES), lambda j: (0, j)),
                  _resident(fa_real.shape, whole), _resident(fr.shape, whole), _resident(fi.shape, whole),
                  _resident(twr.shape, whole), _resident(twi.shape, whole)],
        out_specs=pl.BlockSpec((2, n2l, LANES), lambda j: (0, 0, j)),
        out_shape=jax.ShapeDtypeStruct((2, n2l, cols), BF16),
        scratch_shapes=[pltpu.VMEM((2, prows, LANES), F32)],
        compiler_params=_cparams(("arbitrary",)),
        name="filt_spec",
    )(ktime, ss, fa_real, fr, fi, twr, twi)


def _short_conv_rows(u_ref, bi, r0, nrows, seq, w_ref, b_ref):
    edge = 16
    cur = u_ref[bi, pl.ds(r0, nrows), :].astype(F32)
    before = u_ref[bi, pl.ds(pl.multiple_of(jnp.maximum(r0 - edge, 0), edge), edge), :].astype(F32)
    after = u_ref[bi, pl.ds(pl.multiple_of(jnp.minimum(r0 + nrows, seq - edge), edge), edge), :].astype(F32)
    prev_edge = jnp.where(r0 > 0, before[edge - 1:edge, :], 0.0)
    next_edge = jnp.where(r0 + nrows < seq, after[0:1, :], 0.0)
    w0, w1, w2 = w_ref[0:1, :], w_ref[1:2, :], w_ref[2:3, :]
    out = b_ref[...] + w0 * pltpu.roll(cur, 1, 0) + w1 * cur + w2 * pltpu.roll(cur, nrows - 1, 0)
    row = lax.broadcasted_iota(jnp.int32, (8, cur.shape[1]), 0)
    first = out[0:8] + jnp.where(row == 0, w0 * (prev_edge - cur[nrows - 1:nrows]), 0.0)
    last = out[nrows - 8:] + jnp.where(row == 7, w2 * (next_edge - cur[0:1]), 0.0)
    return jnp.concatenate([first, out[8:nrows - 8], last], axis=0)


def _hy_conv_kernel(z_ref, zw_ref, zb_ref, u_ref, uw_ref, ub_ref, ks_hbm, hb_ref,
                    fa_ref, fc_ref, fr_ref, fi_ref, twr_ref, twi_ref, o_ref, a_ref, v_ref, ks_ref, ks_sem,
                    *, conv_in, ks_tile0):
    r = RADIX
    half = r // 2
    nb, seq, _ = z_ref.shape
    nchunk = seq // CONV_ROWS
    blocks = CONV_ROWS // r

    ks_col = pl.multiple_of((ks_tile0 + pl.program_id(0)) * LANES, LANES)
    ks_copy = pltpu.make_async_copy(ks_hbm.at[:, :, pl.ds(ks_col, LANES)], ks_ref, ks_sem)
    ks_copy.start()

    def load_body(c, carry):
        r0 = pl.multiple_of(c * CONV_ROWS, CONV_ROWS)
        for bi in range(nb):
            if conv_in:
                v = _short_conv_rows(z_ref, bi, r0, CONV_ROWS, seq, zw_ref, zb_ref)
                o_ref[bi, pl.ds(r0, CONV_ROWS), :] = v.astype(BF16)
            else:
                v = z_ref[bi, pl.ds(r0, CONV_ROWS), :].astype(F32)
            for j in range(blocks):
                p0 = pl.multiple_of((c * blocks + j) * PITCH, 8)
                v_ref[bi, pl.ds(p0, r), :] = v[j * r:(j + 1) * r]
        return carry

    lax.fori_loop(0, nchunk, load_body, 0)

    def a_body(n2, carry):
        re, im = _cmatmul(fa_ref[...], _load_pair(v_ref, pl.ds(n2, half, stride=PITCH)))
        _store_pair(a_ref, pl.ds(n2, r, stride=PITCH), re, im)
        return carry

    lax.fori_loop(0, r, a_body, 0, unroll=DFT_UNROLL_STRIDED)

    def b_forward(k1):
        r0 = pl.multiple_of(k1 * PITCH, 8)
        s0 = pl.multiple_of(k1 * r, r)
        blk = _load_pair(a_ref, pl.ds(r0, r))
        sr, si = _cmatmul(_twiddled(fr_ref[...], fi_ref[...], twr_ref[pl.ds(k1, 1), :], twi_ref[pl.ds(k1, 1), :]), blk)
        kr = ks_ref[0, pl.ds(s0, r), :].astype(F32)
        ki = ks_ref[1, pl.ds(s0, r), :].astype(F32)
        return jnp.concatenate([sr * kr - si * ki, sr * ki + si * kr], axis=1).astype(BF16)

    def b_inverse(k1, p):
        tr, ti = _cmatmul(fc_ref[...], p)
        _store_pair(a_ref, pl.ds(pl.multiple_of(k1 * PITCH, 8), r), tr, ti)

    def b_body(k1, p):
        p_next = b_forward(k1 + 1)
        b_inverse(k1, p)
        return p_next

    ks_copy.wait()
    b_inverse(r - 1, lax.fori_loop(0, r - 1, b_body, b_forward(0), unroll=DFT_UNROLL))

    inv_n = 1.0 / (r * r)

    def c_body(n2, carry):
        g = _twiddled(fr_ref[0:half, :], fi_ref[0:half, :], twr_ref[pl.ds(n2, 1), :], twi_ref[pl.ds(n2, 1), :],
                      inv_n, -inv_n)
        yr, yi = _cmatmul(g, _load_pair(a_ref, pl.ds(n2, r, stride=PITCH)))
        _store_pair(v_ref, pl.ds(n2, half, stride=PITCH), yr, yi)
        return carry

    lax.fori_loop(0, r, c_body, 0, unroll=DFT_UNROLL_STRIDED)

    def out_body(c, carry):
        r0 = pl.multiple_of(c * CONV_ROWS, CONV_ROWS)
        for bi in range(nb):
            gate = _short_conv_rows(u_ref, bi, r0, CONV_ROWS, seq, uw_ref, ub_ref)
            conv = jnp.concatenate(
                [v_ref[bi, pl.ds(pl.multiple_of((c * blocks + j) * PITCH, 8), r), :] for j in range(blocks)], axis=0)
            zp = (o_ref if conv_in else z_ref)[bi, pl.ds(r0, CONV_ROWS), :].astype(F32)
            o_ref[bi, pl.ds(r0, CONV_ROWS), :] = (gate * (conv + hb_ref[...] * zp)).astype(BF16)
        return carry

    lax.fori_loop(0, nchunk, out_body, 0)


def _hy_conv(z, z_part, u, u_part, order, short_w, short_b, kspec, hy_bias, tables, conv_in):
    nb, seq, _ = u.shape
    width = hy_bias.shape[1]
    tiles = width // LANES
    fa, fc, fr, fi, twr, twi = tables
    whole = lambda j: (0, 0)
    zcol = lambda j: (0, 0, z_part * tiles + j)
    ucol = lambda j: (0, 0, u_part * tiles + j)
    return pl.pallas_call(
        functools.partial(_hy_conv_kernel, conv_in=conv_in, ks_tile0=order * tiles),
        grid=(tiles,),
        in_specs=[
            pl.BlockSpec((nb, seq, LANES), zcol),
            pl.BlockSpec((3, LANES), lambda j: (0, z_part * tiles + j)),
            pl.BlockSpec((1, LANES), lambda j: (0, z_part * tiles + j)),
            pl.BlockSpec((nb, seq, LANES), ucol),
            pl.BlockSpec((3, LANES), lambda j: (0, u_part * tiles + j)),
            pl.BlockSpec((1, LANES), lambda j: (0, u_part * tiles + j)),
            pl.BlockSpec(memory_space=pl.ANY),
            pl.BlockSpec((1, LANES), lambda j: (0, j)),
            _resident(fa.shape, whole), _resident(fc.shape, whole), _resident(fr.shape, whole),
            _resident(fi.shape, whole), _resident(twr.shape, whole), _resident(twi.shape, whole),
        ],
        out_specs=_resident((nb, seq, LANES), lambda j: (0, 0, j)),
        out_shape=jax.ShapeDtypeStruct((nb, seq, width), BF16),
        scratch_shapes=[pltpu.VMEM((2, 2 * seq // RADIX * PITCH, LANES), F32),
                        pltpu.VMEM((nb, seq // RADIX * PITCH, LANES), F32),
                        pltpu.VMEM((2, 2 * seq, LANES), BF16),
                        pltpu.SemaphoreType.DMA(())],
        compiler_params=_cparams(("arbitrary",)),
        name=f"hy_conv{order}",
    )(z, short_w, short_b, u, short_w, short_b, kspec, hy_bias[order:order + 1], fa, fc, fr, fi, twr, twi)


def _hyena(hyu, short_w, short_b, fw1, fb1, fw2, fb2, fw3, fb3, freq, hy_bias):
    nb, seq, _ = hyu.shape
    assert nb == 2 and 2 * seq == RADIX * RADIX
    width = hy_bias.shape[1]
    hid = fw2.shape[0]
    fr, fi, twr, twi = _dft_tables()
    half = RADIX // 2
    fa_real = jnp.concatenate([fr, fi], axis=0).astype(BF16)
    fa_half = jnp.concatenate([fr[:, :half], fi[:, :half]], axis=0).astype(BF16)
    fc = jnp.concatenate([fr, -fi], axis=0).astype(BF16)

    zz = jnp.asarray(_position_features(seq))
    hdn = _filt_mlp(zz, fw1, fb1, fw2, fb2, freq).reshape(2 * seq, 4 * hid)
    w3d = fw3.reshape(hid, 2, HY_ORDER * width).transpose(1, 0, 2)
    b3d = fb3.reshape(2, 1, HY_ORDER * width)
    max_decay = math.log(HY_DECAY_TARGET) / HY_FAST_DECAY_PCT
    min_decay = math.log(HY_DECAY_TARGET) / HY_SLOW_DECAY_PCT
    delta = np.abs(np.linspace(min_decay, max_decay, width)).astype(np.float32)
    delta2 = jnp.asarray(np.tile(delta, HY_ORDER)[None, :])
    ktime, ss = _filt_time(hdn, w3d, b3d, delta2, seq)
    kspec = _filt_spec(ktime, ss, fa_real, fr, fi, twr, twi)

    tables = (fa_half, fc, fr, fi, twr, twi)
    sb = short_b.reshape(1, -1)
    z1 = _hy_conv(hyu, 0, hyu, 1, 0, short_w, sb, kspec, hy_bias, tables, conv_in=True)
    return _hy_conv(z1, 0, hyu, 2, 1, short_w, sb, kspec, hy_bias, tables, conv_in=False)


def kernel(x, ln1_g, ln1_b, ffn1_w1, ffn1_w3, ffn1_w2, w_in, b_gate, na_rpb, hy_short_w, hy_short_b,
           hy_filt_w1, hy_filt_b1, hy_filt_w2, hy_filt_b2, hy_filt_w3, hy_filt_b3, hy_filt_freq, hy_bias,
           w_pa, w_pb, w_out, ln2_g, ln2_b, ffn2_w1, ffn2_w3, ffn2_w2, ln3_g, ln3_b):
    assert ln1_g.shape[0] == DEPTH
    b, seq, d = x.shape
    rows = seq // GRID_W
    na_width = NA_HEADS * NA_HEAD_DIM
    n_qkv = 3 * na_width
    n_hy = hy_short_w.shape[2]
    vec = lambda a: a[0].reshape(1, -1)

    x1, qkv, hyu, gates = _dense_in(
        x.reshape(b * seq, d), ffn1_w1[0], ffn1_w3[0], ffn1_w2[0], vec(ln1_g), vec(ln1_b),
        w_in[0], vec(b_gate), n_qkv, n_hy)

    bias = _na_bias(na_rpb[0], rows)
    ya = _na_attn(qkv.reshape(b, seq, n_qkv), bias, rows)

    yb = _hyena(hyu.reshape(b, seq, n_hy), hy_short_w[0], hy_short_b[0], hy_filt_w1[0], hy_filt_b1[0],
                hy_filt_w2[0], hy_filt_b2[0], hy_filt_w3[0], hy_filt_b3[0], hy_filt_freq[0], hy_bias[0])

    out = _dense_out(
        x1, ya.reshape(b * seq, na_width), yb.reshape(b * seq, -1), gates,
        w_pa[0], w_pb[0], w_out[0], vec(ln2_g), vec(ln2_b),
        ffn2_w1[0], ffn2_w3[0], ffn2_w2[0], vec(ln3_g), vec(ln3_b))
    return out.reshape(b, seq, d)
```

```python
import functools
import math

import numpy as np
import jax
import jax.numpy as jnp
from jax import lax
from jax.experimental import pallas as pl
from jax.experimental.pallas import tpu as pltpu

F32 = jnp.float32
BF16 = jnp.bfloat16

GRID_W = 64
NA_HEADS = 8
NA_HEAD_DIM = 64
NA_KH = 8
NA_KW = 16
HY_ORDER = 2
HY_EMB_DIM = 33
HY_FAST_DECAY_PCT = 0.3
HY_SLOW_DECAY_PCT = 1.5
HY_DECAY_TARGET = 1e-2
DEPTH = 1
DN_ALPHA = (2 * DEPTH) ** 0.25
LN_EPS = 1e-5

LANES = 128
VMEM_LIMIT = 60 * 1024 * 1024

ROW_TILE = 512
FF_CHUNK = 256
PROJ_CHUNK = 512
CAST_ROWS = 64
CAST_SLOTS = 4
NA_QROWS = 4
NA_KROWS = NA_QROWS + NA_KH
RADIX = 128
PITCH = RADIX + 8
DFT_UNROLL = 8
DFT_UNROLL_STRIDED = 16
CONV_ROWS = 512
NEG = -1e30


def _cparams(sem):
    return pltpu.CompilerParams(dimension_semantics=sem, vmem_limit_bytes=VMEM_LIMIT)


def _resident(shape, index_map):
    return pl.BlockSpec(shape, index_map, pipeline_mode=pl.Buffered(1))


def _cast_weights_in(pairs, stage_ref, sem):
    ahead = CAST_SLOTS - 1
    counts = [src.shape[0] // CAST_ROWS for src, _ in pairs]
    assert all(n >= ahead for n in counts)
    bases = [sum(counts[:w]) for w in range(len(pairs))]

    def copy(w, c, g):
        src, cols = pairs[w][0], pairs[w][0].shape[1]
        slot = g % CAST_SLOTS
        return pltpu.make_async_copy(src.at[pl.ds(c * CAST_ROWS, CAST_ROWS), :],
                                     stage_ref.at[slot, :, pl.ds(0, cols)], sem.at[slot])

    for g in range(ahead):
        copy(0, g, g).start()
    for w, (src, dst) in enumerate(pairs):
        n, cols = counts[w], src.shape[1]

        def body(c, carry, w=w, n=n, cols=cols, dst=dst):
            g = bases[w] + c

            @pl.when(c + ahead < n)
            def _():
                copy(w, c + ahead, g + ahead).start()

            if w + 1 < len(pairs):
                @pl.when(c + ahead >= n)
                def _():
                    copy(w + 1, c + ahead - n, g + ahead).start()

            copy(w, c, g).wait()
            r0 = pl.multiple_of(c * CAST_ROWS, CAST_ROWS)
            dst[pl.ds(r0, CAST_ROWS), :] = stage_ref[g % CAST_SLOTS, :, 0:cols].astype(BF16)
            return carry

        lax.fori_loop(0, n, body, 0)


def _layer_norm(r, g, b):
    mu = jnp.mean(r, axis=-1, keepdims=True)
    d = r - mu
    var = jnp.mean(d * d, axis=-1, keepdims=True)
    return d * lax.rsqrt(var + LN_EPS) * g + b


def _swiglu_ln(x, xb_ref, acc_ref, w1_ref, w3_ref, w2_ref, g_ref, b_ref):
    d_ff = w1_ref.shape[1]
    xb_ref[...] = x.astype(BF16)
    for j in range(d_ff // FF_CHUNK):
        sl = slice(j * FF_CHUNK, (j + 1) * FF_CHUNK)
        a = jnp.dot(xb_ref[...], w1_ref[:, sl], preferred_element_type=F32)
        b = jnp.dot(xb_ref[...], w3_ref[:, sl], preferred_element_type=F32)
        h = (a * jax.nn.sigmoid(a) * b).astype(BF16)
        part = jnp.dot(h, w2_ref[sl, :], preferred_element_type=F32)
        if j == 0:
            acc_ref[...] = part
        else:
            acc_ref[...] += part
    return _layer_norm(DN_ALPHA * x + 0.5 * acc_ref[...], g_ref[...], b_ref[...])


def _dense_in_kernel(x_ref, w1_hbm, w3_hbm, w2_hbm, g_ref, b_ref, win_hbm, bg_ref,
                     x1_ref, qkv_ref, hyu_ref, gate_ref,
                     xb_ref, acc_ref, w1_ref, w3_ref, w2_ref, win_ref, stage_ref, sem):
    @pl.when(pl.program_id(0) == 0)
    def _():
        _cast_weights_in(((w1_hbm, w1_ref), (w3_hbm, w3_ref), (w2_hbm, w2_ref), (win_hbm, win_ref)), stage_ref, sem)

    x1 = _swiglu_ln(x_ref[...], xb_ref, acc_ref, w1_ref, w3_ref, w2_ref, g_ref, b_ref)
    x1_ref[...] = x1
    xb_ref[...] = x1.astype(BF16)
    n_qkv = qkv_ref.shape[1]
    n_hy = hyu_ref.shape[1]
    n_head = n_qkv // 3
    for j in range(win_ref.shape[1] // PROJ_CHUNK):
        c0 = j * PROJ_CHUNK
        p = jnp.dot(xb_ref[...], win_ref[:, c0:c0 + PROJ_CHUNK], preferred_element_type=F32)
        if c0 < n_qkv:
            if c0 < n_head:
                p = p * (NA_HEAD_DIM ** -0.5)
            qkv_ref[:, c0:c0 + PROJ_CHUNK] = p.astype(BF16)
        elif c0 < n_qkv + n_hy:
            hyu_ref[:, c0 - n_qkv:c0 - n_qkv + PROJ_CHUNK] = p.astype(BF16)
        else:
            g0 = c0 - n_qkv - n_hy
            gate_ref[:, g0:g0 + PROJ_CHUNK] = jax.nn.sigmoid(p + bg_ref[:, g0:g0 + PROJ_CHUNK]).astype(BF16)


def _dense_in(x2d, w1, w3, w2, g, b, w_in, b_gate, n_qkv, n_hy):
    n, d = x2d.shape
    d_ff = w1.shape[1]
    n_gate = w_in.shape[1] - n_qkv - n_hy
    assert n % ROW_TILE == 0 and d_ff % FF_CHUNK == 0
    assert n_qkv % (3 * PROJ_CHUNK) == 0 and n_hy % PROJ_CHUNK == 0 and n_gate % PROJ_CHUNK == 0
    weights = (w1, w3, w2, w_in)
    assert all(w.dtype == F32 and w.shape[0] % CAST_ROWS == 0 for w in weights)
    row = lambda i: (i, 0)
    whole = lambda i: (0, 0)
    hbm = pl.BlockSpec(memory_space=pl.ANY)
    return pl.pallas_call(
        _dense_in_kernel,
        grid=(n // ROW_TILE,),
        in_specs=[
            pl.BlockSpec((ROW_TILE, d), row),
            hbm, hbm, hbm,
            _resident((1, d), whole), _resident((1, d), whole),
            hbm, _resident((1, n_gate), whole),
        ],
        out_specs=[
            pl.BlockSpec((ROW_TILE, d), row),
            pl.BlockSpec((ROW_TILE, n_qkv), row),
            pl.BlockSpec((ROW_TILE, n_hy), row),
            pl.BlockSpec((ROW_TILE, n_gate), row),
        ],
        out_shape=[
            jax.ShapeDtypeStruct((n, d), F32),
            jax.ShapeDtypeStruct((n, n_qkv), BF16),
            jax.ShapeDtypeStruct((n, n_hy), BF16),
            jax.ShapeDtypeStruct((n, n_gate), BF16),
        ],
        scratch_shapes=[pltpu.VMEM((ROW_TILE, d), BF16), pltpu.VMEM((ROW_TILE, d), F32)]
        + [pltpu.VMEM(w.shape, BF16) for w in weights]
        + [pltpu.VMEM((CAST_SLOTS, CAST_ROWS, max(w.shape[1] for w in weights)), F32),
           pltpu.SemaphoreType.DMA((CAST_SLOTS,))],
        compiler_params=_cparams(("arbitrary",)),
        name="dense_in",
    )(x2d, w1, w3, w2, g, b, w_in, b_gate)


def _dense_out_kernel(x1_ref, ya_ref, yb_ref, gate_ref, wpa_hbm, wpb_hbm, wout_hbm, g2_ref, b2_ref,
                      w1_hbm, w3_hbm, w2_hbm, g3_ref, b3_ref, o_ref,
                      xb_ref, acc_ref, wpa_ref, wpb_ref, wout_ref, w1_ref, w3_ref, w2_ref, stage_ref, sem):
    @pl.when(pl.program_id(0) == 0)
    def _():
        _cast_weights_in(((wpa_hbm, wpa_ref), (wpb_hbm, wpb_ref), (wout_hbm, wout_ref),
                          (w1_hbm, w1_ref), (w3_hbm, w3_ref), (w2_hbm, w2_ref)), stage_ref, sem)

    d = x1_ref.shape[1]
    pa = jnp.dot(ya_ref[...], wpa_ref[...], preferred_element_type=F32)
    pb = jnp.dot(yb_ref[...], wpb_ref[...], preferred_element_type=F32)
    m = gate_ref[:, :d].astype(F32) * pa + gate_ref[:, d:].astype(F32) * pb
    mix = jnp.dot(m.astype(BF16), wout_ref[...], preferred_element_type=F32)
    x2 = _layer_norm(DN_ALPHA * x1_ref[...] + mix, g2_ref[...], b2_ref[...])
    o_ref[...] = _swiglu_ln(x2, xb_ref, acc_ref, w1_ref, w3_ref, w2_ref, g3_ref, b3_ref)


def _dense_out(x1, ya, yb, gates, wpa, wpb, wout, g2, b2, w1, w3, w2, g3, b3):
    n, d = x1.shape
    weights = (wpa, wpb, wout, w1, w3, w2)
    assert all(w.dtype == F32 and w.shape[0] % CAST_ROWS == 0 for w in weights)
    row = lambda i: (i, 0)
    whole = lambda i: (0, 0)
    hbm = pl.BlockSpec(memory_space=pl.ANY)
    return pl.pallas_call(
        _dense_out_kernel,
        grid=(n // ROW_TILE,),
        in_specs=[
            pl.BlockSpec((ROW_TILE, d), row),
            pl.BlockSpec((ROW_TILE, ya.shape[1]), row),
            pl.BlockSpec((ROW_TILE, yb.shape[1]), row),
            pl.BlockSpec((ROW_TILE, gates.shape[1]), row),
            hbm, hbm, hbm,
            _resident((1, d), whole), _resident((1, d), whole),
            hbm, hbm, hbm,
            _resident((1, d), whole), _resident((1, d), whole),
        ],
        out_specs=pl.BlockSpec((ROW_TILE, d), row),
        out_shape=jax.ShapeDtypeStruct((n, d), F32),
        scratch_shapes=[pltpu.VMEM((ROW_TILE, d), BF16), pltpu.VMEM((ROW_TILE, d), F32)]
        + [pltpu.VMEM(w.shape, BF16) for w in weights]
        + [pltpu.VMEM((CAST_SLOTS, CAST_ROWS, max(w.shape[1] for w in weights)), F32),
           pltpu.SemaphoreType.DMA((CAST_SLOTS,))],
        compiler_params=_cparams(("arbitrary",)),
        name="dense_out",
    )(x1, ya, yb, gates, wpa, wpb, wout, g2, b2, w1, w3, w2, g3, b3)


def _window_start(n, k):
    return np.clip(np.arange(n) - k // 2, 0, n - k)


def _na_group_types(rows):
    last = rows - NA_QROWS
    return ((0, 0), (NA_QROWS * 2, NA_QROWS), (last, rows - NA_KROWS))


def _na_tables(rows):
    w = GRID_W
    shift = np.zeros((2 * NA_KW - 1, w, 2 * w), np.float32)
    q = np.arange(w)[:, None]
    kc = np.arange(w)[None, :]
    for dc in range(2 * NA_KW - 1):
        shift[dc, :, :w] = (kc - q + NA_KW - 1 == dc).astype(np.float32)
    cs = _window_start(w, NA_KW)
    col_ok = (kc >= cs[:, None]) & (kc < cs[:, None] + NA_KW)
    rs_all = _window_start(rows, NA_KH)
    negmask = np.full((3, NA_QROWS, w, NA_KROWS, w), NEG, np.float32)
    for t, (r0, kb) in enumerate(_na_group_types(rows)):
        for i in range(NA_QROWS):
            rs = rs_all[r0 + i]
            for j in range(NA_KROWS):
                if rs <= kb + j < rs + NA_KH:
                    negmask[t, i, :, j, :] = np.where(col_ok, 0.0, NEG)
    return shift, negmask.reshape(3, NA_QROWS * w, NA_KROWS * w)


def _na_bias_kernel(rpb_ref, shift_ref, neg_ref, o_ref, u_ref, *, rows):
    h = pl.program_id(0)
    n_dr = 2 * NA_KH - 1
    n_dc = 2 * NA_KW - 1
    w = GRID_W

    def dr_body(dr, carry):
        base = (h * n_dr + dr) * n_dc
        left = rpb_ref[base] * shift_ref[0]
        for dc in range(1, n_dc):
            left = left + rpb_ref[base + dc] * shift_ref[dc]
        u_ref[0, dr] = left
        u_ref[1, dr] = pltpu.roll(left, w, 1)
        return carry

    lax.fori_loop(0, n_dr, dr_body, 0)
    rs_all = _window_start(rows, NA_KH)
    for t, (r0, kb) in enumerate(_na_group_types(rows)):
        for i in range(NA_QROWS):
            rs = rs_all[r0 + i]
            for jj in range(NA_KROWS // 2):
                val = neg_ref[t, i * w:(i + 1) * w, jj * 2 * w:(jj + 1) * 2 * w]
                for s in range(2):
                    kr = kb + 2 * jj + s
                    if rs <= kr < rs + NA_KH:
                        val = val + u_ref[s, kr - (r0 + i) + NA_KH - 1]
                o_ref[t, 0, i * w:(i + 1) * w, jj * 2 * w:(jj + 1) * 2 * w] = val


def _na_bias(rpb, rows):
    shift, negmask = _na_tables(rows)
    nq, nk = NA_QROWS * GRID_W, NA_KROWS * GRID_W
    return pl.pallas_call(
        functools.partial(_na_bias_kernel, rows=rows),
        grid=(NA_HEADS,),
        in_specs=[
            pl.BlockSpec(memory_space=pltpu.SMEM),
            _resident(shift.shape, lambda h: (0, 0, 0)),
            _resident(negmask.shape, lambda h: (0, 0, 0)),
        ],
        out_specs=pl.BlockSpec((3, 1, nq, nk), lambda h: (0, h, 0, 0)),
        out_shape=jax.ShapeDtypeStruct((3, NA_HEADS, nq, nk), F32),
        scratch_shapes=[pltpu.VMEM((2, 2 * NA_KH - 1, GRID_W, 2 * GRID_W), F32)],
        compiler_params=_cparams(("arbitrary",)),
        name="na_bias",
    )(rpb.reshape(-1), jnp.asarray(shift), jnp.asarray(negmask))


def _na_attn_kernel(q_ref, k_ref, v_ref, bias_ref, o_ref, *, rows):
    g = pl.program_id(1)
    kb = jnp.clip(g * NA_QROWS - NA_KH // 2, 0, rows - NA_KROWS)
    k0 = pl.multiple_of(kb * GRID_W, GRID_W)
    nk = NA_KROWS * GRID_W
    dh = NA_HEAD_DIM
    lane = lax.broadcasted_iota(jnp.int32, (1, 2 * dh), 1)
    for hp in range(NA_HEADS // 2):
        cols = slice(hp * 2 * dh, (hp + 1) * 2 * dh)
        q2 = q_ref[0, :, cols]
        k2 = k_ref[0, pl.ds(k0, nk), cols]
        v2 = v_ref[0, pl.ds(k0, nk), cols]
        pair = []
        for s in range(2):
            own = (lane < dh) if s == 0 else (lane >= dh)
            sc = lax.dot_general(q2, jnp.where(own, k2, jnp.zeros_like(k2)), (((1,), (1,)), ((), ())),
                                 preferred_element_type=F32)
            sc = sc + bias_ref[0, 2 * hp + s]
            e = jnp.exp(sc - jnp.max(sc, axis=-1, keepdims=True)).astype(BF16)
            r = jnp.dot(e, jnp.where(own, v2, jnp.ones_like(v2)), preferred_element_type=F32)
            den = r[:, dh:dh + 1] if s == 0 else r[:, 0:1]
            pair.append(jnp.where(own, r / den, 0.0))
        o_ref[0, :, cols] = (pair[0] + pair[1]).astype(BF16)


def _na_attn(qkv, bias, rows):
    b, seq, three_w = qkv.shape
    width = three_w // 3
    nq, nk = NA_QROWS * GRID_W, NA_KROWS * GRID_W
    groups = rows // NA_QROWS

    def bias_idx(bi, g):
        return (jnp.where(g == 0, 0, jnp.where(g == groups - 1, 2, 1)), 0, 0, 0)

    return pl.pallas_call(
        functools.partial(_na_attn_kernel, rows=rows),
        grid=(b, groups),
        in_specs=[
            pl.BlockSpec((1, nq, width), lambda bi, g: (bi, g, 0)),
            pl.BlockSpec((1, seq, width), lambda bi, g: (bi, 0, 1)),
            pl.BlockSpec((1, seq, width), lambda bi, g: (bi, 0, 2)),
            pl.BlockSpec((1, NA_HEADS, nq, nk), bias_idx),
        ],
        out_specs=pl.BlockSpec((1, nq, width), lambda bi, g: (bi, g, 0)),
        out_shape=jax.ShapeDtypeStruct((b, seq, width), BF16),
        compiler_params=_cparams(("arbitrary", "arbitrary")),
        name="na_attn",
    )(qkv, qkv, qkv, bias)


def _dft_tables():
    k = np.arange(RADIX)
    ang = 2.0 * np.pi * np.outer(k, k) / RADIX
    fr, fi = np.cos(ang), -np.sin(ang)
    ang_t = 2.0 * np.pi * np.outer(k, k) / (RADIX * RADIX)
    twr, twi = np.cos(ang_t), -np.sin(ang_t)
    f32 = lambda a: jnp.asarray(a, F32)
    return f32(fr), f32(fi), f32(twr), f32(twi)


def _cmatmul(c_rows, x_lanes):
    m = c_rows.shape[0] // 2
    r = jnp.dot(c_rows, x_lanes, preferred_element_type=F32)
    return r[:m, :LANES] - r[m:, LANES:], r[:m, LANES:] + r[m:, :LANES]


def _load_pair(ref, rows):
    return jnp.concatenate([ref[0, rows, :], ref[1, rows, :]], axis=1).astype(BF16)


def _store_pair(ref, rows, re, im):
    ref[0, rows, :] = re
    ref[1, rows, :] = im


def _twiddled(fr, fi, twr, twi, scale_re=1.0, scale_im=1.0):
    return jnp.concatenate([(fr * twr - fi * twi) * scale_re, (fr * twi + fi * twr) * scale_im], axis=0).astype(BF16)


def _position_features(seq):
    t = np.linspace(0.0, 1.0, seq)[:, None]
    bands = (HY_EMB_DIM - 1) // 2
    w = (2.0 * math.pi / seq) * np.arange(seq)[:, None]
    f = np.linspace(1e-4, bands - 1, bands)[None, :]
    z = np.concatenate([t, np.cos(f * w), -np.sin(f * w)], axis=-1)
    rev = seq - np.arange(seq)
    rev[0] = 0
    zz = np.zeros((seq, 2 * LANES), np.float32)
    zz[:, :HY_EMB_DIM] = z
    zz[:, LANES:LANES + HY_EMB_DIM] = z[rev]
    return zz


def _filt_mlp_kernel(z_ref, w1_ref, b1_ref, w2_ref, b2_ref, fr_ref, o_ref):
    hp = lax.Precision.HIGHEST
    fr = fr_ref[...]
    hid = LANES // 2
    h = jnp.sin(fr * (jnp.dot(z_ref[...], w1_ref[...], precision=hp, preferred_element_type=F32) + b1_ref[...]))
    h = jnp.sin(fr * (jnp.dot(h, w2_ref[...], precision=hp, preferred_element_type=F32) + b2_ref[...]))
    hi = h.astype(BF16).astype(F32)
    lo = h - hi
    hi_sw, lo_sw = pltpu.roll(hi, hid, 1), pltpu.roll(lo, hid, 1)
    low = lax.broadcasted_iota(jnp.int32, (1, LANES), 1) < hid
    o_ref[0, :, :LANES] = jnp.where(low, hi, lo_sw).astype(BF16)
    o_ref[0, :, LANES:] = jnp.where(low, hi, 0.0).astype(BF16)
    o_ref[1, :, :LANES] = jnp.where(low, hi_sw, lo).astype(BF16)
    o_ref[1, :, LANES:] = jnp.where(low, hi_sw, 0.0).astype(BF16)


def _filt_mlp(zz, fw1, fb1, fw2, fb2, freq):
    seq = zz.shape[0]
    hid = fw2.shape[0]
    assert 2 * hid == LANES
    w1 = jnp.zeros((2 * LANES, LANES), F32)
    w1 = w1.at[:HY_EMB_DIM, :hid].set(fw1).at[LANES:LANES + HY_EMB_DIM, hid:].set(fw1)
    w2 = jnp.zeros((LANES, LANES), F32).at[:hid, :hid].set(fw2).at[hid:, hid:].set(fw2)
    twice = lambda v: jnp.concatenate([v, v]).reshape(1, LANES)
    rows = 2048
    whole = lambda i: (0, 0)
    return pl.pallas_call(
        _filt_mlp_kernel,
        grid=(seq // rows,),
        in_specs=[pl.BlockSpec((rows, 2 * LANES), lambda i: (i, 0)),
                  pl.BlockSpec(w1.shape, whole), pl.BlockSpec((1, LANES), whole),
                  pl.BlockSpec(w2.shape, whole), pl.BlockSpec((1, LANES), whole), pl.BlockSpec((1, LANES), whole)],
        out_specs=pl.BlockSpec((2, rows, 2 * LANES), lambda i: (0, i, 0)),
        out_shape=jax.ShapeDtypeStruct((2, seq, 2 * LANES), BF16),
        compiler_params=_cparams(("arbitrary",)),
        name="filt_mlp",
    )(zz, w1, twice(fb1), w2, twice(fb2), twice(freq))


def _filt_time_kernel(h_ref, w3_ref, b3_ref, delta_ref, k_ref, ss_ref, *, seq):
    i = pl.program_id(0)
    rows = h_ref.shape[0]
    w = w3_ref[0]
    w_hi = w.astype(BF16)
    w_lo = (w - w_hi.astype(F32)).astype(BF16)
    wcat = jnp.concatenate([w_hi, w_hi, w_lo, jnp.zeros_like(w_lo)], axis=0)
    hf = jnp.dot(h_ref[...], wcat, preferred_element_type=F32) + b3_ref[0]
    m = i * rows + lax.broadcasted_iota(jnp.int32, (rows, 1), 0)
    pos = jnp.where(m < seq, m, 2 * seq - m).astype(F32)
    t = pos / (seq - 1)
    k = jnp.where(m == seq, 0.0, hf * jnp.exp(-t * delta_ref[...]))
    for j in range(rows // RADIX):
        k_ref[j * PITCH:j * PITCH + RADIX, :] = k[j * RADIX:(j + 1) * RADIX]
        k_ref[j * PITCH + RADIX:(j + 1) * PITCH, :] = jnp.zeros((PITCH - RADIX, k.shape[1]), F32)
    part = jnp.sum(k * k, axis=0, keepdims=True)

    @pl.when(i == 0)
    def _():
        ss_ref[...] = part

    @pl.when(i != 0)
    def _():
        ss_ref[...] += part


def _filt_time(hdn, w3d, b3d, delta2, seq):
    n2l, packed = hdn.shape
    _, hid, cols = w3d.shape
    assert packed == 4 * hid
    rows = 1024
    half = seq // rows
    prows = rows // RADIX * PITCH
    return pl.pallas_call(
        functools.partial(_filt_time_kernel, seq=seq),
        grid=(n2l // rows,),
        in_specs=[pl.BlockSpec((rows, packed), lambda i: (i, 0)),
                  pl.BlockSpec((1, hid, cols), lambda i: (i // half, 0, 0)),
                  pl.BlockSpec((1, 1, cols), lambda i: (i // half, 0, 0)),
                  pl.BlockSpec((1, cols), lambda i: (0, 0))],
        out_specs=[pl.BlockSpec((prows, cols), lambda i: (i, 0)),
                   pl.BlockSpec((1, cols), lambda i: (0, 0))],
        out_shape=[jax.ShapeDtypeStruct((n2l // RADIX * PITCH, cols), F32), jax.ShapeDtypeStruct((1, cols), F32)],
        compiler_params=_cparams(("arbitrary",)),
        name="filt_time",
    )(hdn, w3d, b3d, delta2)


def _filt_spec_kernel(k_ref, ss_ref, fa_ref, fr_ref, fi_ref, twr_ref, twi_ref, o_ref, a_ref):
    r = RADIX
    scale = lax.rsqrt(ss_ref[...] + 1e-12)

    def a_body(i, c):
        n2 = 2 * i
        rhs = jnp.concatenate([k_ref[pl.ds(n2, r, stride=PITCH), :], k_ref[pl.ds(n2 + 1, r, stride=PITCH), :]],
                              axis=1).astype(BF16)
        out = jnp.dot(fa_ref[...], rhs, preferred_element_type=F32)
        _store_pair(a_ref, pl.ds(n2, r, stride=PITCH), out[:r, :LANES], out[r:, :LANES])
        _store_pair(a_ref, pl.ds(n2 + 1, r, stride=PITCH), out[:r, LANES:], out[r:, LANES:])
        return c

    lax.fori_loop(0, r // 2, a_body, 0, unroll=DFT_UNROLL_STRIDED // 2)

    def b_body(k1, c):
        r0 = pl.multiple_of(k1 * PITCH, 8)
        o0 = pl.multiple_of(k1 * r, r)
        blk = _load_pair(a_ref, pl.ds(r0, r))
        sr, si = _cmatmul(_twiddled(fr_ref[...], fi_ref[...], twr_ref[pl.ds(k1, 1), :], twi_ref[pl.ds(k1, 1), :]), blk)
        o_ref[0, pl.ds(o0, r), :] = (sr * scale).astype(BF16)
        o_ref[1, pl.ds(o0, r), :] = (si * scale).astype(BF16)
        return c

    lax.fori_loop(0, r, b_body, 0, unroll=DFT_UNROLL_STRIDED)


def _filt_spec(ktime, ss, fa_real, fr, fi, twr, twi):
    prows, cols = ktime.shape
    n2l = prows // PITCH * RADIX
    whole = lambda j: (0, 0)
    return pl.pallas_call(
        _filt_spec_kernel,
        grid=(cols // LANES,),
        in_specs=[pl.BlockSpec((prows, LANES), lambda j: (0, j)),
                  pl.BlockSpec((1, LANES), lambda j: (0, j)),
                  _resident(fa_real.shape, whole), _resident(fr.shape, whole), _resident(fi.shape, whole),
                  _resident(twr.shape, whole), _resident(twi.shape, whole)],
        out_specs=pl.BlockSpec((2, n2l, LANES), lambda j: (0, 0, j)),
        out_shape=jax.ShapeDtypeStruct((2, n2l, cols), BF16),
        scratch_shapes=[pltpu.VMEM((2, prows, LANES), F32)],
        compiler_params=_cparams(("arbitrary",)),
        name="filt_spec",
    )(ktime, ss, fa_real, fr, fi, twr, twi)


def _short_conv_rows(u_ref, bi, r0, nrows, seq, w_ref, b_ref):
    edge = 16
    cur = u_ref[bi, pl.ds(r0, nrows), :].astype(F32)
    before = u_ref[bi, pl.ds(pl.multiple_of(jnp.maximum(r0 - edge, 0), edge), edge), :].astype(F32)
    after = u_ref[bi, pl.ds(pl.multiple_of(jnp.minimum(r0 + nrows, seq - edge), edge), edge), :].astype(F32)
    prev_edge = jnp.where(r0 > 0, before[edge - 1:edge, :], 0.0)
    next_edge = jnp.where(r0 + nrows < seq, after[0:1, :], 0.0)
    w0, w1, w2 = w_ref[0:1, :], w_ref[1:2, :], w_ref[2:3, :]
    out = b_ref[...] + w0 * pltpu.roll(cur, 1, 0) + w1 * cur + w2 * pltpu.roll(cur, nrows - 1, 0)
    row = lax.broadcasted_iota(jnp.int32, (8, cur.shape[1]), 0)
    first = out[0:8] + jnp.where(row == 0, w0 * (prev_edge - cur[nrows - 1:nrows]), 0.0)
    last = out[nrows - 8:] + jnp.where(row == 7, w2 * (next_edge - cur[0:1]), 0.0)
    return jnp.concatenate([first, out[8:nrows - 8], last], axis=0)


def _hy_conv_kernel(z_ref, zw_ref, zb_ref, u_ref, uw_ref, ub_ref, ks_hbm, hb_ref,
                    fa_ref, fc_ref, fr_ref, fi_ref, twr_ref, twi_ref, o_ref, a_ref, v_ref, ks_ref, ks_sem,
                    *, conv_in, ks_tile0):
    r = RADIX
    half = r // 2
    nb, seq, _ = z_ref.shape
    nchunk = seq // CONV_ROWS
    blocks = CONV_ROWS // r

    ks_col = pl.multiple_of((ks_tile0 + pl.program_id(0)) * LANES, LANES)
    ks_copy = pltpu.make_async_copy(ks_hbm.at[:, :, pl.ds(ks_col, LANES)], ks_ref, ks_sem)
    ks_copy.start()

    def load_body(c, carry):
        r0 = pl.multiple_of(c * CONV_ROWS, CONV_ROWS)
        for bi in range(nb):
            if conv_in:
                v = _short_conv_rows(z_ref, bi, r0, CONV_ROWS, seq, zw_ref, zb_ref)
                o_ref[bi, pl.ds(r0, CONV_ROWS), :] = v.astype(BF16)
            else:
                v = z_ref[bi, pl.ds(r0, CONV_ROWS), :].astype(F32)
            for j in range(blocks):
                p0 = pl.multiple_of((c * blocks + j) * PITCH, 8)
                v_ref[bi, pl.ds(p0, r), :] = v[j * r:(j + 1) * r]
        return carry

    lax.fori_loop(0, nchunk, load_body, 0)

    def a_body(n2, carry):
        re, im = _cmatmul(fa_ref[...], _load_pair(v_ref, pl.ds(n2, half, stride=PITCH)))
        _store_pair(a_ref, pl.ds(n2, r, stride=PITCH), re, im)
        return carry

    lax.fori_loop(0, r, a_body, 0, unroll=DFT_UNROLL_STRIDED)

    def b_forward(k1):
        r0 = pl.multiple_of(k1 * PITCH, 8)
        s0 = pl.multiple_of(k1 * r, r)
        blk = _load_pair(a_ref, pl.ds(r0, r))
        sr, si = _cmatmul(_twiddled(fr_ref[...], fi_ref[...], twr_ref[pl.ds(k1, 1), :], twi_ref[pl.ds(k1, 1), :]), blk)
        kr = ks_ref[0, pl.ds(s0, r), :].astype(F32)
        ki = ks_ref[1, pl.ds(s0, r), :].astype(F32)
        return jnp.concatenate([sr * kr - si * ki, sr * ki + si * kr], axis=1).astype(BF16)

    def b_inverse(k1, p):
        tr, ti = _cmatmul(fc_ref[...], p)
        _store_pair(a_ref, pl.ds(pl.multiple_of(k1 * PITCH, 8), r), tr, ti)

    def b_body(k1, p):
        p_next = b_forward(k1 + 1)
        b_inverse(k1, p)
        return p_next

    ks_copy.wait()
    b_inverse(r - 1, lax.fori_loop(0, r - 1, b_body, b_forward(0), unroll=DFT_UNROLL))

    inv_n = 1.0 / (r * r)

    def c_body(n2, carry):
        g = _twiddled(fr_ref[0:half, :], fi_ref[0:half, :], twr_ref[pl.ds(n2, 1), :], twi_ref[pl.ds(n2, 1), :],
                      inv_n, -inv_n)
        yr, yi = _cmatmul(g, _load_pair(a_ref, pl.ds(n2, r, stride=PITCH)))
        _store_pair(v_ref, pl.ds(n2, half, stride=PITCH), yr, yi)
        return carry

    lax.fori_loop(0, r, c_body, 0, unroll=DFT_UNROLL_STRIDED)

    def out_body(c, carry):
        r0 = pl.multiple_of(c * CONV_ROWS, CONV_ROWS)
        for bi in range(nb):
            gate = _short_conv_rows(u_ref, bi, r0, CONV_ROWS, seq, uw_ref, ub_ref)
            conv = jnp.concatenate(
                [v_ref[bi, pl.ds(pl.multiple_of((c * blocks + j) * PITCH, 8), r), :] for j in range(blocks)], axis=0)
            zp = (o_ref if conv_in else z_ref)[bi, pl.ds(r0, CONV_ROWS), :].astype(F32)
            o_ref[bi, pl.ds(r0, CONV_ROWS), :] = (gate * (conv + hb_ref[...] * zp)).astype(BF16)
        return carry

    lax.fori_loop(0, nchunk, out_body, 0)


def _hy_conv(z, z_part, u, u_part, order, short_w, short_b, kspec, hy_bias, tables, conv_in):
    nb, seq, _ = u.shape
    width = hy_bias.shape[1]
    tiles = width // LANES
    fa, fc, fr, fi, twr, twi = tables
    whole = lambda j: (0, 0)
    zcol = lambda j: (0, 0, z_part * tiles + j)
    ucol = lambda j: (0, 0, u_part * tiles + j)
    return pl.pallas_call(
        functools.partial(_hy_conv_kernel, conv_in=conv_in, ks_tile0=order * tiles),
        grid=(tiles,),
        in_specs=[
            pl.BlockSpec((nb, seq, LANES), zcol),
            pl.BlockSpec((3, LANES), lambda j: (0, z_part * tiles + j)),
            pl.BlockSpec((1, LANES), lambda j: (0, z_part * tiles + j)),
            pl.BlockSpec((nb, seq, LANES), ucol),
            pl.BlockSpec((3, LANES), lambda j: (0, u_part * tiles + j)),
            pl.BlockSpec((1, LANES), lambda j: (0, u_part * tiles + j)),
            pl.BlockSpec(memory_space=pl.ANY),
            pl.BlockSpec((1, LANES), lambda j: (0, j)),
            _resident(fa.shape, whole), _resident(fc.shape, whole), _resident(fr.shape, whole),
            _resident(fi.shape, whole), _resident(twr.shape, whole), _resident(twi.shape, whole),
        ],
        out_specs=_resident((nb, seq, LANES), lambda j: (0, 0, j)),
        out_shape=jax.ShapeDtypeStruct((nb, seq, width), BF16),
        scratch_shapes=[pltpu.VMEM((2, 2 * seq // RADIX * PITCH, LANES), F32),
                        pltpu.VMEM((nb, seq // RADIX * PITCH, LANES), F32),
                        pltpu.VMEM((2, 2 * seq, LANES), BF16),
                        pltpu.SemaphoreType.DMA(())],
        compiler_params=_cparams(("arbitrary",)),
        name=f"hy_conv{order}",
    )(z, short_w, short_b, u, short_w, short_b, kspec, hy_bias[order:order + 1], fa, fc, fr, fi, twr, twi)


def _hyena(hyu, short_w, short_b, fw1, fb1, fw2, fb2, fw3, fb3, freq, hy_bias):
    nb, seq, _ = hyu.shape
    assert nb == 2 and 2 * seq == RADIX * RADIX
    width = hy_bias.shape[1]
    hid = fw2.shape[0]
    fr, fi, twr, twi = _dft_tables()
    half = RADIX // 2
    fa_real = jnp.concatenate([fr, fi], axis=0).astype(BF16)
    fa_half = jnp.concatenate([fr[:, :half], fi[:, :half]], axis=0).astype(BF16)
    fc = jnp.concatenate([fr, -fi], axis=0).astype(BF16)

    zz = jnp.asarray(_position_features(seq))
    hdn = _filt_mlp(zz, fw1, fb1, fw2, fb2, freq).reshape(2 * seq, 4 * hid)
    w3d = fw3.reshape(hid, 2, HY_ORDER * width).transpose(1, 0, 2)
    b3d = fb3.reshape(2, 1, HY_ORDER * width)
    max_decay = math.log(HY_DECAY_TARGET) / HY_FAST_DECAY_PCT
    min_decay = math.log(HY_DECAY_TARGET) / HY_SLOW_DECAY_PCT
    delta = np.abs(np.linspace(min_decay, max_decay, width)).astype(np.float32)
    delta2 = jnp.asarray(np.tile(delta, HY_ORDER)[None, :])
    ktime, ss = _filt_time(hdn, w3d, b3d, delta2, seq)
    kspec = _filt_spec(ktime, ss, fa_real, fr, fi, twr, twi)

    tables = (fa_half, fc, fr, fi, twr, twi)
    sb = short_b.reshape(1, -1)
    z1 = _hy_conv(hyu, 0, hyu, 1, 0, short_w, sb, kspec, hy_bias, tables, conv_in=True)
    return _hy_conv(z1, 0, hyu, 2, 1, short_w, sb, kspec, hy_bias, tables, conv_in=False)


def kernel(x, ln1_g, ln1_b, ffn1_w1, ffn1_w3, ffn1_w2, w_in, b_gate, na_rpb, hy_short_w, hy_short_b,
           hy_filt_w1, hy_filt_b1, hy_filt_w2, hy_filt_b2, hy_filt_w3, hy_filt_b3, hy_filt_freq, hy_bias,
           w_pa, w_pb, w_out, ln2_g, ln2_b, ffn2_w1, ffn2_w3, ffn2_w2, ln3_g, ln3_b):
    assert ln1_g.shape[0] == DEPTH
    b, seq, d = x.shape
    rows = seq // GRID_W
    na_width = NA_HEADS * NA_HEAD_DIM
    n_qkv = 3 * na_width
    n_hy = hy_short_w.shape[2]
    vec = lambda a: a[0].reshape(1, -1)

    x1, qkv, hyu, gates = _dense_in(
        x.reshape(b * seq, d), ffn1_w1[0], ffn1_w3[0], ffn1_w2[0], vec(ln1_g), vec(ln1_b),
        w_in[0], vec(b_gate), n_qkv, n_hy)

    bias = _na_bias(na_rpb[0], rows)
    ya = _na_attn(qkv.reshape(b, seq, n_qkv), bias, rows)

    yb = _hyena(hyu.reshape(b, seq, n_hy), hy_short_w[0], hy_short_b[0], hy_filt_w1[0], hy_filt_b1[0],
                hy_filt_w2[0], hy_filt_b2[0], hy_filt_w3[0], hy_filt_b3[0], hy_filt_freq[0], hy_bias[0])

    out = _dense_out(
        x1, ya.reshape(b * seq, na_width), yb.reshape(b * seq, -1), gates,
        w_pa[0], w_pb[0], w_out[0], vec(ln2_g), vec(ln2_b),
        ffn2_w1[0], ffn2_w3[0], ffn2_w2[0], vec(ln3_g), vec(ln3_b))
    return out.reshape(b, seq, d)
```

```python
import functools
import math

import numpy as np
import jax
import jax.numpy as jnp
from jax import lax
from jax.experimental import pallas as pl
from jax.experimental.pallas import tpu as pltpu

F32 = jnp.float32
BF16 = jnp.bfloat16

GRID_W = 64
NA_HEADS = 8
NA_HEAD_DIM = 64
NA_KH = 8
NA_KW = 16
HY_ORDER = 2
HY_EMB_DIM = 33
HY_FAST_DECAY_PCT = 0.3
HY_SLOW_DECAY_PCT = 1.5
HY_DECAY_TARGET = 1e-2
DEPTH = 1
DN_ALPHA = (2 * DEPTH) ** 0.25
LN_EPS = 1e-5
LOG2E = math.log2(math.e)

LANES = 128
VMEM_LIMIT = 60 * 1024 * 1024

ROW_TILE = 512
FF_CHUNK = 256
PROJ_CHUNK = 512
CAST_ROWS = 64
CAST_SLOTS = 4
NA_QROWS = 4
NA_KROWS = NA_QROWS + NA_KH
RADIX = 128
PITCH = RADIX + 8
DFT_UNROLL = 8
DFT_UNROLL_STRIDED = 16
CONV_ROWS = 512
NEG = -1e30


def _cparams(sem):
    return pltpu.CompilerParams(dimension_semantics=sem, vmem_limit_bytes=VMEM_LIMIT)


def _resident(shape, index_map):
    return pl.BlockSpec(shape, index_map, pipeline_mode=pl.Buffered(1))


def _cast_weights_in(pairs, stage_ref, sem):
    ahead = CAST_SLOTS - 1
    counts = [src.shape[0] // CAST_ROWS for src, _ in pairs]
    assert all(n >= ahead for n in counts)
    bases = [sum(counts[:w]) for w in range(len(pairs))]

    def copy(w, c, g):
        src, cols = pairs[w][0], pairs[w][0].shape[1]
        slot = g % CAST_SLOTS
        return pltpu.make_async_copy(src.at[pl.ds(c * CAST_ROWS, CAST_ROWS), :],
                                     stage_ref.at[slot, :, pl.ds(0, cols)], sem.at[slot])

    for g in range(ahead):
        copy(0, g, g).start()
    for w, (src, dst) in enumerate(pairs):
        n, cols = counts[w], src.shape[1]

        def body(c, carry, w=w, n=n, cols=cols, dst=dst):
            g = bases[w] + c

            @pl.when(c + ahead < n)
            def _():
                copy(w, c + ahead, g + ahead).start()

            if w + 1 < len(pairs):
                @pl.when(c + ahead >= n)
                def _():
                    copy(w + 1, c + ahead - n, g + ahead).start()

            copy(w, c, g).wait()
            r0 = pl.multiple_of(c * CAST_ROWS, CAST_ROWS)
            dst[pl.ds(r0, CAST_ROWS), :] = stage_ref[g % CAST_SLOTS, :, 0:cols].astype(BF16)
            return carry

        lax.fori_loop(0, n, body, 0)


def _layer_norm(r, g, b):
    mu = jnp.mean(r, axis=-1, keepdims=True)
    d = r - mu
    var = jnp.mean(d * d, axis=-1, keepdims=True)
    return d * lax.rsqrt(var + LN_EPS) * g + b


def _swiglu_ln(x, xb_ref, acc_ref, w1_ref, w3_ref, w2_ref, g_ref, b_ref):
    d_ff = w1_ref.shape[1]
    xb_ref[...] = x.astype(BF16)
    for j in range(d_ff // FF_CHUNK):
        sl = slice(j * FF_CHUNK, (j + 1) * FF_CHUNK)
        a = jnp.dot(xb_ref[...], w1_ref[:, sl], preferred_element_type=F32)
        b = jnp.dot(xb_ref[...], w3_ref[:, sl], preferred_element_type=F32)
        h = (a * jax.nn.sigmoid(a) * b).astype(BF16)
        part = jnp.dot(h, w2_ref[sl, :], preferred_element_type=F32)
        if j == 0:
            acc_ref[...] = part
        else:
            acc_ref[...] += part
    return _layer_norm(DN_ALPHA * x + 0.5 * acc_ref[...], g_ref[...], b_ref[...])


def _dense_in_kernel(x_ref, w1_hbm, w3_hbm, w2_hbm, g_ref, b_ref, win_hbm, bg_ref,
                     x1_ref, qkv_ref, hyu_ref, gate_ref,
                     xb_ref, acc_ref, w1_ref, w3_ref, w2_ref, win_ref, stage_ref, sem):
    @pl.when(pl.program_id(0) == 0)
    def _():
        _cast_weights_in(((w1_hbm, w1_ref), (w3_hbm, w3_ref), (w2_hbm, w2_ref), (win_hbm, win_ref)), stage_ref, sem)

    x1 = _swiglu_ln(x_ref[...], xb_ref, acc_ref, w1_ref, w3_ref, w2_ref, g_ref, b_ref)
    x1_ref[...] = x1
    xb_ref[...] = x1.astype(BF16)
    n_qkv = qkv_ref.shape[1]
    n_hy = hyu_ref.shape[1]
    n_head = n_qkv // 3
    for j in range(win_ref.shape[1] // PROJ_CHUNK):
        c0 = j * PROJ_CHUNK
        p = jnp.dot(xb_ref[...], win_ref[:, c0:c0 + PROJ_CHUNK], preferred_element_type=F32)
        if c0 < n_qkv:
            if c0 < n_head:
                p = p * (NA_HEAD_DIM ** -0.5 * LOG2E)
            qkv_ref[:, c0:c0 + PROJ_CHUNK] = p.astype(BF16)
        elif c0 < n_qkv + n_hy:
            hyu_ref[:, c0 - n_qkv:c0 - n_qkv + PROJ_CHUNK] = p.astype(BF16)
        else:
            g0 = c0 - n_qkv - n_hy
            gate_ref[:, g0:g0 + PROJ_CHUNK] = jax.nn.sigmoid(p + bg_ref[:, g0:g0 + PROJ_CHUNK]).astype(BF16)


def _dense_in(x2d, w1, w3, w2, g, b, w_in, b_gate, n_qkv, n_hy):
    n, d = x2d.shape
    d_ff = w1.shape[1]
    n_gate = w_in.shape[1] - n_qkv - n_hy
    assert n % ROW_TILE == 0 and d_ff % FF_CHUNK == 0
    assert n_qkv % (3 * PROJ_CHUNK) == 0 and n_hy % PROJ_CHUNK == 0 and n_gate % PROJ_CHUNK == 0
    weights = (w1, w3, w2, w_in)
    assert all(w.dtype == F32 and w.shape[0] % CAST_ROWS == 0 for w in weights)
    row = lambda i: (i, 0)
    whole = lambda i: (0, 0)
    hbm = pl.BlockSpec(memory_space=pl.ANY)
    return pl.pallas_call(
        _dense_in_kernel,
        grid=(n // ROW_TILE,),
        in_specs=[
            pl.BlockSpec((ROW_TILE, d), row),
            hbm, hbm, hbm,
            _resident((1, d), whole), _resident((1, d), whole),
            hbm, _resident((1, n_gate), whole),
        ],
        out_specs=[
            pl.BlockSpec((ROW_TILE, d), row),
            pl.BlockSpec((ROW_TILE, n_qkv), row),
            pl.BlockSpec((ROW_TILE, n_hy), row),
            pl.BlockSpec((ROW_TILE, n_gate), row),
        ],
        out_shape=[
            jax.ShapeDtypeStruct((n, d), F32),
            jax.ShapeDtypeStruct((n, n_qkv), BF16),
            jax.ShapeDtypeStruct((n, n_hy), BF16),
            jax.ShapeDtypeStruct((n, n_gate), BF16),
        ],
        scratch_shapes=[pltpu.VMEM((ROW_TILE, d), BF16), pltpu.VMEM((ROW_TILE, d), F32)]
        + [pltpu.VMEM(w.shape, BF16) for w in weights]
        + [pltpu.VMEM((CAST_SLOTS, CAST_ROWS, max(w.shape[1] for w in weights)), F32),
           pltpu.SemaphoreType.DMA((CAST_SLOTS,))],
        compiler_params=_cparams(("arbitrary",)),
        name="dense_in",
    )(x2d, w1, w3, w2, g, b, w_in, b_gate)


def _dense_out_kernel(x1_ref, ya_ref, yb_ref, gate_ref, wpa_hbm, wpb_hbm, wout_hbm, g2_ref, b2_ref,
                      w1_hbm, w3_hbm, w2_hbm, g3_ref, b3_ref, o_ref,
                      xb_ref, acc_ref, wpa_ref, wpb_ref, wout_ref, w1_ref, w3_ref, w2_ref, stage_ref, sem):
    @pl.when(pl.program_id(0) == 0)
    def _():
        _cast_weights_in(((wpa_hbm, wpa_ref), (wpb_hbm, wpb_ref), (wout_hbm, wout_ref),
                          (w1_hbm, w1_ref), (w3_hbm, w3_ref), (w2_hbm, w2_ref)), stage_ref, sem)

    d = x1_ref.shape[1]
    pa = jnp.dot(ya_ref[...], wpa_ref[...], preferred_element_type=F32)
    pb = jnp.dot(yb_ref[...], wpb_ref[...], preferred_element_type=F32)
    m = gate_ref[:, :d].astype(F32) * pa + gate_ref[:, d:].astype(F32) * pb
    mix = jnp.dot(m.astype(BF16), wout_ref[...], preferred_element_type=F32)
    x2 = _layer_norm(DN_ALPHA * x1_ref[...] + mix, g2_ref[...], b2_ref[...])
    o_ref[...] = _swiglu_ln(x2, xb_ref, acc_ref, w1_ref, w3_ref, w2_ref, g3_ref, b3_ref)


def _dense_out(x1, ya, yb, gates, wpa, wpb, wout, g2, b2, w1, w3, w2, g3, b3):
    n, d = x1.shape
    weights = (wpa, wpb, wout, w1, w3, w2)
    assert all(w.dtype == F32 and w.shape[0] % CAST_ROWS == 0 for w in weights)
    row = lambda i: (i, 0)
    whole = lambda i: (0, 0)
    hbm = pl.BlockSpec(memory_space=pl.ANY)
    return pl.pallas_call(
        _dense_out_kernel,
        grid=(n // ROW_TILE,),
        in_specs=[
            pl.BlockSpec((ROW_TILE, d), row),
            pl.BlockSpec((ROW_TILE, ya.shape[1]), row),
            pl.BlockSpec((ROW_TILE, yb.shape[1]), row),
            pl.BlockSpec((ROW_TILE, gates.shape[1]), row),
            hbm, hbm, hbm,
            _resident((1, d), whole), _resident((1, d), whole),
            hbm, hbm, hbm,
            _resident((1, d), whole), _resident((1, d), whole),
        ],
        out_specs=pl.BlockSpec((ROW_TILE, d), row),
        out_shape=jax.ShapeDtypeStruct((n, d), F32),
        scratch_shapes=[pltpu.VMEM((ROW_TILE, d), BF16), pltpu.VMEM((ROW_TILE, d), F32)]
        + [pltpu.VMEM(w.shape, BF16) for w in weights]
        + [pltpu.VMEM((CAST_SLOTS, CAST_ROWS, max(w.shape[1] for w in weights)), F32),
           pltpu.SemaphoreType.DMA((CAST_SLOTS,))],
        compiler_params=_cparams(("arbitrary",)),
        name="dense_out",
    )(x1, ya, yb, gates, wpa, wpb, wout, g2, b2, w1, w3, w2, g3, b3)


def _window_start(n, k):
    return np.clip(np.arange(n) - k // 2, 0, n - k)


def _na_group_types(rows):
    last = rows - NA_QROWS
    return ((0, 0), (NA_QROWS * 2, NA_QROWS), (last, rows - NA_KROWS))


def _na_tables(rows):
    w = GRID_W
    shift = np.zeros((2 * NA_KW - 1, w, 2 * w), np.float32)
    q = np.arange(w)[:, None]
    kc = np.arange(w)[None, :]
    for dc in range(2 * NA_KW - 1):
        shift[dc, :, :w] = (kc - q + NA_KW - 1 == dc).astype(np.float32)
    cs = _window_start(w, NA_KW)
    col_ok = (kc >= cs[:, None]) & (kc < cs[:, None] + NA_KW)
    rs_all = _window_start(rows, NA_KH)
    negmask = np.full((3, NA_QROWS, w, NA_KROWS, w), NEG, np.float32)
    for t, (r0, kb) in enumerate(_na_group_types(rows)):
        for i in range(NA_QROWS):
            rs = rs_all[r0 + i]
            for j in range(NA_KROWS):
                if rs <= kb + j < rs + NA_KH:
                    negmask[t, i, :, j, :] = np.where(col_ok, 0.0, NEG)
    return shift, negmask.reshape(3, NA_QROWS * w, NA_KROWS * w)


def _na_bias_kernel(rpb_ref, shift_ref, neg_ref, o_ref, u_ref, *, rows):
    h = pl.program_id(0)
    n_dr = 2 * NA_KH - 1
    n_dc = 2 * NA_KW - 1
    w = GRID_W

    def dr_body(dr, carry):
        base = (h * n_dr + dr) * n_dc
        left = rpb_ref[base] * shift_ref[0]
        for dc in range(1, n_dc):
            left = left + rpb_ref[base + dc] * shift_ref[dc]
        left = left * LOG2E
        u_ref[0, dr] = left
        u_ref[1, dr] = pltpu.roll(left, w, 1)
        return carry

    lax.fori_loop(0, n_dr, dr_body, 0)
    rs_all = _window_start(rows, NA_KH)
    for t, (r0, kb) in enumerate(_na_group_types(rows)):
        for i in range(NA_QROWS):
            rs = rs_all[r0 + i]
            for jj in range(NA_KROWS // 2):
                val = neg_ref[t, i * w:(i + 1) * w, jj * 2 * w:(jj + 1) * 2 * w]
                for s in range(2):
                    kr = kb + 2 * jj + s
                    if rs <= kr < rs + NA_KH:
                        val = val + u_ref[s, kr - (r0 + i) + NA_KH - 1]
                o_ref[t, 0, i * w:(i + 1) * w, jj * 2 * w:(jj + 1) * 2 * w] = val


def _na_bias(rpb, rows):
    shift, negmask = _na_tables(rows)
    nq, nk = NA_QROWS * GRID_W, NA_KROWS * GRID_W
    return pl.pallas_call(
        functools.partial(_na_bias_kernel, rows=rows),
        grid=(NA_HEADS,),
        in_specs=[
            pl.BlockSpec(memory_space=pltpu.SMEM),
            _resident(shift.shape, lambda h: (0, 0, 0)),
            _resident(negmask.shape, lambda h: (0, 0, 0)),
        ],
        out_specs=pl.BlockSpec((3, 1, nq, nk), lambda h: (0, h, 0, 0)),
        out_shape=jax.ShapeDtypeStruct((3, NA_HEADS, nq, nk), F32),
        scratch_shapes=[pltpu.VMEM((2, 2 * NA_KH - 1, GRID_W, 2 * GRID_W), F32)],
        compiler_params=_cparams(("arbitrary",)),
        name="na_bias",
    )(rpb.reshape(-1), jnp.asarray(shift), jnp.asarray(negmask))


def _na_attn_kernel(q_ref, k_ref, v_ref, bias_ref, o_ref, s_ref, e_ref, *, rows):
    g = pl.program_id(1)
    kb = jnp.clip(g * NA_QROWS - NA_KH // 2, 0, rows - NA_KROWS)
    k0 = pl.multiple_of(kb * GRID_W, GRID_W)
    nk = NA_KROWS * GRID_W
    dh = NA_HEAD_DIM
    lane = lax.broadcasted_iota(jnp.int32, (1, 2 * dh), 1)
    own = (lane < dh, lane >= dh)
    pair_cols = [slice(hp * 2 * dh, (hp + 1) * 2 * dh) for hp in range(NA_HEADS // 2)]
    for hp, cols in enumerate(pair_cols):
        q2 = q_ref[0, :, cols]
        k2 = k_ref[0, pl.ds(k0, nk), cols]
        for s in range(2):
            s_ref[2 * hp + s] = lax.dot_general(q2, jnp.where(own[s], k2, jnp.zeros_like(k2)),
                                                (((1,), (1,)), ((), ())), preferred_element_type=F32)
    for h in range(NA_HEADS):
        sc = s_ref[h] + bias_ref[0, h]
        e_ref[h] = jnp.exp2(sc - jnp.max(sc, axis=-1, keepdims=True)).astype(BF16)
    for hp, cols in enumerate(pair_cols):
        v2 = v_ref[0, pl.ds(k0, nk), cols]
        pair = []
        for s in range(2):
            r = jnp.dot(e_ref[2 * hp + s], jnp.where(own[s], v2, jnp.ones_like(v2)), preferred_element_type=F32)
            den = r[:, dh:dh + 1] if s == 0 else r[:, 0:1]
            pair.append(jnp.where(own[s], r / den, 0.0))
        o_ref[0, :, cols] = (pair[0] + pair[1]).astype(BF16)


def _na_attn(qkv, bias, rows):
    b, seq, three_w = qkv.shape
    width = three_w // 3
    nq, nk = NA_QROWS * GRID_W, NA_KROWS * GRID_W
    groups = rows // NA_QROWS

    def bias_idx(bi, g):
        return (jnp.where(g == 0, 0, jnp.where(g == groups - 1, 2, 1)), 0, 0, 0)

    return pl.pallas_call(
        functools.partial(_na_attn_kernel, rows=rows),
        grid=(b, groups),
        in_specs=[
            pl.BlockSpec((1, nq, width), lambda bi, g: (bi, g, 0)),
            pl.BlockSpec((1, seq, width), lambda bi, g: (bi, 0, 1)),
            pl.BlockSpec((1, seq, width), lambda bi, g: (bi, 0, 2)),
            pl.BlockSpec((1, NA_HEADS, nq, nk), bias_idx),
        ],
        out_specs=pl.BlockSpec((1, nq, width), lambda bi, g: (bi, g, 0)),
        out_shape=jax.ShapeDtypeStruct((b, seq, width), BF16),
        scratch_shapes=[pltpu.VMEM((NA_HEADS, nq, nk), F32), pltpu.VMEM((NA_HEADS, nq, nk), BF16)],
        compiler_params=_cparams(("arbitrary", "arbitrary")),
        name="na_attn",
    )(qkv, qkv, qkv, bias)


def _dft_tables():
    k = np.arange(RADIX)
    ang = 2.0 * np.pi * np.outer(k, k) / RADIX
    fr, fi = np.cos(ang), -np.sin(ang)
    ang_t = 2.0 * np.pi * np.outer(k, k) / (RADIX * RADIX)
    twr, twi = np.cos(ang_t), -np.sin(ang_t)
    f32 = lambda a: jnp.asarray(a, F32)
    return f32(fr), f32(fi), f32(twr), f32(twi)


def _cmatmul(c_rows, x_lanes):
    m = c_rows.shape[0] // 2
    r = jnp.dot(c_rows, x_lanes, preferred_element_type=F32)
    return r[:m, :LANES] - r[m:, LANES:], r[:m, LANES:] + r[m:, :LANES]


def _load_pair(ref, rows):
    return jnp.concatenate([ref[0, rows, :], ref[1, rows, :]], axis=1).astype(BF16)


def _store_pair(ref, rows, re, im):
    ref[0, rows, :] = re
    ref[1, rows, :] = im


def _twiddled(fr, fi, twr, twi, scale_re=1.0, scale_im=1.0):
    return jnp.concatenate([(fr * twr - fi * twi) * scale_re, (fr * twi + fi * twr) * scale_im], axis=0).astype(BF16)


def _position_features(seq):
    t = np.linspace(0.0, 1.0, seq)[:, None]
    bands = (HY_EMB_DIM - 1) // 2
    w = (2.0 * math.pi / seq) * np.arange(seq)[:, None]
    f = np.linspace(1e-4, bands - 1, bands)[None, :]
    z = np.concatenate([t, np.cos(f * w), -np.sin(f * w)], axis=-1)
    rev = seq - np.arange(seq)
    rev[0] = 0
    zz = np.zeros((seq, 2 * LANES), np.float32)
    zz[:, :HY_EMB_DIM] = z
    zz[:, LANES:LANES + HY_EMB_DIM] = z[rev]
    return zz


def _filt_mlp_kernel(z_ref, w1_ref, b1_ref, w2_ref, b2_ref, fr_ref, o_ref):
    hp = lax.Precision.HIGHEST
    fr = fr_ref[...]
    hid = LANES // 2
    h = jnp.sin(fr * (jnp.dot(z_ref[...], w1_ref[...], precision=hp, preferred_element_type=F32) + b1_ref[...]))
    h = jnp.sin(fr * (jnp.dot(h, w2_ref[...], precision=hp, preferred_element_type=F32) + b2_ref[...]))
    hi = h.astype(BF16).astype(F32)
    lo = h - hi
    hi_sw, lo_sw = pltpu.roll(hi, hid, 1), pltpu.roll(lo, hid, 1)
    low = lax.broadcasted_iota(jnp.int32, (1, LANES), 1) < hid
    o_ref[0, :, :LANES] = jnp.where(low, hi, lo_sw).astype(BF16)
    o_ref[0, :, LANES:] = jnp.where(low, hi, 0.0).astype(BF16)
    o_ref[1, :, :LANES] = jnp.where(low, hi_sw, lo).astype(BF16)
    o_ref[1, :, LANES:] = jnp.where(low, hi_sw, 0.0).astype(BF16)


def _filt_mlp(zz, fw1, fb1, fw2, fb2, freq):
    seq = zz.shape[0]
    hid = fw2.shape[0]
    assert 2 * hid == LANES
    w1 = jnp.zeros((2 * LANES, LANES), F32)
    w1 = w1.at[:HY_EMB_DIM, :hid].set(fw1).at[LANES:LANES + HY_EMB_DIM, hid:].set(fw1)
    w2 = jnp.zeros((LANES, LANES), F32).at[:hid, :hid].set(fw2).at[hid:, hid:].set(fw2)
    twice = lambda v: jnp.concatenate([v, v]).reshape(1, LANES)
    rows = 2048
    whole = lambda i: (0, 0)
    return pl.pallas_call(
        _filt_mlp_kernel,
        grid=(seq // rows,),
        in_specs=[pl.BlockSpec((rows, 2 * LANES), lambda i: (i, 0)),
                  pl.BlockSpec(w1.shape, whole), pl.BlockSpec((1, LANES), whole),
                  pl.BlockSpec(w2.shape, whole), pl.BlockSpec((1, LANES), whole), pl.BlockSpec((1, LANES), whole)],
        out_specs=pl.BlockSpec((2, rows, 2 * LANES), lambda i: (0, i, 0)),
        out_shape=jax.ShapeDtypeStruct((2, seq, 2 * LANES), BF16),
        compiler_params=_cparams(("arbitrary",)),
        name="filt_mlp",
    )(zz, w1, twice(fb1), w2, twice(fb2), twice(freq))


def _filt_time_kernel(h_ref, w3_ref, b3_ref, delta_ref, k_ref, ss_ref, *, seq):
    i = pl.program_id(0)
    rows = h_ref.shape[0]
    w = w3_ref[0]
    w_hi = w.astype(BF16)
    w_lo = (w - w_hi.astype(F32)).astype(BF16)
    wcat = jnp.concatenate([w_hi, w_hi, w_lo, jnp.zeros_like(w_lo)], axis=0)
    hf = jnp.dot(h_ref[...], wcat, preferred_element_type=F32) + b3_ref[0]
    m = i * rows + lax.broadcasted_iota(jnp.int32, (rows, 1), 0)
    pos = jnp.where(m < seq, m, 2 * seq - m).astype(F32)
    t = pos / (seq - 1)
    k = jnp.where(m == seq, 0.0, hf * jnp.exp(-t * delta_ref[...]))
    for j in range(rows // RADIX):
        k_ref[j * PITCH:j * PITCH + RADIX, :] = k[j * RADIX:(j + 1) * RADIX]
        k_ref[j * PITCH + RADIX:(j + 1) * PITCH, :] = jnp.zeros((PITCH - RADIX, k.shape[1]), F32)
    part = jnp.sum(k * k, axis=0, keepdims=True)

    @pl.when(i == 0)
    def _():
        ss_ref[...] = part

    @pl.when(i != 0)
    def _():
        ss_ref[...] += part


def _filt_time(hdn, w3d, b3d, delta2, seq):
    n2l, packed = hdn.shape
    _, hid, cols = w3d.shape
    assert packed == 4 * hid
    rows = 1024
    half = seq // rows
    prows = rows // RADIX * PITCH
    return pl.pallas_call(
        functools.partial(_filt_time_kernel, seq=seq),
        grid=(n2l // rows,),
        in_specs=[pl.BlockSpec((rows, packed), lambda i: (i, 0)),
                  pl.BlockSpec((1, hid, cols), lambda i: (i // half, 0, 0)),
                  pl.BlockSpec((1, 1, cols), lambda i: (i // half, 0, 0)),
                  pl.BlockSpec((1, cols), lambda i: (0, 0))],
        out_specs=[pl.BlockSpec((prows, cols), lambda i: (i, 0)),
                   pl.BlockSpec((1, cols), lambda i: (0, 0))],
        out_shape=[jax.ShapeDtypeStruct((n2l // RADIX * PITCH, cols), F32), jax.ShapeDtypeStruct((1, cols), F32)],
        compiler_params=_cparams(("arbitrary",)),
        name="filt_time",
    )(hdn, w3d, b3d, delta2)


def _filt_spec_kernel(k_ref, ss_ref, fa_ref, fr_ref, fi_ref, twr_ref, twi_ref, o_ref, a_ref):
    r = RADIX
    scale = lax.rsqrt(ss_ref[...] + 1e-12)

    def a_body(i, c):
        n2 = 2 * i
        rhs = jnp.concatenate([k_ref[pl.ds(n2, r, stride=PITCH), :], k_ref[pl.ds(n2 + 1, r, stride=PITCH), :]],
                              axis=1).astype(BF16)
        out = jnp.dot(fa_ref[...], rhs, preferred_element_type=F32)
        _store_pair(a_ref, pl.ds(n2, r, stride=PITCH), out[:r, :LANES], out[r:, :LANES])
        _store_pair(a_ref, pl.ds(n2 + 1, r, stride=PITCH), out[:r, LANES:], out[r:, LANES:])
        return c

    lax.fori_loop(0, r // 2, a_body, 0, unroll=DFT_UNROLL_STRIDED // 2)

    def b_body(k1, c):
        r0 = pl.multiple_of(k1 * PITCH, 8)
        o0 = pl.multiple_of(k1 * r, r)
        blk = _load_pair(a_ref, pl.ds(r0, r))
        sr, si = _cmatmul(_twiddled(fr_ref[...], fi_ref[...], twr_ref[pl.ds(k1, 1), :], twi_ref[pl.ds(k1, 1), :]), blk)
        o_ref[0, pl.ds(o0, r), :] = (sr * scale).astype(BF16)
        o_ref[1, pl.ds(o0, r), :] = (si * scale).astype(BF16)
        return c

    lax.fori_loop(0, r, b_body, 0, unroll=DFT_UNROLL_STRIDED)


def _filt_spec(ktime, ss, fa_real, fr, fi, twr, twi):
    prows, cols = ktime.shape
    n2l = prows // PITCH * RADIX
    whole = lambda j: (0, 0)
    return pl.pallas_call(
        _filt_spec_kernel,
        grid=(cols // LANES,),
        in_specs=[pl.BlockSpec((prows, LANES), lambda j: (0, j)),
                  pl.BlockSpec((1, LANES), lambda j: (0, j)),
                  _resident(fa_real.shape, whole), _resident(fr.shape, whole), _resident(fi.shape, whole),
                  _resident(twr.shape, whole), _resident(twi.shape, whole)],
        out_specs=pl.BlockSpec((2, n2l, LANES), lambda j: (0, 0, j)),
        out_shape=jax.ShapeDtypeStruct((2, n2l, cols), BF16),
        scratch_shapes=[pltpu.VMEM((2, prows, LANES), F32)],
        compiler_params=_cparams(("arbitrary",)),
        name="filt_spec",
    )(ktime, ss, fa_real, fr, fi, twr, twi)


def _short_conv_rows(u_ref, bi, r0, nrows, seq, w_ref, b_ref):
    edge = 16
    cur = u_ref[bi, pl.ds(r0, nrows), :].astype(F32)
    before = u_ref[bi, pl.ds(pl.multiple_of(jnp.maximum(r0 - edge, 0), edge), edge), :].astype(F32)
    after = u_ref[bi, pl.ds(pl.multiple_of(jnp.minimum(r0 + nrows, seq - edge), edge), edge), :].astype(F32)
    prev_edge = jnp.where(r0 > 0, before[edge - 1:edge, :], 0.0)
    next_edge = jnp.where(r0 + nrows < seq, after[0:1, :], 0.0)
    w0, w1, w2 = w_ref[0:1, :], w_ref[1:2, :], w_ref[2:3, :]
    out = b_ref[...] + w0 * pltpu.roll(cur, 1, 0) + w1 * cur + w2 * pltpu.roll(cur, nrows - 1, 0)
    row = lax.broadcasted_iota(jnp.int32, (8, cur.shape[1]), 0)
    first = out[0:8] + jnp.where(row == 0, w0 * (prev_edge - cur[nrows - 1:nrows]), 0.0)
    last = out[nrows - 8:] + jnp.where(row == 7, w2 * (next_edge - cur[0:1]), 0.0)
    return jnp.concatenate([first, out[8:nrows - 8], last], axis=0)


def _hy_conv_kernel(z_ref, zw_ref, zb_ref, u_ref, uw_ref, ub_ref, ks_hbm, hb_ref,
                    fa_ref, fc_ref, fr_ref, fi_ref, twr_ref, twi_ref, o_ref, a_ref, v_ref, ks_ref, ks_sem,
                    *, conv_in, ks_tile0):
    r = RADIX
    half = r // 2
    nb, seq, _ = z_ref.shape
    nchunk = seq // CONV_ROWS
    blocks = CONV_ROWS // r

    ks_col = pl.multiple_of((ks_tile0 + pl.program_id(0)) * LANES, LANES)
    ks_copy = pltpu.make_async_copy(ks_hbm.at[:, :, pl.ds(ks_col, LANES)], ks_ref, ks_sem)
    ks_copy.start()

    def load_body(c, carry):
        r0 = pl.multiple_of(c * CONV_ROWS, CONV_ROWS)
        for bi in range(nb):
            if conv_in:
                v = _short_conv_rows(z_ref, bi, r0, CONV_ROWS, seq, zw_ref, zb_ref)
                o_ref[bi, pl.ds(r0, CONV_ROWS), :] = v.astype(BF16)
            else:
                v = z_ref[bi, pl.ds(r0, CONV_ROWS), :].astype(F32)
            for j in range(blocks):
                p0 = pl.multiple_of((c * blocks + j) * PITCH, 8)
                v_ref[bi, pl.ds(p0, r), :] = v[j * r:(j + 1) * r]
        return carry

    lax.fori_loop(0, nchunk, load_body, 0)

    def a_body(n2, carry):
        re, im = _cmatmul(fa_ref[...], _load_pair(v_ref, pl.ds(n2, half, stride=PITCH)))
        _store_pair(a_ref, pl.ds(n2, r, stride=PITCH), re, im)
        return carry

    lax.fori_loop(0, r, a_body, 0, unroll=DFT_UNROLL_STRIDED)

    def b_forward(k1):
        r0 = pl.multiple_of(k1 * PITCH, 8)
        s0 = pl.multiple_of(k1 * r, r)
        blk = _load_pair(a_ref, pl.ds(r0, r))
        sr, si = _cmatmul(_twiddled(fr_ref[...], fi_ref[...], twr_ref[pl.ds(k1, 1), :], twi_ref[pl.ds(k1, 1), :]), blk)
        kr = ks_ref[0, pl.ds(s0, r), :].astype(F32)
        ki = ks_ref[1, pl.ds(s0, r), :].astype(F32)
        return jnp.concatenate([sr * kr - si * ki, sr * ki + si * kr], axis=1).astype(BF16)

    def b_inverse(k1, p):
        tr, ti = _cmatmul(fc_ref[...], p)
        _store_pair(a_ref, pl.ds(pl.multiple_of(k1 * PITCH, 8), r), tr, ti)

    def b_body(k1, p):
        p_next = b_forward(k1 + 1)
        b_inverse(k1, p)
        return p_next

    ks_copy.wait()
    b_inverse(r - 1, lax.fori_loop(0, r - 1, b_body, b_forward(0), unroll=DFT_UNROLL))

    inv_n = 1.0 / (r * r)

    def c_body(n2, carry):
        g = _twiddled(fr_ref[0:half, :], fi_ref[0:half, :], twr_ref[pl.ds(n2, 1), :], twi_ref[pl.ds(n2, 1), :],
                      inv_n, -inv_n)
        yr, yi = _cmatmul(g, _load_pair(a_ref, pl.ds(n2, r, stride=PITCH)))
        _store_pair(v_ref, pl.ds(n2, half, stride=PITCH), yr, yi)
        return carry

    lax.fori_loop(0, r, c_body, 0, unroll=DFT_UNROLL_STRIDED)

    def out_body(c, carry):
        r0 = pl.multiple_of(c * CONV_ROWS, CONV_ROWS)
        for bi in range(nb):
            gate = _short_conv_rows(u_ref, bi, r0, CONV_ROWS, seq, uw_ref, ub_ref)
            conv = jnp.concatenate(
                [v_ref[bi, pl.ds(pl.multiple_of((c * blocks + j) * PITCH, 8), r), :] for j in range(blocks)], axis=0)
            zp = (o_ref if conv_in else z_ref)[bi, pl.ds(r0, CONV_ROWS), :].astype(F32)
            o_ref[bi, pl.ds(r0, CONV_ROWS), :] = (gate * (conv + hb_ref[...] * zp)).astype(BF16)
        return carry

    lax.fori_loop(0, nchunk, out_body, 0)


def _hy_conv(z, z_part, u, u_part, order, short_w, short_b, kspec, hy_bias, tables, conv_in):
    nb, seq, _ = u.shape
    width = hy_bias.shape[1]
    tiles = width // LANES
    fa, fc, fr, fi, twr, twi = tables
    whole = lambda j: (0, 0)
    zcol = lambda j: (0, 0, z_part * tiles + j)
    ucol = lambda j: (0, 0, u_part * tiles + j)
    return pl.pallas_call(
        functools.partial(_hy_conv_kernel, conv_in=conv_in, ks_tile0=order * tiles),
        grid=(tiles,),
        in_specs=[
            pl.BlockSpec((nb, seq, LANES), zcol),
            pl.BlockSpec((3, LANES), lambda j: (0, z_part * tiles + j)),
            pl.BlockSpec((1, LANES), lambda j: (0, z_part * tiles + j)),
            pl.BlockSpec((nb, seq, LANES), ucol),
            pl.BlockSpec((3, LANES), lambda j: (0, u_part * tiles + j)),
            pl.BlockSpec((1, LANES), lambda j: (0, u_part * tiles + j)),
            pl.BlockSpec(memory_space=pl.ANY),
            pl.BlockSpec((1, LANES), lambda j: (0, j)),
            _resident(fa.shape, whole), _resident(fc.shape, whole), _resident(fr.shape, whole),
            _resident(fi.shape, whole), _resident(twr.shape, whole), _resident(twi.shape, whole),
        ],
        out_specs=_resident((nb, seq, LANES), lambda j: (0, 0, j)),
        out_shape=jax.ShapeDtypeStruct((nb, seq, width), BF16),
        scratch_shapes=[pltpu.VMEM((2, 2 * seq // RADIX * PITCH, LANES), F32),
                        pltpu.VMEM((nb, seq // RADIX * PITCH, LANES), F32),
                        pltpu.VMEM((2, 2 * seq, LANES), BF16),
                        pltpu.SemaphoreType.DMA(())],
        compiler_params=_cparams(("arbitrary",)),
        name=f"hy_conv{order}",
    )(z, short_w, short_b, u, short_w, short_b, kspec, hy_bias[order:order + 1], fa, fc, fr, fi, twr, twi)


def _hyena(hyu, short_w, short_b, fw1, fb1, fw2, fb2, fw3, fb3, freq, hy_bias):
    nb, seq, _ = hyu.shape
    assert nb == 2 and 2 * seq == RADIX * RADIX
    width = hy_bias.shape[1]
    hid = fw2.shape[0]
    fr, fi, twr, twi = _dft_tables()
    half = RADIX // 2
    fa_real = jnp.concatenate([fr, fi], axis=0).astype(BF16)
    fa_half = jnp.concatenate([fr[:, :half], fi[:, :half]], axis=0).astype(BF16)
    fc = jnp.concatenate([fr, -fi], axis=0).astype(BF16)

    zz = jnp.asarray(_position_features(seq))
    hdn = _filt_mlp(zz, fw1, fb1, fw2, fb2, freq).reshape(2 * seq, 4 * hid)
    w3d = fw3.reshape(hid, 2, HY_ORDER * width).transpose(1, 0, 2)
    b3d = fb3.reshape(2, 1, HY_ORDER * width)
    max_decay = math.log(HY_DECAY_TARGET) / HY_FAST_DECAY_PCT
    min_decay = math.log(HY_DECAY_TARGET) / HY_SLOW_DECAY_PCT
    delta = np.abs(np.linspace(min_decay, max_decay, width)).astype(np.float32)
    delta2 = jnp.asarray(np.tile(delta, HY_ORDER)[None, :])
    ktime, ss = _filt_time(hdn, w3d, b3d, delta2, seq)
    kspec = _filt_spec(ktime, ss, fa_real, fr, fi, twr, twi)

    tables = (fa_half, fc, fr, fi, twr, twi)
    sb = short_b.reshape(1, -1)
    z1 = _hy_conv(hyu, 0, hyu, 1, 0, short_w, sb, kspec, hy_bias, tables, conv_in=True)
    return _hy_conv(z1, 0, hyu, 2, 1, short_w, sb, kspec, hy_bias, tables, conv_in=False)


def kernel(x, ln1_g, ln1_b, ffn1_w1, ffn1_w3, ffn1_w2, w_in, b_gate, na_rpb, hy_short_w, hy_short_b,
           hy_filt_w1, hy_filt_b1, hy_filt_w2, hy_filt_b2, hy_filt_w3, hy_filt_b3, hy_filt_freq, hy_bias,
           w_pa, w_pb, w_out, ln2_g, ln2_b, ffn2_w1, ffn2_w3, ffn2_w2, ln3_g, ln3_b):
    assert ln1_g.shape[0] == DEPTH
    b, seq, d = x.shape
    rows = seq // GRID_W
    na_width = NA_HEADS * NA_HEAD_DIM
    n_qkv = 3 * na_width
    n_hy = hy_short_w.shape[2]
    vec = lambda a: a[0].reshape(1, -1)

    x1, qkv, hyu, gates = _dense_in(
        x.reshape(b * seq, d), ffn1_w1[0], ffn1_w3[0], ffn1_w2[0], vec(ln1_g), vec(ln1_b),
        w_in[0], vec(b_gate), n_qkv, n_hy)

    bias = _na_bias(na_rpb[0], rows)
    ya = _na_attn(qkv.reshape(b, seq, n_qkv), bias, rows)

    yb = _hyena(hyu.reshape(b, seq, n_hy), hy_short_w[0], hy_short_b[0], hy_filt_w1[0], hy_filt_b1[0],
                hy_filt_w2[0], hy_filt_b2[0], hy_filt_w3[0], hy_filt_b3[0], hy_filt_freq[0], hy_bias[0])

    out = _dense_out(
        x1, ya.reshape(b * seq, na_width), yb.reshape(b * seq, -1), gates,
        w_pa[0], w_pb[0], w_out[0], vec(ln2_g), vec(ln2_b),
        ffn2_w1[0], ffn2_w3[0], ffn2_w2[0], vec(ln3_g), vec(ln3_b))
    return out.reshape(b, seq, d)
```

```python
import functools
import math

import numpy as np
import jax
import jax.numpy as jnp
from jax import lax
from jax.experimental import pallas as pl
from jax.experimental.pallas import tpu as pltpu

F32 = jnp.float32
BF16 = jnp.bfloat16

GRID_W = 64
NA_HEADS = 8
NA_HEAD_DIM = 64
NA_KH = 8
NA_KW = 16
HY_ORDER = 2
HY_EMB_DIM = 33
HY_FAST_DECAY_PCT = 0.3
HY_SLOW_DECAY_PCT = 1.5
HY_DECAY_TARGET = 1e-2
DEPTH = 1
DN_ALPHA = (2 * DEPTH) ** 0.25
LN_EPS = 1e-5
LOG2E = math.log2(math.e)

LANES = 128
VMEM_LIMIT = 60 * 1024 * 1024

ROW_TILE = 512
FF_CHUNK = 256
PROJ_CHUNK = 512
CAST_ROWS = 64
CAST_SLOTS = 4
NA_QROWS = 4
NA_KROWS = NA_QROWS + NA_KH
RADIX = 128
PITCH = RADIX + 8
DFT_UNROLL = 8
DFT_UNROLL_STRIDED = 16
CONV_ROWS = 512
NEG = -1e30


def _cparams(sem):
    return pltpu.CompilerParams(dimension_semantics=sem, vmem_limit_bytes=VMEM_LIMIT)


def _resident(shape, index_map):
    return pl.BlockSpec(shape, index_map, pipeline_mode=pl.Buffered(1))


def _cast_weights_in(pairs, stage_ref, sem):
    ahead = CAST_SLOTS - 1
    counts = [src.shape[0] // CAST_ROWS for src, _ in pairs]
    assert all(n >= ahead for n in counts)
    bases = [sum(counts[:w]) for w in range(len(pairs))]

    def copy(w, c, g):
        src, cols = pairs[w][0], pairs[w][0].shape[1]
        slot = g % CAST_SLOTS
        return pltpu.make_async_copy(src.at[pl.ds(c * CAST_ROWS, CAST_ROWS), :],
                                     stage_ref.at[slot, :, pl.ds(0, cols)], sem.at[slot])

    for g in range(ahead):
        copy(0, g, g).start()
    for w, (src, dst) in enumerate(pairs):
        n, cols = counts[w], src.shape[1]

        def body(c, carry, w=w, n=n, cols=cols, dst=dst):
            g = bases[w] + c

            @pl.when(c + ahead < n)
            def _():
                copy(w, c + ahead, g + ahead).start()

            if w + 1 < len(pairs):
                @pl.when(c + ahead >= n)
                def _():
                    copy(w + 1, c + ahead - n, g + ahead).start()

            copy(w, c, g).wait()
            r0 = pl.multiple_of(c * CAST_ROWS, CAST_ROWS)
            dst[pl.ds(r0, CAST_ROWS), :] = stage_ref[g % CAST_SLOTS, :, 0:cols].astype(BF16)
            return carry

        lax.fori_loop(0, n, body, 0)


def _layer_norm(r, g, b):
    mu = jnp.mean(r, axis=-1, keepdims=True)
    d = r - mu
    var = jnp.mean(d * d, axis=-1, keepdims=True)
    return d * lax.rsqrt(var + LN_EPS) * g + b


def _swiglu_ln(x, xb_ref, acc_ref, w1_ref, w3_ref, w2_ref, g_ref, b_ref):
    d_ff = w1_ref.shape[1]
    xb_ref[...] = x.astype(BF16)
    for j in range(d_ff // FF_CHUNK):
        sl = slice(j * FF_CHUNK, (j + 1) * FF_CHUNK)
        a = jnp.dot(xb_ref[...], w1_ref[:, sl], preferred_element_type=F32)
        b = jnp.dot(xb_ref[...], w3_ref[:, sl], preferred_element_type=F32)
        h = (a * jax.nn.sigmoid(a) * b).astype(BF16)
        part = jnp.dot(h, w2_ref[sl, :], preferred_element_type=F32)
        if j == 0:
            acc_ref[...] = part
        else:
            acc_ref[...] += part
    return _layer_norm(DN_ALPHA * x + 0.5 * acc_ref[...], g_ref[...], b_ref[...])


def _dense_in_kernel(x_ref, w1_hbm, w3_hbm, w2_hbm, g_ref, b_ref, win_hbm, bg_ref,
                     x1_ref, qkv_ref, hyu_ref, gate_ref,
                     xb_ref, acc_ref, w1_ref, w3_ref, w2_ref, win_ref, stage_ref, sem):
    @pl.when(pl.program_id(0) == 0)
    def _():
        _cast_weights_in(((w1_hbm, w1_ref), (w3_hbm, w3_ref), (w2_hbm, w2_ref), (win_hbm, win_ref)), stage_ref, sem)

    x1 = _swiglu_ln(x_ref[...], xb_ref, acc_ref, w1_ref, w3_ref, w2_ref, g_ref, b_ref)
    x1_ref[...] = x1
    xb_ref[...] = x1.astype(BF16)
    n_qkv = qkv_ref.shape[1]
    n_hy = hyu_ref.shape[1]
    n_head = n_qkv // 3
    for j in range(win_ref.shape[1] // PROJ_CHUNK):
        c0 = j * PROJ_CHUNK
        p = jnp.dot(xb_ref[...], win_ref[:, c0:c0 + PROJ_CHUNK], preferred_element_type=F32)
        if c0 < n_qkv:
            if c0 < n_head:
                p = p * (NA_HEAD_DIM ** -0.5 * LOG2E)
            qkv_ref[:, c0:c0 + PROJ_CHUNK] = p.astype(BF16)
        elif c0 < n_qkv + n_hy:
            hyu_ref[:, c0 - n_qkv:c0 - n_qkv + PROJ_CHUNK] = p.astype(BF16)
        else:
            g0 = c0 - n_qkv - n_hy
            gate_ref[:, g0:g0 + PROJ_CHUNK] = jax.nn.sigmoid(p + bg_ref[:, g0:g0 + PROJ_CHUNK]).astype(BF16)


def _dense_in(x2d, w1, w3, w2, g, b, w_in, b_gate, n_qkv, n_hy):
    n, d = x2d.shape
    d_ff = w1.shape[1]
    n_gate = w_in.shape[1] - n_qkv - n_hy
    assert n % ROW_TILE == 0 and d_ff % FF_CHUNK == 0
    assert n_qkv % (3 * PROJ_CHUNK) == 0 and n_hy % PROJ_CHUNK == 0 and n_gate % PROJ_CHUNK == 0
    weights = (w1, w3, w2, w_in)
    assert all(w.dtype == F32 and w.shape[0] % CAST_ROWS == 0 for w in weights)
    row = lambda i: (i, 0)
    whole = lambda i: (0, 0)
    hbm = pl.BlockSpec(memory_space=pl.ANY)
    return pl.pallas_call(
        _dense_in_kernel,
        grid=(n // ROW_TILE,),
        in_specs=[
            pl.BlockSpec((ROW_TILE, d), row),
            hbm, hbm, hbm,
            _resident((1, d), whole), _resident((1, d), whole),
            hbm, _resident((1, n_gate), whole),
        ],
        out_specs=[
            pl.BlockSpec((ROW_TILE, d), row),
            pl.BlockSpec((ROW_TILE, n_qkv), row),
            pl.BlockSpec((ROW_TILE, n_hy), row),
            pl.BlockSpec((ROW_TILE, n_gate), row),
        ],
        out_shape=[
            jax.ShapeDtypeStruct((n, d), F32),
            jax.ShapeDtypeStruct((n, n_qkv), BF16),
            jax.ShapeDtypeStruct((n, n_hy), BF16),
            jax.ShapeDtypeStruct((n, n_gate), BF16),
        ],
        scratch_shapes=[pltpu.VMEM((ROW_TILE, d), BF16), pltpu.VMEM((ROW_TILE, d), F32)]
        + [pltpu.VMEM(w.shape, BF16) for w in weights]
        + [pltpu.VMEM((CAST_SLOTS, CAST_ROWS, max(w.shape[1] for w in weights)), F32),
           pltpu.SemaphoreType.DMA((CAST_SLOTS,))],
        compiler_params=_cparams(("arbitrary",)),
        name="dense_in",
    )(x2d, w1, w3, w2, g, b, w_in, b_gate)


def _dense_out_kernel(x1_ref, ya_ref, yb_ref, gate_ref, wpa_hbm, wpb_hbm, wout_hbm, g2_ref, b2_ref,
                      w1_hbm, w3_hbm, w2_hbm, g3_ref, b3_ref, o_ref,
                      xb_ref, acc_ref, wpa_ref, wpb_ref, wout_ref, w1_ref, w3_ref, w2_ref, stage_ref, sem):
    @pl.when(pl.program_id(0) == 0)
    def _():
        _cast_weights_in(((wpa_hbm, wpa_ref), (wpb_hbm, wpb_ref), (wout_hbm, wout_ref),
                          (w1_hbm, w1_ref), (w3_hbm, w3_ref), (w2_hbm, w2_ref)), stage_ref, sem)

    d = x1_ref.shape[1]
    pa = jnp.dot(ya_ref[...], wpa_ref[...], preferred_element_type=F32)
    pb = jnp.dot(yb_ref[...], wpb_ref[...], preferred_element_type=F32)
    m = gate_ref[:, :d].astype(F32) * pa + gate_ref[:, d:].astype(F32) * pb
    mix = jnp.dot(m.astype(BF16), wout_ref[...], preferred_element_type=F32)
    x2 = _layer_norm(DN_ALPHA * x1_ref[...] + mix, g2_ref[...], b2_ref[...])
    o_ref[...] = _swiglu_ln(x2, xb_ref, acc_ref, w1_ref, w3_ref, w2_ref, g3_ref, b3_ref)


def _dense_out(x1, ya, yb, gates, wpa, wpb, wout, g2, b2, w1, w3, w2, g3, b3):
    n, d = x1.shape
    weights = (wpa, wpb, wout, w1, w3, w2)
    assert all(w.dtype == F32 and w.shape[0] % CAST_ROWS == 0 for w in weights)
    row = lambda i: (i, 0)
    whole = lambda i: (0, 0)
    hbm = pl.BlockSpec(memory_space=pl.ANY)
    return pl.pallas_call(
        _dense_out_kernel,
        grid=(n // ROW_TILE,),
        in_specs=[
            pl.BlockSpec((ROW_TILE, d), row),
            pl.BlockSpec((ROW_TILE, ya.shape[1]), row),
            pl.BlockSpec((ROW_TILE, yb.shape[1]), row),
            pl.BlockSpec((ROW_TILE, gates.shape[1]), row),
            hbm, hbm, hbm,
            _resident((1, d), whole), _resident((1, d), whole),
            hbm, hbm, hbm,
            _resident((1, d), whole), _resident((1, d), whole),
        ],
        out_specs=pl.BlockSpec((ROW_TILE, d), row),
        out_shape=jax.ShapeDtypeStruct((n, d), F32),
        scratch_shapes=[pltpu.VMEM((ROW_TILE, d), BF16), pltpu.VMEM((ROW_TILE, d), F32)]
        + [pltpu.VMEM(w.shape, BF16) for w in weights]
        + [pltpu.VMEM((CAST_SLOTS, CAST_ROWS, max(w.shape[1] for w in weights)), F32),
           pltpu.SemaphoreType.DMA((CAST_SLOTS,))],
        compiler_params=_cparams(("arbitrary",)),
        name="dense_out",
    )(x1, ya, yb, gates, wpa, wpb, wout, g2, b2, w1, w3, w2, g3, b3)


def _window_start(n, k):
    return np.clip(np.arange(n) - k // 2, 0, n - k)


def _na_group_types(rows):
    last = rows - NA_QROWS
    return ((0, 0), (NA_QROWS * 2, NA_QROWS), (last, rows - NA_KROWS))


def _na_tables(rows):
    w = GRID_W
    shift = np.zeros((2 * NA_KW - 1, w, 2 * w), np.float32)
    q = np.arange(w)[:, None]
    kc = np.arange(w)[None, :]
    for dc in range(2 * NA_KW - 1):
        shift[dc, :, :w] = (kc - q + NA_KW - 1 == dc).astype(np.float32)
    cs = _window_start(w, NA_KW)
    col_ok = (kc >= cs[:, None]) & (kc < cs[:, None] + NA_KW)
    rs_all = _window_start(rows, NA_KH)
    negmask = np.full((3, NA_QROWS, w, NA_KROWS, w), NEG, np.float32)
    for t, (r0, kb) in enumerate(_na_group_types(rows)):
        for i in range(NA_QROWS):
            rs = rs_all[r0 + i]
            for j in range(NA_KROWS):
                if rs <= kb + j < rs + NA_KH:
                    negmask[t, i, :, j, :] = np.where(col_ok, 0.0, NEG)
    return shift, negmask.reshape(3, NA_QROWS * w, NA_KROWS * w)


def _na_bias_kernel(rpb_ref, shift_ref, neg_ref, o_ref, u_ref, *, rows):
    h = pl.program_id(0)
    n_dr = 2 * NA_KH - 1
    n_dc = 2 * NA_KW - 1
    w = GRID_W

    def dr_body(dr, carry):
        base = (h * n_dr + dr) * n_dc
        left = rpb_ref[base] * shift_ref[0]
        for dc in range(1, n_dc):
            left = left + rpb_ref[base + dc] * shift_ref[dc]
        left = left * LOG2E
        u_ref[0, dr] = left
        u_ref[1, dr] = pltpu.roll(left, w, 1)
        return carry

    lax.fori_loop(0, n_dr, dr_body, 0)
    rs_all = _window_start(rows, NA_KH)
    for t, (r0, kb) in enumerate(_na_group_types(rows)):
        for i in range(NA_QROWS):
            rs = rs_all[r0 + i]
            for jj in range(NA_KROWS // 2):
                val = neg_ref[t, i * w:(i + 1) * w, jj * 2 * w:(jj + 1) * 2 * w]
                for s in range(2):
                    kr = kb + 2 * jj + s
                    if rs <= kr < rs + NA_KH:
                        val = val + u_ref[s, kr - (r0 + i) + NA_KH - 1]
                o_ref[t, 0, i * w:(i + 1) * w, jj * 2 * w:(jj + 1) * 2 * w] = val


def _na_bias(rpb, rows):
    shift, negmask = _na_tables(rows)
    nq, nk = NA_QROWS * GRID_W, NA_KROWS * GRID_W
    return pl.pallas_call(
        functools.partial(_na_bias_kernel, rows=rows),
        grid=(NA_HEADS,),
        in_specs=[
            pl.BlockSpec(memory_space=pltpu.SMEM),
            _resident(shift.shape, lambda h: (0, 0, 0)),
            _resident(negmask.shape, lambda h: (0, 0, 0)),
        ],
        out_specs=pl.BlockSpec((3, 1, nq, nk), lambda h: (0, h, 0, 0)),
        out_shape=jax.ShapeDtypeStruct((3, NA_HEADS, nq, nk), F32),
        scratch_shapes=[pltpu.VMEM((2, 2 * NA_KH - 1, GRID_W, 2 * GRID_W), F32)],
        compiler_params=_cparams(("arbitrary",)),
        name="na_bias",
    )(rpb.reshape(-1), jnp.asarray(shift), jnp.asarray(negmask))


def _na_attn_kernel(q_ref, k_ref, v_ref, bias_ref, o_ref, s_ref, e_ref, *, rows):
    g = pl.program_id(1)
    kb = jnp.clip(g * NA_QROWS - NA_KH // 2, 0, rows - NA_KROWS)
    k0 = pl.multiple_of(kb * GRID_W, GRID_W)
    nk = NA_KROWS * GRID_W
    dh = NA_HEAD_DIM
    lane = lax.broadcasted_iota(jnp.int32, (1, 2 * dh), 1)
    own = (lane < dh, lane >= dh)
    pair_cols = [slice(hp * 2 * dh, (hp + 1) * 2 * dh) for hp in range(NA_HEADS // 2)]
    for hp, cols in enumerate(pair_cols):
        q2 = q_ref[0, :, cols]
        k2 = k_ref[0, pl.ds(k0, nk), cols]
        for s in range(2):
            s_ref[2 * hp + s] = lax.dot_general(q2, jnp.where(own[s], k2, jnp.zeros_like(k2)),
                                                (((1,), (1,)), ((), ())), preferred_element_type=F32)
    for h in range(NA_HEADS):
        sc = s_ref[h] + bias_ref[0, h]
        e_ref[h] = jnp.exp2(sc - jnp.max(sc, axis=-1, keepdims=True)).astype(BF16)
    for hp, cols in enumerate(pair_cols):
        v2 = v_ref[0, pl.ds(k0, nk), cols]
        pair = []
        for s in range(2):
            r = jnp.dot(e_ref[2 * hp + s], jnp.where(own[s], v2, jnp.ones_like(v2)), preferred_element_type=F32)
            den = r[:, dh:dh + 1] if s == 0 else r[:, 0:1]
            pair.append(jnp.where(own[s], r / den, 0.0))
        o_ref[0, :, cols] = (pair[0] + pair[1]).astype(BF16)


def _na_attn(qkv, bias, rows):
    b, seq, three_w = qkv.shape
    width = three_w // 3
    nq, nk = NA_QROWS * GRID_W, NA_KROWS * GRID_W
    groups = rows // NA_QROWS

    def bias_idx(bi, g):
        return (jnp.where(g == 0, 0, jnp.where(g == groups - 1, 2, 1)), 0, 0, 0)

    return pl.pallas_call(
        functools.partial(_na_attn_kernel, rows=rows),
        grid=(b, groups),
        in_specs=[
            pl.BlockSpec((1, nq, width), lambda bi, g: (bi, g, 0)),
            pl.BlockSpec((1, seq, width), lambda bi, g: (bi, 0, 1)),
            pl.BlockSpec((1, seq, width), lambda bi, g: (bi, 0, 2)),
            pl.BlockSpec((1, NA_HEADS, nq, nk), bias_idx),
        ],
        out_specs=pl.BlockSpec((1, nq, width), lambda bi, g: (bi, g, 0)),
        out_shape=jax.ShapeDtypeStruct((b, seq, width), BF16),
        scratch_shapes=[pltpu.VMEM((NA_HEADS, nq, nk), F32), pltpu.VMEM((NA_HEADS, nq, nk), BF16)],
        compiler_params=_cparams(("arbitrary", "arbitrary")),
        name="na_attn",
    )(qkv, qkv, qkv, bias)


def _dft_tables():
    k = np.arange(RADIX)
    ang = 2.0 * np.pi * np.outer(k, k) / RADIX
    fr, fi = np.cos(ang), -np.sin(ang)
    ang_t = 2.0 * np.pi * np.outer(k, k) / (RADIX * RADIX)
    twr, twi = np.cos(ang_t), -np.sin(ang_t)
    f32 = lambda a: jnp.asarray(a, F32)
    return f32(fr), f32(fi), f32(twr), f32(twi)


def _embed(cr, ci):
    return jnp.concatenate([jnp.concatenate([cr, -ci], axis=1), jnp.concatenate([ci, cr], axis=1)], axis=0)


def _cmatmul(c_rows, x_lanes):
    m = c_rows.shape[0] // 2
    r = jnp.dot(c_rows, x_lanes, preferred_element_type=F32)
    return r[:m, :LANES] - r[m:, LANES:], r[:m, LANES:] + r[m:, :LANES]


def _load_pair(ref, rows):
    return jnp.concatenate([ref[0, rows, :], ref[1, rows, :]], axis=1).astype(BF16)


def _store_pair(ref, rows, re, im):
    ref[0, rows, :] = re
    ref[1, rows, :] = im


def _twiddled(fr, fi, twr, twi, scale_re=1.0, scale_im=1.0):
    return jnp.concatenate([(fr * twr - fi * twi) * scale_re, (fr * twi + fi * twr) * scale_im], axis=0).astype(BF16)


def _position_features(seq):
    t = np.linspace(0.0, 1.0, seq)[:, None]
    bands = (HY_EMB_DIM - 1) // 2
    w = (2.0 * math.pi / seq) * np.arange(seq)[:, None]
    f = np.linspace(1e-4, bands - 1, bands)[None, :]
    z = np.concatenate([t, np.cos(f * w), -np.sin(f * w)], axis=-1)
    rev = seq - np.arange(seq)
    rev[0] = 0
    zz = np.zeros((seq, 2 * LANES), np.float32)
    zz[:, :HY_EMB_DIM] = z
    zz[:, LANES:LANES + HY_EMB_DIM] = z[rev]
    return zz


def _filt_mlp_kernel(z_ref, w1_ref, b1_ref, w2_ref, b2_ref, fr_ref, o_ref):
    hp = lax.Precision.HIGHEST
    fr = fr_ref[...]
    hid = LANES // 2
    h = jnp.sin(fr * (jnp.dot(z_ref[...], w1_ref[...], precision=hp, preferred_element_type=F32) + b1_ref[...]))
    h = jnp.sin(fr * (jnp.dot(h, w2_ref[...], precision=hp, preferred_element_type=F32) + b2_ref[...]))
    hi = h.astype(BF16).astype(F32)
    lo = h - hi
    hi_sw, lo_sw = pltpu.roll(hi, hid, 1), pltpu.roll(lo, hid, 1)
    low = lax.broadcasted_iota(jnp.int32, (1, LANES), 1) < hid
    o_ref[0, :, :LANES] = jnp.where(low, hi, lo_sw).astype(BF16)
    o_ref[0, :, LANES:] = jnp.where(low, hi, 0.0).astype(BF16)
    o_ref[1, :, :LANES] = jnp.where(low, hi_sw, lo).astype(BF16)
    o_ref[1, :, LANES:] = jnp.where(low, hi_sw, 0.0).astype(BF16)


def _filt_mlp(zz, fw1, fb1, fw2, fb2, freq):
    seq = zz.shape[0]
    hid = fw2.shape[0]
    assert 2 * hid == LANES
    w1 = jnp.zeros((2 * LANES, LANES), F32)
    w1 = w1.at[:HY_EMB_DIM, :hid].set(fw1).at[LANES:LANES + HY_EMB_DIM, hid:].set(fw1)
    w2 = jnp.zeros((LANES, LANES), F32).at[:hid, :hid].set(fw2).at[hid:, hid:].set(fw2)
    twice = lambda v: jnp.concatenate([v, v]).reshape(1, LANES)
    rows = 2048
    whole = lambda i: (0, 0)
    return pl.pallas_call(
        _filt_mlp_kernel,
        grid=(seq // rows,),
        in_specs=[pl.BlockSpec((rows, 2 * LANES), lambda i: (i, 0)),
                  pl.BlockSpec(w1.shape, whole), pl.BlockSpec((1, LANES), whole),
                  pl.BlockSpec(w2.shape, whole), pl.BlockSpec((1, LANES), whole), pl.BlockSpec((1, LANES), whole)],
        out_specs=pl.BlockSpec((2, rows, 2 * LANES), lambda i: (0, i, 0)),
        out_shape=jax.ShapeDtypeStruct((2, seq, 2 * LANES), BF16),
        compiler_params=_cparams(("arbitrary",)),
        name="filt_mlp",
    )(zz, w1, twice(fb1), w2, twice(fb2), twice(freq))


def _filt_time_kernel(h_ref, w3_ref, b3_ref, delta_ref, k_ref, ss_ref, *, seq):
    i = pl.program_id(0)
    rows = h_ref.shape[0]
    w = w3_ref[0]
    w_hi = w.astype(BF16)
    w_lo = (w - w_hi.astype(F32)).astype(BF16)
    wcat = jnp.concatenate([w_hi, w_hi, w_lo, jnp.zeros_like(w_lo)], axis=0)
    hf = jnp.dot(h_ref[...], wcat, preferred_element_type=F32) + b3_ref[0]
    m = i * rows + lax.broadcasted_iota(jnp.int32, (rows, 1), 0)
    pos = jnp.where(m < seq, m, 2 * seq - m).astype(F32)
    t = pos / (seq - 1)
    k = jnp.where(m == seq, 0.0, hf * jnp.exp(-t * delta_ref[...]))
    for j in range(rows // RADIX):
        k_ref[j * PITCH:j * PITCH + RADIX, :] = k[j * RADIX:(j + 1) * RADIX]
        k_ref[j * PITCH + RADIX:(j + 1) * PITCH, :] = jnp.zeros((PITCH - RADIX, k.shape[1]), F32)
    part = jnp.sum(k * k, axis=0, keepdims=True)

    @pl.when(i == 0)
    def _():
        ss_ref[...] = part

    @pl.when(i != 0)
    def _():
        ss_ref[...] += part


def _filt_time(hdn, w3d, b3d, delta2, seq):
    n2l, packed = hdn.shape
    _, hid, cols = w3d.shape
    assert packed == 4 * hid
    rows = 1024
    half = seq // rows
    prows = rows // RADIX * PITCH
    return pl.pallas_call(
        functools.partial(_filt_time_kernel, seq=seq),
        grid=(n2l // rows,),
        in_specs=[pl.BlockSpec((rows, packed), lambda i: (i, 0)),
                  pl.BlockSpec((1, hid, cols), lambda i: (i // half, 0, 0)),
                  pl.BlockSpec((1, 1, cols), lambda i: (i // half, 0, 0)),
                  pl.BlockSpec((1, cols), lambda i: (0, 0))],
        out_specs=[pl.BlockSpec((prows, cols), lambda i: (i, 0)),
                   pl.BlockSpec((1, cols), lambda i: (0, 0))],
        out_shape=[jax.ShapeDtypeStruct((n2l // RADIX * PITCH, cols), F32), jax.ShapeDtypeStruct((1, cols), F32)],
        compiler_params=_cparams(("arbitrary",)),
        name="filt_time",
    )(hdn, w3d, b3d, delta2)


def _filt_spec_kernel(k_ref, ss_ref, fa_ref, fr_ref, fi_ref, twr_ref, twi_ref, o_ref, a_ref):
    r = RADIX
    scale = lax.rsqrt(ss_ref[...] + 1e-12)

    def a_body(i, c):
        n2 = 2 * i
        rhs = jnp.concatenate([k_ref[pl.ds(n2, r, stride=PITCH), :], k_ref[pl.ds(n2 + 1, r, stride=PITCH), :]],
                              axis=1).astype(BF16)
        out = jnp.dot(fa_ref[...], rhs, preferred_element_type=F32)
        _store_pair(a_ref, pl.ds(n2, r, stride=PITCH), out[:r, :LANES], out[r:, :LANES])
        _store_pair(a_ref, pl.ds(n2 + 1, r, stride=PITCH), out[:r, LANES:], out[r:, LANES:])
        return c

    lax.fori_loop(0, r // 2, a_body, 0, unroll=DFT_UNROLL_STRIDED // 2)

    def b_body(k1, c):
        r0 = pl.multiple_of(k1 * PITCH, 8)
        o0 = pl.multiple_of(k1 * r, r)
        blk = _load_pair(a_ref, pl.ds(r0, r))
        sr, si = _cmatmul(_twiddled(fr_ref[...], fi_ref[...], twr_ref[pl.ds(k1, 1), :], twi_ref[pl.ds(k1, 1), :]), blk)
        o_ref[0, pl.ds(o0, r), :] = (sr * scale).astype(BF16)
        o_ref[1, pl.ds(o0, r), :] = (si * scale).astype(BF16)
        return c

    lax.fori_loop(0, r, b_body, 0, unroll=DFT_UNROLL_STRIDED)


def _filt_spec(ktime, ss, fa_real, fr, fi, twr, twi):
    prows, cols = ktime.shape
    n2l = prows // PITCH * RADIX
    whole = lambda j: (0, 0)
    return pl.pallas_call(
        _filt_spec_kernel,
        grid=(cols // LANES,),
        in_specs=[pl.BlockSpec((prows, LANES), lambda j: (0, j)),
                  pl.BlockSpec((1, LANES), lambda j: (0, j)),
                  _resident(fa_real.shape, whole), _resident(fr.shape, whole), _resident(fi.shape, whole),
                  _resident(twr.shape, whole), _resident(twi.shape, whole)],
        out_specs=pl.BlockSpec((2, n2l, LANES), lambda j: (0, 0, j)),
        out_shape=jax.ShapeDtypeStruct((2, n2l, cols), BF16),
        scratch_shapes=[pltpu.VMEM((2, prows, LANES), F32)],
        compiler_params=_cparams(("arbitrary",)),
        name="filt_spec",
    )(ktime, ss, fa_real, fr, fi, twr, twi)


def _short_conv_rows(u_ref, bi, r0, nrows, seq, w_ref, b_ref):
    edge = 16
    cur = u_ref[bi, pl.ds(r0, nrows), :].astype(F32)
    before = u_ref[bi, pl.ds(pl.multiple_of(jnp.maximum(r0 - edge, 0), edge), edge), :].astype(F32)
    after = u_ref[bi, pl.ds(pl.multiple_of(jnp.minimum(r0 + nrows, seq - edge), edge), edge), :].astype(F32)
    prev_edge = jnp.where(r0 > 0, before[edge - 1:edge, :], 0.0)
    next_edge = jnp.where(r0 + nrows < seq, after[0:1, :], 0.0)
    w0, w1, w2 = w_ref[0:1, :], w_ref[1:2, :], w_ref[2:3, :]
    out = b_ref[...] + w0 * pltpu.roll(cur, 1, 0) + w1 * cur + w2 * pltpu.roll(cur, nrows - 1, 0)
    row = lax.broadcasted_iota(jnp.int32, (8, cur.shape[1]), 0)
    first = out[0:8] + jnp.where(row == 0, w0 * (prev_edge - cur[nrows - 1:nrows]), 0.0)
    last = out[nrows - 8:] + jnp.where(row == 7, w2 * (next_edge - cur[0:1]), 0.0)
    return jnp.concatenate([first, out[8:nrows - 8], last], axis=0)


def _hy_conv_kernel(z_ref, zw_ref, zb_ref, u_ref, uw_ref, ub_ref, ks_hbm, hb_ref,
                    fa_ref, fc_ref, fr_ref, fi_ref, twr_ref, twi_ref, o_ref, a_ref, v_ref, ks_ref, ks_sem,
                    *, conv_in, ks_tile0):
    r = RADIX
    half = r // 2
    nb, seq, _ = z_ref.shape
    nchunk = seq // CONV_ROWS
    blocks = CONV_ROWS // r

    ks_col = pl.multiple_of((ks_tile0 + pl.program_id(0)) * LANES, LANES)
    ks_copy = pltpu.make_async_copy(ks_hbm.at[:, :, pl.ds(ks_col, LANES)], ks_ref, ks_sem)
    ks_copy.start()

    def load_body(c, carry):
        r0 = pl.multiple_of(c * CONV_ROWS, CONV_ROWS)
        for bi in range(nb):
            if conv_in:
                v = _short_conv_rows(z_ref, bi, r0, CONV_ROWS, seq, zw_ref, zb_ref)
                o_ref[bi, pl.ds(r0, CONV_ROWS), :] = v.astype(BF16)
            else:
                v = z_ref[bi, pl.ds(r0, CONV_ROWS), :].astype(F32)
            for j in range(blocks):
                p0 = pl.multiple_of((c * blocks + j) * PITCH, 8)
                v_ref[bi, pl.ds(p0, r), :] = v[j * r:(j + 1) * r]
        return carry

    lax.fori_loop(0, nchunk, load_body, 0)

    def pair_store(ref, rows0, rows1, out):
        _store_pair(ref, rows0, out[:r, :LANES], out[r:, :LANES])
        _store_pair(ref, rows1, out[:r, LANES:], out[r:, LANES:])

    def a_body(i, carry):
        n2 = 2 * i

        def column(n):
            return jnp.concatenate([v_ref[0, pl.ds(n, half, stride=PITCH), :],
                                    v_ref[1, pl.ds(n, half, stride=PITCH), :]], axis=0)

        rhs = jnp.concatenate([column(n2), column(n2 + 1)], axis=1).astype(BF16)
        out = jnp.dot(fa_ref[...], rhs, preferred_element_type=F32)
        pair_store(a_ref, pl.ds(n2, r, stride=PITCH), pl.ds(n2 + 1, r, stride=PITCH), out)
        return carry

    lax.fori_loop(0, r // 2, a_body, 0, unroll=DFT_UNROLL_STRIDED // 2)

    def b_forward(k1):
        r0 = pl.multiple_of(k1 * PITCH, 8)
        s0 = pl.multiple_of(k1 * r, r)
        blk = _load_pair(a_ref, pl.ds(r0, r))
        sr, si = _cmatmul(_twiddled(fr_ref[...], fi_ref[...], twr_ref[pl.ds(k1, 1), :], twi_ref[pl.ds(k1, 1), :]), blk)
        kr = ks_ref[0, pl.ds(s0, r), :].astype(F32)
        ki = ks_ref[1, pl.ds(s0, r), :].astype(F32)
        return jnp.concatenate([sr * kr - si * ki, sr * ki + si * kr], axis=0).astype(BF16)

    def b_forward_pair(i):
        return jnp.concatenate([b_forward(2 * i), b_forward(2 * i + 1)], axis=1)

    def b_inverse_pair(i, p):
        out = jnp.dot(fc_ref[...], p, preferred_element_type=F32)
        pair_store(a_ref, pl.ds(pl.multiple_of(2 * i * PITCH, 8), r), pl.ds(pl.multiple_of((2 * i + 1) * PITCH, 8), r), out)

    def b_body(i, p):
        p_next = b_forward_pair(i + 1)
        b_inverse_pair(i, p)
        return p_next

    ks_copy.wait()
    b_inverse_pair(r // 2 - 1, lax.fori_loop(0, r // 2 - 1, b_body, b_forward_pair(0), unroll=DFT_UNROLL // 2))

    inv_n = 1.0 / (r * r)

    def c_body(n2, carry):
        g = _twiddled(fr_ref[0:half, :], fi_ref[0:half, :], twr_ref[pl.ds(n2, 1), :], twi_ref[pl.ds(n2, 1), :],
                      inv_n, -inv_n)
        yr, yi = _cmatmul(g, _load_pair(a_ref, pl.ds(n2, r, stride=PITCH)))
        _store_pair(v_ref, pl.ds(n2, half, stride=PITCH), yr, yi)
        return carry

    lax.fori_loop(0, r, c_body, 0, unroll=DFT_UNROLL_STRIDED)

    def out_body(c, carry):
        r0 = pl.multiple_of(c * CONV_ROWS, CONV_ROWS)
        for bi in range(nb):
            gate = _short_conv_rows(u_ref, bi, r0, CONV_ROWS, seq, uw_ref, ub_ref)
            conv = jnp.concatenate(
                [v_ref[bi, pl.ds(pl.multiple_of((c * blocks + j) * PITCH, 8), r), :] for j in range(blocks)], axis=0)
            zp = (o_ref if conv_in else z_ref)[bi, pl.ds(r0, CONV_ROWS), :].astype(F32)
            o_ref[bi, pl.ds(r0, CONV_ROWS), :] = (gate * (conv + hb_ref[...] * zp)).astype(BF16)
        return carry

    lax.fori_loop(0, nchunk, out_body, 0)


def _hy_conv(z, z_part, u, u_part, order, short_w, short_b, kspec, hy_bias, tables, conv_in):
    nb, seq, _ = u.shape
    width = hy_bias.shape[1]
    tiles = width // LANES
    fa, fc, fr, fi, twr, twi = tables
    whole = lambda j: (0, 0)
    zcol = lambda j: (0, 0, z_part * tiles + j)
    ucol = lambda j: (0, 0, u_part * tiles + j)
    return pl.pallas_call(
        functools.partial(_hy_conv_kernel, conv_in=conv_in, ks_tile0=order * tiles),
        grid=(tiles,),
        in_specs=[
            pl.BlockSpec((nb, seq, LANES), zcol),
            pl.BlockSpec((3, LANES), lambda j: (0, z_part * tiles + j)),
            pl.BlockSpec((1, LANES), lambda j: (0, z_part * tiles + j)),
            pl.BlockSpec((nb, seq, LANES), ucol),
            pl.BlockSpec((3, LANES), lambda j: (0, u_part * tiles + j)),
            pl.BlockSpec((1, LANES), lambda j: (0, u_part * tiles + j)),
            pl.BlockSpec(memory_space=pl.ANY),
            pl.BlockSpec((1, LANES), lambda j: (0, j)),
            _resident(fa.shape, whole), _resident(fc.shape, whole), _resident(fr.shape, whole),
            _resident(fi.shape, whole), _resident(twr.shape, whole), _resident(twi.shape, whole),
        ],
        out_specs=_resident((nb, seq, LANES), lambda j: (0, 0, j)),
        out_shape=jax.ShapeDtypeStruct((nb, seq, width), BF16),
        scratch_shapes=[pltpu.VMEM((2, 2 * seq // RADIX * PITCH, LANES), F32),
                        pltpu.VMEM((nb, seq // RADIX * PITCH, LANES), F32),
                        pltpu.VMEM((2, 2 * seq, LANES), BF16),
                        pltpu.SemaphoreType.DMA(())],
        compiler_params=_cparams(("arbitrary",)),
        name=f"hy_conv{order}",
    )(z, short_w, short_b, u, short_w, short_b, kspec, hy_bias[order:order + 1], fa, fc, fr, fi, twr, twi)


def _hyena(hyu, short_w, short_b, fw1, fb1, fw2, fb2, fw3, fb3, freq, hy_bias):
    nb, seq, _ = hyu.shape
    assert nb == 2 and 2 * seq == RADIX * RADIX
    width = hy_bias.shape[1]
    hid = fw2.shape[0]
    fr, fi, twr, twi = _dft_tables()
    half = RADIX // 2
    fa_real = jnp.concatenate([fr, fi], axis=0).astype(BF16)
    fa_half = _embed(fr[:, :half], fi[:, :half]).astype(BF16)
    fc = _embed(fr, -fi).astype(BF16)

    zz = jnp.asarray(_position_features(seq))
    hdn = _filt_mlp(zz, fw1, fb1, fw2, fb2, freq).reshape(2 * seq, 4 * hid)
    w3d = fw3.reshape(hid, 2, HY_ORDER * width).transpose(1, 0, 2)
    b3d = fb3.reshape(2, 1, HY_ORDER * width)
    max_decay = math.log(HY_DECAY_TARGET) / HY_FAST_DECAY_PCT
    min_decay = math.log(HY_DECAY_TARGET) / HY_SLOW_DECAY_PCT
    delta = np.abs(np.linspace(min_decay, max_decay, width)).astype(np.float32)
    delta2 = jnp.asarray(np.tile(delta, HY_ORDER)[None, :])
    ktime, ss = _filt_time(hdn, w3d, b3d, delta2, seq)
    kspec = _filt_spec(ktime, ss, fa_real, fr, fi, twr, twi)

    tables = (fa_half, fc, fr, fi, twr, twi)
    sb = short_b.reshape(1, -1)
    z1 = _hy_conv(hyu, 0, hyu, 1, 0, short_w, sb, kspec, hy_bias, tables, conv_in=True)
    return _hy_conv(z1, 0, hyu, 2, 1, short_w, sb, kspec, hy_bias, tables, conv_in=False)


def kernel(x, ln1_g, ln1_b, ffn1_w1, ffn1_w3, ffn1_w2, w_in, b_gate, na_rpb, hy_short_w, hy_short_b,
           hy_filt_w1, hy_filt_b1, hy_filt_w2, hy_filt_b2, hy_filt_w3, hy_filt_b3, hy_filt_freq, hy_bias,
           w_pa, w_pb, w_out, ln2_g, ln2_b, ffn2_w1, ffn2_w3, ffn2_w2, ln3_g, ln3_b):
    assert ln1_g.shape[0] == DEPTH
    b, seq, d = x.shape
    rows = seq // GRID_W
    na_width = NA_HEADS * NA_HEAD_DIM
    n_qkv = 3 * na_width
    n_hy = hy_short_w.shape[2]
    vec = lambda a: a[0].reshape(1, -1)

    x1, qkv, hyu, gates = _dense_in(
        x.reshape(b * seq, d), ffn1_w1[0], ffn1_w3[0], ffn1_w2[0], vec(ln1_g), vec(ln1_b),
        w_in[0], vec(b_gate), n_qkv, n_hy)

    bias = _na_bias(na_rpb[0], rows)
    ya = _na_attn(qkv.reshape(b, seq, n_qkv), bias, rows)

    yb = _hyena(hyu.reshape(b, seq, n_hy), hy_short_w[0], hy_short_b[0], hy_filt_w1[0], hy_filt_b1[0],
                hy_filt_w2[0], hy_filt_b2[0], hy_filt_w3[0], hy_filt_b3[0], hy_filt_freq[0], hy_bias[0])

    out = _dense_out(
        x1, ya.reshape(b * seq, na_width), yb.reshape(b * seq, -1), gates,
        w_pa[0], w_pb[0], w_out[0], vec(ln2_g), vec(ln2_b),
        ffn2_w1[0], ffn2_w3[0], ffn2_w2[0], vec(ln3_g), vec(ln3_b))
    return out.reshape(b, seq, d)
```

```python
import functools
import math

import numpy as np
import jax
import jax.numpy as jnp
from jax import lax
from jax.experimental import pallas as pl
from jax.experimental.pallas import tpu as pltpu

F32 = jnp.float32
BF16 = jnp.bfloat16

GRID_W = 64
NA_HEADS = 8
NA_HEAD_DIM = 64
NA_KH = 8
NA_KW = 16
HY_ORDER = 2
HY_EMB_DIM = 33
HY_FAST_DECAY_PCT = 0.3
HY_SLOW_DECAY_PCT = 1.5
HY_DECAY_TARGET = 1e-2
DEPTH = 1
DN_ALPHA = (2 * DEPTH) ** 0.25
LN_EPS = 1e-5
LOG2E = math.log2(math.e)

LANES = 128
VMEM_LIMIT = 60 * 1024 * 1024

ROW_TILE = 512
FF_CHUNK = 256
PROJ_CHUNK = 512
CAST_ROWS = 64
CAST_SLOTS = 6
NA_QROWS = 4
NA_KROWS = NA_QROWS + NA_KH
NA_BIAS_GROUP = 5
RADIX = 128
PITCH = RADIX + 8
DFT_UNROLL = 8
DFT_UNROLL_STRIDED = 16
CONV_ROWS = 512
NEG = -1e30


def _cparams(sem):
    return pltpu.CompilerParams(dimension_semantics=sem, vmem_limit_bytes=VMEM_LIMIT)


def _resident(shape, index_map):
    return pl.BlockSpec(shape, index_map, pipeline_mode=pl.Buffered(1))


def _cast_weights_in(pairs, stage_ref, sem):
    ahead = CAST_SLOTS - 1
    counts = [src.shape[0] // CAST_ROWS for src, _ in pairs]
    assert all(n >= ahead for n in counts)
    bases = [sum(counts[:w]) for w in range(len(pairs))]

    def copy(w, c, g):
        src, cols = pairs[w][0], pairs[w][0].shape[1]
        slot = g % CAST_SLOTS
        return pltpu.make_async_copy(src.at[pl.ds(c * CAST_ROWS, CAST_ROWS), :],
                                     stage_ref.at[slot, :, pl.ds(0, cols)], sem.at[slot])

    for g in range(ahead):
        copy(0, g, g).start()
    for w, (src, dst) in enumerate(pairs):
        n, cols = counts[w], src.shape[1]

        def body(c, carry, w=w, n=n, cols=cols, dst=dst):
            g = bases[w] + c

            @pl.when(c + ahead < n)
            def _():
                copy(w, c + ahead, g + ahead).start()

            if w + 1 < len(pairs):
                @pl.when(c + ahead >= n)
                def _():
                    copy(w + 1, c + ahead - n, g + ahead).start()

            copy(w, c, g).wait()
            r0 = pl.multiple_of(c * CAST_ROWS, CAST_ROWS)
            dst[pl.ds(r0, CAST_ROWS), :] = stage_ref[g % CAST_SLOTS, :, 0:cols].astype(BF16)
            return carry

        lax.fori_loop(0, n, body, 0)


def _layer_norm(r, g, b):
    mu = jnp.mean(r, axis=-1, keepdims=True)
    d = r - mu
    var = jnp.mean(d * d, axis=-1, keepdims=True)
    return d * lax.rsqrt(var + LN_EPS) * g + b


def _swiglu_ln(x, xb_ref, acc_ref, w1_ref, w3_ref, w2_ref, g_ref, b_ref):
    d_ff = w1_ref.shape[1]
    xb_ref[...] = x.astype(BF16)
    for j in range(d_ff // FF_CHUNK):
        sl = slice(j * FF_CHUNK, (j + 1) * FF_CHUNK)
        a = jnp.dot(xb_ref[...], w1_ref[:, sl], preferred_element_type=F32)
        b = jnp.dot(xb_ref[...], w3_ref[:, sl], preferred_element_type=F32)
        h = (a * jax.nn.sigmoid(a) * b).astype(BF16)
        part = jnp.dot(h, w2_ref[sl, :], preferred_element_type=F32)
        if j == 0:
            acc_ref[...] = part
        else:
            acc_ref[...] += part
    return _layer_norm(DN_ALPHA * x + 0.5 * acc_ref[...], g_ref[...], b_ref[...])


def _dense_in_kernel(x_ref, w1_hbm, w3_hbm, w2_hbm, g_ref, b_ref, win_hbm, bg_ref,
                     x1_ref, qkv_ref, hyu_ref, gate_ref,
                     xb_ref, acc_ref, w1_ref, w3_ref, w2_ref, win_ref, stage_ref, sem):
    @pl.when(pl.program_id(0) == 0)
    def _():
        _cast_weights_in(((w1_hbm, w1_ref), (w3_hbm, w3_ref), (w2_hbm, w2_ref), (win_hbm, win_ref)), stage_ref, sem)

    x1 = _swiglu_ln(x_ref[...], xb_ref, acc_ref, w1_ref, w3_ref, w2_ref, g_ref, b_ref)
    x1_ref[...] = x1
    xb_ref[...] = x1.astype(BF16)
    n_qkv = qkv_ref.shape[1]
    n_hy = hyu_ref.shape[1]
    n_head = n_qkv // 3
    for j in range(win_ref.shape[1] // PROJ_CHUNK):
        c0 = j * PROJ_CHUNK
        p = jnp.dot(xb_ref[...], win_ref[:, c0:c0 + PROJ_CHUNK], preferred_element_type=F32)
        if c0 < n_qkv:
            if c0 < n_head:
                p = p * (NA_HEAD_DIM ** -0.5 * LOG2E)
            qkv_ref[:, c0:c0 + PROJ_CHUNK] = p.astype(BF16)
        elif c0 < n_qkv + n_hy:
            hyu_ref[:, c0 - n_qkv:c0 - n_qkv + PROJ_CHUNK] = p.astype(BF16)
        else:
            g0 = c0 - n_qkv - n_hy
            gate_ref[:, g0:g0 + PROJ_CHUNK] = jax.nn.sigmoid(p + bg_ref[:, g0:g0 + PROJ_CHUNK]).astype(BF16)


def _dense_in(x2d, w1, w3, w2, g, b, w_in, b_gate, n_qkv, n_hy):
    n, d = x2d.shape
    d_ff = w1.shape[1]
    n_gate = w_in.shape[1] - n_qkv - n_hy
    assert n % ROW_TILE == 0 and d_ff % FF_CHUNK == 0
    assert n_qkv % (3 * PROJ_CHUNK) == 0 and n_hy % PROJ_CHUNK == 0 and n_gate % PROJ_CHUNK == 0
    weights = (w1, w3, w2, w_in)
    assert all(w.dtype == F32 and w.shape[0] % CAST_ROWS == 0 for w in weights)
    row = lambda i: (i, 0)
    whole = lambda i: (0, 0)
    hbm = pl.BlockSpec(memory_space=pl.ANY)
    return pl.pallas_call(
        _dense_in_kernel,
        grid=(n // ROW_TILE,),
        in_specs=[
            pl.BlockSpec((ROW_TILE, d), row),
            hbm, hbm, hbm,
            _resident((1, d), whole), _resident((1, d), whole),
            hbm, _resident((1, n_gate), whole),
        ],
        out_specs=[
            pl.BlockSpec((ROW_TILE, d), row),
            pl.BlockSpec((ROW_TILE, n_qkv), row),
            pl.BlockSpec((ROW_TILE, n_hy), row),
            pl.BlockSpec((ROW_TILE, n_gate), row),
        ],
        out_shape=[
            jax.ShapeDtypeStruct((n, d), F32),
            jax.ShapeDtypeStruct((n, n_qkv), BF16),
            jax.ShapeDtypeStruct((n, n_hy), BF16),
            jax.ShapeDtypeStruct((n, n_gate), BF16),
        ],
        scratch_shapes=[pltpu.VMEM((ROW_TILE, d), BF16), pltpu.VMEM((ROW_TILE, d), F32)]
        + [pltpu.VMEM(w.shape, BF16) for w in weights]
        + [pltpu.VMEM((CAST_SLOTS, CAST_ROWS, max(w.shape[1] for w in weights)), F32),
           pltpu.SemaphoreType.DMA((CAST_SLOTS,))],
        compiler_params=_cparams(("arbitrary",)),
        name="dense_in",
    )(x2d, w1, w3, w2, g, b, w_in, b_gate)


def _dense_out_kernel(x1_ref, ya_ref, yb_ref, gate_ref, wpa_hbm, wpb_hbm, wout_hbm, g2_ref, b2_ref,
                      w1_hbm, w3_hbm, w2_hbm, g3_ref, b3_ref, o_ref,
                      xb_ref, acc_ref, wpa_ref, wpb_ref, wout_ref, w1_ref, w3_ref, w2_ref, stage_ref, sem):
    @pl.when(pl.program_id(0) == 0)
    def _():
        _cast_weights_in(((wpa_hbm, wpa_ref), (wpb_hbm, wpb_ref), (wout_hbm, wout_ref),
                          (w1_hbm, w1_ref), (w3_hbm, w3_ref), (w2_hbm, w2_ref)), stage_ref, sem)

    d = x1_ref.shape[1]
    pa = jnp.dot(ya_ref[...], wpa_ref[...], preferred_element_type=F32)
    pb = jnp.dot(yb_ref[...], wpb_ref[...], preferred_element_type=F32)
    m = gate_ref[:, :d].astype(F32) * pa + gate_ref[:, d:].astype(F32) * pb
    mix = jnp.dot(m.astype(BF16), wout_ref[...], preferred_element_type=F32)
    x2 = _layer_norm(DN_ALPHA * x1_ref[...] + mix, g2_ref[...], b2_ref[...])
    o_ref[...] = _swiglu_ln(x2, xb_ref, acc_ref, w1_ref, w3_ref, w2_ref, g3_ref, b3_ref)


def _dense_out(x1, ya, yb, gates, wpa, wpb, wout, g2, b2, w1, w3, w2, g3, b3):
    n, d = x1.shape
    weights = (wpa, wpb, wout, w1, w3, w2)
    assert all(w.dtype == F32 and w.shape[0] % CAST_ROWS == 0 for w in weights)
    row = lambda i: (i, 0)
    whole = lambda i: (0, 0)
    hbm = pl.BlockSpec(memory_space=pl.ANY)
    return pl.pallas_call(
        _dense_out_kernel,
        grid=(n // ROW_TILE,),
        in_specs=[
            pl.BlockSpec((ROW_TILE, d), row),
            pl.BlockSpec((ROW_TILE, ya.shape[1]), row),
            pl.BlockSpec((ROW_TILE, yb.shape[1]), row),
            pl.BlockSpec((ROW_TILE, gates.shape[1]), row),
            hbm, hbm, hbm,
            _resident((1, d), whole), _resident((1, d), whole),
            hbm, hbm, hbm,
            _resident((1, d), whole), _resident((1, d), whole),
        ],
        out_specs=pl.BlockSpec((ROW_TILE, d), row),
        out_shape=jax.ShapeDtypeStruct((n, d), F32),
        scratch_shapes=[pltpu.VMEM((ROW_TILE, d), BF16), pltpu.VMEM((ROW_TILE, d), F32)]
        + [pltpu.VMEM(w.shape, BF16) for w in weights]
        + [pltpu.VMEM((CAST_SLOTS, CAST_ROWS, max(w.shape[1] for w in weights)), F32),
           pltpu.SemaphoreType.DMA((CAST_SLOTS,))],
        compiler_params=_cparams(("arbitrary",)),
        name="dense_out",
    )(x1, ya, yb, gates, wpa, wpb, wout, g2, b2, w1, w3, w2, g3, b3)


def _window_start(n, k):
    return np.clip(np.arange(n) - k // 2, 0, n - k)


def _na_group_types(rows):
    last = rows - NA_QROWS
    return ((0, 0), (NA_QROWS * 2, NA_QROWS), (last, rows - NA_KROWS))


def _na_tables(rows):
    w = GRID_W
    shift = np.zeros((2 * NA_KW - 1, w, 2 * w), np.float32)
    q = np.arange(w)[:, None]
    kc = np.arange(w)[None, :]
    for dc in range(2 * NA_KW - 1):
        shift[dc, :, :w] = (kc - q + NA_KW - 1 == dc).astype(np.float32)
    cs = _window_start(w, NA_KW)
    col_ok = (kc >= cs[:, None]) & (kc < cs[:, None] + NA_KW)
    rs_all = _window_start(rows, NA_KH)
    negmask = np.full((3, NA_QROWS, w, NA_KROWS, w), NEG, np.float32)
    for t, (r0, kb) in enumerate(_na_group_types(rows)):
        for i in range(NA_QROWS):
            rs = rs_all[r0 + i]
            for j in range(NA_KROWS):
                if rs <= kb + j < rs + NA_KH:
                    negmask[t, i, :, j, :] = np.where(col_ok, 0.0, NEG)
    return shift, negmask.reshape(3, NA_QROWS * w, NA_KROWS * w)


def _na_bias_kernel(rpb_ref, shift_ref, neg_ref, o_ref, u_ref, *, rows):
    h = pl.program_id(0)
    n_dr = 2 * NA_KH - 1
    n_dc = 2 * NA_KW - 1
    w = GRID_W

    for dr0 in range(0, n_dr, NA_BIAS_GROUP):
        base = (h * n_dr + dr0) * n_dc
        accs = [None] * NA_BIAS_GROUP
        for dc in range(n_dc):
            mask = shift_ref[dc]
            for j in range(NA_BIAS_GROUP):
                term = rpb_ref[base + j * n_dc + dc] * mask
                accs[j] = term if dc == 0 else accs[j] + term
        for j in range(NA_BIAS_GROUP):
            left = accs[j] * LOG2E
            u_ref[0, dr0 + j] = left
            u_ref[1, dr0 + j] = pltpu.roll(left, w, 1)
    rs_all = _window_start(rows, NA_KH)
    for t, (r0, kb) in enumerate(_na_group_types(rows)):
        for i in range(NA_QROWS):
            rs = rs_all[r0 + i]
            for jj in range(NA_KROWS // 2):
                val = neg_ref[t, i * w:(i + 1) * w, jj * 2 * w:(jj + 1) * 2 * w]
                for s in range(2):
                    kr = kb + 2 * jj + s
                    if rs <= kr < rs + NA_KH:
                        val = val + u_ref[s, kr - (r0 + i) + NA_KH - 1]
                o_ref[t, 0, i * w:(i + 1) * w, jj * 2 * w:(jj + 1) * 2 * w] = val


def _na_bias(rpb, rows):
    shift, negmask = _na_tables(rows)
    nq, nk = NA_QROWS * GRID_W, NA_KROWS * GRID_W
    return pl.pallas_call(
        functools.partial(_na_bias_kernel, rows=rows),
        grid=(NA_HEADS,),
        in_specs=[
            pl.BlockSpec(memory_space=pltpu.SMEM),
            _resident(shift.shape, lambda h: (0, 0, 0)),
            _resident(negmask.shape, lambda h: (0, 0, 0)),
        ],
        out_specs=pl.BlockSpec((3, 1, nq, nk), lambda h: (0, h, 0, 0)),
        out_shape=jax.ShapeDtypeStruct((3, NA_HEADS, nq, nk), F32),
        scratch_shapes=[pltpu.VMEM((2, 2 * NA_KH - 1, GRID_W, 2 * GRID_W), F32)],
        compiler_params=_cparams(("arbitrary",)),
        name="na_bias",
    )(rpb.reshape(-1), jnp.asarray(shift), jnp.asarray(negmask))


def _na_attn_kernel(q_ref, k_ref, v_ref, bias_ref, o_ref, s_ref, e_ref, *, rows):
    g = pl.program_id(1)
    kb = jnp.clip(g * NA_QROWS - NA_KH // 2, 0, rows - NA_KROWS)
    k0 = pl.multiple_of(kb * GRID_W, GRID_W)
    nk = NA_KROWS * GRID_W
    dh = NA_HEAD_DIM
    lane = lax.broadcasted_iota(jnp.int32, (1, 2 * dh), 1)
    own = (lane < dh, lane >= dh)
    pair_cols = [slice(hp * 2 * dh, (hp + 1) * 2 * dh) for hp in range(NA_HEADS // 2)]
    for hp, cols in enumerate(pair_cols):
        q2 = q_ref[0, :, cols]
        k2 = k_ref[0, pl.ds(k0, nk), cols]
        for s in range(2):
            s_ref[2 * hp + s] = lax.dot_general(q2, jnp.where(own[s], k2, jnp.zeros_like(k2)),
                                                (((1,), (1,)), ((), ())), preferred_element_type=F32)
    for h in range(NA_HEADS):
        sc = s_ref[h] + bias_ref[0, h]
        e_ref[h] = jnp.exp2(sc - jnp.max(sc, axis=-1, keepdims=True)).astype(BF16)
    for hp, cols in enumerate(pair_cols):
        v2 = v_ref[0, pl.ds(k0, nk), cols]
        pair = []
        for s in range(2):
            r = jnp.dot(e_ref[2 * hp + s], jnp.where(own[s], v2, jnp.ones_like(v2)), preferred_element_type=F32)
            den = r[:, dh:dh + 1] if s == 0 else r[:, 0:1]
            pair.append(jnp.where(own[s], r / den, 0.0))
        o_ref[0, :, cols] = (pair[0] + pair[1]).astype(BF16)


def _na_attn(qkv, bias, rows):
    b, seq, three_w = qkv.shape
    width = three_w // 3
    nq, nk = NA_QROWS * GRID_W, NA_KROWS * GRID_W
    groups = rows // NA_QROWS

    def bias_idx(bi, g):
        return (jnp.where(g == 0, 0, jnp.where(g == groups - 1, 2, 1)), 0, 0, 0)

    return pl.pallas_call(
        functools.partial(_na_attn_kernel, rows=rows),
        grid=(b, groups),
        in_specs=[
            pl.BlockSpec((1, nq, width), lambda bi, g: (bi, g, 0)),
            pl.BlockSpec((1, seq, width), lambda bi, g: (bi, 0, 1)),
            pl.BlockSpec((1, seq, width), lambda bi, g: (bi, 0, 2)),
            pl.BlockSpec((1, NA_HEADS, nq, nk), bias_idx),
        ],
        out_specs=pl.BlockSpec((1, nq, width), lambda bi, g: (bi, g, 0)),
        out_shape=jax.ShapeDtypeStruct((b, seq, width), BF16),
        scratch_shapes=[pltpu.VMEM((NA_HEADS, nq, nk), F32), pltpu.VMEM((NA_HEADS, nq, nk), BF16)],
        compiler_params=_cparams(("arbitrary", "arbitrary")),
        name="na_attn",
    )(qkv, qkv, qkv, bias)


def _dft_tables():
    k = np.arange(RADIX)
    ang = 2.0 * np.pi * np.outer(k, k) / RADIX
    fr, fi = np.cos(ang), -np.sin(ang)
    ang_t = 2.0 * np.pi * np.outer(k, k) / (RADIX * RADIX)
    twr, twi = np.cos(ang_t), -np.sin(ang_t)
    f32 = lambda a: jnp.asarray(a, F32)
    return f32(fr), f32(fi), f32(twr), f32(twi)


def _embed(cr, ci):
    return jnp.concatenate([jnp.concatenate([cr, -ci], axis=1), jnp.concatenate([ci, cr], axis=1)], axis=0)


def _cmatmul(c_rows, x_lanes):
    m = c_rows.shape[0] // 2
    r = jnp.dot(c_rows, x_lanes, preferred_element_type=F32)
    return r[:m, :LANES] - r[m:, LANES:], r[:m, LANES:] + r[m:, :LANES]


def _load_pair(ref, rows):
    return jnp.concatenate([ref[0, rows, :], ref[1, rows, :]], axis=1).astype(BF16)


def _store_pair(ref, rows, re, im):
    ref[0, rows, :] = re
    ref[1, rows, :] = im


def _twiddled(fr, fi, twr, twi, scale_re=1.0, scale_im=1.0):
    return jnp.concatenate([(fr * twr - fi * twi) * scale_re, (fr * twi + fi * twr) * scale_im], axis=0).astype(BF16)


def _position_features(seq):
    t = np.linspace(0.0, 1.0, seq)[:, None]
    bands = (HY_EMB_DIM - 1) // 2
    w = (2.0 * math.pi / seq) * np.arange(seq)[:, None]
    f = np.linspace(1e-4, bands - 1, bands)[None, :]
    z = np.concatenate([t, np.cos(f * w), -np.sin(f * w)], axis=-1)
    rev = seq - np.arange(seq)
    rev[0] = 0
    zz = np.zeros((seq, 2 * LANES), np.float32)
    zz[:, :HY_EMB_DIM] = z
    zz[:, LANES:LANES + HY_EMB_DIM] = z[rev]
    return zz


def _filt_mlp_kernel(z_ref, w1_ref, b1_ref, w2_ref, b2_ref, fr_ref, o_ref):
    hp = lax.Precision.HIGHEST
    fr = fr_ref[...]
    hid = LANES // 2
    h = jnp.sin(fr * (jnp.dot(z_ref[...], w1_ref[...], precision=hp, preferred_element_type=F32) + b1_ref[...]))
    h = jnp.sin(fr * (jnp.dot(h, w2_ref[...], precision=hp, preferred_element_type=F32) + b2_ref[...]))
    hi = h.astype(BF16).astype(F32)
    lo = h - hi
    hi_sw, lo_sw = pltpu.roll(hi, hid, 1), pltpu.roll(lo, hid, 1)
    low = lax.broadcasted_iota(jnp.int32, (1, LANES), 1) < hid
    o_ref[0, :, :LANES] = jnp.where(low, hi, lo_sw).astype(BF16)
    o_ref[0, :, LANES:] = jnp.where(low, hi, 0.0).astype(BF16)
    o_ref[1, :, :LANES] = jnp.where(low, hi_sw, lo).astype(BF16)
    o_ref[1, :, LANES:] = jnp.where(low, hi_sw, 0.0).astype(BF16)


def _filt_mlp(zz, fw1, fb1, fw2, fb2, freq):
    seq = zz.shape[0]
    hid = fw2.shape[0]
    assert 2 * hid == LANES
    w1 = jnp.zeros((2 * LANES, LANES), F32)
    w1 = w1.at[:HY_EMB_DIM, :hid].set(fw1).at[LANES:LANES + HY_EMB_DIM, hid:].set(fw1)
    w2 = jnp.zeros((LANES, LANES), F32).at[:hid, :hid].set(fw2).at[hid:, hid:].set(fw2)
    twice = lambda v: jnp.concatenate([v, v]).reshape(1, LANES)
    rows = 2048
    whole = lambda i: (0, 0)
    return pl.pallas_call(
        _filt_mlp_kernel,
        grid=(seq // rows,),
        in_specs=[pl.BlockSpec((rows, 2 * LANES), lambda i: (i, 0)),
                  pl.BlockSpec(w1.shape, whole), pl.BlockSpec((1, LANES), whole),
                  pl.BlockSpec(w2.shape, whole), pl.BlockSpec((1, LANES), whole), pl.BlockSpec((1, LANES), whole)],
        out_specs=pl.BlockSpec((2, rows, 2 * LANES), lambda i: (0, i, 0)),
        out_shape=jax.ShapeDtypeStruct((2, seq, 2 * LANES), BF16),
        compiler_params=_cparams(("arbitrary",)),
        name="filt_mlp",
    )(zz, w1, twice(fb1), w2, twice(fb2), twice(freq))


def _filt_time_kernel(h_ref, w3_ref, b3_ref, delta_ref, k_ref, ss_ref, *, seq):
    i = pl.program_id(0)
    rows = h_ref.shape[0]
    w = w3_ref[0]
    w_hi = w.astype(BF16)
    w_lo = (w - w_hi.astype(F32)).astype(BF16)
    wcat = jnp.concatenate([w_hi, w_hi, w_lo, jnp.zeros_like(w_lo)], axis=0)
    hf = jnp.dot(h_ref[...], wcat, preferred_element_type=F32) + b3_ref[0]
    m = i * rows + lax.broadcasted_iota(jnp.int32, (rows, 1), 0)
    pos = jnp.where(m < seq, m, 2 * seq - m).astype(F32)
    t = pos / (seq - 1)
    k = jnp.where(m == seq, 0.0, hf * jnp.exp(-t * delta_ref[...]))
    for j in range(rows // RADIX):
        k_ref[j * PITCH:j * PITCH + RADIX, :] = k[j * RADIX:(j + 1) * RADIX]
        k_ref[j * PITCH + RADIX:(j + 1) * PITCH, :] = jnp.zeros((PITCH - RADIX, k.shape[1]), F32)
    part = jnp.sum(k * k, axis=0, keepdims=True)

    @pl.when(i == 0)
    def _():
        ss_ref[...] = part

    @pl.when(i != 0)
    def _():
        ss_ref[...] += part


def _filt_time(hdn, w3d, b3d, delta2, seq):
    n2l, packed = hdn.shape
    _, hid, cols = w3d.shape
    assert packed == 4 * hid
    rows = 1024
    half = seq // rows
    prows = rows // RADIX * PITCH
    return pl.pallas_call(
        functools.partial(_filt_time_kernel, seq=seq),
        grid=(n2l // rows,),
        in_specs=[pl.BlockSpec((rows, packed), lambda i: (i, 0)),
                  pl.BlockSpec((1, hid, cols), lambda i: (i // half, 0, 0)),
                  pl.BlockSpec((1, 1, cols), lambda i: (i // half, 0, 0)),
                  pl.BlockSpec((1, cols), lambda i: (0, 0))],
        out_specs=[pl.BlockSpec((prows, cols), lambda i: (i, 0)),
                   pl.BlockSpec((1, cols), lambda i: (0, 0))],
        out_shape=[jax.ShapeDtypeStruct((n2l // RADIX * PITCH, cols), F32), jax.ShapeDtypeStruct((1, cols), F32)],
        compiler_params=_cparams(("arbitrary",)),
        name="filt_time",
    )(hdn, w3d, b3d, delta2)


def _filt_spec_kernel(k_ref, ss_ref, fa_ref, fr_ref, fi_ref, twr_ref, twi_ref, o_ref, a_ref):
    r = RADIX
    scale = lax.rsqrt(ss_ref[...] + 1e-12)

    def a_body(i, c):
        n2 = 2 * i
        rhs = jnp.concatenate([k_ref[pl.ds(n2, r, stride=PITCH), :], k_ref[pl.ds(n2 + 1, r, stride=PITCH), :]],
                              axis=1).astype(BF16)
        out = jnp.dot(fa_ref[...], rhs, preferred_element_type=F32)
        _store_pair(a_ref, pl.ds(n2, r, stride=PITCH), out[:r, :LANES], out[r:, :LANES])
        _store_pair(a_ref, pl.ds(n2 + 1, r, stride=PITCH), out[:r, LANES:], out[r:, LANES:])
        return c

    lax.fori_loop(0, r // 2, a_body, 0, unroll=DFT_UNROLL_STRIDED // 2)

    def b_body(k1, c):
        r0 = pl.multiple_of(k1 * PITCH, 8)
        o0 = pl.multiple_of(k1 * r, r)
        blk = _load_pair(a_ref, pl.ds(r0, r))
        sr, si = _cmatmul(_twiddled(fr_ref[...], fi_ref[...], twr_ref[pl.ds(k1, 1), :], twi_ref[pl.ds(k1, 1), :]), blk)
        o_ref[0, pl.ds(o0, r), :] = (sr * scale).astype(BF16)
        o_ref[1, pl.ds(o0, r), :] = (si * scale).astype(BF16)
        return c

    lax.fori_loop(0, r, b_body, 0, unroll=DFT_UNROLL_STRIDED)


def _filt_spec(ktime, ss, fa_real, fr, fi, twr, twi):
    prows, cols = ktime.shape
    n2l = prows // PITCH * RADIX
    whole = lambda j: (0, 0)
    return pl.pallas_call(
        _filt_spec_kernel,
        grid=(cols // LANES,),
        in_specs=[pl.BlockSpec((prows, LANES), lambda j: (0, j)),
                  pl.BlockSpec((1, LANES), lambda j: (0, j)),
                  _resident(fa_real.shape, whole), _resident(fr.shape, whole), _resident(fi.shape, whole),
                  _resident(twr.shape, whole), _resident(twi.shape, whole)],
        out_specs=pl.BlockSpec((2, n2l, LANES), lambda j: (0, 0, j)),
        out_shape=jax.ShapeDtypeStruct((2, n2l, cols), BF16),
        scratch_shapes=[pltpu.VMEM((2, prows, LANES), F32)],
        compiler_params=_cparams(("arbitrary",)),
        name="filt_spec",
    )(ktime, ss, fa_real, fr, fi, twr, twi)


def _short_conv_rows(u_ref, bi, r0, nrows, seq, w_ref, b_ref):
    edge = 16
    cur = u_ref[bi, pl.ds(r0, nrows), :].astype(F32)
    before = u_ref[bi, pl.ds(pl.multiple_of(jnp.maximum(r0 - edge, 0), edge), edge), :].astype(F32)
    after = u_ref[bi, pl.ds(pl.multiple_of(jnp.minimum(r0 + nrows, seq - edge), edge), edge), :].astype(F32)
    prev_edge = jnp.where(r0 > 0, before[edge - 1:edge, :], 0.0)
    next_edge = jnp.where(r0 + nrows < seq, after[0:1, :], 0.0)
    w0, w1, w2 = w_ref[0:1, :], w_ref[1:2, :], w_ref[2:3, :]
    out = b_ref[...] + w0 * pltpu.roll(cur, 1, 0) + w1 * cur + w2 * pltpu.roll(cur, nrows - 1, 0)
    row = lax.broadcasted_iota(jnp.int32, (8, cur.shape[1]), 0)
    first = out[0:8] + jnp.where(row == 0, w0 * (prev_edge - cur[nrows - 1:nrows]), 0.0)
    last = out[nrows - 8:] + jnp.where(row == 7, w2 * (next_edge - cur[0:1]), 0.0)
    return jnp.concatenate([first, out[8:nrows - 8], last], axis=0)


def _hy_conv_kernel(z_ref, zw_ref, zb_ref, u_ref, uw_ref, ub_ref, ks_hbm, hb_ref,
                    fa_ref, fc_ref, fr_ref, fi_ref, twr_ref, twi_ref, o_ref, a_ref, v_ref, ks_ref, ks_sem,
                    *, conv_in, ks_tile0):
    r = RADIX
    half = r // 2
    nb, seq, _ = z_ref.shape
    nchunk = seq // CONV_ROWS
    blocks = CONV_ROWS // r

    ks_col = pl.multiple_of((ks_tile0 + pl.program_id(0)) * LANES, LANES)
    ks_copy = pltpu.make_async_copy(ks_hbm.at[:, :, pl.ds(ks_col, LANES)], ks_ref, ks_sem)
    ks_copy.start()

    def load_body(c, carry):
        r0 = pl.multiple_of(c * CONV_ROWS, CONV_ROWS)
        for bi in range(nb):
            if conv_in:
                v = _short_conv_rows(z_ref, bi, r0, CONV_ROWS, seq, zw_ref, zb_ref)
                o_ref[bi, pl.ds(r0, CONV_ROWS), :] = v.astype(BF16)
            else:
                v = z_ref[bi, pl.ds(r0, CONV_ROWS), :].astype(F32)
            for j in range(blocks):
                p0 = pl.multiple_of((c * blocks + j) * PITCH, 8)
                v_ref[bi, pl.ds(p0, r), :] = v[j * r:(j + 1) * r]
        return carry

    lax.fori_loop(0, nchunk, load_body, 0)

    def pair_store(ref, rows0, rows1, out):
        _store_pair(ref, rows0, out[:r, :LANES], out[r:, :LANES])
        _store_pair(ref, rows1, out[:r, LANES:], out[r:, LANES:])

    def a_body(i, carry):
        n2 = 2 * i

        def column(n):
            return jnp.concatenate([v_ref[0, pl.ds(n, half, stride=PITCH), :],
                                    v_ref[1, pl.ds(n, half, stride=PITCH), :]], axis=0)

        rhs = jnp.concatenate([column(n2), column(n2 + 1)], axis=1).astype(BF16)
        out = jnp.dot(fa_ref[...], rhs, preferred_element_type=F32)
        pair_store(a_ref, pl.ds(n2, r, stride=PITCH), pl.ds(n2 + 1, r, stride=PITCH), out)
        return carry

    lax.fori_loop(0, r // 2, a_body, 0, unroll=DFT_UNROLL_STRIDED // 2)

    def b_forward(k1):
        r0 = pl.multiple_of(k1 * PITCH, 8)
        s0 = pl.multiple_of(k1 * r, r)
        blk = _load_pair(a_ref, pl.ds(r0, r))
        sr, si = _cmatmul(_twiddled(fr_ref[...], fi_ref[...], twr_ref[pl.ds(k1, 1), :], twi_ref[pl.ds(k1, 1), :]), blk)
        kr = ks_ref[0, pl.ds(s0, r), :].astype(F32)
        ki = ks_ref[1, pl.ds(s0, r), :].astype(F32)
        return jnp.concatenate([sr * kr - si * ki, sr * ki + si * kr], axis=0).astype(BF16)

    def b_forward_pair(i):
        return jnp.concatenate([b_forward(2 * i), b_forward(2 * i + 1)], axis=1)

    def b_inverse_pair(i, p):
        out = jnp.dot(fc_ref[...], p, preferred_element_type=F32)
        pair_store(a_ref, pl.ds(pl.multiple_of(2 * i * PITCH, 8), r), pl.ds(pl.multiple_of((2 * i + 1) * PITCH, 8), r), out)

    def b_body(i, p):
        p_next = b_forward_pair(i + 1)
        b_inverse_pair(i, p)
        return p_next

    ks_copy.wait()
    b_inverse_pair(r // 2 - 1, lax.fori_loop(0, r // 2 - 1, b_body, b_forward_pair(0), unroll=DFT_UNROLL // 2))

    inv_n = 1.0 / (r * r)

    def c_body(n2, carry):
        g = _twiddled(fr_ref[0:half, :], fi_ref[0:half, :], twr_ref[pl.ds(n2, 1), :], twi_ref[pl.ds(n2, 1), :],
                      inv_n, -inv_n)
        yr, yi = _cmatmul(g, _load_pair(a_ref, pl.ds(n2, r, stride=PITCH)))
        _store_pair(v_ref, pl.ds(n2, half, stride=PITCH), yr, yi)
        return carry

    lax.fori_loop(0, r, c_body, 0, unroll=DFT_UNROLL_STRIDED)

    def out_body(c, carry):
        r0 = pl.multiple_of(c * CONV_ROWS, CONV_ROWS)
        for bi in range(nb):
            gate = _short_conv_rows(u_ref, bi, r0, CONV_ROWS, seq, uw_ref, ub_ref)
            conv = jnp.concatenate(
                [v_ref[bi, pl.ds(pl.multiple_of((c * blocks + j) * PITCH, 8), r), :] for j in range(blocks)], axis=0)
            zp = (o_ref if conv_in else z_ref)[bi, pl.ds(r0, CONV_ROWS), :].astype(F32)
            o_ref[bi, pl.ds(r0, CONV_ROWS), :] = (gate * (conv + hb_ref[...] * zp)).astype(BF16)
        return carry

    lax.fori_loop(0, nchunk, out_body, 0)


def _hy_conv(z, z_part, u, u_part, order, short_w, short_b, kspec, hy_bias, tables, conv_in):
    nb, seq, _ = u.shape
    width = hy_bias.shape[1]
    tiles = width // LANES
    fa, fc, fr, fi, twr, twi = tables
    whole = lambda j: (0, 0)
    zcol = lambda j: (0, 0, z_part * tiles + j)
    ucol = lambda j: (0, 0, u_part * tiles + j)
    return pl.pallas_call(
        functools.partial(_hy_conv_kernel, conv_in=conv_in, ks_tile0=order * tiles),
        grid=(tiles,),
        in_specs=[
            pl.BlockSpec((nb, seq, LANES), zcol),
            pl.BlockSpec((3, LANES), lambda j: (0, z_part * tiles + j)),
            pl.BlockSpec((1, LANES), lambda j: (0, z_part * tiles + j)),
            pl.BlockSpec((nb, seq, LANES), ucol),
            pl.BlockSpec((3, LANES), lambda j: (0, u_part * tiles + j)),
            pl.BlockSpec((1, LANES), lambda j: (0, u_part * tiles + j)),
            pl.BlockSpec(memory_space=pl.ANY),
            pl.BlockSpec((1, LANES), lambda j: (0, j)),
            _resident(fa.shape, whole), _resident(fc.shape, whole), _resident(fr.shape, whole),
            _resident(fi.shape, whole), _resident(twr.shape, whole), _resident(twi.shape, whole),
        ],
        out_specs=_resident((nb, seq, LANES), lambda j: (0, 0, j)),
        out_shape=jax.ShapeDtypeStruct((nb, seq, width), BF16),
        scratch_shapes=[pltpu.VMEM((2, 2 * seq // RADIX * PITCH, LANES), F32),
                        pltpu.VMEM((nb, seq // RADIX * PITCH, LANES), F32),
                        pltpu.VMEM((2, 2 * seq, LANES), BF16),
                        pltpu.SemaphoreType.DMA(())],
        compiler_params=_cparams(("arbitrary",)),
        name=f"hy_conv{order}",
    )(z, short_w, short_b, u, short_w, short_b, kspec, hy_bias[order:order + 1], fa, fc, fr, fi, twr, twi)


def _hyena(hyu, short_w, short_b, fw1, fb1, fw2, fb2, fw3, fb3, freq, hy_bias):
    nb, seq, _ = hyu.shape
    assert nb == 2 and 2 * seq == RADIX * RADIX
    width = hy_bias.shape[1]
    hid = fw2.shape[0]
    fr, fi, twr, twi = _dft_tables()
    half = RADIX // 2
    fa_real = jnp.concatenate([fr, fi], axis=0).astype(BF16)
    fa_half = _embed(fr[:, :half], fi[:, :half]).astype(BF16)
    fc = _embed(fr, -fi).astype(BF16)

    zz = jnp.asarray(_position_features(seq))
    hdn = _filt_mlp(zz, fw1, fb1, fw2, fb2, freq).reshape(2 * seq, 4 * hid)
    w3d = fw3.reshape(hid, 2, HY_ORDER * width).transpose(1, 0, 2)
    b3d = fb3.reshape(2, 1, HY_ORDER * width)
    max_decay = math.log(HY_DECAY_TARGET) / HY_FAST_DECAY_PCT
    min_decay = math.log(HY_DECAY_TARGET) / HY_SLOW_DECAY_PCT
    delta = np.abs(np.linspace(min_decay, max_decay, width)).astype(np.float32)
    delta2 = jnp.asarray(np.tile(delta, HY_ORDER)[None, :])
    ktime, ss = _filt_time(hdn, w3d, b3d, delta2, seq)
    kspec = _filt_spec(ktime, ss, fa_real, fr, fi, twr, twi)

    tables = (fa_half, fc, fr, fi, twr, twi)
    sb = short_b.reshape(1, -1)
    z1 = _hy_conv(hyu, 0, hyu, 1, 0, short_w, sb, kspec, hy_bias, tables, conv_in=True)
    return _hy_conv(z1, 0, hyu, 2, 1, short_w, sb, kspec, hy_bias, tables, conv_in=False)


def kernel(x, ln1_g, ln1_b, ffn1_w1, ffn1_w3, ffn1_w2, w_in, b_gate, na_rpb, hy_short_w, hy_short_b,
           hy_filt_w1, hy_filt_b1, hy_filt_w2, hy_filt_b2, hy_filt_w3, hy_filt_b3, hy_filt_freq, hy_bias,
           w_pa, w_pb, w_out, ln2_g, ln2_b, ffn2_w1, ffn2_w3, ffn2_w2, ln3_g, ln3_b):
    assert ln1_g.shape[0] == DEPTH
    b, seq, d = x.shape
    rows = seq // GRID_W
    na_width = NA_HEADS * NA_HEAD_DIM
    n_qkv = 3 * na_width
    n_hy = hy_short_w.shape[2]
    vec = lambda a: a[0].reshape(1, -1)

    x1, qkv, hyu, gates = _dense_in(
        x.reshape(b * seq, d), ffn1_w1[0], ffn1_w3[0], ffn1_w2[0], vec(ln1_g), vec(ln1_b),
        w_in[0], vec(b_gate), n_qkv, n_hy)

    bias = _na_bias(na_rpb[0], rows)
    ya = _na_attn(qkv.reshape(b, seq, n_qkv), bias, rows)

    yb = _hyena(hyu.reshape(b, seq, n_hy), hy_short_w[0], hy_short_b[0], hy_filt_w1[0], hy_filt_b1[0],
                hy_filt_w2[0], hy_filt_b2[0], hy_filt_w3[0], hy_filt_b3[0], hy_filt_freq[0], hy_bias[0])

    out = _dense_out(
        x1, ya.reshape(b * seq, na_width), yb.reshape(b * seq, -1), gates,
        w_pa[0], w_pb[0], w_out[0], vec(ln2_g), vec(ln2_b),
        ffn2_w1[0], ffn2_w3[0], ffn2_w2[0], vec(ln3_g), vec(ln3_b))
    return out.reshape(b, seq, d)
```

```python
import functools
import math

import numpy as np
import jax
import jax.numpy as jnp
from jax import lax
from jax.experimental import pallas as pl
from jax.experimental.pallas import tpu as pltpu

F32 = jnp.float32
BF16 = jnp.bfloat16

GRID_W = 64
NA_HEADS = 8
NA_HEAD_DIM = 64
NA_KH = 8
NA_KW = 16
HY_ORDER = 2
HY_EMB_DIM = 33
HY_FAST_DECAY_PCT = 0.3
HY_SLOW_DECAY_PCT = 1.5
HY_DECAY_TARGET = 1e-2
DEPTH = 1
DN_ALPHA = (2 * DEPTH) ** 0.25
LN_EPS = 1e-5
LOG2E = math.log2(math.e)

LANES = 128
VMEM_LIMIT = 60 * 1024 * 1024

ROW_TILE = 512
FF_CHUNK = 256
PROJ_CHUNK = 512
CAST_ROWS = 64
CAST_SLOTS = 6
NA_QROWS = 4
NA_KROWS = NA_QROWS + NA_KH
NA_BIAS_GROUP = 5
RADIX = 128
PITCH = RADIX + 8
DFT_UNROLL = 8
DFT_UNROLL_STRIDED = 16
CONV_ROWS = 512
NEG = -1e30


def _cparams(sem):
    return pltpu.CompilerParams(dimension_semantics=sem, vmem_limit_bytes=VMEM_LIMIT)


def _resident(shape, index_map):
    return pl.BlockSpec(shape, index_map, pipeline_mode=pl.Buffered(1))


def _cast_weights_in(pairs, stage_ref, sem):
    ahead = CAST_SLOTS - 1
    counts = [src.shape[0] // CAST_ROWS for src, _ in pairs]
    assert all(n >= ahead for n in counts)
    bases = [sum(counts[:w]) for w in range(len(pairs))]

    def copy(w, c, g):
        src, cols = pairs[w][0], pairs[w][0].shape[1]
        slot = g % CAST_SLOTS
        return pltpu.make_async_copy(src.at[pl.ds(c * CAST_ROWS, CAST_ROWS), :],
                                     stage_ref.at[slot, :, pl.ds(0, cols)], sem.at[slot])

    for g in range(ahead):
        copy(0, g, g).start()
    for w, (src, dst) in enumerate(pairs):
        n, cols = counts[w], src.shape[1]

        def body(c, carry, w=w, n=n, cols=cols, dst=dst):
            g = bases[w] + c

            @pl.when(c + ahead < n)
            def _():
                copy(w, c + ahead, g + ahead).start()

            if w + 1 < len(pairs):
                @pl.when(c + ahead >= n)
                def _():
                    copy(w + 1, c + ahead - n, g + ahead).start()

            copy(w, c, g).wait()
            r0 = pl.multiple_of(c * CAST_ROWS, CAST_ROWS)
            dst[pl.ds(r0, CAST_ROWS), :] = stage_ref[g % CAST_SLOTS, :, 0:cols].astype(BF16)
            return carry

        lax.fori_loop(0, n, body, 0)


def _layer_norm(r, g, b):
    mu = jnp.mean(r, axis=-1, keepdims=True)
    d = r - mu
    var = jnp.mean(d * d, axis=-1, keepdims=True)
    return d * lax.rsqrt(var + LN_EPS) * g + b


def _swiglu_ln(x, xb_ref, acc_ref, w1_ref, w3_ref, w2_ref, g_ref, b_ref):
    d_ff = w1_ref.shape[1]
    xb_ref[...] = x.astype(BF16)
    for j in range(d_ff // FF_CHUNK):
        sl = slice(j * FF_CHUNK, (j + 1) * FF_CHUNK)
        a = jnp.dot(xb_ref[...], w1_ref[:, sl], preferred_element_type=F32)
        b = jnp.dot(xb_ref[...], w3_ref[:, sl], preferred_element_type=F32)
        h = (a * jax.nn.sigmoid(a) * b).astype(BF16)
        part = jnp.dot(h, w2_ref[sl, :], preferred_element_type=F32)
        if j == 0:
            acc_ref[...] = part
        else:
            acc_ref[...] += part
    return _layer_norm(DN_ALPHA * x + 0.5 * acc_ref[...], g_ref[...], b_ref[...])


def _dense_in_kernel(x_ref, w1_hbm, w3_hbm, w2_hbm, g_ref, b_ref, win_hbm, bg_ref,
                     x1_ref, qkv_ref, hyu_ref, gate_ref,
                     xb_ref, acc_ref, w1_ref, w3_ref, w2_ref, win_ref, stage_ref, sem):
    @pl.when(pl.program_id(0) == 0)
    def _():
        _cast_weights_in(((w1_hbm, w1_ref), (w3_hbm, w3_ref), (w2_hbm, w2_ref), (win_hbm, win_ref)), stage_ref, sem)

    x1 = _swiglu_ln(x_ref[...], xb_ref, acc_ref, w1_ref, w3_ref, w2_ref, g_ref, b_ref)
    x1_ref[...] = x1
    xb_ref[...] = x1.astype(BF16)
    n_qkv = qkv_ref.shape[1]
    n_hy = hyu_ref.shape[1]
    n_head = n_qkv // 3
    for j in range(win_ref.shape[1] // PROJ_CHUNK):
        c0 = j * PROJ_CHUNK
        p = jnp.dot(xb_ref[...], win_ref[:, c0:c0 + PROJ_CHUNK], preferred_element_type=F32)
        if c0 < n_qkv:
            if c0 < n_head:
                p = p * (NA_HEAD_DIM ** -0.5 * LOG2E)
            qkv_ref[:, c0:c0 + PROJ_CHUNK] = p.astype(BF16)
        elif c0 < n_qkv + n_hy:
            hyu_ref[:, c0 - n_qkv:c0 - n_qkv + PROJ_CHUNK] = p.astype(BF16)
        else:
            g0 = c0 - n_qkv - n_hy
            gate_ref[:, g0:g0 + PROJ_CHUNK] = jax.nn.sigmoid(p + bg_ref[:, g0:g0 + PROJ_CHUNK]).astype(BF16)


def _dense_in(x2d, w1, w3, w2, g, b, w_in, b_gate, n_qkv, n_hy):
    n, d = x2d.shape
    d_ff = w1.shape[1]
    n_gate = w_in.shape[1] - n_qkv - n_hy
    assert n % ROW_TILE == 0 and d_ff % FF_CHUNK == 0
    assert n_qkv % (3 * PROJ_CHUNK) == 0 and n_hy % PROJ_CHUNK == 0 and n_gate % PROJ_CHUNK == 0
    weights = (w1, w3, w2, w_in)
    assert all(w.dtype == F32 and w.shape[0] % CAST_ROWS == 0 for w in weights)
    row = lambda i: (i, 0)
    whole = lambda i: (0, 0)
    hbm = pl.BlockSpec(memory_space=pl.ANY)
    return pl.pallas_call(
        _dense_in_kernel,
        grid=(n // ROW_TILE,),
        in_specs=[
            pl.BlockSpec((ROW_TILE, d), row),
            hbm, hbm, hbm,
            _resident((1, d), whole), _resident((1, d), whole),
            hbm, _resident((1, n_gate), whole),
        ],
        out_specs=[
            pl.BlockSpec((ROW_TILE, d), row),
            pl.BlockSpec((ROW_TILE, n_qkv), row),
            pl.BlockSpec((ROW_TILE, n_hy), row),
            pl.BlockSpec((ROW_TILE, n_gate), row),
        ],
        out_shape=[
            jax.ShapeDtypeStruct((n, d), F32),
            jax.ShapeDtypeStruct((n, n_qkv), BF16),
            jax.ShapeDtypeStruct((n, n_hy), BF16),
            jax.ShapeDtypeStruct((n, n_gate), BF16),
        ],
        scratch_shapes=[pltpu.VMEM((ROW_TILE, d), BF16), pltpu.VMEM((ROW_TILE, d), F32)]
        + [pltpu.VMEM(w.shape, BF16) for w in weights]
        + [pltpu.VMEM((CAST_SLOTS, CAST_ROWS, max(w.shape[1] for w in weights)), F32),
           pltpu.SemaphoreType.DMA((CAST_SLOTS,))],
        compiler_params=_cparams(("arbitrary",)),
        name="dense_in",
    )(x2d, w1, w3, w2, g, b, w_in, b_gate)


def _dense_out_kernel(x1_ref, ya_ref, yb_ref, gate_ref, wpa_hbm, wpb_hbm, wout_hbm, g2_ref, b2_ref,
                      w1_hbm, w3_hbm, w2_hbm, g3_ref, b3_ref, o_ref,
                      xb_ref, acc_ref, wpa_ref, wpb_ref, wout_ref, w1_ref, w3_ref, w2_ref, stage_ref, sem):
    @pl.when(pl.program_id(0) == 0)
    def _():
        _cast_weights_in(((wpa_hbm, wpa_ref), (wpb_hbm, wpb_ref), (wout_hbm, wout_ref),
                          (w1_hbm, w1_ref), (w3_hbm, w3_ref), (w2_hbm, w2_ref)), stage_ref, sem)

    d = x1_ref.shape[1]
    pa = jnp.dot(ya_ref[...], wpa_ref[...], preferred_element_type=F32)
    pb = jnp.dot(yb_ref[...], wpb_ref[...], preferred_element_type=F32)
    m = gate_ref[:, :d].astype(F32) * pa + gate_ref[:, d:].astype(F32) * pb
    mix = jnp.dot(m.astype(BF16), wout_ref[...], preferred_element_type=F32)
    x2 = _layer_norm(DN_ALPHA * x1_ref[...] + mix, g2_ref[...], b2_ref[...])
    o_ref[...] = _swiglu_ln(x2, xb_ref, acc_ref, w1_ref, w3_ref, w2_ref, g3_ref, b3_ref)


def _dense_out(x1, ya, yb, gates, wpa, wpb, wout, g2, b2, w1, w3, w2, g3, b3):
    n, d = x1.shape
    weights = (wpa, wpb, wout, w1, w3, w2)
    assert all(w.dtype == F32 and w.shape[0] % CAST_ROWS == 0 for w in weights)
    row = lambda i: (i, 0)
    whole = lambda i: (0, 0)
    hbm = pl.BlockSpec(memory_space=pl.ANY)
    return pl.pallas_call(
        _dense_out_kernel,
        grid=(n // ROW_TILE,),
        in_specs=[
            pl.BlockSpec((ROW_TILE, d), row),
            pl.BlockSpec((ROW_TILE, ya.shape[1]), row),
            pl.BlockSpec((ROW_TILE, yb.shape[1]), row),
            pl.BlockSpec((ROW_TILE, gates.shape[1]), row),
            hbm, hbm, hbm,
            _resident((1, d), whole), _resident((1, d), whole),
            hbm, hbm, hbm,
            _resident((1, d), whole), _resident((1, d), whole),
        ],
        out_specs=pl.BlockSpec((ROW_TILE, d), row),
        out_shape=jax.ShapeDtypeStruct((n, d), F32),
        scratch_shapes=[pltpu.VMEM((ROW_TILE, d), BF16), pltpu.VMEM((ROW_TILE, d), F32)]
        + [pltpu.VMEM(w.shape, BF16) for w in weights]
        + [pltpu.VMEM((CAST_SLOTS, CAST_ROWS, max(w.shape[1] for w in weights)), F32),
           pltpu.SemaphoreType.DMA((CAST_SLOTS,))],
        compiler_params=_cparams(("arbitrary",)),
        name="dense_out",
    )(x1, ya, yb, gates, wpa, wpb, wout, g2, b2, w1, w3, w2, g3, b3)


def _window_start(n, k):
    return np.clip(np.arange(n) - k // 2, 0, n - k)


def _na_group_types(rows):
    last = rows - NA_QROWS
    return ((0, 0), (NA_QROWS * 2, NA_QROWS), (last, rows - NA_KROWS))


def _na_tables(rows):
    w = GRID_W
    shift = np.zeros((2 * NA_KW - 1, w, 2 * w), np.float32)
    q = np.arange(w)[:, None]
    kc = np.arange(w)[None, :]
    for dc in range(2 * NA_KW - 1):
        shift[dc, :, :w] = (kc - q + NA_KW - 1 == dc).astype(np.float32)
    cs = _window_start(w, NA_KW)
    col_ok = (kc >= cs[:, None]) & (kc < cs[:, None] + NA_KW)
    rs_all = _window_start(rows, NA_KH)
    negmask = np.full((3, NA_QROWS, w, NA_KROWS, w), NEG, np.float32)
    for t, (r0, kb) in enumerate(_na_group_types(rows)):
        for i in range(NA_QROWS):
            rs = rs_all[r0 + i]
            for j in range(NA_KROWS):
                if rs <= kb + j < rs + NA_KH:
                    negmask[t, i, :, j, :] = np.where(col_ok, 0.0, NEG)
    return shift, negmask.reshape(3, NA_QROWS * w, NA_KROWS * w)


def _na_bias_kernel(rpb_ref, shift_ref, neg_ref, o_ref, u_ref, *, rows):
    h = pl.program_id(0)
    n_dr = 2 * NA_KH - 1
    n_dc = 2 * NA_KW - 1
    w = GRID_W

    for dr0 in range(0, n_dr, NA_BIAS_GROUP):
        base = (h * n_dr + dr0) * n_dc
        accs = [None] * NA_BIAS_GROUP
        for dc in range(n_dc):
            mask = shift_ref[dc]
            for j in range(NA_BIAS_GROUP):
                term = rpb_ref[base + j * n_dc + dc] * mask
                accs[j] = term if dc == 0 else accs[j] + term
        for j in range(NA_BIAS_GROUP):
            left = accs[j] * LOG2E
            u_ref[0, dr0 + j] = left
            u_ref[1, dr0 + j] = pltpu.roll(left, w, 1)
    rs_all = _window_start(rows, NA_KH)
    for t, (r0, kb) in enumerate(_na_group_types(rows)):
        for i in range(NA_QROWS):
            rs = rs_all[r0 + i]
            for jj in range(NA_KROWS // 2):
                val = neg_ref[t, i * w:(i + 1) * w, jj * 2 * w:(jj + 1) * 2 * w]
                for s in range(2):
                    kr = kb + 2 * jj + s
                    if rs <= kr < rs + NA_KH:
                        val = val + u_ref[s, kr - (r0 + i) + NA_KH - 1]
                o_ref[t, 0, i * w:(i + 1) * w, jj * 2 * w:(jj + 1) * 2 * w] = val


def _na_bias(rpb, rows):
    shift, negmask = _na_tables(rows)
    nq, nk = NA_QROWS * GRID_W, NA_KROWS * GRID_W
    return pl.pallas_call(
        functools.partial(_na_bias_kernel, rows=rows),
        grid=(NA_HEADS,),
        in_specs=[
            pl.BlockSpec(memory_space=pltpu.SMEM),
            _resident(shift.shape, lambda h: (0, 0, 0)),
            _resident(negmask.shape, lambda h: (0, 0, 0)),
        ],
        out_specs=pl.BlockSpec((3, 1, nq, nk), lambda h: (0, h, 0, 0)),
        out_shape=jax.ShapeDtypeStruct((3, NA_HEADS, nq, nk), F32),
        scratch_shapes=[pltpu.VMEM((2, 2 * NA_KH - 1, GRID_W, 2 * GRID_W), F32)],
        compiler_params=_cparams(("arbitrary",)),
        name="na_bias",
    )(rpb.reshape(-1), jnp.asarray(shift), jnp.asarray(negmask))


def _na_attn_kernel(q_ref, k_ref, v_ref, bias_ref, o_ref, s_ref, e_ref, *, rows):
    g = pl.program_id(1)
    kb = jnp.clip(g * NA_QROWS - NA_KH // 2, 0, rows - NA_KROWS)
    k0 = pl.multiple_of(kb * GRID_W, GRID_W)
    nk = NA_KROWS * GRID_W
    dh = NA_HEAD_DIM
    lane = lax.broadcasted_iota(jnp.int32, (1, 2 * dh), 1)
    own = (lane < dh, lane >= dh)
    pair_cols = [slice(hp * 2 * dh, (hp + 1) * 2 * dh) for hp in range(NA_HEADS // 2)]
    for hp, cols in enumerate(pair_cols):
        q2 = q_ref[0, :, cols]
        k2 = k_ref[0, pl.ds(k0, nk), cols]
        for s in range(2):
            s_ref[2 * hp + s] = lax.dot_general(q2, jnp.where(own[s], k2, jnp.zeros_like(k2)),
                                                (((1,), (1,)), ((), ())), preferred_element_type=F32)
    for h in range(NA_HEADS):
        sc = s_ref[h] + bias_ref[0, h]
        e_ref[h] = jnp.exp2(sc - jnp.max(sc, axis=-1, keepdims=True)).astype(BF16)
    for hp, cols in enumerate(pair_cols):
        v2 = v_ref[0, pl.ds(k0, nk), cols]
        pair = []
        for s in range(2):
            r = jnp.dot(e_ref[2 * hp + s], jnp.where(own[s], v2, jnp.ones_like(v2)), preferred_element_type=F32)
            den = r[:, dh:dh + 1] if s == 0 else r[:, 0:1]
            pair.append(jnp.where(own[s], r / den, 0.0))
        o_ref[0, :, cols] = (pair[0] + pair[1]).astype(BF16)


def _na_attn(qkv, bias, rows):
    b, seq, three_w = qkv.shape
    width = three_w // 3
    nq, nk = NA_QROWS * GRID_W, NA_KROWS * GRID_W
    groups = rows // NA_QROWS

    def bias_idx(bi, g):
        return (jnp.where(g == 0, 0, jnp.where(g == groups - 1, 2, 1)), 0, 0, 0)

    return pl.pallas_call(
        functools.partial(_na_attn_kernel, rows=rows),
        grid=(b, groups),
        in_specs=[
            pl.BlockSpec((1, nq, width), lambda bi, g: (bi, g, 0)),
            pl.BlockSpec((1, seq, width), lambda bi, g: (bi, 0, 1)),
            pl.BlockSpec((1, seq, width), lambda bi, g: (bi, 0, 2)),
            pl.BlockSpec((1, NA_HEADS, nq, nk), bias_idx),
        ],
        out_specs=pl.BlockSpec((1, nq, width), lambda bi, g: (bi, g, 0)),
        out_shape=jax.ShapeDtypeStruct((b, seq, width), BF16),
        scratch_shapes=[pltpu.VMEM((NA_HEADS, nq, nk), F32), pltpu.VMEM((NA_HEADS, nq, nk), BF16)],
        compiler_params=_cparams(("arbitrary", "arbitrary")),
        name="na_attn",
    )(qkv, qkv, qkv, bias)


def _dft_tables():
    k = np.arange(RADIX)
    ang = 2.0 * np.pi * np.outer(k, k) / RADIX
    fr, fi = np.cos(ang), -np.sin(ang)
    ang_t = 2.0 * np.pi * np.outer(k, k) / (RADIX * RADIX)
    twr, twi = np.cos(ang_t), -np.sin(ang_t)
    f32 = lambda a: jnp.asarray(a, F32)
    return f32(fr), f32(fi), f32(twr), f32(twi)


def _embed(cr, ci):
    return jnp.concatenate([jnp.concatenate([cr, -ci], axis=1), jnp.concatenate([ci, cr], axis=1)], axis=0)


def _cmatmul(c_rows, x_lanes):
    m = c_rows.shape[0] // 2
    r = jnp.dot(c_rows, x_lanes, preferred_element_type=F32)
    return r[:m, :LANES] - r[m:, LANES:], r[:m, LANES:] + r[m:, :LANES]


def _load_pair(ref, rows):
    return jnp.concatenate([ref[0, rows, :], ref[1, rows, :]], axis=1).astype(BF16)


def _store_pair(ref, rows, re, im):
    ref[0, rows, :] = re
    ref[1, rows, :] = im


def _twiddled(fr, fi, twr, twi, scale_re=1.0, scale_im=1.0):
    return jnp.concatenate([(fr * twr - fi * twi) * scale_re, (fr * twi + fi * twr) * scale_im], axis=0).astype(BF16)


def _position_features(seq):
    t = np.linspace(0.0, 1.0, seq)[:, None]
    bands = (HY_EMB_DIM - 1) // 2
    w = (2.0 * math.pi / seq) * np.arange(seq)[:, None]
    f = np.linspace(1e-4, bands - 1, bands)[None, :]
    z = np.concatenate([t, np.cos(f * w), -np.sin(f * w)], axis=-1)
    zz = np.zeros((seq // 2, 2 * LANES), np.float32)
    zz[:, :HY_EMB_DIM] = z[:seq // 2]
    zz[:, LANES:LANES + HY_EMB_DIM] = z[seq // 2:]
    return zz


def _filt_mlp_kernel(z_ref, w1_ref, b1_ref, w2_ref, b2_ref, fr_ref, o_ref):
    hp = lax.Precision.HIGHEST
    fr = fr_ref[...]
    hid = LANES // 2
    h = jnp.sin(fr * (jnp.dot(z_ref[...], w1_ref[...], precision=hp, preferred_element_type=F32) + b1_ref[...]))
    h = jnp.sin(fr * (jnp.dot(h, w2_ref[...], precision=hp, preferred_element_type=F32) + b2_ref[...]))
    hi = h.astype(BF16).astype(F32)
    lo = h - hi
    hi_sw, lo_sw = pltpu.roll(hi, hid, 1), pltpu.roll(lo, hid, 1)
    low = lax.broadcasted_iota(jnp.int32, (1, LANES), 1) < hid
    o_ref[0, :, :LANES] = jnp.where(low, hi, lo_sw).astype(BF16)
    o_ref[0, :, LANES:] = jnp.where(low, hi, 0.0).astype(BF16)
    o_ref[1, :, :LANES] = jnp.where(low, hi_sw, lo).astype(BF16)
    o_ref[1, :, LANES:] = jnp.where(low, hi_sw, 0.0).astype(BF16)


def _filt_mlp(zz, fw1, fb1, fw2, fb2, freq):
    seq = zz.shape[0]
    hid = fw2.shape[0]
    assert 2 * hid == LANES
    w1 = jnp.zeros((2 * LANES, LANES), F32)
    w1 = w1.at[:HY_EMB_DIM, :hid].set(fw1).at[LANES:LANES + HY_EMB_DIM, hid:].set(fw1)
    w2 = jnp.zeros((LANES, LANES), F32).at[:hid, :hid].set(fw2).at[hid:, hid:].set(fw2)
    twice = lambda v: jnp.concatenate([v, v]).reshape(1, LANES)
    rows = 2048
    whole = lambda i: (0, 0)
    return pl.pallas_call(
        _filt_mlp_kernel,
        grid=(seq // rows,),
        in_specs=[pl.BlockSpec((rows, 2 * LANES), lambda i: (i, 0)),
                  pl.BlockSpec(w1.shape, whole), pl.BlockSpec((1, LANES), whole),
                  pl.BlockSpec(w2.shape, whole), pl.BlockSpec((1, LANES), whole), pl.BlockSpec((1, LANES), whole)],
        out_specs=pl.BlockSpec((2, rows, 2 * LANES), lambda i: (0, i, 0)),
        out_shape=jax.ShapeDtypeStruct((2, seq, 2 * LANES), BF16),
        compiler_params=_cparams(("arbitrary",)),
        name="filt_mlp",
    )(zz, w1, twice(fb1), w2, twice(fb2), twice(freq))


def _filt_time_kernel(ha_ref, hb_ref, perm_ref, w3_ref, b3_ref, delta_ref, k_ref, ss_ref, hsel_ref, *, seq):
    i = pl.program_id(0)
    rows = ha_ref.shape[0]
    half = seq // rows

    @pl.when(i < half)
    def _():
        hsel_ref[...] = ha_ref[...]

    @pl.when(i >= half)
    def _():
        for s in range(rows // RADIX):
            lo_row = rows - RADIX * (s + 1)
            nxt = hb_ref[0:RADIX, :] if s == 0 else ha_ref[lo_row + RADIX:lo_row + 2 * RADIX, :]
            src = jnp.concatenate([ha_ref[lo_row:lo_row + RADIX, :], nxt], axis=0)
            hsel_ref[s * RADIX:(s + 1) * RADIX, :] = jnp.dot(perm_ref[...], src, preferred_element_type=F32).astype(BF16)

    w = w3_ref[0]
    w_hi = w.astype(BF16)
    w_lo = (w - w_hi.astype(F32)).astype(BF16)
    wcat = jnp.concatenate([w_hi, w_hi, w_lo, jnp.zeros_like(w_lo)], axis=0)
    hf = jnp.dot(hsel_ref[...], wcat, preferred_element_type=F32) + b3_ref[0]
    m = i * rows + lax.broadcasted_iota(jnp.int32, (rows, 1), 0)
    pos = jnp.where(m < seq, m, 2 * seq - m).astype(F32)
    t = pos / (seq - 1)
    k = jnp.where(m == seq, 0.0, hf * jnp.exp(-t * delta_ref[...]))
    for j in range(rows // RADIX):
        k_ref[j * PITCH:j * PITCH + RADIX, :] = k[j * RADIX:(j + 1) * RADIX]
        k_ref[j * PITCH + RADIX:(j + 1) * PITCH, :] = jnp.zeros((PITCH - RADIX, k.shape[1]), F32)
    part = jnp.sum(k * k, axis=0, keepdims=True)

    @pl.when(i == 0)
    def _():
        ss_ref[...] = part

    @pl.when(i != 0)
    def _():
        ss_ref[...] += part


def _filt_time(hdn, w3d, b3d, delta2):
    seq, packed = hdn.shape
    _, hid, cols = w3d.shape
    assert packed == 4 * hid
    n2l = 2 * seq
    rows = 1024
    half = seq // rows
    prows = rows // RADIX * PITCH
    perm = np.zeros((RADIX, 2 * RADIX), np.float32)
    perm[0, RADIX] = 1.0
    perm[np.arange(1, RADIX), RADIX - np.arange(1, RADIX)] = 1.0
    return pl.pallas_call(
        functools.partial(_filt_time_kernel, seq=seq),
        grid=(n2l // rows,),
        in_specs=[pl.BlockSpec((rows, packed), lambda i: (jnp.where(i < half, i, 2 * half - 1 - i), 0)),
                  pl.BlockSpec((rows, packed), lambda i: (jnp.clip(2 * half - i, 0, half - 1), 0)),
                  pl.BlockSpec(perm.shape, lambda i: (0, 0)),
                  pl.BlockSpec((1, hid, cols), lambda i: (i // half, 0, 0)),
                  pl.BlockSpec((1, 1, cols), lambda i: (i // half, 0, 0)),
                  pl.BlockSpec((1, cols), lambda i: (0, 0))],
        out_specs=[pl.BlockSpec((prows, cols), lambda i: (i, 0)),
                   pl.BlockSpec((1, cols), lambda i: (0, 0))],
        out_shape=[jax.ShapeDtypeStruct((n2l // RADIX * PITCH, cols), F32), jax.ShapeDtypeStruct((1, cols), F32)],
        scratch_shapes=[pltpu.VMEM((rows, packed), BF16)],
        compiler_params=_cparams(("arbitrary",)),
        name="filt_time",
    )(hdn, hdn, jnp.asarray(perm, BF16), w3d, b3d, delta2)


def _filt_spec_kernel(k_ref, ss_ref, fa_ref, fr_ref, fi_ref, twr_ref, twi_ref, o_ref, a_ref):
    r = RADIX
    scale = lax.rsqrt(ss_ref[...] + 1e-12)

    def a_body(i, c):
        n2 = 2 * i
        rhs = jnp.concatenate([k_ref[pl.ds(n2, r, stride=PITCH), :], k_ref[pl.ds(n2 + 1, r, stride=PITCH), :]],
                              axis=1).astype(BF16)
        out = jnp.dot(fa_ref[...], rhs, preferred_element_type=F32)
        _store_pair(a_ref, pl.ds(n2, r, stride=PITCH), out[:r, :LANES], out[r:, :LANES])
        _store_pair(a_ref, pl.ds(n2 + 1, r, stride=PITCH), out[:r, LANES:], out[r:, LANES:])
        return c

    lax.fori_loop(0, r // 2, a_body, 0, unroll=DFT_UNROLL_STRIDED // 2)

    def b_body(k1, c):
        r0 = pl.multiple_of(k1 * PITCH, 8)
        o0 = pl.multiple_of(k1 * r, r)
        blk = _load_pair(a_ref, pl.ds(r0, r))
        sr, si = _cmatmul(_twiddled(fr_ref[...], fi_ref[...], twr_ref[pl.ds(k1, 1), :], twi_ref[pl.ds(k1, 1), :]), blk)
        o_ref[0, pl.ds(o0, r), :] = (sr * scale).astype(BF16)
        o_ref[1, pl.ds(o0, r), :] = (si * scale).astype(BF16)
        return c

    lax.fori_loop(0, r, b_body, 0, unroll=DFT_UNROLL_STRIDED)


def _filt_spec(ktime, ss, fa_real, fr, fi, twr, twi):
    prows, cols = ktime.shape
    n2l = prows // PITCH * RADIX
    whole = lambda j: (0, 0)
    return pl.pallas_call(
        _filt_spec_kernel,
        grid=(cols // LANES,),
        in_specs=[pl.BlockSpec((prows, LANES), lambda j: (0, j)),
                  pl.BlockSpec((1, LANES), lambda j: (0, j)),
                  _resident(fa_real.shape, whole), _resident(fr.shape, whole), _resident(fi.shape, whole),
                  _resident(twr.shape, whole), _resident(twi.shape, whole)],
        out_specs=pl.BlockSpec((2, n2l, LANES), lambda j: (0, 0, j)),
        out_shape=jax.ShapeDtypeStruct((2, n2l, cols), BF16),
        scratch_shapes=[pltpu.VMEM((2, prows, LANES), F32)],
        compiler_params=_cparams(("arbitrary",)),
        name="filt_spec",
    )(ktime, ss, fa_real, fr, fi, twr, twi)


def _short_conv_rows(u_ref, bi, r0, nrows, seq, w_ref, b_ref):
    edge = 16
    cur = u_ref[bi, pl.ds(r0, nrows), :].astype(F32)
    before = u_ref[bi, pl.ds(pl.multiple_of(jnp.maximum(r0 - edge, 0), edge), edge), :].astype(F32)
    after = u_ref[bi, pl.ds(pl.multiple_of(jnp.minimum(r0 + nrows, seq - edge), edge), edge), :].astype(F32)
    prev_edge = jnp.where(r0 > 0, before[edge - 1:edge, :], 0.0)
    next_edge = jnp.where(r0 + nrows < seq, after[0:1, :], 0.0)
    w0, w1, w2 = w_ref[0:1, :], w_ref[1:2, :], w_ref[2:3, :]
    out = b_ref[...] + w0 * pltpu.roll(cur, 1, 0) + w1 * cur + w2 * pltpu.roll(cur, nrows - 1, 0)
    row = lax.broadcasted_iota(jnp.int32, (8, cur.shape[1]), 0)
    first = out[0:8] + jnp.where(row == 0, w0 * (prev_edge - cur[nrows - 1:nrows]), 0.0)
    last = out[nrows - 8:] + jnp.where(row == 7, w2 * (next_edge - cur[0:1]), 0.0)
    return jnp.concatenate([first, out[8:nrows - 8], last], axis=0)


def _hy_conv_kernel(z_ref, zw_ref, zb_ref, u_ref, uw_ref, ub_ref, ks_hbm, hb_ref,
                    fa_ref, fc_ref, fr_ref, fi_ref, twr_ref, twi_ref, o_ref, a_ref, v_ref, ks_ref, ks_sem,
                    *, conv_in, ks_tile0):
    r = RADIX
    half = r // 2
    nb, seq, _ = z_ref.shape
    nchunk = seq // CONV_ROWS
    blocks = CONV_ROWS // r

    ks_col = pl.multiple_of((ks_tile0 + pl.program_id(0)) * LANES, LANES)
    ks_copy = pltpu.make_async_copy(ks_hbm.at[:, :, pl.ds(ks_col, LANES)], ks_ref, ks_sem)
    ks_copy.start()

    def load_body(c, carry):
        r0 = pl.multiple_of(c * CONV_ROWS, CONV_ROWS)
        for bi in range(nb):
            if conv_in:
                v = _short_conv_rows(z_ref, bi, r0, CONV_ROWS, seq, zw_ref, zb_ref)
                o_ref[bi, pl.ds(r0, CONV_ROWS), :] = v.astype(BF16)
            else:
                v = z_ref[bi, pl.ds(r0, CONV_ROWS), :].astype(F32)
            for j in range(blocks):
                p0 = pl.multiple_of((c * blocks + j) * PITCH, 8)
                v_ref[bi, pl.ds(p0, r), :] = v[j * r:(j + 1) * r]
        return carry

    lax.fori_loop(0, nchunk, load_body, 0)

    def pair_store(ref, rows0, rows1, out):
        _store_pair(ref, rows0, out[:r, :LANES], out[r:, :LANES])
        _store_pair(ref, rows1, out[:r, LANES:], out[r:, LANES:])

    def a_body(i, carry):
        n2 = 2 * i

        def column(n):
            return jnp.concatenate([v_ref[0, pl.ds(n, half, stride=PITCH), :],
                                    v_ref[1, pl.ds(n, half, stride=PITCH), :]], axis=0)

        rhs = jnp.concatenate([column(n2), column(n2 + 1)], axis=1).astype(BF16)
        out = jnp.dot(fa_ref[...], rhs, preferred_element_type=F32)
        pair_store(a_ref, pl.ds(n2, r, stride=PITCH), pl.ds(n2 + 1, r, stride=PITCH), out)
        return carry

    lax.fori_loop(0, r // 2, a_body, 0, unroll=DFT_UNROLL_STRIDED // 2)

    def b_forward(k1):
        r0 = pl.multiple_of(k1 * PITCH, 8)
        s0 = pl.multiple_of(k1 * r, r)
        blk = _load_pair(a_ref, pl.ds(r0, r))
        sr, si = _cmatmul(_twiddled(fr_ref[...], fi_ref[...], twr_ref[pl.ds(k1, 1), :], twi_ref[pl.ds(k1, 1), :]), blk)
        kr = ks_ref[0, pl.ds(s0, r), :].astype(F32)
        ki = ks_ref[1, pl.ds(s0, r), :].astype(F32)
        return jnp.concatenate([sr * kr - si * ki, sr * ki + si * kr], axis=0).astype(BF16)

    def b_forward_pair(i):
        return jnp.concatenate([b_forward(2 * i), b_forward(2 * i + 1)], axis=1)

    def b_inverse_pair(i, p):
        out = jnp.dot(fc_ref[...], p, preferred_element_type=F32)
        pair_store(a_ref, pl.ds(pl.multiple_of(2 * i * PITCH, 8), r), pl.ds(pl.multiple_of((2 * i + 1) * PITCH, 8), r), out)

    def b_body(i, p):
        p_next = b_forward_pair(i + 1)
        b_inverse_pair(i, p)
        return p_next

    ks_copy.wait()
    b_inverse_pair(r // 2 - 1, lax.fori_loop(0, r // 2 - 1, b_body, b_forward_pair(0), unroll=DFT_UNROLL // 2))

    inv_n = 1.0 / (r * r)

    def c_body(n2, carry):
        g = _twiddled(fr_ref[0:half, :], fi_ref[0:half, :], twr_ref[pl.ds(n2, 1), :], twi_ref[pl.ds(n2, 1), :],
                      inv_n, -inv_n)
        yr, yi = _cmatmul(g, _load_pair(a_ref, pl.ds(n2, r, stride=PITCH)))
        _store_pair(v_ref, pl.ds(n2, half, stride=PITCH), yr, yi)
        return carry

    lax.fori_loop(0, r, c_body, 0, unroll=DFT_UNROLL_STRIDED)

    def out_body(c, carry):
        r0 = pl.multiple_of(c * CONV_ROWS, CONV_ROWS)
        for bi in range(nb):
            gate = _short_conv_rows(u_ref, bi, r0, CONV_ROWS, seq, uw_ref, ub_ref)
            conv = jnp.concatenate(
                [v_ref[bi, pl.ds(pl.multiple_of((c * blocks + j) * PITCH, 8), r), :] for j in range(blocks)], axis=0)
            zp = (o_ref if conv_in else z_ref)[bi, pl.ds(r0, CONV_ROWS), :].astype(F32)
            o_ref[bi, pl.ds(r0, CONV_ROWS), :] = (gate * (conv + hb_ref[...] * zp)).astype(BF16)
        return carry

    lax.fori_loop(0, nchunk, out_body, 0)


def _hy_conv(z, z_part, u, u_part, order, short_w, short_b, kspec, hy_bias, tables, conv_in):
    nb, seq, _ = u.shape
    width = hy_bias.shape[1]
    tiles = width // LANES
    fa, fc, fr, fi, twr, twi = tables
    whole = lambda j: (0, 0)
    zcol = lambda j: (0, 0, z_part * tiles + j)
    ucol = lambda j: (0, 0, u_part * tiles + j)
    return pl.pallas_call(
        functools.partial(_hy_conv_kernel, conv_in=conv_in, ks_tile0=order * tiles),
        grid=(tiles,),
        in_specs=[
            pl.BlockSpec((nb, seq, LANES), zcol),
            pl.BlockSpec((3, LANES), lambda j: (0, z_part * tiles + j)),
            pl.BlockSpec((1, LANES), lambda j: (0, z_part * tiles + j)),
            pl.BlockSpec((nb, seq, LANES), ucol),
            pl.BlockSpec((3, LANES), lambda j: (0, u_part * tiles + j)),
            pl.BlockSpec((1, LANES), lambda j: (0, u_part * tiles + j)),
            pl.BlockSpec(memory_space=pl.ANY),
            pl.BlockSpec((1, LANES), lambda j: (0, j)),
            _resident(fa.shape, whole), _resident(fc.shape, whole), _resident(fr.shape, whole),
            _resident(fi.shape, whole), _resident(twr.shape, whole), _resident(twi.shape, whole),
        ],
        out_specs=_resident((nb, seq, LANES), lambda j: (0, 0, j)),
        out_shape=jax.ShapeDtypeStruct((nb, seq, width), BF16),
        scratch_shapes=[pltpu.VMEM((2, 2 * seq // RADIX * PITCH, LANES), F32),
                        pltpu.VMEM((nb, seq // RADIX * PITCH, LANES), F32),
                        pltpu.VMEM((2, 2 * seq, LANES), BF16),
                        pltpu.SemaphoreType.DMA(())],
        compiler_params=_cparams(("arbitrary",)),
        name=f"hy_conv{order}",
    )(z, short_w, short_b, u, short_w, short_b, kspec, hy_bias[order:order + 1], fa, fc, fr, fi, twr, twi)


def _hyena(hyu, short_w, short_b, fw1, fb1, fw2, fb2, fw3, fb3, freq, hy_bias):
    nb, seq, _ = hyu.shape
    assert nb == 2 and 2 * seq == RADIX * RADIX
    width = hy_bias.shape[1]
    hid = fw2.shape[0]
    fr, fi, twr, twi = _dft_tables()
    half = RADIX // 2
    fa_real = jnp.concatenate([fr, fi], axis=0).astype(BF16)
    fa_half = _embed(fr[:, :half], fi[:, :half]).astype(BF16)
    fc = _embed(fr, -fi).astype(BF16)

    zz = jnp.asarray(_position_features(seq))
    hdn = _filt_mlp(zz, fw1, fb1, fw2, fb2, freq).reshape(seq, 4 * hid)
    w3d = fw3.reshape(hid, 2, HY_ORDER * width).transpose(1, 0, 2)
    b3d = fb3.reshape(2, 1, HY_ORDER * width)
    max_decay = math.log(HY_DECAY_TARGET) / HY_FAST_DECAY_PCT
    min_decay = math.log(HY_DECAY_TARGET) / HY_SLOW_DECAY_PCT
    delta = np.abs(np.linspace(min_decay, max_decay, width)).astype(np.float32)
    delta2 = jnp.asarray(np.tile(delta, HY_ORDER)[None, :])
    ktime, ss = _filt_time(hdn, w3d, b3d, delta2)
    kspec = _filt_spec(ktime, ss, fa_real, fr, fi, twr, twi)

    tables = (fa_half, fc, fr, fi, twr, twi)
    sb = short_b.reshape(1, -1)
    z1 = _hy_conv(hyu, 0, hyu, 1, 0, short_w, sb, kspec, hy_bias, tables, conv_in=True)
    return _hy_conv(z1, 0, hyu, 2, 1, short_w, sb, kspec, hy_bias, tables, conv_in=False)


def kernel(x, ln1_g, ln1_b, ffn1_w1, ffn1_w3, ffn1_w2, w_in, b_gate, na_rpb, hy_short_w, hy_short_b,
           hy_filt_w1, hy_filt_b1, hy_filt_w2, hy_filt_b2, hy_filt_w3, hy_filt_b3, hy_filt_freq, hy_bias,
           w_pa, w_pb, w_out, ln2_g, ln2_b, ffn2_w1, ffn2_w3, ffn2_w2, ln3_g, ln3_b):
    assert ln1_g.shape[0] == DEPTH
    b, seq, d = x.shape
    rows = seq // GRID_W
    na_width = NA_HEADS * NA_HEAD_DIM
    n_qkv = 3 * na_width
    n_hy = hy_short_w.shape[2]
    vec = lambda a: a[0].reshape(1, -1)

    x1, qkv, hyu, gates = _dense_in(
        x.reshape(b * seq, d), ffn1_w1[0], ffn1_w3[0], ffn1_w2[0], vec(ln1_g), vec(ln1_b),
        w_in[0], vec(b_gate), n_qkv, n_hy)

    bias = _na_bias(na_rpb[0], rows)
    ya = _na_attn(qkv.reshape(b, seq, n_qkv), bias, rows)

    yb = _hyena(hyu.reshape(b, seq, n_hy), hy_short_w[0], hy_short_b[0], hy_filt_w1[0], hy_filt_b1[0],
                hy_filt_w2[0], hy_filt_b2[0], hy_filt_w3[0], hy_filt_b3[0], hy_filt_freq[0], hy_bias[0])

    out = _dense_out(
        x1, ya.reshape(b * seq, na_width), yb.reshape(b * seq, -1), gates,
        w_pa[0], w_pb[0], w_out[0], vec(ln2_g), vec(ln2_b),
        ffn2_w1[0], ffn2_w3[0], ffn2_w2[0], vec(ln3_g), vec(ln3_b))
    return out.reshape(b, seq, d)
```

```python
import functools
import math

import numpy as np
import jax
import jax.numpy as jnp
from jax import lax
from jax.experimental import pallas as pl
from jax.experimental.pallas import tpu as pltpu

F32 = jnp.float32
BF16 = jnp.bfloat16

GRID_W = 64
NA_HEADS = 8
NA_HEAD_DIM = 64
NA_KH = 8
NA_KW = 16
HY_ORDER = 2
HY_EMB_DIM = 33
HY_FAST_DECAY_PCT = 0.3
HY_SLOW_DECAY_PCT = 1.5
HY_DECAY_TARGET = 1e-2
DEPTH = 1
DN_ALPHA = (2 * DEPTH) ** 0.25
LN_EPS = 1e-5
LOG2E = math.log2(math.e)

LANES = 128
VMEM_LIMIT = 60 * 1024 * 1024

ROW_TILE = 512
FF_CHUNK = 256
PROJ_CHUNK = 512
CAST_ROWS = 64
CAST_SLOTS = 8
NA_QROWS = 4
NA_KROWS = NA_QROWS + NA_KH
NA_BIAS_GROUP = 5
RADIX = 128
PITCH = RADIX + 8
DFT_UNROLL = 8
DFT_UNROLL_STRIDED = 32
CONV_ROWS = 512
NEG = -1e30


def _cparams(sem):
    return pltpu.CompilerParams(dimension_semantics=sem, vmem_limit_bytes=VMEM_LIMIT)


def _resident(shape, index_map):
    return pl.BlockSpec(shape, index_map, pipeline_mode=pl.Buffered(1))


def _cast_weights_in(pairs, stage_ref, sem):
    ahead = CAST_SLOTS - 1
    counts = [src.shape[0] // CAST_ROWS for src, _ in pairs]
    assert all(n >= ahead for n in counts)
    bases = [sum(counts[:w]) for w in range(len(pairs))]

    def copy(w, c, g):
        src, cols = pairs[w][0], pairs[w][0].shape[1]
        slot = g % CAST_SLOTS
        return pltpu.make_async_copy(src.at[pl.ds(c * CAST_ROWS, CAST_ROWS), :],
                                     stage_ref.at[slot, :, pl.ds(0, cols)], sem.at[slot])

    for g in range(ahead):
        copy(0, g, g).start()
    for w, (src, dst) in enumerate(pairs):
        n, cols = counts[w], src.shape[1]

        def body(c, carry, w=w, n=n, cols=cols, dst=dst):
            g = bases[w] + c

            @pl.when(c + ahead < n)
            def _():
                copy(w, c + ahead, g + ahead).start()

            if w + 1 < len(pairs):
                @pl.when(c + ahead >= n)
                def _():
                    copy(w + 1, c + ahead - n, g + ahead).start()

            copy(w, c, g).wait()
            r0 = pl.multiple_of(c * CAST_ROWS, CAST_ROWS)
            dst[pl.ds(r0, CAST_ROWS), :] = stage_ref[g % CAST_SLOTS, :, 0:cols].astype(BF16)
            return carry

        lax.fori_loop(0, n, body, 0)


def _layer_norm(r, g, b):
    mu = jnp.mean(r, axis=-1, keepdims=True)
    d = r - mu
    var = jnp.mean(d * d, axis=-1, keepdims=True)
    return d * lax.rsqrt(var + LN_EPS) * g + b


def _swiglu_ln(x, xb_ref, acc_ref, w1_ref, w3_ref, w2_ref, g_ref, b_ref):
    d_ff = w1_ref.shape[1]
    xb_ref[...] = x.astype(BF16)
    for j in range(d_ff // FF_CHUNK):
        sl = slice(j * FF_CHUNK, (j + 1) * FF_CHUNK)
        a = jnp.dot(xb_ref[...], w1_ref[:, sl], preferred_element_type=F32)
        b = jnp.dot(xb_ref[...], w3_ref[:, sl], preferred_element_type=F32)
        h = (a * jax.nn.sigmoid(a) * b).astype(BF16)
        part = jnp.dot(h, w2_ref[sl, :], preferred_element_type=F32)
        if j == 0:
            acc_ref[...] = part
        else:
            acc_ref[...] += part
    return _layer_norm(DN_ALPHA * x + 0.5 * acc_ref[...], g_ref[...], b_ref[...])


def _dense_in_kernel(x_ref, w1_hbm, w3_hbm, w2_hbm, g_ref, b_ref, win_hbm, bg_ref,
                     x1_ref, qkv_ref, hyu_ref, gate_ref,
                     xb_ref, acc_ref, w1_ref, w3_ref, w2_ref, win_ref, stage_ref, sem):
    @pl.when(pl.program_id(0) == 0)
    def _():
        _cast_weights_in(((w1_hbm, w1_ref), (w3_hbm, w3_ref), (w2_hbm, w2_ref), (win_hbm, win_ref)), stage_ref, sem)

    x1 = _swiglu_ln(x_ref[...], xb_ref, acc_ref, w1_ref, w3_ref, w2_ref, g_ref, b_ref)
    x1_ref[...] = x1
    xb_ref[...] = x1.astype(BF16)
    n_qkv = qkv_ref.shape[1]
    n_hy = hyu_ref.shape[1]
    n_head = n_qkv // 3
    for j in range(win_ref.shape[1] // PROJ_CHUNK):
        c0 = j * PROJ_CHUNK
        p = jnp.dot(xb_ref[...], win_ref[:, c0:c0 + PROJ_CHUNK], preferred_element_type=F32)
        if c0 < n_qkv:
            if c0 < n_head:
                p = p * (NA_HEAD_DIM ** -0.5 * LOG2E)
            qkv_ref[:, c0:c0 + PROJ_CHUNK] = p.astype(BF16)
        elif c0 < n_qkv + n_hy:
            hyu_ref[:, c0 - n_qkv:c0 - n_qkv + PROJ_CHUNK] = p.astype(BF16)
        else:
            g0 = c0 - n_qkv - n_hy
            gate_ref[:, g0:g0 + PROJ_CHUNK] = jax.nn.sigmoid(p + bg_ref[:, g0:g0 + PROJ_CHUNK]).astype(BF16)


def _dense_in(x2d, w1, w3, w2, g, b, w_in, b_gate, n_qkv, n_hy):
    n, d = x2d.shape
    d_ff = w1.shape[1]
    n_gate = w_in.shape[1] - n_qkv - n_hy
    assert n % ROW_TILE == 0 and d_ff % FF_CHUNK == 0
    assert n_qkv % (3 * PROJ_CHUNK) == 0 and n_hy % PROJ_CHUNK == 0 and n_gate % PROJ_CHUNK == 0
    weights = (w1, w3, w2, w_in)
    assert all(w.dtype == F32 and w.shape[0] % CAST_ROWS == 0 for w in weights)
    row = lambda i: (i, 0)
    whole = lambda i: (0, 0)
    hbm = pl.BlockSpec(memory_space=pl.ANY)
    return pl.pallas_call(
        _dense_in_kernel,
        grid=(n // ROW_TILE,),
        in_specs=[
            pl.BlockSpec((ROW_TILE, d), row),
            hbm, hbm, hbm,
            _resident((1, d), whole), _resident((1, d), whole),
            hbm, _resident((1, n_gate), whole),
        ],
        out_specs=[
            pl.BlockSpec((ROW_TILE, d), row),
            pl.BlockSpec((ROW_TILE, n_qkv), row),
            pl.BlockSpec((ROW_TILE, n_hy), row),
            pl.BlockSpec((ROW_TILE, n_gate), row),
        ],
        out_shape=[
            jax.ShapeDtypeStruct((n, d), F32),
            jax.ShapeDtypeStruct((n, n_qkv), BF16),
            jax.ShapeDtypeStruct((n, n_hy), BF16),
            jax.ShapeDtypeStruct((n, n_gate), BF16),
        ],
        scratch_shapes=[pltpu.VMEM((ROW_TILE, d), BF16), pltpu.VMEM((ROW_TILE, d), F32)]
        + [pltpu.VMEM(w.shape, BF16) for w in weights]
        + [pltpu.VMEM((CAST_SLOTS, CAST_ROWS, max(w.shape[1] for w in weights)), F32),
           pltpu.SemaphoreType.DMA((CAST_SLOTS,))],
        compiler_params=_cparams(("arbitrary",)),
        name="dense_in",
    )(x2d, w1, w3, w2, g, b, w_in, b_gate)


def _dense_out_kernel(x1_ref, ya_ref, yb_ref, gate_ref, wpa_hbm, wpb_hbm, wout_hbm, g2_ref, b2_ref,
                      w1_hbm, w3_hbm, w2_hbm, g3_ref, b3_ref, o_ref,
                      xb_ref, acc_ref, wpa_ref, wpb_ref, wout_ref, w1_ref, w3_ref, w2_ref, stage_ref, sem):
    @pl.when(pl.program_id(0) == 0)
    def _():
        _cast_weights_in(((wpa_hbm, wpa_ref), (wpb_hbm, wpb_ref), (wout_hbm, wout_ref),
                          (w1_hbm, w1_ref), (w3_hbm, w3_ref), (w2_hbm, w2_ref)), stage_ref, sem)

    d = x1_ref.shape[1]
    pa = jnp.dot(ya_ref[...], wpa_ref[...], preferred_element_type=F32)
    pb = jnp.dot(yb_ref[...], wpb_ref[...], preferred_element_type=F32)
    m = gate_ref[:, :d].astype(F32) * pa + gate_ref[:, d:].astype(F32) * pb
    mix = jnp.dot(m.astype(BF16), wout_ref[...], preferred_element_type=F32)
    x2 = _layer_norm(DN_ALPHA * x1_ref[...] + mix, g2_ref[...], b2_ref[...])
    o_ref[...] = _swiglu_ln(x2, xb_ref, acc_ref, w1_ref, w3_ref, w2_ref, g3_ref, b3_ref)


def _dense_out(x1, ya, yb, gates, wpa, wpb, wout, g2, b2, w1, w3, w2, g3, b3):
    n, d = x1.shape
    weights = (wpa, wpb, wout, w1, w3, w2)
    assert all(w.dtype == F32 and w.shape[0] % CAST_ROWS == 0 for w in weights)
    row = lambda i: (i, 0)
    whole = lambda i: (0, 0)
    hbm = pl.BlockSpec(memory_space=pl.ANY)
    return pl.pallas_call(
        _dense_out_kernel,
        grid=(n // ROW_TILE,),
        in_specs=[
            pl.BlockSpec((ROW_TILE, d), row),
            pl.BlockSpec((ROW_TILE, ya.shape[1]), row),
            pl.BlockSpec((ROW_TILE, yb.shape[1]), row),
            pl.BlockSpec((ROW_TILE, gates.shape[1]), row),
            hbm, hbm, hbm,
            _resident((1, d), whole), _resident((1, d), whole),
            hbm, hbm, hbm,
            _resident((1, d), whole), _resident((1, d), whole),
        ],
        out_specs=pl.BlockSpec((ROW_TILE, d), row),
        out_shape=jax.ShapeDtypeStruct((n, d), F32),
        scratch_shapes=[pltpu.VMEM((ROW_TILE, d), BF16), pltpu.VMEM((ROW_TILE, d), F32)]
        + [pltpu.VMEM(w.shape, BF16) for w in weights]
        + [pltpu.VMEM((CAST_SLOTS, CAST_ROWS, max(w.shape[1] for w in weights)), F32),
           pltpu.SemaphoreType.DMA((CAST_SLOTS,))],
        compiler_params=_cparams(("arbitrary",)),
        name="dense_out",
    )(x1, ya, yb, gates, wpa, wpb, wout, g2, b2, w1, w3, w2, g3, b3)


def _window_start(n, k):
    return np.clip(np.arange(n) - k // 2, 0, n - k)


def _na_group_types(rows):
    last = rows - NA_QROWS
    return ((0, 0), (NA_QROWS * 2, NA_QROWS), (last, rows - NA_KROWS))


def _na_tables(rows):
    w = GRID_W
    shift = np.zeros((2 * NA_KW - 1, w, 2 * w), np.float32)
    q = np.arange(w)[:, None]
    kc = np.arange(w)[None, :]
    for dc in range(2 * NA_KW - 1):
        shift[dc, :, :w] = (kc - q + NA_KW - 1 == dc).astype(np.float32)
    cs = _window_start(w, NA_KW)
    col_ok = (kc >= cs[:, None]) & (kc < cs[:, None] + NA_KW)
    rs_all = _window_start(rows, NA_KH)
    negmask = np.full((3, NA_QROWS, w, NA_KROWS, w), NEG, np.float32)
    for t, (r0, kb) in enumerate(_na_group_types(rows)):
        for i in range(NA_QROWS):
            rs = rs_all[r0 + i]
            for j in range(NA_KROWS):
                if rs <= kb + j < rs + NA_KH:
                    negmask[t, i, :, j, :] = np.where(col_ok, 0.0, NEG)
    return shift, negmask.reshape(3, NA_QROWS * w, NA_KROWS * w)


def _na_bias_kernel(rpb_ref, shift_ref, neg_ref, o_ref, u_ref, *, rows):
    h = pl.program_id(0)
    n_dr = 2 * NA_KH - 1
    n_dc = 2 * NA_KW - 1
    w = GRID_W

    for dr0 in range(0, n_dr, NA_BIAS_GROUP):
        base = (h * n_dr + dr0) * n_dc
        accs = [None] * NA_BIAS_GROUP
        for dc in range(n_dc):
            mask = shift_ref[dc]
            for j in range(NA_BIAS_GROUP):
                term = rpb_ref[base + j * n_dc + dc] * mask
                accs[j] = term if dc == 0 else accs[j] + term
        for j in range(NA_BIAS_GROUP):
            left = accs[j] * LOG2E
            u_ref[0, dr0 + j] = left
            u_ref[1, dr0 + j] = pltpu.roll(left, w, 1)
    rs_all = _window_start(rows, NA_KH)
    for t, (r0, kb) in enumerate(_na_group_types(rows)):
        for i in range(NA_QROWS):
            rs = rs_all[r0 + i]
            for jj in range(NA_KROWS // 2):
                val = neg_ref[t, i * w:(i + 1) * w, jj * 2 * w:(jj + 1) * 2 * w]
                for s in range(2):
                    kr = kb + 2 * jj + s
                    if rs <= kr < rs + NA_KH:
                        val = val + u_ref[s, kr - (r0 + i) + NA_KH - 1]
                o_ref[t, 0, i * w:(i + 1) * w, jj * 2 * w:(jj + 1) * 2 * w] = val


def _na_bias(rpb, rows):
    shift, negmask = _na_tables(rows)
    nq, nk = NA_QROWS * GRID_W, NA_KROWS * GRID_W
    return pl.pallas_call(
        functools.partial(_na_bias_kernel, rows=rows),
        grid=(NA_HEADS,),
        in_specs=[
            pl.BlockSpec(memory_space=pltpu.SMEM),
            _resident(shift.shape, lambda h: (0, 0, 0)),
            _resident(negmask.shape, lambda h: (0, 0, 0)),
        ],
        out_specs=pl.BlockSpec((3, 1, nq, nk), lambda h: (0, h, 0, 0)),
        out_shape=jax.ShapeDtypeStruct((3, NA_HEADS, nq, nk), F32),
        scratch_shapes=[pltpu.VMEM((2, 2 * NA_KH - 1, GRID_W, 2 * GRID_W), F32)],
        compiler_params=_cparams(("arbitrary",)),
        name="na_bias",
    )(rpb.reshape(-1), jnp.asarray(shift), jnp.asarray(negmask))


def _na_attn_kernel(q_ref, k_ref, v_ref, bias_ref, o_ref, s_ref, e_ref, *, rows):
    g = pl.program_id(1)
    kb = jnp.clip(g * NA_QROWS - NA_KH // 2, 0, rows - NA_KROWS)
    k0 = pl.multiple_of(kb * GRID_W, GRID_W)
    nk = NA_KROWS * GRID_W
    dh = NA_HEAD_DIM
    lane = lax.broadcasted_iota(jnp.int32, (1, 2 * dh), 1)
    own = (lane < dh, lane >= dh)
    pair_cols = [slice(hp * 2 * dh, (hp + 1) * 2 * dh) for hp in range(NA_HEADS // 2)]
    for hp, cols in enumerate(pair_cols):
        q2 = q_ref[0, :, cols]
        k2 = k_ref[0, pl.ds(k0, nk), cols]
        for s in range(2):
            s_ref[2 * hp + s] = lax.dot_general(q2, jnp.where(own[s], k2, jnp.zeros_like(k2)),
                                                (((1,), (1,)), ((), ())), preferred_element_type=F32)
    for h in range(NA_HEADS):
        sc = s_ref[h] + bias_ref[0, h]
        e_ref[h] = jnp.exp2(sc - jnp.max(sc, axis=-1, keepdims=True)).astype(BF16)
    for hp, cols in enumerate(pair_cols):
        v2 = v_ref[0, pl.ds(k0, nk), cols]
        pair = []
        for s in range(2):
            r = jnp.dot(e_ref[2 * hp + s], jnp.where(own[s], v2, jnp.ones_like(v2)), preferred_element_type=F32)
            den = r[:, dh:dh + 1] if s == 0 else r[:, 0:1]
            pair.append(jnp.where(own[s], r / den, 0.0))
        o_ref[0, :, cols] = (pair[0] + pair[1]).astype(BF16)


def _na_attn(qkv, bias, rows):
    b, seq, three_w = qkv.shape
    width = three_w // 3
    nq, nk = NA_QROWS * GRID_W, NA_KROWS * GRID_W
    groups = rows // NA_QROWS

    def bias_idx(bi, g):
        return (jnp.where(g == 0, 0, jnp.where(g == groups - 1, 2, 1)), 0, 0, 0)

    return pl.pallas_call(
        functools.partial(_na_attn_kernel, rows=rows),
        grid=(b, groups),
        in_specs=[
            pl.BlockSpec((1, nq, width), lambda bi, g: (bi, g, 0)),
            pl.BlockSpec((1, seq, width), lambda bi, g: (bi, 0, 1)),
            pl.BlockSpec((1, seq, width), lambda bi, g: (bi, 0, 2)),
            pl.BlockSpec((1, NA_HEADS, nq, nk), bias_idx),
        ],
        out_specs=pl.BlockSpec((1, nq, width), lambda bi, g: (bi, g, 0)),
        out_shape=jax.ShapeDtypeStruct((b, seq, width), BF16),
        scratch_shapes=[pltpu.VMEM((NA_HEADS, nq, nk), F32), pltpu.VMEM((NA_HEADS, nq, nk), BF16)],
        compiler_params=_cparams(("arbitrary", "arbitrary")),
        name="na_attn",
    )(qkv, qkv, qkv, bias)


def _dft_tables():
    k = np.arange(RADIX)
    ang = 2.0 * np.pi * np.outer(k, k) / RADIX
    fr, fi = np.cos(ang), -np.sin(ang)
    ang_t = 2.0 * np.pi * np.outer(k, k) / (RADIX * RADIX)
    twr, twi = np.cos(ang_t), -np.sin(ang_t)
    f32 = lambda a: jnp.asarray(a, F32)
    return f32(fr), f32(fi), f32(twr), f32(twi)


def _embed(cr, ci):
    return jnp.concatenate([jnp.concatenate([cr, -ci], axis=1), jnp.concatenate([ci, cr], axis=1)], axis=0)


def _cmatmul(c_rows, x_lanes):
    m = c_rows.shape[0] // 2
    r = jnp.dot(c_rows, x_lanes, preferred_element_type=F32)
    return r[:m, :LANES] - r[m:, LANES:], r[:m, LANES:] + r[m:, :LANES]


def _load_pair(ref, rows):
    return jnp.concatenate([ref[0, rows, :], ref[1, rows, :]], axis=1).astype(BF16)


def _store_pair(ref, rows, re, im):
    ref[0, rows, :] = re
    ref[1, rows, :] = im


def _twiddled(fr, fi, twr, twi, scale_re=1.0, scale_im=1.0):
    return jnp.concatenate([(fr * twr - fi * twi) * scale_re, (fr * twi + fi * twr) * scale_im], axis=0).astype(BF16)


def _position_features(seq):
    t = np.linspace(0.0, 1.0, seq)[:, None]
    bands = (HY_EMB_DIM - 1) // 2
    w = (2.0 * math.pi / seq) * np.arange(seq)[:, None]
    f = np.linspace(1e-4, bands - 1, bands)[None, :]
    z = np.concatenate([t, np.cos(f * w), -np.sin(f * w)], axis=-1)
    zz = np.zeros((seq // 2, 2 * LANES), np.float32)
    zz[:, :HY_EMB_DIM] = z[:seq // 2]
    zz[:, LANES:LANES + HY_EMB_DIM] = z[seq // 2:]
    return zz


def _filt_mlp_kernel(z_ref, w1_ref, b1_ref, w2_ref, b2_ref, fr_ref, o_ref):
    hp = lax.Precision.HIGHEST
    fr = fr_ref[...]
    hid = LANES // 2
    h = jnp.sin(fr * (jnp.dot(z_ref[...], w1_ref[...], precision=hp, preferred_element_type=F32) + b1_ref[...]))
    h = jnp.sin(fr * (jnp.dot(h, w2_ref[...], precision=hp, preferred_element_type=F32) + b2_ref[...]))
    hi = h.astype(BF16).astype(F32)
    lo = h - hi
    hi_sw, lo_sw = pltpu.roll(hi, hid, 1), pltpu.roll(lo, hid, 1)
    low = lax.broadcasted_iota(jnp.int32, (1, LANES), 1) < hid
    o_ref[0, :, :LANES] = jnp.where(low, hi, lo_sw).astype(BF16)
    o_ref[0, :, LANES:] = jnp.where(low, hi, 0.0).astype(BF16)
    o_ref[1, :, :LANES] = jnp.where(low, hi_sw, lo).astype(BF16)
    o_ref[1, :, LANES:] = jnp.where(low, hi_sw, 0.0).astype(BF16)


def _filt_mlp(zz, fw1, fb1, fw2, fb2, freq):
    seq = zz.shape[0]
    hid = fw2.shape[0]
    assert 2 * hid == LANES
    w1 = jnp.zeros((2 * LANES, LANES), F32)
    w1 = w1.at[:HY_EMB_DIM, :hid].set(fw1).at[LANES:LANES + HY_EMB_DIM, hid:].set(fw1)
    w2 = jnp.zeros((LANES, LANES), F32).at[:hid, :hid].set(fw2).at[hid:, hid:].set(fw2)
    twice = lambda v: jnp.concatenate([v, v]).reshape(1, LANES)
    rows = 2048
    whole = lambda i: (0, 0)
    return pl.pallas_call(
        _filt_mlp_kernel,
        grid=(seq // rows,),
        in_specs=[pl.BlockSpec((rows, 2 * LANES), lambda i: (i, 0)),
                  pl.BlockSpec(w1.shape, whole), pl.BlockSpec((1, LANES), whole),
                  pl.BlockSpec(w2.shape, whole), pl.BlockSpec((1, LANES), whole), pl.BlockSpec((1, LANES), whole)],
        out_specs=pl.BlockSpec((2, rows, 2 * LANES), lambda i: (0, i, 0)),
        out_shape=jax.ShapeDtypeStruct((2, seq, 2 * LANES), BF16),
        compiler_params=_cparams(("arbitrary",)),
        name="filt_mlp",
    )(zz, w1, twice(fb1), w2, twice(fb2), twice(freq))


def _filt_time_kernel(ha_ref, hb_ref, perm_ref, w3_ref, b3_ref, delta_ref, k_ref, ss_ref, hsel_ref, *, seq):
    i = pl.program_id(0)
    rows = ha_ref.shape[0]
    half = seq // rows

    @pl.when(i < half)
    def _():
        hsel_ref[...] = ha_ref[...]

    @pl.when(i >= half)
    def _():
        for s in range(rows // RADIX):
            lo_row = rows - RADIX * (s + 1)
            nxt = hb_ref[0:RADIX, :] if s == 0 else ha_ref[lo_row + RADIX:lo_row + 2 * RADIX, :]
            src = jnp.concatenate([ha_ref[lo_row:lo_row + RADIX, :], nxt], axis=0)
            hsel_ref[s * RADIX:(s + 1) * RADIX, :] = jnp.dot(perm_ref[...], src, preferred_element_type=F32).astype(BF16)

    w = w3_ref[0]
    w_hi = w.astype(BF16)
    w_lo = (w - w_hi.astype(F32)).astype(BF16)
    wcat = jnp.concatenate([w_hi, w_hi, w_lo, jnp.zeros_like(w_lo)], axis=0)
    hf = jnp.dot(hsel_ref[...], wcat, preferred_element_type=F32) + b3_ref[0]
    m = i * rows + lax.broadcasted_iota(jnp.int32, (rows, 1), 0)
    pos = jnp.where(m < seq, m, 2 * seq - m).astype(F32)
    t = pos / (seq - 1)
    k = jnp.where(m == seq, 0.0, hf * jnp.exp(-t * delta_ref[...]))
    for j in range(rows // RADIX):
        k_ref[j * PITCH:j * PITCH + RADIX, :] = k[j * RADIX:(j + 1) * RADIX]
        k_ref[j * PITCH + RADIX:(j + 1) * PITCH, :] = jnp.zeros((PITCH - RADIX, k.shape[1]), F32)
    part = jnp.sum(k * k, axis=0, keepdims=True)

    @pl.when(i == 0)
    def _():
        ss_ref[...] = part

    @pl.when(i != 0)
    def _():
        ss_ref[...] += part


def _filt_time(hdn, w3d, b3d, delta2):
    seq, packed = hdn.shape
    _, hid, cols = w3d.shape
    assert packed == 4 * hid
    n2l = 2 * seq
    rows = 1024
    half = seq // rows
    prows = rows // RADIX * PITCH
    perm = np.zeros((RADIX, 2 * RADIX), np.float32)
    perm[0, RADIX] = 1.0
    perm[np.arange(1, RADIX), RADIX - np.arange(1, RADIX)] = 1.0
    return pl.pallas_call(
        functools.partial(_filt_time_kernel, seq=seq),
        grid=(n2l // rows,),
        in_specs=[pl.BlockSpec((rows, packed), lambda i: (jnp.where(i < half, i, 2 * half - 1 - i), 0)),
                  pl.BlockSpec((rows, packed), lambda i: (jnp.clip(2 * half - i, 0, half - 1), 0)),
                  pl.BlockSpec(perm.shape, lambda i: (0, 0)),
                  pl.BlockSpec((1, hid, cols), lambda i: (i // half, 0, 0)),
                  pl.BlockSpec((1, 1, cols), lambda i: (i // half, 0, 0)),
                  pl.BlockSpec((1, cols), lambda i: (0, 0))],
        out_specs=[pl.BlockSpec((prows, cols), lambda i: (i, 0)),
                   pl.BlockSpec((1, cols), lambda i: (0, 0))],
        out_shape=[jax.ShapeDtypeStruct((n2l // RADIX * PITCH, cols), F32), jax.ShapeDtypeStruct((1, cols), F32)],
        scratch_shapes=[pltpu.VMEM((rows, packed), BF16)],
        compiler_params=_cparams(("arbitrary",)),
        name="filt_time",
    )(hdn, hdn, jnp.asarray(perm, BF16), w3d, b3d, delta2)


def _filt_spec_kernel(k_ref, ss_ref, fa_ref, fr_ref, fi_ref, twr_ref, twi_ref, o_ref, a_ref):
    r = RADIX
    scale = lax.rsqrt(ss_ref[...] + 1e-12)

    def a_body(i, c):
        n2 = 2 * i
        rhs = jnp.concatenate([k_ref[pl.ds(n2, r, stride=PITCH), :], k_ref[pl.ds(n2 + 1, r, stride=PITCH), :]],
                              axis=1).astype(BF16)
        out = jnp.dot(fa_ref[...], rhs, preferred_element_type=F32)
        _store_pair(a_ref, pl.ds(n2, r, stride=PITCH), out[:r, :LANES], out[r:, :LANES])
        _store_pair(a_ref, pl.ds(n2 + 1, r, stride=PITCH), out[:r, LANES:], out[r:, LANES:])
        return c

    lax.fori_loop(0, r // 2, a_body, 0, unroll=DFT_UNROLL_STRIDED // 2)

    def b_body(k1, c):
        r0 = pl.multiple_of(k1 * PITCH, 8)
        o0 = pl.multiple_of(k1 * r, r)
        blk = _load_pair(a_ref, pl.ds(r0, r))
        sr, si = _cmatmul(_twiddled(fr_ref[...], fi_ref[...], twr_ref[pl.ds(k1, 1), :], twi_ref[pl.ds(k1, 1), :]), blk)
        o_ref[0, pl.ds(o0, r), :] = (sr * scale).astype(BF16)
        o_ref[1, pl.ds(o0, r), :] = (si * scale).astype(BF16)
        return c

    lax.fori_loop(0, r, b_body, 0, unroll=DFT_UNROLL_STRIDED)


def _filt_spec(ktime, ss, fa_real, fr, fi, twr, twi):
    prows, cols = ktime.shape
    n2l = prows // PITCH * RADIX
    whole = lambda j: (0, 0)
    return pl.pallas_call(
        _filt_spec_kernel,
        grid=(cols // LANES,),
        in_specs=[pl.BlockSpec((prows, LANES), lambda j: (0, j)),
                  pl.BlockSpec((1, LANES), lambda j: (0, j)),
                  _resident(fa_real.shape, whole), _resident(fr.shape, whole), _resident(fi.shape, whole),
                  _resident(twr.shape, whole), _resident(twi.shape, whole)],
        out_specs=pl.BlockSpec((2, n2l, LANES), lambda j: (0, 0, j)),
        out_shape=jax.ShapeDtypeStruct((2, n2l, cols), BF16),
        scratch_shapes=[pltpu.VMEM((2, prows, LANES), F32)],
        compiler_params=_cparams(("arbitrary",)),
        name="filt_spec",
    )(ktime, ss, fa_real, fr, fi, twr, twi)


def _short_conv_rows(u_ref, bi, r0, nrows, seq, w_ref, b_ref):
    edge = 16
    cur = u_ref[bi, pl.ds(r0, nrows), :].astype(F32)
    before = u_ref[bi, pl.ds(pl.multiple_of(jnp.maximum(r0 - edge, 0), edge), edge), :].astype(F32)
    after = u_ref[bi, pl.ds(pl.multiple_of(jnp.minimum(r0 + nrows, seq - edge), edge), edge), :].astype(F32)
    prev_edge = jnp.where(r0 > 0, before[edge - 1:edge, :], 0.0)
    next_edge = jnp.where(r0 + nrows < seq, after[0:1, :], 0.0)
    w0, w1, w2 = w_ref[0:1, :], w_ref[1:2, :], w_ref[2:3, :]
    out = b_ref[...] + w0 * pltpu.roll(cur, 1, 0) + w1 * cur + w2 * pltpu.roll(cur, nrows - 1, 0)
    row = lax.broadcasted_iota(jnp.int32, (8, cur.shape[1]), 0)
    first = out[0:8] + jnp.where(row == 0, w0 * (prev_edge - cur[nrows - 1:nrows]), 0.0)
    last = out[nrows - 8:] + jnp.where(row == 7, w2 * (next_edge - cur[0:1]), 0.0)
    return jnp.concatenate([first, out[8:nrows - 8], last], axis=0)


def _hy_conv_kernel(z_ref, zw_ref, zb_ref, u_ref, uw_ref, ub_ref, ks_hbm, hb_ref,
                    fa_ref, fc_ref, fr_ref, fi_ref, twr_ref, twi_ref, o_ref, a_ref, v_ref, ks_ref, ks_sem,
                    *, conv_in, ks_tile0):
    r = RADIX
    half = r // 2
    nb, seq, _ = z_ref.shape
    nchunk = seq // CONV_ROWS
    blocks = CONV_ROWS // r

    ks_col = pl.multiple_of((ks_tile0 + pl.program_id(0)) * LANES, LANES)
    ks_copy = pltpu.make_async_copy(ks_hbm.at[:, :, pl.ds(ks_col, LANES)], ks_ref, ks_sem)
    ks_copy.start()

    def load_body(c, carry):
        r0 = pl.multiple_of(c * CONV_ROWS, CONV_ROWS)
        for bi in range(nb):
            if conv_in:
                v = _short_conv_rows(z_ref, bi, r0, CONV_ROWS, seq, zw_ref, zb_ref)
                o_ref[bi, pl.ds(r0, CONV_ROWS), :] = v.astype(BF16)
            else:
                v = z_ref[bi, pl.ds(r0, CONV_ROWS), :].astype(F32)
            for j in range(blocks):
                p0 = pl.multiple_of((c * blocks + j) * PITCH, 8)
                v_ref[bi, pl.ds(p0, r), :] = v[j * r:(j + 1) * r]
        return carry

    lax.fori_loop(0, nchunk, load_body, 0)

    def pair_store(ref, rows0, rows1, out):
        _store_pair(ref, rows0, out[:r, :LANES], out[r:, :LANES])
        _store_pair(ref, rows1, out[:r, LANES:], out[r:, LANES:])

    def a_body(i, carry):
        n2 = 2 * i

        def column(n):
            return jnp.concatenate([v_ref[0, pl.ds(n, half, stride=PITCH), :],
                                    v_ref[1, pl.ds(n, half, stride=PITCH), :]], axis=0)

        rhs = jnp.concatenate([column(n2), column(n2 + 1)], axis=1).astype(BF16)
        out = jnp.dot(fa_ref[...], rhs, preferred_element_type=F32)
        pair_store(a_ref, pl.ds(n2, r, stride=PITCH), pl.ds(n2 + 1, r, stride=PITCH), out)
        return carry

    lax.fori_loop(0, r // 2, a_body, 0, unroll=DFT_UNROLL_STRIDED // 2)

    def b_forward(k1):
        r0 = pl.multiple_of(k1 * PITCH, 8)
        s0 = pl.multiple_of(k1 * r, r)
        blk = _load_pair(a_ref, pl.ds(r0, r))
        sr, si = _cmatmul(_twiddled(fr_ref[...], fi_ref[...], twr_ref[pl.ds(k1, 1), :], twi_ref[pl.ds(k1, 1), :]), blk)
        kr = ks_ref[0, pl.ds(s0, r), :].astype(F32)
        ki = ks_ref[1, pl.ds(s0, r), :].astype(F32)
        return jnp.concatenate([sr * kr - si * ki, sr * ki + si * kr], axis=0).astype(BF16)

    def b_forward_pair(i):
        return jnp.concatenate([b_forward(2 * i), b_forward(2 * i + 1)], axis=1)

    def b_inverse_pair(i, p):
        out = jnp.dot(fc_ref[...], p, preferred_element_type=F32)
        pair_store(a_ref, pl.ds(pl.multiple_of(2 * i * PITCH, 8), r), pl.ds(pl.multiple_of((2 * i + 1) * PITCH, 8), r), out)

    def b_body(i, p):
        p_next = b_forward_pair(i + 1)
        b_inverse_pair(i, p)
        return p_next

    ks_copy.wait()
    b_inverse_pair(r // 2 - 1, lax.fori_loop(0, r // 2 - 1, b_body, b_forward_pair(0), unroll=DFT_UNROLL // 2))

    inv_n = 1.0 / (r * r)

    def c_body(n2, carry):
        g = _twiddled(fr_ref[0:half, :], fi_ref[0:half, :], twr_ref[pl.ds(n2, 1), :], twi_ref[pl.ds(n2, 1), :],
                      inv_n, -inv_n)
        yr, yi = _cmatmul(g, _load_pair(a_ref, pl.ds(n2, r, stride=PITCH)))
        _store_pair(v_ref, pl.ds(n2, half, stride=PITCH), yr, yi)
        return carry

    lax.fori_loop(0, r, c_body, 0, unroll=DFT_UNROLL_STRIDED)

    def out_body(c, carry):
        r0 = pl.multiple_of(c * CONV_ROWS, CONV_ROWS)
        for bi in range(nb):
            gate = _short_conv_rows(u_ref, bi, r0, CONV_ROWS, seq, uw_ref, ub_ref)
            conv = jnp.concatenate(
                [v_ref[bi, pl.ds(pl.multiple_of((c * blocks + j) * PITCH, 8), r), :] for j in range(blocks)], axis=0)
            zp = (o_ref if conv_in else z_ref)[bi, pl.ds(r0, CONV_ROWS), :].astype(F32)
            o_ref[bi, pl.ds(r0, CONV_ROWS), :] = (gate * (conv + hb_ref[...] * zp)).astype(BF16)
        return carry

    lax.fori_loop(0, nchunk, out_body, 0)


def _hy_conv(z, z_part, u, u_part, order, short_w, short_b, kspec, hy_bias, tables, conv_in):
    nb, seq, _ = u.shape
    width = hy_bias.shape[1]
    tiles = width // LANES
    fa, fc, fr, fi, twr, twi = tables
    whole = lambda j: (0, 0)
    zcol = lambda j: (0, 0, z_part * tiles + j)
    ucol = lambda j: (0, 0, u_part * tiles + j)
    return pl.pallas_call(
        functools.partial(_hy_conv_kernel, conv_in=conv_in, ks_tile0=order * tiles),
        grid=(tiles,),
        in_specs=[
            pl.BlockSpec((nb, seq, LANES), zcol),
            pl.BlockSpec((3, LANES), lambda j: (0, z_part * tiles + j)),
            pl.BlockSpec((1, LANES), lambda j: (0, z_part * tiles + j)),
            pl.BlockSpec((nb, seq, LANES), ucol),
            pl.BlockSpec((3, LANES), lambda j: (0, u_part * tiles + j)),
            pl.BlockSpec((1, LANES), lambda j: (0, u_part * tiles + j)),
            pl.BlockSpec(memory_space=pl.ANY),
            pl.BlockSpec((1, LANES), lambda j: (0, j)),
            _resident(fa.shape, whole), _resident(fc.shape, whole), _resident(fr.shape, whole),
            _resident(fi.shape, whole), _resident(twr.shape, whole), _resident(twi.shape, whole),
        ],
        out_specs=_resident((nb, seq, LANES), lambda j: (0, 0, j)),
        out_shape=jax.ShapeDtypeStruct((nb, seq, width), BF16),
        scratch_shapes=[pltpu.VMEM((2, 2 * seq // RADIX * PITCH, LANES), F32),
                        pltpu.VMEM((nb, seq // RADIX * PITCH, LANES), F32),
                        pltpu.VMEM((2, 2 * seq, LANES), BF16),
                        pltpu.SemaphoreType.DMA(())],
        compiler_params=_cparams(("arbitrary",)),
        name=f"hy_conv{order}",
    )(z, short_w, short_b, u, short_w, short_b, kspec, hy_bias[order:order + 1], fa, fc, fr, fi, twr, twi)


def _hyena(hyu, short_w, short_b, fw1, fb1, fw2, fb2, fw3, fb3, freq, hy_bias):
    nb, seq, _ = hyu.shape
    assert nb == 2 and 2 * seq == RADIX * RADIX
    width = hy_bias.shape[1]
    hid = fw2.shape[0]
    fr, fi, twr, twi = _dft_tables()
    half = RADIX // 2
    fa_real = jnp.concatenate([fr, fi], axis=0).astype(BF16)
    fa_half = _embed(fr[:, :half], fi[:, :half]).astype(BF16)
    fc = _embed(fr, -fi).astype(BF16)

    zz = jnp.asarray(_position_features(seq))
    hdn = _filt_mlp(zz, fw1, fb1, fw2, fb2, freq).reshape(seq, 4 * hid)
    w3d = fw3.reshape(hid, 2, HY_ORDER * width).transpose(1, 0, 2)
    b3d = fb3.reshape(2, 1, HY_ORDER * width)
    max_decay = math.log(HY_DECAY_TARGET) / HY_FAST_DECAY_PCT
    min_decay = math.log(HY_DECAY_TARGET) / HY_SLOW_DECAY_PCT
    delta = np.abs(np.linspace(min_decay, max_decay, width)).astype(np.float32)
    delta2 = jnp.asarray(np.tile(delta, HY_ORDER)[None, :])
    ktime, ss = _filt_time(hdn, w3d, b3d, delta2)
    kspec = _filt_spec(ktime, ss, fa_real, fr, fi, twr, twi)

    tables = (fa_half, fc, fr, fi, twr, twi)
    sb = short_b.reshape(1, -1)
    z1 = _hy_conv(hyu, 0, hyu, 1, 0, short_w, sb, kspec, hy_bias, tables, conv_in=True)
    return _hy_conv(z1, 0, hyu, 2, 1, short_w, sb, kspec, hy_bias, tables, conv_in=False)


def kernel(x, ln1_g, ln1_b, ffn1_w1, ffn1_w3, ffn1_w2, w_in, b_gate, na_rpb, hy_short_w, hy_short_b,
           hy_filt_w1, hy_filt_b1, hy_filt_w2, hy_filt_b2, hy_filt_w3, hy_filt_b3, hy_filt_freq, hy_bias,
           w_pa, w_pb, w_out, ln2_g, ln2_b, ffn2_w1, ffn2_w3, ffn2_w2, ln3_g, ln3_b):
    assert ln1_g.shape[0] == DEPTH
    b, seq, d = x.shape
    rows = seq // GRID_W
    na_width = NA_HEADS * NA_HEAD_DIM
    n_qkv = 3 * na_width
    n_hy = hy_short_w.shape[2]
    vec = lambda a: a[0].reshape(1, -1)

    x1, qkv, hyu, gates = _dense_in(
        x.reshape(b * seq, d), ffn1_w1[0], ffn1_w3[0], ffn1_w2[0], vec(ln1_g), vec(ln1_b),
        w_in[0], vec(b_gate), n_qkv, n_hy)

    bias = _na_bias(na_rpb[0], rows)
    ya = _na_attn(qkv.reshape(b, seq, n_qkv), bias, rows)

    yb = _hyena(hyu.reshape(b, seq, n_hy), hy_short_w[0], hy_short_b[0], hy_filt_w1[0], hy_filt_b1[0],
                hy_filt_w2[0], hy_filt_b2[0], hy_filt_w3[0], hy_filt_b3[0], hy_filt_freq[0], hy_bias[0])

    out = _dense_out(
        x1, ya.reshape(b * seq, na_width), yb.reshape(b * seq, -1), gates,
        w_pa[0], w_pb[0], w_out[0], vec(ln2_g), vec(ln2_b),
        ffn2_w1[0], ffn2_w3[0], ffn2_w2[0], vec(ln3_g), vec(ln3_b))
    return out.reshape(b, seq, d)
```

```python
import functools
import math

import numpy as np
import jax
import jax.numpy as jnp
from jax import lax
from jax.experimental import pallas as pl
from jax.experimental.pallas import tpu as pltpu

F32 = jnp.float32
BF16 = jnp.bfloat16

GRID_W = 64
NA_HEADS = 8
NA_HEAD_DIM = 64
NA_KH = 8
NA_KW = 16
HY_ORDER = 2
HY_EMB_DIM = 33
HY_FAST_DECAY_PCT = 0.3
HY_SLOW_DECAY_PCT = 1.5
HY_DECAY_TARGET = 1e-2
DEPTH = 1
DN_ALPHA = (2 * DEPTH) ** 0.25
LN_EPS = 1e-5
LOG2E = math.log2(math.e)

LANES = 128
VMEM_LIMIT = 60 * 1024 * 1024

ROW_TILE = 512
FF_CHUNK = 256
PROJ_CHUNK = 512
CAST_ROWS = 64
CAST_SLOTS = 6
NA_QROWS = 4
NA_KROWS = NA_QROWS + NA_KH
NA_BIAS_GROUP = 5
RADIX = 128
PITCH = RADIX + 8
DFT_UNROLL = 8
DFT_UNROLL_STRIDED = 32
CONV_ROWS = 512
NEG = -1e30


def _cparams(sem):
    return pltpu.CompilerParams(dimension_semantics=sem, vmem_limit_bytes=VMEM_LIMIT)


def _resident(shape, index_map):
    return pl.BlockSpec(shape, index_map, pipeline_mode=pl.Buffered(1))


def _cast_weights_in(pairs, stage_ref, sem):
    ahead = CAST_SLOTS - 1
    counts = [src.shape[0] // CAST_ROWS for src, _ in pairs]
    assert all(n >= ahead for n in counts)
    bases = [sum(counts[:w]) for w in range(len(pairs))]

    def copy(w, c, g):
        src, cols = pairs[w][0], pairs[w][0].shape[1]
        slot = g % CAST_SLOTS
        return pltpu.make_async_copy(src.at[pl.ds(c * CAST_ROWS, CAST_ROWS), :],
                                     stage_ref.at[slot, :, pl.ds(0, cols)], sem.at[slot])

    for g in range(ahead):
        copy(0, g, g).start()
    for w, (src, dst) in enumerate(pairs):
        n, cols = counts[w], src.shape[1]

        def body(c, carry, w=w, n=n, cols=cols, dst=dst):
            g = bases[w] + c

            @pl.when(c + ahead < n)
            def _():
                copy(w, c + ahead, g + ahead).start()

            if w + 1 < len(pairs):
                @pl.when(c + ahead >= n)
                def _():
                    copy(w + 1, c + ahead - n, g + ahead).start()

            copy(w, c, g).wait()
            r0 = pl.multiple_of(c * CAST_ROWS, CAST_ROWS)
            dst[pl.ds(r0, CAST_ROWS), :] = stage_ref[g % CAST_SLOTS, :, 0:cols].astype(BF16)
            return carry

        lax.fori_loop(0, n, body, 0)


def _layer_norm(r, g, b):
    mu = jnp.mean(r, axis=-1, keepdims=True)
    d = r - mu
    var = jnp.mean(d * d, axis=-1, keepdims=True)
    return d * lax.rsqrt(var + LN_EPS) * g + b


def _swiglu_ln(x, xb_ref, acc_ref, w1_ref, w3_ref, w2_ref, g_ref, b_ref):
    d_ff = w1_ref.shape[1]
    xb_ref[...] = x.astype(BF16)
    for j in range(d_ff // FF_CHUNK):
        sl = slice(j * FF_CHUNK, (j + 1) * FF_CHUNK)
        a = jnp.dot(xb_ref[...], w1_ref[:, sl], preferred_element_type=F32)
        b = jnp.dot(xb_ref[...], w3_ref[:, sl], preferred_element_type=F32)
        h = (a * jax.nn.sigmoid(a) * b).astype(BF16)
        part = jnp.dot(h, w2_ref[sl, :], preferred_element_type=F32)
        if j == 0:
            acc_ref[...] = part
        else:
            acc_ref[...] += part
    return _layer_norm(DN_ALPHA * x + 0.5 * acc_ref[...], g_ref[...], b_ref[...])


def _dense_in_kernel(x_ref, w1_hbm, w3_hbm, w2_hbm, g_ref, b_ref, win_hbm, bg_ref,
                     x1_ref, qkv_ref, hyu_ref, gate_ref,
                     xb_ref, acc_ref, w1_ref, w3_ref, w2_ref, win_ref, stage_ref, sem):
    @pl.when(pl.program_id(0) == 0)
    def _():
        _cast_weights_in(((w1_hbm, w1_ref), (w3_hbm, w3_ref), (w2_hbm, w2_ref), (win_hbm, win_ref)), stage_ref, sem)

    x1 = _swiglu_ln(x_ref[...], xb_ref, acc_ref, w1_ref, w3_ref, w2_ref, g_ref, b_ref)
    x1_ref[...] = x1
    xb_ref[...] = x1.astype(BF16)
    n_qkv = qkv_ref.shape[1]
    n_hy = hyu_ref.shape[1]
    n_head = n_qkv // 3
    for j in range(win_ref.shape[1] // PROJ_CHUNK):
        c0 = j * PROJ_CHUNK
        p = jnp.dot(xb_ref[...], win_ref[:, c0:c0 + PROJ_CHUNK], preferred_element_type=F32)
        if c0 < n_qkv:
            if c0 < n_head:
                p = p * (NA_HEAD_DIM ** -0.5 * LOG2E)
            qkv_ref[:, c0:c0 + PROJ_CHUNK] = p.astype(BF16)
        elif c0 < n_qkv + n_hy:
            hyu_ref[:, c0 - n_qkv:c0 - n_qkv + PROJ_CHUNK] = p.astype(BF16)
        else:
            g0 = c0 - n_qkv - n_hy
            gate_ref[:, g0:g0 + PROJ_CHUNK] = jax.nn.sigmoid(p + bg_ref[:, g0:g0 + PROJ_CHUNK]).astype(BF16)


def _dense_in(x2d, w1, w3, w2, g, b, w_in, b_gate, n_qkv, n_hy):
    n, d = x2d.shape
    d_ff = w1.shape[1]
    n_gate = w_in.shape[1] - n_qkv - n_hy
    assert n % ROW_TILE == 0 and d_ff % FF_CHUNK == 0
    assert n_qkv % (3 * PROJ_CHUNK) == 0 and n_hy % PROJ_CHUNK == 0 and n_gate % PROJ_CHUNK == 0
    weights = (w1, w3, w2, w_in)
    assert all(w.dtype == F32 and w.shape[0] % CAST_ROWS == 0 for w in weights)
    row = lambda i: (i, 0)
    whole = lambda i: (0, 0)
    hbm = pl.BlockSpec(memory_space=pl.ANY)
    return pl.pallas_call(
        _dense_in_kernel,
        grid=(n // ROW_TILE,),
        in_specs=[
            pl.BlockSpec((ROW_TILE, d), row),
            hbm, hbm, hbm,
            _resident((1, d), whole), _resident((1, d), whole),
            hbm, _resident((1, n_gate), whole),
        ],
        out_specs=[
            pl.BlockSpec((ROW_TILE, d), row),
            pl.BlockSpec((ROW_TILE, n_qkv), row),
            pl.BlockSpec((ROW_TILE, n_hy), row),
            pl.BlockSpec((ROW_TILE, n_gate), row),
        ],
        out_shape=[
            jax.ShapeDtypeStruct((n, d), F32),
            jax.ShapeDtypeStruct((n, n_qkv), BF16),
            jax.ShapeDtypeStruct((n, n_hy), BF16),
            jax.ShapeDtypeStruct((n, n_gate), BF16),
        ],
        scratch_shapes=[pltpu.VMEM((ROW_TILE, d), BF16), pltpu.VMEM((ROW_TILE, d), F32)]
        + [pltpu.VMEM(w.shape, BF16) for w in weights]
        + [pltpu.VMEM((CAST_SLOTS, CAST_ROWS, max(w.shape[1] for w in weights)), F32),
           pltpu.SemaphoreType.DMA((CAST_SLOTS,))],
        compiler_params=_cparams(("arbitrary",)),
        name="dense_in",
    )(x2d, w1, w3, w2, g, b, w_in, b_gate)


def _dense_out_kernel(x1_ref, ya_ref, yb_ref, gate_ref, wpa_hbm, wpb_hbm, wout_hbm, g2_ref, b2_ref,
                      w1_hbm, w3_hbm, w2_hbm, g3_ref, b3_ref, o_ref,
                      xb_ref, acc_ref, wpa_ref, wpb_ref, wout_ref, w1_ref, w3_ref, w2_ref, stage_ref, sem):
    @pl.when(pl.program_id(0) == 0)
    def _():
        _cast_weights_in(((wpa_hbm, wpa_ref), (wpb_hbm, wpb_ref), (wout_hbm, wout_ref),
                          (w1_hbm, w1_ref), (w3_hbm, w3_ref), (w2_hbm, w2_ref)), stage_ref, sem)

    d = x1_ref.shape[1]
    pa = jnp.dot(ya_ref[...], wpa_ref[...], preferred_element_type=F32)
    pb = jnp.dot(yb_ref[...], wpb_ref[...], preferred_element_type=F32)
    m = gate_ref[:, :d].astype(F32) * pa + gate_ref[:, d:].astype(F32) * pb
    mix = jnp.dot(m.astype(BF16), wout_ref[...], preferred_element_type=F32)
    x2 = _layer_norm(DN_ALPHA * x1_ref[...] + mix, g2_ref[...], b2_ref[...])
    o_ref[...] = _swiglu_ln(x2, xb_ref, acc_ref, w1_ref, w3_ref, w2_ref, g3_ref, b3_ref)


def _dense_out(x1, ya, yb, gates, wpa, wpb, wout, g2, b2, w1, w3, w2, g3, b3):
    n, d = x1.shape
    weights = (wpa, wpb, wout, w1, w3, w2)
    assert all(w.dtype == F32 and w.shape[0] % CAST_ROWS == 0 for w in weights)
    row = lambda i: (i, 0)
    whole = lambda i: (0, 0)
    hbm = pl.BlockSpec(memory_space=pl.ANY)
    return pl.pallas_call(
        _dense_out_kernel,
        grid=(n // ROW_TILE,),
        in_specs=[
            pl.BlockSpec((ROW_TILE, d), row),
            pl.BlockSpec((ROW_TILE, ya.shape[1]), row),
            pl.BlockSpec((ROW_TILE, yb.shape[1]), row),
            pl.BlockSpec((ROW_TILE, gates.shape[1]), row),
            hbm, hbm, hbm,
            _resident((1, d), whole), _resident((1, d), whole),
            hbm, hbm, hbm,
            _resident((1, d), whole), _resident((1, d), whole),
        ],
        out_specs=pl.BlockSpec((ROW_TILE, d), row),
        out_shape=jax.ShapeDtypeStruct((n, d), F32),
        scratch_shapes=[pltpu.VMEM((ROW_TILE, d), BF16), pltpu.VMEM((ROW_TILE, d), F32)]
        + [pltpu.VMEM(w.shape, BF16) for w in weights]
        + [pltpu.VMEM((CAST_SLOTS, CAST_ROWS, max(w.shape[1] for w in weights)), F32),
           pltpu.SemaphoreType.DMA((CAST_SLOTS,))],
        compiler_params=_cparams(("arbitrary",)),
        name="dense_out",
    )(x1, ya, yb, gates, wpa, wpb, wout, g2, b2, w1, w3, w2, g3, b3)


def _window_start(n, k):
    return np.clip(np.arange(n) - k // 2, 0, n - k)


def _na_group_types(rows):
    last = rows - NA_QROWS
    return ((0, 0), (NA_QROWS * 2, NA_QROWS), (last, rows - NA_KROWS))


def _na_tables(rows):
    w = GRID_W
    shift = np.zeros((2 * NA_KW - 1, w, 2 * w), np.float32)
    q = np.arange(w)[:, None]
    kc = np.arange(w)[None, :]
    for dc in range(2 * NA_KW - 1):
        shift[dc, :, :w] = (kc - q + NA_KW - 1 == dc).astype(np.float32)
    cs = _window_start(w, NA_KW)
    col_ok = (kc >= cs[:, None]) & (kc < cs[:, None] + NA_KW)
    rs_all = _window_start(rows, NA_KH)
    negmask = np.full((3, NA_QROWS, w, NA_KROWS, w), NEG, np.float32)
    for t, (r0, kb) in enumerate(_na_group_types(rows)):
        for i in range(NA_QROWS):
            rs = rs_all[r0 + i]
            for j in range(NA_KROWS):
                if rs <= kb + j < rs + NA_KH:
                    negmask[t, i, :, j, :] = np.where(col_ok, 0.0, NEG)
    return shift, negmask.reshape(3, NA_QROWS * w, NA_KROWS * w)


def _na_bias_kernel(rpb_ref, shift_ref, neg_ref, o_ref, u_ref, *, rows):
    h = pl.program_id(0)
    n_dr = 2 * NA_KH - 1
    n_dc = 2 * NA_KW - 1
    w = GRID_W

    for dr0 in range(0, n_dr, NA_BIAS_GROUP):
        base = (h * n_dr + dr0) * n_dc
        accs = [None] * NA_BIAS_GROUP
        for dc in range(n_dc):
            mask = shift_ref[dc]
            for j in range(NA_BIAS_GROUP):
                term = rpb_ref[base + j * n_dc + dc] * mask
                accs[j] = term if dc == 0 else accs[j] + term
        for j in range(NA_BIAS_GROUP):
            left = accs[j] * LOG2E
            u_ref[0, dr0 + j] = left
            u_ref[1, dr0 + j] = pltpu.roll(left, w, 1)
    rs_all = _window_start(rows, NA_KH)
    for t, (r0, kb) in enumerate(_na_group_types(rows)):
        for i in range(NA_QROWS):
            rs = rs_all[r0 + i]
            for jj in range(NA_KROWS // 2):
                val = neg_ref[t, i * w:(i + 1) * w, jj * 2 * w:(jj + 1) * 2 * w]
                for s in range(2):
                    kr = kb + 2 * jj + s
                    if rs <= kr < rs + NA_KH:
                        val = val + u_ref[s, kr - (r0 + i) + NA_KH - 1]
                o_ref[t, 0, i * w:(i + 1) * w, jj * 2 * w:(jj + 1) * 2 * w] = val


def _na_bias(rpb, rows):
    shift, negmask = _na_tables(rows)
    nq, nk = NA_QROWS * GRID_W, NA_KROWS * GRID_W
    return pl.pallas_call(
        functools.partial(_na_bias_kernel, rows=rows),
        grid=(NA_HEADS,),
        in_specs=[
            pl.BlockSpec(memory_space=pltpu.SMEM),
            _resident(shift.shape, lambda h: (0, 0, 0)),
            _resident(negmask.shape, lambda h: (0, 0, 0)),
        ],
        out_specs=pl.BlockSpec((3, 1, nq, nk), lambda h: (0, h, 0, 0)),
        out_shape=jax.ShapeDtypeStruct((3, NA_HEADS, nq, nk), F32),
        scratch_shapes=[pltpu.VMEM((2, 2 * NA_KH - 1, GRID_W, 2 * GRID_W), F32)],
        compiler_params=_cparams(("arbitrary",)),
        name="na_bias",
    )(rpb.reshape(-1), jnp.asarray(shift), jnp.asarray(negmask))


def _na_attn_kernel(q_ref, k_ref, v_ref, bias_ref, o_ref, s_ref, e_ref, *, rows):
    g = pl.program_id(1)
    kb = jnp.clip(g * NA_QROWS - NA_KH // 2, 0, rows - NA_KROWS)
    k0 = pl.multiple_of(kb * GRID_W, GRID_W)
    nk = NA_KROWS * GRID_W
    dh = NA_HEAD_DIM
    lane = lax.broadcasted_iota(jnp.int32, (1, 2 * dh), 1)
    own = (lane < dh, lane >= dh)
    pair_cols = [slice(hp * 2 * dh, (hp + 1) * 2 * dh) for hp in range(NA_HEADS // 2)]
    for hp, cols in enumerate(pair_cols):
        q2 = q_ref[0, :, cols]
        k2 = k_ref[0, pl.ds(k0, nk), cols]
        for s in range(2):
            s_ref[2 * hp + s] = lax.dot_general(q2, jnp.where(own[s], k2, jnp.zeros_like(k2)),
                                                (((1,), (1,)), ((), ())), preferred_element_type=F32)
    for h in range(NA_HEADS):
        sc = s_ref[h] + bias_ref[0, h]
        e_ref[h] = jnp.exp2(sc - jnp.max(sc, axis=-1, keepdims=True)).astype(BF16)
    for hp, cols in enumerate(pair_cols):
        v2 = v_ref[0, pl.ds(k0, nk), cols]
        pair = []
        for s in range(2):
            r = jnp.dot(e_ref[2 * hp + s], jnp.where(own[s], v2, jnp.ones_like(v2)), preferred_element_type=F32)
            den = r[:, dh:dh + 1] if s == 0 else r[:, 0:1]
            pair.append(jnp.where(own[s], r / den, 0.0))
        o_ref[0, :, cols] = (pair[0] + pair[1]).astype(BF16)


def _na_attn(qkv, bias, rows):
    b, seq, three_w = qkv.shape
    width = three_w // 3
    nq, nk = NA_QROWS * GRID_W, NA_KROWS * GRID_W
    groups = rows // NA_QROWS

    def bias_idx(bi, g):
        return (jnp.where(g == 0, 0, jnp.where(g == groups - 1, 2, 1)), 0, 0, 0)

    return pl.pallas_call(
        functools.partial(_na_attn_kernel, rows=rows),
        grid=(b, groups),
        in_specs=[
            pl.BlockSpec((1, nq, width), lambda bi, g: (bi, g, 0)),
            pl.BlockSpec((1, seq, width), lambda bi, g: (bi, 0, 1)),
            pl.BlockSpec((1, seq, width), lambda bi, g: (bi, 0, 2)),
            pl.BlockSpec((1, NA_HEADS, nq, nk), bias_idx),
        ],
        out_specs=pl.BlockSpec((1, nq, width), lambda bi, g: (bi, g, 0)),
        out_shape=jax.ShapeDtypeStruct((b, seq, width), BF16),
        scratch_shapes=[pltpu.VMEM((NA_HEADS, nq, nk), F32), pltpu.VMEM((NA_HEADS, nq, nk), BF16)],
        compiler_params=_cparams(("arbitrary", "arbitrary")),
        name="na_attn",
    )(qkv, qkv, qkv, bias)


def _dft_tables():
    k = np.arange(RADIX)
    ang = 2.0 * np.pi * np.outer(k, k) / RADIX
    fr, fi = np.cos(ang), -np.sin(ang)
    ang_t = 2.0 * np.pi * np.outer(k, k) / (RADIX * RADIX)
    twr, twi = np.cos(ang_t), -np.sin(ang_t)
    f32 = lambda a: jnp.asarray(a, F32)
    return f32(fr), f32(fi), f32(twr), f32(twi)


def _embed(cr, ci):
    return jnp.concatenate([jnp.concatenate([cr, -ci], axis=1), jnp.concatenate([ci, cr], axis=1)], axis=0)


def _cmatmul(c_rows, x_lanes):
    m = c_rows.shape[0] // 2
    r = jnp.dot(c_rows, x_lanes, preferred_element_type=F32)
    return r[:m, :LANES] - r[m:, LANES:], r[:m, LANES:] + r[m:, :LANES]


def _load_pair(ref, rows):
    return jnp.concatenate([ref[0, rows, :], ref[1, rows, :]], axis=1).astype(BF16)


def _store_pair(ref, rows, re, im):
    ref[0, rows, :] = re
    ref[1, rows, :] = im


def _twiddled(fr, fi, twr, twi, scale_re=1.0, scale_im=1.0):
    return jnp.concatenate([(fr * twr - fi * twi) * scale_re, (fr * twi + fi * twr) * scale_im], axis=0).astype(BF16)


def _position_features(seq):
    t = np.linspace(0.0, 1.0, seq)[:, None]
    bands = (HY_EMB_DIM - 1) // 2
    w = (2.0 * math.pi / seq) * np.arange(seq)[:, None]
    f = np.linspace(1e-4, bands - 1, bands)[None, :]
    z = np.concatenate([t, np.cos(f * w), -np.sin(f * w)], axis=-1)
    zz = np.zeros((seq // 2, 2 * LANES), np.float32)
    zz[:, :HY_EMB_DIM] = z[:seq // 2]
    zz[:, LANES:LANES + HY_EMB_DIM] = z[seq // 2:]
    return zz


def _filt_mlp_kernel(z_ref, w1_ref, b1_ref, w2_ref, b2_ref, fr_ref, o_ref):
    hp = lax.Precision.HIGHEST
    fr = fr_ref[...]
    hid = LANES // 2
    h = jnp.sin(fr * (jnp.dot(z_ref[...], w1_ref[...], precision=hp, preferred_element_type=F32) + b1_ref[...]))
    h = jnp.sin(fr * (jnp.dot(h, w2_ref[...], precision=hp, preferred_element_type=F32) + b2_ref[...]))
    hi = h.astype(BF16).astype(F32)
    lo = h - hi
    hi_sw, lo_sw = pltpu.roll(hi, hid, 1), pltpu.roll(lo, hid, 1)
    low = lax.broadcasted_iota(jnp.int32, (1, LANES), 1) < hid
    o_ref[0, :, :LANES] = jnp.where(low, hi, lo_sw).astype(BF16)
    o_ref[0, :, LANES:] = jnp.where(low, hi, 0.0).astype(BF16)
    o_ref[1, :, :LANES] = jnp.where(low, hi_sw, lo).astype(BF16)
    o_ref[1, :, LANES:] = jnp.where(low, hi_sw, 0.0).astype(BF16)


def _filt_mlp(zz, fw1, fb1, fw2, fb2, freq):
    seq = zz.shape[0]
    hid = fw2.shape[0]
    assert 2 * hid == LANES
    w1 = jnp.zeros((2 * LANES, LANES), F32)
    w1 = w1.at[:HY_EMB_DIM, :hid].set(fw1).at[LANES:LANES + HY_EMB_DIM, hid:].set(fw1)
    w2 = jnp.zeros((LANES, LANES), F32).at[:hid, :hid].set(fw2).at[hid:, hid:].set(fw2)
    twice = lambda v: jnp.concatenate([v, v]).reshape(1, LANES)
    rows = 2048
    whole = lambda i: (0, 0)
    return pl.pallas_call(
        _filt_mlp_kernel,
        grid=(seq // rows,),
        in_specs=[pl.BlockSpec((rows, 2 * LANES), lambda i: (i, 0)),
                  pl.BlockSpec(w1.shape, whole), pl.BlockSpec((1, LANES), whole),
                  pl.BlockSpec(w2.shape, whole), pl.BlockSpec((1, LANES), whole), pl.BlockSpec((1, LANES), whole)],
        out_specs=pl.BlockSpec((2, rows, 2 * LANES), lambda i: (0, i, 0)),
        out_shape=jax.ShapeDtypeStruct((2, seq, 2 * LANES), BF16),
        compiler_params=_cparams(("arbitrary",)),
        name="filt_mlp",
    )(zz, w1, twice(fb1), w2, twice(fb2), twice(freq))


def _filt_time_kernel(ha_ref, hb_ref, perm_ref, w3_ref, b3_ref, delta_ref, k_ref, ss_ref, hsel_ref, *, seq):
    i = pl.program_id(0)
    rows = ha_ref.shape[0]
    half = seq // rows

    @pl.when(i < half)
    def _():
        hsel_ref[...] = ha_ref[...]

    @pl.when(i >= half)
    def _():
        for s in range(rows // RADIX):
            lo_row = rows - RADIX * (s + 1)
            nxt = hb_ref[0:RADIX, :] if s == 0 else ha_ref[lo_row + RADIX:lo_row + 2 * RADIX, :]
            src = jnp.concatenate([ha_ref[lo_row:lo_row + RADIX, :], nxt], axis=0)
            hsel_ref[s * RADIX:(s + 1) * RADIX, :] = jnp.dot(perm_ref[...], src, preferred_element_type=F32).astype(BF16)

    w = w3_ref[0]
    w_hi = w.astype(BF16)
    w_lo = (w - w_hi.astype(F32)).astype(BF16)
    wcat = jnp.concatenate([w_hi, w_hi, w_lo, jnp.zeros_like(w_lo)], axis=0)
    hf = jnp.dot(hsel_ref[...], wcat, preferred_element_type=F32) + b3_ref[0]
    m = i * rows + lax.broadcasted_iota(jnp.int32, (rows, 1), 0)
    pos = jnp.where(m < seq, m, 2 * seq - m).astype(F32)
    t = pos / (seq - 1)
    k = jnp.where(m == seq, 0.0, hf * jnp.exp(-t * delta_ref[...]))
    for j in range(rows // RADIX):
        k_ref[j * PITCH:j * PITCH + RADIX, :] = k[j * RADIX:(j + 1) * RADIX]
        k_ref[j * PITCH + RADIX:(j + 1) * PITCH, :] = jnp.zeros((PITCH - RADIX, k.shape[1]), F32)
    part = jnp.sum(k * k, axis=0, keepdims=True)

    @pl.when(i == 0)
    def _():
        ss_ref[...] = part

    @pl.when(i != 0)
    def _():
        ss_ref[...] += part


def _filt_time(hdn, w3d, b3d, delta2):
    seq, packed = hdn.shape
    _, hid, cols = w3d.shape
    assert packed == 4 * hid
    n2l = 2 * seq
    rows = 1024
    half = seq // rows
    prows = rows // RADIX * PITCH
    perm = np.zeros((RADIX, 2 * RADIX), np.float32)
    perm[0, RADIX] = 1.0
    perm[np.arange(1, RADIX), RADIX - np.arange(1, RADIX)] = 1.0
    return pl.pallas_call(
        functools.partial(_filt_time_kernel, seq=seq),
        grid=(n2l // rows,),
        in_specs=[pl.BlockSpec((rows, packed), lambda i: (jnp.where(i < half, i, 2 * half - 1 - i), 0)),
                  pl.BlockSpec((rows, packed), lambda i: (jnp.clip(2 * half - i, 0, half - 1), 0)),
                  pl.BlockSpec(perm.shape, lambda i: (0, 0)),
                  pl.BlockSpec((1, hid, cols), lambda i: (i // half, 0, 0)),
                  pl.BlockSpec((1, 1, cols), lambda i: (i // half, 0, 0)),
                  pl.BlockSpec((1, cols), lambda i: (0, 0))],
        out_specs=[pl.BlockSpec((prows, cols), lambda i: (i, 0)),
                   pl.BlockSpec((1, cols), lambda i: (0, 0))],
        out_shape=[jax.ShapeDtypeStruct((n2l // RADIX * PITCH, cols), F32), jax.ShapeDtypeStruct((1, cols), F32)],
        scratch_shapes=[pltpu.VMEM((rows, packed), BF16)],
        compiler_params=_cparams(("arbitrary",)),
        name="filt_time",
    )(hdn, hdn, jnp.asarray(perm, BF16), w3d, b3d, delta2)


def _filt_spec_kernel(k_ref, ss_ref, fa_ref, fr_ref, fi_ref, twr_ref, twi_ref, o_ref, a_ref):
    r = RADIX
    scale = lax.rsqrt(ss_ref[...] + 1e-12)

    def a_body(i, c):
        n2 = 2 * i
        rhs = jnp.concatenate([k_ref[pl.ds(n2, r, stride=PITCH), :], k_ref[pl.ds(n2 + 1, r, stride=PITCH), :]],
                              axis=1).astype(BF16)
        out = jnp.dot(fa_ref[...], rhs, preferred_element_type=F32)
        _store_pair(a_ref, pl.ds(n2, r, stride=PITCH), out[:r, :LANES], out[r:, :LANES])
        _store_pair(a_ref, pl.ds(n2 + 1, r, stride=PITCH), out[:r, LANES:], out[r:, LANES:])
        return c

    lax.fori_loop(0, r // 2, a_body, 0, unroll=DFT_UNROLL_STRIDED // 2)

    def b_body(k1, c):
        r0 = pl.multiple_of(k1 * PITCH, 8)
        o0 = pl.multiple_of(k1 * r, r)
        blk = _load_pair(a_ref, pl.ds(r0, r))
        sr, si = _cmatmul(_twiddled(fr_ref[...], fi_ref[...], twr_ref[pl.ds(k1, 1), :], twi_ref[pl.ds(k1, 1), :]), blk)
        o_ref[0, pl.ds(o0, r), :] = (sr * scale).astype(BF16)
        o_ref[1, pl.ds(o0, r), :] = (si * scale).astype(BF16)
        return c

    lax.fori_loop(0, r, b_body, 0, unroll=DFT_UNROLL_STRIDED)


def _filt_spec(ktime, ss, fa_real, fr, fi, twr, twi):
    prows, cols = ktime.shape
    n2l = prows // PITCH * RADIX
    whole = lambda j: (0, 0)
    return pl.pallas_call(
        _filt_spec_kernel,
        grid=(cols // LANES,),
        in_specs=[pl.BlockSpec((prows, LANES), lambda j: (0, j)),
                  pl.BlockSpec((1, LANES), lambda j: (0, j)),
                  _resident(fa_real.shape, whole), _resident(fr.shape, whole), _resident(fi.shape, whole),
                  _resident(twr.shape, whole), _resident(twi.shape, whole)],
        out_specs=pl.BlockSpec((2, n2l, LANES), lambda j: (0, 0, j)),
        out_shape=jax.ShapeDtypeStruct((2, n2l, cols), BF16),
        scratch_shapes=[pltpu.VMEM((2, prows, LANES), F32)],
        compiler_params=_cparams(("arbitrary",)),
        name="filt_spec",
    )(ktime, ss, fa_real, fr, fi, twr, twi)


def _short_conv_rows(u_ref, bi, r0, nrows, seq, w_ref, b_ref):
    edge = 16
    cur = u_ref[bi, pl.ds(r0, nrows), :].astype(F32)
    before = u_ref[bi, pl.ds(pl.multiple_of(jnp.maximum(r0 - edge, 0), edge), edge), :].astype(F32)
    after = u_ref[bi, pl.ds(pl.multiple_of(jnp.minimum(r0 + nrows, seq - edge), edge), edge), :].astype(F32)
    prev_edge = jnp.where(r0 > 0, before[edge - 1:edge, :], 0.0)
    next_edge = jnp.where(r0 + nrows < seq, after[0:1, :], 0.0)
    w0, w1, w2 = w_ref[0:1, :], w_ref[1:2, :], w_ref[2:3, :]
    out = b_ref[...] + w0 * pltpu.roll(cur, 1, 0) + w1 * cur + w2 * pltpu.roll(cur, nrows - 1, 0)
    row = lax.broadcasted_iota(jnp.int32, (8, cur.shape[1]), 0)
    first = out[0:8] + jnp.where(row == 0, w0 * (prev_edge - cur[nrows - 1:nrows]), 0.0)
    last = out[nrows - 8:] + jnp.where(row == 7, w2 * (next_edge - cur[0:1]), 0.0)
    return jnp.concatenate([first, out[8:nrows - 8], last], axis=0)


def _hy_conv_kernel(z_ref, zw_ref, zb_ref, u_ref, uw_ref, ub_ref, ks_hbm, hb_ref,
                    fa_ref, fc_ref, fr_ref, fi_ref, twr_ref, twi_ref, o_ref, a_ref, v_ref, ks_ref, ks_sem,
                    *, conv_in, ks_tile0):
    r = RADIX
    half = r // 2
    nb, seq, _ = z_ref.shape
    nchunk = seq // CONV_ROWS
    blocks = CONV_ROWS // r

    ks_col = pl.multiple_of((ks_tile0 + pl.program_id(0)) * LANES, LANES)
    ks_copy = pltpu.make_async_copy(ks_hbm.at[:, :, pl.ds(ks_col, LANES)], ks_ref, ks_sem)
    ks_copy.start()

    def load_body(c, carry):
        r0 = pl.multiple_of(c * CONV_ROWS, CONV_ROWS)
        for bi in range(nb):
            if conv_in:
                v = _short_conv_rows(z_ref, bi, r0, CONV_ROWS, seq, zw_ref, zb_ref)
                o_ref[bi, pl.ds(r0, CONV_ROWS), :] = v.astype(BF16)
            else:
                v = z_ref[bi, pl.ds(r0, CONV_ROWS), :].astype(F32)
            for j in range(blocks):
                p0 = pl.multiple_of((c * blocks + j) * PITCH, 8)
                v_ref[bi, pl.ds(p0, r), :] = v[j * r:(j + 1) * r]
        return carry

    lax.fori_loop(0, nchunk, load_body, 0)

    def pair_store(ref, rows0, rows1, out):
        _store_pair(ref, rows0, out[:r, :LANES], out[r:, :LANES])
        _store_pair(ref, rows1, out[:r, LANES:], out[r:, LANES:])

    def a_body(i, carry):
        n2 = 2 * i

        def column(n):
            return jnp.concatenate([v_ref[0, pl.ds(n, half, stride=PITCH), :],
                                    v_ref[1, pl.ds(n, half, stride=PITCH), :]], axis=0)

        rhs = jnp.concatenate([column(n2), column(n2 + 1)], axis=1).astype(BF16)
        out = jnp.dot(fa_ref[...], rhs, preferred_element_type=F32)
        pair_store(a_ref, pl.ds(n2, r, stride=PITCH), pl.ds(n2 + 1, r, stride=PITCH), out)
        return carry

    lax.fori_loop(0, r // 2, a_body, 0, unroll=DFT_UNROLL_STRIDED // 2)

    def b_forward(k1):
        r0 = pl.multiple_of(k1 * PITCH, 8)
        s0 = pl.multiple_of(k1 * r, r)
        blk = _load_pair(a_ref, pl.ds(r0, r))
        sr, si = _cmatmul(_twiddled(fr_ref[...], fi_ref[...], twr_ref[pl.ds(k1, 1), :], twi_ref[pl.ds(k1, 1), :]), blk)
        kr = ks_ref[0, pl.ds(s0, r), :].astype(F32)
        ki = ks_ref[1, pl.ds(s0, r), :].astype(F32)
        return jnp.concatenate([sr * kr - si * ki, sr * ki + si * kr], axis=0).astype(BF16)

    def b_forward_pair(i):
        return jnp.concatenate([b_forward(2 * i), b_forward(2 * i + 1)], axis=1)

    def b_inverse_pair(i, p):
        out = jnp.dot(fc_ref[...], p, preferred_element_type=F32)
        pair_store(a_ref, pl.ds(pl.multiple_of(2 * i * PITCH, 8), r), pl.ds(pl.multiple_of((2 * i + 1) * PITCH, 8), r), out)

    def b_body(i, p):
        p_next = b_forward_pair(i + 1)
        b_inverse_pair(i, p)
        return p_next

    ks_copy.wait()
    b_inverse_pair(r // 2 - 1, lax.fori_loop(0, r // 2 - 1, b_body, b_forward_pair(0), unroll=DFT_UNROLL // 2))

    inv_n = 1.0 / (r * r)

    def c_body(n2, carry):
        g = _twiddled(fr_ref[0:half, :], fi_ref[0:half, :], twr_ref[pl.ds(n2, 1), :], twi_ref[pl.ds(n2, 1), :],
                      inv_n, -inv_n)
        yr, yi = _cmatmul(g, _load_pair(a_ref, pl.ds(n2, r, stride=PITCH)))
        _store_pair(v_ref, pl.ds(n2, half, stride=PITCH), yr, yi)
        return carry

    lax.fori_loop(0, r, c_body, 0, unroll=DFT_UNROLL_STRIDED)

    def out_body(c, carry):
        r0 = pl.multiple_of(c * CONV_ROWS, CONV_ROWS)
        for bi in range(nb):
            gate = _short_conv_rows(u_ref, bi, r0, CONV_ROWS, seq, uw_ref, ub_ref)
            conv = jnp.concatenate(
                [v_ref[bi, pl.ds(pl.multiple_of((c * blocks + j) * PITCH, 8), r), :] for j in range(blocks)], axis=0)
            zp = (o_ref if conv_in else z_ref)[bi, pl.ds(r0, CONV_ROWS), :].astype(F32)
            o_ref[bi, pl.ds(r0, CONV_ROWS), :] = (gate * (conv + hb_ref[...] * zp)).astype(BF16)
        return carry

    lax.fori_loop(0, nchunk, out_body, 0)


def _hy_conv(z, z_part, u, u_part, order, short_w, short_b, kspec, hy_bias, tables, conv_in):
    nb, seq, _ = u.shape
    width = hy_bias.shape[1]
    tiles = width // LANES
    fa, fc, fr, fi, twr, twi = tables
    whole = lambda j: (0, 0)
    zcol = lambda j: (0, 0, z_part * tiles + j)
    ucol = lambda j: (0, 0, u_part * tiles + j)
    return pl.pallas_call(
        functools.partial(_hy_conv_kernel, conv_in=conv_in, ks_tile0=order * tiles),
        grid=(tiles,),
        in_specs=[
            pl.BlockSpec((nb, seq, LANES), zcol),
            pl.BlockSpec((3, LANES), lambda j: (0, z_part * tiles + j)),
            pl.BlockSpec((1, LANES), lambda j: (0, z_part * tiles + j)),
            pl.BlockSpec((nb, seq, LANES), ucol),
            pl.BlockSpec((3, LANES), lambda j: (0, u_part * tiles + j)),
            pl.BlockSpec((1, LANES), lambda j: (0, u_part * tiles + j)),
            pl.BlockSpec(memory_space=pl.ANY),
            pl.BlockSpec((1, LANES), lambda j: (0, j)),
            _resident(fa.shape, whole), _resident(fc.shape, whole), _resident(fr.shape, whole),
            _resident(fi.shape, whole), _resident(twr.shape, whole), _resident(twi.shape, whole),
        ],
        out_specs=_resident((nb, seq, LANES), lambda j: (0, 0, j)),
        out_shape=jax.ShapeDtypeStruct((nb, seq, width), BF16),
        scratch_shapes=[pltpu.VMEM((2, 2 * seq // RADIX * PITCH, LANES), F32),
                        pltpu.VMEM((nb, seq // RADIX * PITCH, LANES), F32),
                        pltpu.VMEM((2, 2 * seq, LANES), BF16),
                        pltpu.SemaphoreType.DMA(())],
        compiler_params=_cparams(("arbitrary",)),
        name=f"hy_conv{order}",
    )(z, short_w, short_b, u, short_w, short_b, kspec, hy_bias[order:order + 1], fa, fc, fr, fi, twr, twi)


def _hyena(hyu, short_w, short_b, fw1, fb1, fw2, fb2, fw3, fb3, freq, hy_bias):
    nb, seq, _ = hyu.shape
    assert nb == 2 and 2 * seq == RADIX * RADIX
    width = hy_bias.shape[1]
    hid = fw2.shape[0]
    fr, fi, twr, twi = _dft_tables()
    half = RADIX // 2
    fa_real = jnp.concatenate([fr, fi], axis=0).astype(BF16)
    fa_half = _embed(fr[:, :half], fi[:, :half]).astype(BF16)
    fc = _embed(fr, -fi).astype(BF16)

    zz = jnp.asarray(_position_features(seq))
    hdn = _filt_mlp(zz, fw1, fb1, fw2, fb2, freq).reshape(seq, 4 * hid)
    w3d = fw3.reshape(hid, 2, HY_ORDER * width).transpose(1, 0, 2)
    b3d = fb3.reshape(2, 1, HY_ORDER * width)
    max_decay = math.log(HY_DECAY_TARGET) / HY_FAST_DECAY_PCT
    min_decay = math.log(HY_DECAY_TARGET) / HY_SLOW_DECAY_PCT
    delta = np.abs(np.linspace(min_decay, max_decay, width)).astype(np.float32)
    delta2 = jnp.asarray(np.tile(delta, HY_ORDER)[None, :])
    ktime, ss = _filt_time(hdn, w3d, b3d, delta2)
    kspec = _filt_spec(ktime, ss, fa_real, fr, fi, twr, twi)

    tables = (fa_half, fc, fr, fi, twr, twi)
    sb = short_b.reshape(1, -1)
    z1 = _hy_conv(hyu, 0, hyu, 1, 0, short_w, sb, kspec, hy_bias, tables, conv_in=True)
    return _hy_conv(z1, 0, hyu, 2, 1, short_w, sb, kspec, hy_bias, tables, conv_in=False)


def kernel(x, ln1_g, ln1_b, ffn1_w1, ffn1_w3, ffn1_w2, w_in, b_gate, na_rpb, hy_short_w, hy_short_b,
           hy_filt_w1, hy_filt_b1, hy_filt_w2, hy_filt_b2, hy_filt_w3, hy_filt_b3, hy_filt_freq, hy_bias,
           w_pa, w_pb, w_out, ln2_g, ln2_b, ffn2_w1, ffn2_w3, ffn2_w2, ln3_g, ln3_b):
    assert ln1_g.shape[0] == DEPTH
    b, seq, d = x.shape
    rows = seq // GRID_W
    na_width = NA_HEADS * NA_HEAD_DIM
    n_qkv = 3 * na_width
    n_hy = hy_short_w.shape[2]
    vec = lambda a: a[0].reshape(1, -1)

    x1, qkv, hyu, gates = _dense_in(
        x.reshape(b * seq, d), ffn1_w1[0], ffn1_w3[0], ffn1_w2[0], vec(ln1_g), vec(ln1_b),
        w_in[0], vec(b_gate), n_qkv, n_hy)

    bias = _na_bias(na_rpb[0], rows)
    ya = _na_attn(qkv.reshape(b, seq, n_qkv), bias, rows)

    yb = _hyena(hyu.reshape(b, seq, n_hy), hy_short_w[0], hy_short_b[0], hy_filt_w1[0], hy_filt_b1[0],
                hy_filt_w2[0], hy_filt_b2[0], hy_filt_w3[0], hy_filt_b3[0], hy_filt_freq[0], hy_bias[0])

    out = _dense_out(
        x1, ya.reshape(b * seq, na_width), yb.reshape(b * seq, -1), gates,
        w_pa[0], w_pb[0], w_out[0], vec(ln2_g), vec(ln2_b),
        ffn2_w1[0], ffn2_w3[0], ffn2_w2[0], vec(ln3_g), vec(ln3_b))
    return out.reshape(b, seq, d)
```

```python
import functools
import math

import numpy as np
import jax
import jax.numpy as jnp
from jax import lax
from jax.experimental import pallas as pl
from jax.experimental.pallas import tpu as pltpu

F32 = jnp.float32
BF16 = jnp.bfloat16

GRID_W = 64
NA_HEADS = 8
NA_HEAD_DIM = 64
NA_KH = 8
NA_KW = 16
HY_ORDER = 2
HY_EMB_DIM = 33
HY_FAST_DECAY_PCT = 0.3
HY_SLOW_DECAY_PCT = 1.5
HY_DECAY_TARGET = 1e-2
DEPTH = 1
DN_ALPHA = (2 * DEPTH) ** 0.25
LN_EPS = 1e-5
LOG2E = math.log2(math.e)

LANES = 128
VMEM_LIMIT = 60 * 1024 * 1024

ROW_TILE = 512
FF_CHUNK = 256
PROJ_CHUNK = 512
CAST_ROWS = 64
CAST_SLOTS = 8
NA_QROWS = 4
NA_KROWS = NA_QROWS + NA_KH
NA_BIAS_GROUP = 5
RADIX = 128
PITCH = RADIX + 8
DFT_UNROLL = 18
DFT_UNROLL_STRIDED = 32
CONV_ROWS = 512
NEG = -1e30


def _cparams(sem):
    return pltpu.CompilerParams(dimension_semantics=sem, vmem_limit_bytes=VMEM_LIMIT)


def _resident(shape, index_map):
    return pl.BlockSpec(shape, index_map, pipeline_mode=pl.Buffered(1))


def _cast_weights_in(pairs, stage_ref, sem):
    ahead = CAST_SLOTS - 1
    counts = [src.shape[0] // CAST_ROWS for src, _ in pairs]
    assert all(n >= ahead for n in counts)
    bases = [sum(counts[:w]) for w in range(len(pairs))]

    def copy(w, c, g):
        src, cols = pairs[w][0], pairs[w][0].shape[1]
        slot = g % CAST_SLOTS
        return pltpu.make_async_copy(src.at[pl.ds(c * CAST_ROWS, CAST_ROWS), :],
                                     stage_ref.at[slot, :, pl.ds(0, cols)], sem.at[slot])

    for g in range(ahead):
        copy(0, g, g).start()
    for w, (src, dst) in enumerate(pairs):
        n, cols = counts[w], src.shape[1]

        def body(c, carry, w=w, n=n, cols=cols, dst=dst):
            g = bases[w] + c

            @pl.when(c + ahead < n)
            def _():
                copy(w, c + ahead, g + ahead).start()

            if w + 1 < len(pairs):
                @pl.when(c + ahead >= n)
                def _():
                    copy(w + 1, c + ahead - n, g + ahead).start()

            copy(w, c, g).wait()
            r0 = pl.multiple_of(c * CAST_ROWS, CAST_ROWS)
            dst[pl.ds(r0, CAST_ROWS), :] = stage_ref[g % CAST_SLOTS, :, 0:cols].astype(BF16)
            return carry

        lax.fori_loop(0, n, body, 0)


def _layer_norm(r, g, b):
    mu = jnp.mean(r, axis=-1, keepdims=True)
    d = r - mu
    var = jnp.mean(d * d, axis=-1, keepdims=True)
    return d * lax.rsqrt(var + LN_EPS) * g + b


def _swiglu_ln(x, xb_ref, acc_ref, w1_ref, w3_ref, w2_ref, g_ref, b_ref):
    d_ff = w1_ref.shape[1]
    xb_ref[...] = x.astype(BF16)
    for j in range(d_ff // FF_CHUNK):
        sl = slice(j * FF_CHUNK, (j + 1) * FF_CHUNK)
        a = jnp.dot(xb_ref[...], w1_ref[:, sl], preferred_element_type=F32)
        b = jnp.dot(xb_ref[...], w3_ref[:, sl], preferred_element_type=F32)
        h = (a * jax.nn.sigmoid(a) * b).astype(BF16)
        part = jnp.dot(h, w2_ref[sl, :], preferred_element_type=F32)
        if j == 0:
            acc_ref[...] = part
        else:
            acc_ref[...] += part
    return _layer_norm(DN_ALPHA * x + 0.5 * acc_ref[...], g_ref[...], b_ref[...])


def _dense_in_kernel(x_ref, w1_hbm, w3_hbm, w2_hbm, g_ref, b_ref, win_hbm, bg_ref,
                     x1_ref, qkv_ref, hyu_ref, gate_ref,
                     xb_ref, acc_ref, w1_ref, w3_ref, w2_ref, win_ref, stage_ref, sem):
    @pl.when(pl.program_id(0) == 0)
    def _():
        _cast_weights_in(((w1_hbm, w1_ref), (w3_hbm, w3_ref), (w2_hbm, w2_ref), (win_hbm, win_ref)), stage_ref, sem)

    x1 = _swiglu_ln(x_ref[...], xb_ref, acc_ref, w1_ref, w3_ref, w2_ref, g_ref, b_ref)
    x1_ref[...] = x1
    xb_ref[...] = x1.astype(BF16)
    n_qkv = qkv_ref.shape[1]
    n_hy = hyu_ref.shape[1]
    n_head = n_qkv // 3
    for j in range(win_ref.shape[1] // PROJ_CHUNK):
        c0 = j * PROJ_CHUNK
        p = jnp.dot(xb_ref[...], win_ref[:, c0:c0 + PROJ_CHUNK], preferred_element_type=F32)
        if c0 < n_qkv:
            if c0 < n_head:
                p = p * (NA_HEAD_DIM ** -0.5 * LOG2E)
            qkv_ref[:, c0:c0 + PROJ_CHUNK] = p.astype(BF16)
        elif c0 < n_qkv + n_hy:
            hyu_ref[:, c0 - n_qkv:c0 - n_qkv + PROJ_CHUNK] = p.astype(BF16)
        else:
            g0 = c0 - n_qkv - n_hy
            gate_ref[:, g0:g0 + PROJ_CHUNK] = jax.nn.sigmoid(p + bg_ref[:, g0:g0 + PROJ_CHUNK]).astype(BF16)


def _dense_in(x2d, w1, w3, w2, g, b, w_in, b_gate, n_qkv, n_hy):
    n, d = x2d.shape
    d_ff = w1.shape[1]
    n_gate = w_in.shape[1] - n_qkv - n_hy
    assert n % ROW_TILE == 0 and d_ff % FF_CHUNK == 0
    assert n_qkv % (3 * PROJ_CHUNK) == 0 and n_hy % PROJ_CHUNK == 0 and n_gate % PROJ_CHUNK == 0
    weights = (w1, w3, w2, w_in)
    assert all(w.dtype == F32 and w.shape[0] % CAST_ROWS == 0 for w in weights)
    row = lambda i: (i, 0)
    whole = lambda i: (0, 0)
    hbm = pl.BlockSpec(memory_space=pl.ANY)
    return pl.pallas_call(
        _dense_in_kernel,
        grid=(n // ROW_TILE,),
        in_specs=[
            pl.BlockSpec((ROW_TILE, d), row),
            hbm, hbm, hbm,
            _resident((1, d), whole), _resident((1, d), whole),
            hbm, _resident((1, n_gate), whole),
        ],
        out_specs=[
            pl.BlockSpec((ROW_TILE, d), row),
            pl.BlockSpec((ROW_TILE, n_qkv), row),
            pl.BlockSpec((ROW_TILE, n_hy), row),
            pl.BlockSpec((ROW_TILE, n_gate), row),
        ],
        out_shape=[
            jax.ShapeDtypeStruct((n, d), F32),
            jax.ShapeDtypeStruct((n, n_qkv), BF16),
            jax.ShapeDtypeStruct((n, n_hy), BF16),
            jax.ShapeDtypeStruct((n, n_gate), BF16),
        ],
        scratch_shapes=[pltpu.VMEM((ROW_TILE, d), BF16), pltpu.VMEM((ROW_TILE, d), F32)]
        + [pltpu.VMEM(w.shape, BF16) for w in weights]
        + [pltpu.VMEM((CAST_SLOTS, CAST_ROWS, max(w.shape[1] for w in weights)), F32),
           pltpu.SemaphoreType.DMA((CAST_SLOTS,))],
        compiler_params=_cparams(("arbitrary",)),
        name="dense_in",
    )(x2d, w1, w3, w2, g, b, w_in, b_gate)


def _dense_out_kernel(x1_ref, ya_ref, yb_ref, gate_ref, wpa_hbm, wpb_hbm, wout_hbm, g2_ref, b2_ref,
                      w1_hbm, w3_hbm, w2_hbm, g3_ref, b3_ref, o_ref,
                      xb_ref, acc_ref, wpa_ref, wpb_ref, wout_ref, w1_ref, w3_ref, w2_ref, stage_ref, sem):
    @pl.when(pl.program_id(0) == 0)
    def _():
        _cast_weights_in(((wpa_hbm, wpa_ref), (wpb_hbm, wpb_ref), (wout_hbm, wout_ref),
                          (w1_hbm, w1_ref), (w3_hbm, w3_ref), (w2_hbm, w2_ref)), stage_ref, sem)

    d = x1_ref.shape[1]
    pa = jnp.dot(ya_ref[...], wpa_ref[...], preferred_element_type=F32)
    pb = jnp.dot(yb_ref[...], wpb_ref[...], preferred_element_type=F32)
    m = gate_ref[:, :d].astype(F32) * pa + gate_ref[:, d:].astype(F32) * pb
    mix = jnp.dot(m.astype(BF16), wout_ref[...], preferred_element_type=F32)
    x2 = _layer_norm(DN_ALPHA * x1_ref[...] + mix, g2_ref[...], b2_ref[...])
    o_ref[...] = _swiglu_ln(x2, xb_ref, acc_ref, w1_ref, w3_ref, w2_ref, g3_ref, b3_ref)


def _dense_out(x1, ya, yb, gates, wpa, wpb, wout, g2, b2, w1, w3, w2, g3, b3):
    n, d = x1.shape
    weights = (wpa, wpb, wout, w1, w3, w2)
    assert all(w.dtype == F32 and w.shape[0] % CAST_ROWS == 0 for w in weights)
    row = lambda i: (i, 0)
    whole = lambda i: (0, 0)
    hbm = pl.BlockSpec(memory_space=pl.ANY)
    return pl.pallas_call(
        _dense_out_kernel,
        grid=(n // ROW_TILE,),
        in_specs=[
            pl.BlockSpec((ROW_TILE, d), row),
            pl.BlockSpec((ROW_TILE, ya.shape[1]), row),
            pl.BlockSpec((ROW_TILE, yb.shape[1]), row),
            pl.BlockSpec((ROW_TILE, gates.shape[1]), row),
            hbm, hbm, hbm,
            _resident((1, d), whole), _resident((1, d), whole),
            hbm, hbm, hbm,
            _resident((1, d), whole), _resident((1, d), whole),
        ],
        out_specs=pl.BlockSpec((ROW_TILE, d), row),
        out_shape=jax.ShapeDtypeStruct((n, d), F32),
        scratch_shapes=[pltpu.VMEM((ROW_TILE, d), BF16), pltpu.VMEM((ROW_TILE, d), F32)]
        + [pltpu.VMEM(w.shape, BF16) for w in weights]
        + [pltpu.VMEM((CAST_SLOTS, CAST_ROWS, max(w.shape[1] for w in weights)), F32),
           pltpu.SemaphoreType.DMA((CAST_SLOTS,))],
        compiler_params=_cparams(("arbitrary",)),
        name="dense_out",
    )(x1, ya, yb, gates, wpa, wpb, wout, g2, b2, w1, w3, w2, g3, b3)


def _window_start(n, k):
    return np.clip(np.arange(n) - k // 2, 0, n - k)


def _na_group_types(rows):
    last = rows - NA_QROWS
    return ((0, 0), (NA_QROWS * 2, NA_QROWS), (last, rows - NA_KROWS))


def _na_tables(rows):
    w = GRID_W
    shift = np.zeros((2 * NA_KW - 1, w, 2 * w), np.float32)
    q = np.arange(w)[:, None]
    kc = np.arange(w)[None, :]
    for dc in range(2 * NA_KW - 1):
        shift[dc, :, :w] = (kc - q + NA_KW - 1 == dc).astype(np.float32)
    cs = _window_start(w, NA_KW)
    col_ok = (kc >= cs[:, None]) & (kc < cs[:, None] + NA_KW)
    rs_all = _window_start(rows, NA_KH)
    negmask = np.full((3, NA_QROWS, w, NA_KROWS, w), NEG, np.float32)
    for t, (r0, kb) in enumerate(_na_group_types(rows)):
        for i in range(NA_QROWS):
            rs = rs_all[r0 + i]
            for j in range(NA_KROWS):
                if rs <= kb + j < rs + NA_KH:
                    negmask[t, i, :, j, :] = np.where(col_ok, 0.0, NEG)
    return shift, negmask.reshape(3, NA_QROWS * w, NA_KROWS * w)


def _na_bias_kernel(rpb_ref, shift_ref, neg_ref, o_ref, u_ref, *, rows):
    h = pl.program_id(0)
    n_dr = 2 * NA_KH - 1
    n_dc = 2 * NA_KW - 1
    w = GRID_W

    for dr0 in range(0, n_dr, NA_BIAS_GROUP):
        base = (h * n_dr + dr0) * n_dc
        accs = [None] * NA_BIAS_GROUP
        for dc in range(n_dc):
            mask = shift_ref[dc]
            for j in range(NA_BIAS_GROUP):
                term = rpb_ref[base + j * n_dc + dc] * mask
                accs[j] = term if dc == 0 else accs[j] + term
        for j in range(NA_BIAS_GROUP):
            left = accs[j] * LOG2E
            u_ref[0, dr0 + j] = left
            u_ref[1, dr0 + j] = pltpu.roll(left, w, 1)
    rs_all = _window_start(rows, NA_KH)
    for t, (r0, kb) in enumerate(_na_group_types(rows)):
        for i in range(NA_QROWS):
            rs = rs_all[r0 + i]
            for jj in range(NA_KROWS // 2):
                val = neg_ref[t, i * w:(i + 1) * w, jj * 2 * w:(jj + 1) * 2 * w]
                for s in range(2):
                    kr = kb + 2 * jj + s
                    if rs <= kr < rs + NA_KH:
                        val = val + u_ref[s, kr - (r0 + i) + NA_KH - 1]
                o_ref[t, 0, i * w:(i + 1) * w, jj * 2 * w:(jj + 1) * 2 * w] = val


def _na_bias(rpb, rows):
    shift, negmask = _na_tables(rows)
    nq, nk = NA_QROWS * GRID_W, NA_KROWS * GRID_W
    return pl.pallas_call(
        functools.partial(_na_bias_kernel, rows=rows),
        grid=(NA_HEADS,),
        in_specs=[
            pl.BlockSpec(memory_space=pltpu.SMEM),
            _resident(shift.shape, lambda h: (0, 0, 0)),
            _resident(negmask.shape, lambda h: (0, 0, 0)),
        ],
        out_specs=pl.BlockSpec((3, 1, nq, nk), lambda h: (0, h, 0, 0)),
        out_shape=jax.ShapeDtypeStruct((3, NA_HEADS, nq, nk), F32),
        scratch_shapes=[pltpu.VMEM((2, 2 * NA_KH - 1, GRID_W, 2 * GRID_W), F32)],
        compiler_params=_cparams(("arbitrary",)),
        name="na_bias",
    )(rpb.reshape(-1), jnp.asarray(shift), jnp.asarray(negmask))


def _na_attn_kernel(q_ref, k_ref, v_ref, bias_ref, o_ref, s_ref, e_ref, *, rows):
    g = pl.program_id(1)
    kb = jnp.clip(g * NA_QROWS - NA_KH // 2, 0, rows - NA_KROWS)
    k0 = pl.multiple_of(kb * GRID_W, GRID_W)
    nk = NA_KROWS * GRID_W
    dh = NA_HEAD_DIM
    lane = lax.broadcasted_iota(jnp.int32, (1, 2 * dh), 1)
    own = (lane < dh, lane >= dh)
    pair_cols = [slice(hp * 2 * dh, (hp + 1) * 2 * dh) for hp in range(NA_HEADS // 2)]
    for hp, cols in enumerate(pair_cols):
        q2 = q_ref[0, :, cols]
        k2 = k_ref[0, pl.ds(k0, nk), cols]
        for s in range(2):
            s_ref[2 * hp + s] = lax.dot_general(q2, jnp.where(own[s], k2, jnp.zeros_like(k2)),
                                                (((1,), (1,)), ((), ())), preferred_element_type=F32)
    for h in range(NA_HEADS):
        sc = s_ref[h] + bias_ref[0, h]
        e_ref[h] = jnp.exp2(sc - jnp.max(sc, axis=-1, keepdims=True)).astype(BF16)
    for hp, cols in enumerate(pair_cols):
        v2 = v_ref[0, pl.ds(k0, nk), cols]
        pair = []
        for s in range(2):
            r = jnp.dot(e_ref[2 * hp + s], jnp.where(own[s], v2, jnp.ones_like(v2)), preferred_element_type=F32)
            den = r[:, dh:dh + 1] if s == 0 else r[:, 0:1]
            pair.append(jnp.where(own[s], r / den, 0.0))
        o_ref[0, :, cols] = (pair[0] + pair[1]).astype(BF16)


def _na_attn(qkv, bias, rows):
    b, seq, three_w = qkv.shape
    width = three_w // 3
    nq, nk = NA_QROWS * GRID_W, NA_KROWS * GRID_W
    groups = rows // NA_QROWS

    def bias_idx(bi, g):
        return (jnp.where(g == 0, 0, jnp.where(g == groups - 1, 2, 1)), 0, 0, 0)

    return pl.pallas_call(
        functools.partial(_na_attn_kernel, rows=rows),
        grid=(b, groups),
        in_specs=[
            pl.BlockSpec((1, nq, width), lambda bi, g: (bi, g, 0)),
            pl.BlockSpec((1, seq, width), lambda bi, g: (bi, 0, 1)),
            pl.BlockSpec((1, seq, width), lambda bi, g: (bi, 0, 2)),
            pl.BlockSpec((1, NA_HEADS, nq, nk), bias_idx),
        ],
        out_specs=pl.BlockSpec((1, nq, width), lambda bi, g: (bi, g, 0)),
        out_shape=jax.ShapeDtypeStruct((b, seq, width), BF16),
        scratch_shapes=[pltpu.VMEM((NA_HEADS, nq, nk), F32), pltpu.VMEM((NA_HEADS, nq, nk), BF16)],
        compiler_params=_cparams(("arbitrary", "arbitrary")),
        name="na_attn",
    )(qkv, qkv, qkv, bias)


def _dft_tables():
    k = np.arange(RADIX)
    ang = 2.0 * np.pi * np.outer(k, k) / RADIX
    fr, fi = np.cos(ang), -np.sin(ang)
    ang_t = 2.0 * np.pi * np.outer(k, k) / (RADIX * RADIX)
    twr, twi = np.cos(ang_t), -np.sin(ang_t)
    f32 = lambda a: jnp.asarray(a, F32)
    return f32(fr), f32(fi), f32(twr), f32(twi)


def _embed(cr, ci):
    return jnp.concatenate([jnp.concatenate([cr, -ci], axis=1), jnp.concatenate([ci, cr], axis=1)], axis=0)


def _cmatmul(c_rows, x_lanes):
    m = c_rows.shape[0] // 2
    r = jnp.dot(c_rows, x_lanes, preferred_element_type=F32)
    return r[:m, :LANES] - r[m:, LANES:], r[:m, LANES:] + r[m:, :LANES]


def _load_pair(ref, rows):
    return jnp.concatenate([ref[0, rows, :], ref[1, rows, :]], axis=1).astype(BF16)


def _store_pair(ref, rows, re, im):
    ref[0, rows, :] = re
    ref[1, rows, :] = im


def _twiddled(fr, fi, twr, twi, scale_re=1.0, scale_im=1.0):
    return jnp.concatenate([(fr * twr - fi * twi) * scale_re, (fr * twi + fi * twr) * scale_im], axis=0).astype(BF16)


def _position_features(seq):
    t = np.linspace(0.0, 1.0, seq)[:, None]
    bands = (HY_EMB_DIM - 1) // 2
    w = (2.0 * math.pi / seq) * np.arange(seq)[:, None]
    f = np.linspace(1e-4, bands - 1, bands)[None, :]
    z = np.concatenate([t, np.cos(f * w), -np.sin(f * w)], axis=-1)
    zz = np.zeros((seq // 2, 2 * LANES), np.float32)
    zz[:, :HY_EMB_DIM] = z[:seq // 2]
    zz[:, LANES:LANES + HY_EMB_DIM] = z[seq // 2:]
    return zz


def _filt_mlp_kernel(z_ref, w1_ref, b1_ref, w2_ref, b2_ref, fr_ref, o_ref):
    hp = lax.Precision.HIGHEST
    fr = fr_ref[...]
    hid = LANES // 2
    h = jnp.sin(fr * (jnp.dot(z_ref[...], w1_ref[...], precision=hp, preferred_element_type=F32) + b1_ref[...]))
    h = jnp.sin(fr * (jnp.dot(h, w2_ref[...], precision=hp, preferred_element_type=F32) + b2_ref[...]))
    hi = h.astype(BF16).astype(F32)
    lo = h - hi
    hi_sw, lo_sw = pltpu.roll(hi, hid, 1), pltpu.roll(lo, hid, 1)
    low = lax.broadcasted_iota(jnp.int32, (1, LANES), 1) < hid
    o_ref[0, :, :LANES] = jnp.where(low, hi, lo_sw).astype(BF16)
    o_ref[0, :, LANES:] = jnp.where(low, hi, 0.0).astype(BF16)
    o_ref[1, :, :LANES] = jnp.where(low, hi_sw, lo).astype(BF16)
    o_ref[1, :, LANES:] = jnp.where(low, hi_sw, 0.0).astype(BF16)


def _filt_mlp(zz, fw1, fb1, fw2, fb2, freq):
    seq = zz.shape[0]
    hid = fw2.shape[0]
    assert 2 * hid == LANES
    w1 = jnp.zeros((2 * LANES, LANES), F32)
    w1 = w1.at[:HY_EMB_DIM, :hid].set(fw1).at[LANES:LANES + HY_EMB_DIM, hid:].set(fw1)
    w2 = jnp.zeros((LANES, LANES), F32).at[:hid, :hid].set(fw2).at[hid:, hid:].set(fw2)
    twice = lambda v: jnp.concatenate([v, v]).reshape(1, LANES)
    rows = 2048
    whole = lambda i: (0, 0)
    return pl.pallas_call(
        _filt_mlp_kernel,
        grid=(seq // rows,),
        in_specs=[pl.BlockSpec((rows, 2 * LANES), lambda i: (i, 0)),
                  pl.BlockSpec(w1.shape, whole), pl.BlockSpec((1, LANES), whole),
                  pl.BlockSpec(w2.shape, whole), pl.BlockSpec((1, LANES), whole), pl.BlockSpec((1, LANES), whole)],
        out_specs=pl.BlockSpec((2, rows, 2 * LANES), lambda i: (0, i, 0)),
        out_shape=jax.ShapeDtypeStruct((2, seq, 2 * LANES), BF16),
        compiler_params=_cparams(("arbitrary",)),
        name="filt_mlp",
    )(zz, w1, twice(fb1), w2, twice(fb2), twice(freq))


def _filt_time_kernel(ha_ref, hb_ref, perm_ref, w3_ref, b3_ref, delta_ref, k_ref, ss_ref, hsel_ref, *, seq):
    i = pl.program_id(0)
    rows = ha_ref.shape[0]
    half = seq // rows

    @pl.when(i < half)
    def _():
        hsel_ref[...] = ha_ref[...]

    @pl.when(i >= half)
    def _():
        for s in range(rows // RADIX):
            lo_row = rows - RADIX * (s + 1)
            nxt = hb_ref[0:RADIX, :] if s == 0 else ha_ref[lo_row + RADIX:lo_row + 2 * RADIX, :]
            src = jnp.concatenate([ha_ref[lo_row:lo_row + RADIX, :], nxt], axis=0)
            hsel_ref[s * RADIX:(s + 1) * RADIX, :] = jnp.dot(perm_ref[...], src, preferred_element_type=F32).astype(BF16)

    w = w3_ref[0]
    w_hi = w.astype(BF16)
    w_lo = (w - w_hi.astype(F32)).astype(BF16)
    wcat = jnp.concatenate([w_hi, w_hi, w_lo, jnp.zeros_like(w_lo)], axis=0)
    hf = jnp.dot(hsel_ref[...], wcat, preferred_element_type=F32) + b3_ref[0]
    m = i * rows + lax.broadcasted_iota(jnp.int32, (rows, 1), 0)
    pos = jnp.where(m < seq, m, 2 * seq - m).astype(F32)
    t = pos / (seq - 1)
    k = jnp.where(m == seq, 0.0, hf * jnp.exp(-t * delta_ref[...]))
    for j in range(rows // RADIX):
        k_ref[j * PITCH:j * PITCH + RADIX, :] = k[j * RADIX:(j + 1) * RADIX]
        k_ref[j * PITCH + RADIX:(j + 1) * PITCH, :] = jnp.zeros((PITCH - RADIX, k.shape[1]), F32)
    part = jnp.sum(k * k, axis=0, keepdims=True)

    @pl.when(i == 0)
    def _():
        ss_ref[...] = part

    @pl.when(i != 0)
    def _():
        ss_ref[...] += part


def _filt_time(hdn, w3d, b3d, delta2):
    seq, packed = hdn.shape
    _, hid, cols = w3d.shape
    assert packed == 4 * hid
    n2l = 2 * seq
    rows = 1024
    half = seq // rows
    prows = rows // RADIX * PITCH
    perm = np.zeros((RADIX, 2 * RADIX), np.float32)
    perm[0, RADIX] = 1.0
    perm[np.arange(1, RADIX), RADIX - np.arange(1, RADIX)] = 1.0
    return pl.pallas_call(
        functools.partial(_filt_time_kernel, seq=seq),
        grid=(n2l // rows,),
        in_specs=[pl.BlockSpec((rows, packed), lambda i: (jnp.where(i < half, i, 2 * half - 1 - i), 0)),
                  pl.BlockSpec((rows, packed), lambda i: (jnp.clip(2 * half - i, 0, half - 1), 0)),
                  pl.BlockSpec(perm.shape, lambda i: (0, 0)),
                  pl.BlockSpec((1, hid, cols), lambda i: (i // half, 0, 0)),
                  pl.BlockSpec((1, 1, cols), lambda i: (i // half, 0, 0)),
                  pl.BlockSpec((1, cols), lambda i: (0, 0))],
        out_specs=[pl.BlockSpec((prows, cols), lambda i: (i, 0)),
                   pl.BlockSpec((1, cols), lambda i: (0, 0))],
        out_shape=[jax.ShapeDtypeStruct((n2l // RADIX * PITCH, cols), F32), jax.ShapeDtypeStruct((1, cols), F32)],
        scratch_shapes=[pltpu.VMEM((rows, packed), BF16)],
        compiler_params=_cparams(("arbitrary",)),
        name="filt_time",
    )(hdn, hdn, jnp.asarray(perm, BF16), w3d, b3d, delta2)


def _filt_spec_kernel(k_ref, ss_ref, fa_ref, fr_ref, fi_ref, twr_ref, twi_ref, o_ref, a_ref):
    r = RADIX
    scale = lax.rsqrt(ss_ref[...] + 1e-12)

    def a_body(i, c):
        n2 = 2 * i
        rhs = jnp.concatenate([k_ref[pl.ds(n2, r, stride=PITCH), :], k_ref[pl.ds(n2 + 1, r, stride=PITCH), :]],
                              axis=1).astype(BF16)
        out = jnp.dot(fa_ref[...], rhs, preferred_element_type=F32)
        _store_pair(a_ref, pl.ds(n2, r, stride=PITCH), out[:r, :LANES], out[r:, :LANES])
        _store_pair(a_ref, pl.ds(n2 + 1, r, stride=PITCH), out[:r, LANES:], out[r:, LANES:])
        return c

    lax.fori_loop(0, r // 2, a_body, 0, unroll=DFT_UNROLL_STRIDED // 2)

    def b_body(k1, c):
        r0 = pl.multiple_of(k1 * PITCH, 8)
        o0 = pl.multiple_of(k1 * r, r)
        blk = _load_pair(a_ref, pl.ds(r0, r))
        sr, si = _cmatmul(_twiddled(fr_ref[...], fi_ref[...], twr_ref[pl.ds(k1, 1), :], twi_ref[pl.ds(k1, 1), :]), blk)
        o_ref[0, pl.ds(o0, r), :] = (sr * scale).astype(BF16)
        o_ref[1, pl.ds(o0, r), :] = (si * scale).astype(BF16)
        return c

    lax.fori_loop(0, r, b_body, 0, unroll=DFT_UNROLL_STRIDED)


def _filt_spec(ktime, ss, fa_real, fr, fi, twr, twi):
    prows, cols = ktime.shape
    n2l = prows // PITCH * RADIX
    whole = lambda j: (0, 0)
    return pl.pallas_call(
        _filt_spec_kernel,
        grid=(cols // LANES,),
        in_specs=[pl.BlockSpec((prows, LANES), lambda j: (0, j)),
                  pl.BlockSpec((1, LANES), lambda j: (0, j)),
                  _resident(fa_real.shape, whole), _resident(fr.shape, whole), _resident(fi.shape, whole),
                  _resident(twr.shape, whole), _resident(twi.shape, whole)],
        out_specs=pl.BlockSpec((2, n2l, LANES), lambda j: (0, 0, j)),
        out_shape=jax.ShapeDtypeStruct((2, n2l, cols), BF16),
        scratch_shapes=[pltpu.VMEM((2, prows, LANES), F32)],
        compiler_params=_cparams(("arbitrary",)),
        name="filt_spec",
    )(ktime, ss, fa_real, fr, fi, twr, twi)


def _short_conv_rows(u_ref, bi, r0, nrows, seq, w_ref, b_ref):
    edge = 16
    cur = u_ref[bi, pl.ds(r0, nrows), :].astype(F32)
    before = u_ref[bi, pl.ds(pl.multiple_of(jnp.maximum(r0 - edge, 0), edge), edge), :].astype(F32)
    after = u_ref[bi, pl.ds(pl.multiple_of(jnp.minimum(r0 + nrows, seq - edge), edge), edge), :].astype(F32)
    prev_edge = jnp.where(r0 > 0, before[edge - 1:edge, :], 0.0)
    next_edge = jnp.where(r0 + nrows < seq, after[0:1, :], 0.0)
    w0, w1, w2 = w_ref[0:1, :], w_ref[1:2, :], w_ref[2:3, :]
    out = b_ref[...] + w0 * pltpu.roll(cur, 1, 0) + w1 * cur + w2 * pltpu.roll(cur, nrows - 1, 0)
    row = lax.broadcasted_iota(jnp.int32, (8, cur.shape[1]), 0)
    first = out[0:8] + jnp.where(row == 0, w0 * (prev_edge - cur[nrows - 1:nrows]), 0.0)
    last = out[nrows - 8:] + jnp.where(row == 7, w2 * (next_edge - cur[0:1]), 0.0)
    return jnp.concatenate([first, out[8:nrows - 8], last], axis=0)


def _hy_conv_kernel(z_ref, zw_ref, zb_ref, u_ref, uw_ref, ub_ref, ks_hbm, hb_ref,
                    fa_ref, fc_ref, fr_ref, fi_ref, twr_ref, twi_ref, o_ref, a_ref, v_ref, ks_ref, ks_sem,
                    *, conv_in, ks_tile0):
    r = RADIX
    half = r // 2
    nb, seq, _ = z_ref.shape
    nchunk = seq // CONV_ROWS
    blocks = CONV_ROWS // r

    ks_col = pl.multiple_of((ks_tile0 + pl.program_id(0)) * LANES, LANES)
    ks_copy = pltpu.make_async_copy(ks_hbm.at[:, :, pl.ds(ks_col, LANES)], ks_ref, ks_sem)
    ks_copy.start()

    def load_body(c, carry):
        r0 = pl.multiple_of(c * CONV_ROWS, CONV_ROWS)
        for bi in range(nb):
            if conv_in:
                v = _short_conv_rows(z_ref, bi, r0, CONV_ROWS, seq, zw_ref, zb_ref)
                o_ref[bi, pl.ds(r0, CONV_ROWS), :] = v.astype(BF16)
            else:
                v = z_ref[bi, pl.ds(r0, CONV_ROWS), :].astype(F32)
            for j in range(blocks):
                p0 = pl.multiple_of((c * blocks + j) * PITCH, 8)
                v_ref[bi, pl.ds(p0, r), :] = v[j * r:(j + 1) * r]
        return carry

    lax.fori_loop(0, nchunk, load_body, 0)

    def pair_store(ref, rows0, rows1, out):
        _store_pair(ref, rows0, out[:r, :LANES], out[r:, :LANES])
        _store_pair(ref, rows1, out[:r, LANES:], out[r:, LANES:])

    def a_body(i, carry):
        n2 = 2 * i

        def column(n):
            return jnp.concatenate([v_ref[0, pl.ds(n, half, stride=PITCH), :],
                                    v_ref[1, pl.ds(n, half, stride=PITCH), :]], axis=0)

        rhs = jnp.concatenate([column(n2), column(n2 + 1)], axis=1).astype(BF16)
        out = jnp.dot(fa_ref[...], rhs, preferred_element_type=F32)
        pair_store(a_ref, pl.ds(n2, r, stride=PITCH), pl.ds(n2 + 1, r, stride=PITCH), out)
        return carry

    lax.fori_loop(0, r // 2, a_body, 0, unroll=DFT_UNROLL_STRIDED // 2)

    def b_forward(k1):
        r0 = pl.multiple_of(k1 * PITCH, 8)
        s0 = pl.multiple_of(k1 * r, r)
        blk = _load_pair(a_ref, pl.ds(r0, r))
        sr, si = _cmatmul(_twiddled(fr_ref[...], fi_ref[...], twr_ref[pl.ds(k1, 1), :], twi_ref[pl.ds(k1, 1), :]), blk)
        kr = ks_ref[0, pl.ds(s0, r), :].astype(F32)
        ki = ks_ref[1, pl.ds(s0, r), :].astype(F32)
        return jnp.concatenate([sr * kr - si * ki, sr * ki + si * kr], axis=0).astype(BF16)

    def b_forward_pair(i):
        return jnp.concatenate([b_forward(2 * i), b_forward(2 * i + 1)], axis=1)

    def b_inverse_pair(i, p):
        out = jnp.dot(fc_ref[...], p, preferred_element_type=F32)
        pair_store(a_ref, pl.ds(pl.multiple_of(2 * i * PITCH, 8), r), pl.ds(pl.multiple_of((2 * i + 1) * PITCH, 8), r), out)

    def b_body(i, p):
        p_next = b_forward_pair(i + 1)
        b_inverse_pair(i, p)
        return p_next

    ks_copy.wait()
    b_inverse_pair(r // 2 - 1, lax.fori_loop(0, r // 2 - 1, b_body, b_forward_pair(0), unroll=DFT_UNROLL // 2))

    inv_n = 1.0 / (r * r)

    def c_body(n2, carry):
        g = _twiddled(fr_ref[0:half, :], fi_ref[0:half, :], twr_ref[pl.ds(n2, 1), :], twi_ref[pl.ds(n2, 1), :],
                      inv_n, -inv_n)
        yr, yi = _cmatmul(g, _load_pair(a_ref, pl.ds(n2, r, stride=PITCH)))
        _store_pair(v_ref, pl.ds(n2, half, stride=PITCH), yr, yi)
        return carry

    lax.fori_loop(0, r, c_body, 0, unroll=DFT_UNROLL_STRIDED)

    def out_body(c, carry):
        r0 = pl.multiple_of(c * CONV_ROWS, CONV_ROWS)
        for bi in range(nb):
            gate = _short_conv_rows(u_ref, bi, r0, CONV_ROWS, seq, uw_ref, ub_ref)
            conv = jnp.concatenate(
                [v_ref[bi, pl.ds(pl.multiple_of((c * blocks + j) * PITCH, 8), r), :] for j in range(blocks)], axis=0)
            zp = (o_ref if conv_in else z_ref)[bi, pl.ds(r0, CONV_ROWS), :].astype(F32)
            o_ref[bi, pl.ds(r0, CONV_ROWS), :] = (gate * (conv + hb_ref[...] * zp)).astype(BF16)
        return carry

    lax.fori_loop(0, nchunk, out_body, 0)


def _hy_conv(z, z_part, u, u_part, order, short_w, short_b, kspec, hy_bias, tables, conv_in):
    nb, seq, _ = u.shape
    width = hy_bias.shape[1]
    tiles = width // LANES
    fa, fc, fr, fi, twr, twi = tables
    whole = lambda j: (0, 0)
    zcol = lambda j: (0, 0, z_part * tiles + j)
    ucol = lambda j: (0, 0, u_part * tiles + j)
    return pl.pallas_call(
        functools.partial(_hy_conv_kernel, conv_in=conv_in, ks_tile0=order * tiles),
        grid=(tiles,),
        in_specs=[
            pl.BlockSpec((nb, seq, LANES), zcol),
            pl.BlockSpec((3, LANES), lambda j: (0, z_part * tiles + j)),
            pl.BlockSpec((1, LANES), lambda j: (0, z_part * tiles + j)),
            pl.BlockSpec((nb, seq, LANES), ucol),
            pl.BlockSpec((3, LANES), lambda j: (0, u_part * tiles + j)),
            pl.BlockSpec((1, LANES), lambda j: (0, u_part * tiles + j)),
            pl.BlockSpec(memory_space=pl.ANY),
            pl.BlockSpec((1, LANES), lambda j: (0, j)),
            _resident(fa.shape, whole), _resident(fc.shape, whole), _resident(fr.shape, whole),
            _resident(fi.shape, whole), _resident(twr.shape, whole), _resident(twi.shape, whole),
        ],
        out_specs=_resident((nb, seq, LANES), lambda j: (0, 0, j)),
        out_shape=jax.ShapeDtypeStruct((nb, seq, width), BF16),
        scratch_shapes=[pltpu.VMEM((2, 2 * seq // RADIX * PITCH, LANES), F32),
                        pltpu.VMEM((nb, seq // RADIX * PITCH, LANES), F32),
                        pltpu.VMEM((2, 2 * seq, LANES), BF16),
                        pltpu.SemaphoreType.DMA(())],
        compiler_params=_cparams(("arbitrary",)),
        name=f"hy_conv{order}",
    )(z, short_w, short_b, u, short_w, short_b, kspec, hy_bias[order:order + 1], fa, fc, fr, fi, twr, twi)


def _hyena(hyu, short_w, short_b, fw1, fb1, fw2, fb2, fw3, fb3, freq, hy_bias):
    nb, seq, _ = hyu.shape
    assert nb == 2 and 2 * seq == RADIX * RADIX
    width = hy_bias.shape[1]
    hid = fw2.shape[0]
    fr, fi, twr, twi = _dft_tables()
    half = RADIX // 2
    fa_real = jnp.concatenate([fr, fi], axis=0).astype(BF16)
    fa_half = _embed(fr[:, :half], fi[:, :half]).astype(BF16)
    fc = _embed(fr, -fi).astype(BF16)

    zz = jnp.asarray(_position_features(seq))
    hdn = _filt_mlp(zz, fw1, fb1, fw2, fb2, freq).reshape(seq, 4 * hid)
    w3d = fw3.reshape(hid, 2, HY_ORDER * width).transpose(1, 0, 2)
    b3d = fb3.reshape(2, 1, HY_ORDER * width)
    max_decay = math.log(HY_DECAY_TARGET) / HY_FAST_DECAY_PCT
    min_decay = math.log(HY_DECAY_TARGET) / HY_SLOW_DECAY_PCT
    delta = np.abs(np.linspace(min_decay, max_decay, width)).astype(np.float32)
    delta2 = jnp.asarray(np.tile(delta, HY_ORDER)[None, :])
    ktime, ss = _filt_time(hdn, w3d, b3d, delta2)
    kspec = _filt_spec(ktime, ss, fa_real, fr, fi, twr, twi)

    tables = (fa_half, fc, fr, fi, twr, twi)
    sb = short_b.reshape(1, -1)
    z1 = _hy_conv(hyu, 0, hyu, 1, 0, short_w, sb, kspec, hy_bias, tables, conv_in=True)
    return _hy_conv(z1, 0, hyu, 2, 1, short_w, sb, kspec, hy_bias, tables, conv_in=False)


def kernel(x, ln1_g, ln1_b, ffn1_w1, ffn1_w3, ffn1_w2, w_in, b_gate, na_rpb, hy_short_w, hy_short_b,
           hy_filt_w1, hy_filt_b1, hy_filt_w2, hy_filt_b2, hy_filt_w3, hy_filt_b3, hy_filt_freq, hy_bias,
           w_pa, w_pb, w_out, ln2_g, ln2_b, ffn2_w1, ffn2_w3, ffn2_w2, ln3_g, ln3_b):
    assert ln1_g.shape[0] == DEPTH
    b, seq, d = x.shape
    rows = seq // GRID_W
    na_width = NA_HEADS * NA_HEAD_DIM
    n_qkv = 3 * na_width
    n_hy = hy_short_w.shape[2]
    vec = lambda a: a[0].reshape(1, -1)

    x1, qkv, hyu, gates = _dense_in(
        x.reshape(b * seq, d), ffn1_w1[0], ffn1_w3[0], ffn1_w2[0], vec(ln1_g), vec(ln1_b),
        w_in[0], vec(b_gate), n_qkv, n_hy)

    bias = _na_bias(na_rpb[0], rows)
    ya = _na_attn(qkv.reshape(b, seq, n_qkv), bias, rows)

    yb = _hyena(hyu.reshape(b, seq, n_hy), hy_short_w[0], hy_short_b[0], hy_filt_w1[0], hy_filt_b1[0],
                hy_filt_w2[0], hy_filt_b2[0], hy_filt_w3[0], hy_filt_b3[0], hy_filt_freq[0], hy_bias[0])

    out = _dense_out(
        x1, ya.reshape(b * seq, na_width), yb.reshape(b * seq, -1), gates,
        w_pa[0], w_pb[0], w_out[0], vec(ln2_g), vec(ln2_b),
        ffn2_w1[0], ffn2_w3[0], ffn2_w2[0], vec(ln3_g), vec(ln3_b))
    return out.reshape(b, seq, d)
```

```python
import functools
import math

import numpy as np
import jax
import jax.numpy as jnp
from jax import lax
from jax.experimental import pallas as pl
from jax.experimental.pallas import tpu as pltpu

F32 = jnp.float32
BF16 = jnp.bfloat16

GRID_W = 64
NA_HEADS = 8
NA_HEAD_DIM = 64
NA_KH = 8
NA_KW = 16
HY_ORDER = 2
HY_EMB_DIM = 33
HY_FAST_DECAY_PCT = 0.3
HY_SLOW_DECAY_PCT = 1.5
HY_DECAY_TARGET = 1e-2
DEPTH = 1
DN_ALPHA = (2 * DEPTH) ** 0.25
LN_EPS = 1e-5
LOG2E = math.log2(math.e)

LANES = 128
VMEM_LIMIT = 60 * 1024 * 1024

ROW_TILE = 512
FF_CHUNK = 256
PROJ_CHUNK = 512
CAST_ROWS = 64
CAST_SLOTS = 8
NA_QROWS = 4
NA_KROWS = NA_QROWS + NA_KH
NA_BIAS_GROUP = 5
RADIX = 128
PITCH = RADIX + 8
DFT_UNROLL = 42
DFT_UNROLL_STRIDED = 64
CONV_ROWS = 512
NEG = -1e30


def _cparams(sem):
    return pltpu.CompilerParams(dimension_semantics=sem, vmem_limit_bytes=VMEM_LIMIT)


def _resident(shape, index_map):
    return pl.BlockSpec(shape, index_map, pipeline_mode=pl.Buffered(1))


def _cast_weights_in(pairs, stage_ref, sem):
    ahead = CAST_SLOTS - 1
    counts = [src.shape[0] // CAST_ROWS for src, _ in pairs]
    assert all(n >= ahead for n in counts)
    bases = [sum(counts[:w]) for w in range(len(pairs))]

    def copy(w, c, g):
        src, cols = pairs[w][0], pairs[w][0].shape[1]
        slot = g % CAST_SLOTS
        return pltpu.make_async_copy(src.at[pl.ds(c * CAST_ROWS, CAST_ROWS), :],
                                     stage_ref.at[slot, :, pl.ds(0, cols)], sem.at[slot])

    for g in range(ahead):
        copy(0, g, g).start()
    for w, (src, dst) in enumerate(pairs):
        n, cols = counts[w], src.shape[1]

        def body(c, carry, w=w, n=n, cols=cols, dst=dst):
            g = bases[w] + c

            @pl.when(c + ahead < n)
            def _():
                copy(w, c + ahead, g + ahead).start()

            if w + 1 < len(pairs):
                @pl.when(c + ahead >= n)
                def _():
                    copy(w + 1, c + ahead - n, g + ahead).start()

            copy(w, c, g).wait()
            r0 = pl.multiple_of(c * CAST_ROWS, CAST_ROWS)
            dst[pl.ds(r0, CAST_ROWS), :] = stage_ref[g % CAST_SLOTS, :, 0:cols].astype(BF16)
            return carry

        lax.fori_loop(0, n, body, 0)


def _layer_norm(r, g, b):
    mu = jnp.mean(r, axis=-1, keepdims=True)
    d = r - mu
    var = jnp.mean(d * d, axis=-1, keepdims=True)
    return d * lax.rsqrt(var + LN_EPS) * g + b


def _swiglu_ln(x, xb_ref, acc_ref, w1_ref, w3_ref, w2_ref, g_ref, b_ref):
    d_ff = w1_ref.shape[1]
    xb_ref[...] = x.astype(BF16)
    for j in range(d_ff // FF_CHUNK):
        sl = slice(j * FF_CHUNK, (j + 1) * FF_CHUNK)
        a = jnp.dot(xb_ref[...], w1_ref[:, sl], preferred_element_type=F32)
        b = jnp.dot(xb_ref[...], w3_ref[:, sl], preferred_element_type=F32)
        h = (a * jax.nn.sigmoid(a) * b).astype(BF16)
        part = jnp.dot(h, w2_ref[sl, :], preferred_element_type=F32)
        if j == 0:
            acc_ref[...] = part
        else:
            acc_ref[...] += part
    return _layer_norm(DN_ALPHA * x + 0.5 * acc_ref[...], g_ref[...], b_ref[...])


def _dense_in_kernel(x_ref, w1_hbm, w3_hbm, w2_hbm, g_ref, b_ref, win_hbm, bg_ref,
                     x1_ref, qkv_ref, hyu_ref, gate_ref,
                     xb_ref, acc_ref, w1_ref, w3_ref, w2_ref, win_ref, stage_ref, sem):
    @pl.when(pl.program_id(0) == 0)
    def _():
        _cast_weights_in(((w1_hbm, w1_ref), (w3_hbm, w3_ref), (w2_hbm, w2_ref), (win_hbm, win_ref)), stage_ref, sem)

    x1 = _swiglu_ln(x_ref[...], xb_ref, acc_ref, w1_ref, w3_ref, w2_ref, g_ref, b_ref)
    x1_ref[...] = x1
    xb_ref[...] = x1.astype(BF16)
    n_qkv = qkv_ref.shape[1]
    n_hy = hyu_ref.shape[1]
    n_head = n_qkv // 3
    for j in range(win_ref.shape[1] // PROJ_CHUNK):
        c0 = j * PROJ_CHUNK
        p = jnp.dot(xb_ref[...], win_ref[:, c0:c0 + PROJ_CHUNK], preferred_element_type=F32)
        if c0 < n_qkv:
            if c0 < n_head:
                p = p * (NA_HEAD_DIM ** -0.5 * LOG2E)
            qkv_ref[:, c0:c0 + PROJ_CHUNK] = p.astype(BF16)
        elif c0 < n_qkv + n_hy:
            hyu_ref[:, c0 - n_qkv:c0 - n_qkv + PROJ_CHUNK] = p.astype(BF16)
        else:
            g0 = c0 - n_qkv - n_hy
            gate_ref[:, g0:g0 + PROJ_CHUNK] = jax.nn.sigmoid(p + bg_ref[:, g0:g0 + PROJ_CHUNK]).astype(BF16)


def _dense_in(x2d, w1, w3, w2, g, b, w_in, b_gate, n_qkv, n_hy):
    n, d = x2d.shape
    d_ff = w1.shape[1]
    n_gate = w_in.shape[1] - n_qkv - n_hy
    assert n % ROW_TILE == 0 and d_ff % FF_CHUNK == 0
    assert n_qkv % (3 * PROJ_CHUNK) == 0 and n_hy % PROJ_CHUNK == 0 and n_gate % PROJ_CHUNK == 0
    weights = (w1, w3, w2, w_in)
    assert all(w.dtype == F32 and w.shape[0] % CAST_ROWS == 0 for w in weights)
    row = lambda i: (i, 0)
    whole = lambda i: (0, 0)
    hbm = pl.BlockSpec(memory_space=pl.ANY)
    return pl.pallas_call(
        _dense_in_kernel,
        grid=(n // ROW_TILE,),
        in_specs=[
            pl.BlockSpec((ROW_TILE, d), row),
            hbm, hbm, hbm,
            _resident((1, d), whole), _resident((1, d), whole),
            hbm, _resident((1, n_gate), whole),
        ],
        out_specs=[
            pl.BlockSpec((ROW_TILE, d), row),
            pl.BlockSpec((ROW_TILE, n_qkv), row),
            pl.BlockSpec((ROW_TILE, n_hy), row),
            pl.BlockSpec((ROW_TILE, n_gate), row),
        ],
        out_shape=[
            jax.ShapeDtypeStruct((n, d), F32),
            jax.ShapeDtypeStruct((n, n_qkv), BF16),
            jax.ShapeDtypeStruct((n, n_hy), BF16),
            jax.ShapeDtypeStruct((n, n_gate), BF16),
        ],
        scratch_shapes=[pltpu.VMEM((ROW_TILE, d), BF16), pltpu.VMEM((ROW_TILE, d), F32)]
        + [pltpu.VMEM(w.shape, BF16) for w in weights]
        + [pltpu.VMEM((CAST_SLOTS, CAST_ROWS, max(w.shape[1] for w in weights)), F32),
           pltpu.SemaphoreType.DMA((CAST_SLOTS,))],
        compiler_params=_cparams(("arbitrary",)),
        name="dense_in",
    )(x2d, w1, w3, w2, g, b, w_in, b_gate)


def _dense_out_kernel(x1_ref, ya_ref, yb_ref, gate_ref, wpa_hbm, wpb_hbm, wout_hbm, g2_ref, b2_ref,
                      w1_hbm, w3_hbm, w2_hbm, g3_ref, b3_ref, o_ref,
                      xb_ref, acc_ref, wpa_ref, wpb_ref, wout_ref, w1_ref, w3_ref, w2_ref, stage_ref, sem):
    @pl.when(pl.program_id(0) == 0)
    def _():
        _cast_weights_in(((wpa_hbm, wpa_ref), (wpb_hbm, wpb_ref), (wout_hbm, wout_ref),
                          (w1_hbm, w1_ref), (w3_hbm, w3_ref), (w2_hbm, w2_ref)), stage_ref, sem)

    d = x1_ref.shape[1]
    pa = jnp.dot(ya_ref[...], wpa_ref[...], preferred_element_type=F32)
    pb = jnp.dot(yb_ref[...], wpb_ref[...], preferred_element_type=F32)
    m = gate_ref[:, :d].astype(F32) * pa + gate_ref[:, d:].astype(F32) * pb
    mix = jnp.dot(m.astype(BF16), wout_ref[...], preferred_element_type=F32)
    x2 = _layer_norm(DN_ALPHA * x1_ref[...] + mix, g2_ref[...], b2_ref[...])
    o_ref[...] = _swiglu_ln(x2, xb_ref, acc_ref, w1_ref, w3_ref, w2_ref, g3_ref, b3_ref)


def _dense_out(x1, ya, yb, gates, wpa, wpb, wout, g2, b2, w1, w3, w2, g3, b3):
    n, d = x1.shape
    weights = (wpa, wpb, wout, w1, w3, w2)
    assert all(w.dtype == F32 and w.shape[0] % CAST_ROWS == 0 for w in weights)
    row = lambda i: (i, 0)
    whole = lambda i: (0, 0)
    hbm = pl.BlockSpec(memory_space=pl.ANY)
    return pl.pallas_call(
        _dense_out_kernel,
        grid=(n // ROW_TILE,),
        in_specs=[
            pl.BlockSpec((ROW_TILE, d), row),
            pl.BlockSpec((ROW_TILE, ya.shape[1]), row),
            pl.BlockSpec((ROW_TILE, yb.shape[1]), row),
            pl.BlockSpec((ROW_TILE, gates.shape[1]), row),
            hbm, hbm, hbm,
            _resident((1, d), whole), _resident((1, d), whole),
            hbm, hbm, hbm,
            _resident((1, d), whole), _resident((1, d), whole),
        ],
        out_specs=pl.BlockSpec((ROW_TILE, d), row),
        out_shape=jax.ShapeDtypeStruct((n, d), F32),
        scratch_shapes=[pltpu.VMEM((ROW_TILE, d), BF16), pltpu.VMEM((ROW_TILE, d), F32)]
        + [pltpu.VMEM(w.shape, BF16) for w in weights]
        + [pltpu.VMEM((CAST_SLOTS, CAST_ROWS, max(w.shape[1] for w in weights)), F32),
           pltpu.SemaphoreType.DMA((CAST_SLOTS,))],
        compiler_params=_cparams(("arbitrary",)),
        name="dense_out",
    )(x1, ya, yb, gates, wpa, wpb, wout, g2, b2, w1, w3, w2, g3, b3)


def _window_start(n, k):
    return np.clip(np.arange(n) - k // 2, 0, n - k)


def _na_group_types(rows):
    last = rows - NA_QROWS
    return ((0, 0), (NA_QROWS * 2, NA_QROWS), (last, rows - NA_KROWS))


def _na_tables(rows):
    w = GRID_W
    shift = np.zeros((2 * NA_KW - 1, w, 2 * w), np.float32)
    q = np.arange(w)[:, None]
    kc = np.arange(w)[None, :]
    for dc in range(2 * NA_KW - 1):
        shift[dc, :, :w] = (kc - q + NA_KW - 1 == dc).astype(np.float32)
    cs = _window_start(w, NA_KW)
    col_ok = (kc >= cs[:, None]) & (kc < cs[:, None] + NA_KW)
    rs_all = _window_start(rows, NA_KH)
    negmask = np.full((3, NA_QROWS, w, NA_KROWS, w), NEG, np.float32)
    for t, (r0, kb) in enumerate(_na_group_types(rows)):
        for i in range(NA_QROWS):
            rs = rs_all[r0 + i]
            for j in range(NA_KROWS):
                if rs <= kb + j < rs + NA_KH:
                    negmask[t, i, :, j, :] = np.where(col_ok, 0.0, NEG)
    return shift, negmask.reshape(3, NA_QROWS * w, NA_KROWS * w)


def _na_bias_kernel(rpb_ref, shift_ref, neg_ref, o_ref, u_ref, *, rows):
    h = pl.program_id(0)
    n_dr = 2 * NA_KH - 1
    n_dc = 2 * NA_KW - 1
    w = GRID_W

    for dr0 in range(0, n_dr, NA_BIAS_GROUP):
        base = (h * n_dr + dr0) * n_dc
        accs = [None] * NA_BIAS_GROUP
        for dc in range(n_dc):
            mask = shift_ref[dc]
            for j in range(NA_BIAS_GROUP):
                term = rpb_ref[base + j * n_dc + dc] * mask
                accs[j] = term if dc == 0 else accs[j] + term
        for j in range(NA_BIAS_GROUP):
            left = accs[j] * LOG2E
            u_ref[0, dr0 + j] = left
            u_ref[1, dr0 + j] = pltpu.roll(left, w, 1)
    rs_all = _window_start(rows, NA_KH)
    for t, (r0, kb) in enumerate(_na_group_types(rows)):
        for i in range(NA_QROWS):
            rs = rs_all[r0 + i]
            for jj in range(NA_KROWS // 2):
                val = neg_ref[t, i * w:(i + 1) * w, jj * 2 * w:(jj + 1) * 2 * w]
                for s in range(2):
                    kr = kb + 2 * jj + s
                    if rs <= kr < rs + NA_KH:
                        val = val + u_ref[s, kr - (r0 + i) + NA_KH - 1]
                o_ref[t, 0, i * w:(i + 1) * w, jj * 2 * w:(jj + 1) * 2 * w] = val


def _na_bias(rpb, rows):
    shift, negmask = _na_tables(rows)
    nq, nk = NA_QROWS * GRID_W, NA_KROWS * GRID_W
    return pl.pallas_call(
        functools.partial(_na_bias_kernel, rows=rows),
        grid=(NA_HEADS,),
        in_specs=[
            pl.BlockSpec(memory_space=pltpu.SMEM),
            _resident(shift.shape, lambda h: (0, 0, 0)),
            _resident(negmask.shape, lambda h: (0, 0, 0)),
        ],
        out_specs=pl.BlockSpec((3, 1, nq, nk), lambda h: (0, h, 0, 0)),
        out_shape=jax.ShapeDtypeStruct((3, NA_HEADS, nq, nk), F32),
        scratch_shapes=[pltpu.VMEM((2, 2 * NA_KH - 1, GRID_W, 2 * GRID_W), F32)],
        compiler_params=_cparams(("arbitrary",)),
        name="na_bias",
    )(rpb.reshape(-1), jnp.asarray(shift), jnp.asarray(negmask))


def _na_attn_kernel(q_ref, k_ref, v_ref, bias_ref, o_ref, s_ref, e_ref, *, rows):
    g = pl.program_id(1)
    kb = jnp.clip(g * NA_QROWS - NA_KH // 2, 0, rows - NA_KROWS)
    k0 = pl.multiple_of(kb * GRID_W, GRID_W)
    nk = NA_KROWS * GRID_W
    dh = NA_HEAD_DIM
    lane = lax.broadcasted_iota(jnp.int32, (1, 2 * dh), 1)
    own = (lane < dh, lane >= dh)
    keep = [m.astype(BF16) for m in own]
    pair_cols = [slice(hp * 2 * dh, (hp + 1) * 2 * dh) for hp in range(NA_HEADS // 2)]
    for hp, cols in enumerate(pair_cols):
        q2 = q_ref[0, :, cols]
        k2 = k_ref[0, pl.ds(k0, nk), cols]
        for s in range(2):
            s_ref[2 * hp + s] = lax.dot_general(q2, k2 * keep[s], (((1,), (1,)), ((), ())), preferred_element_type=F32)
    for h in range(NA_HEADS):
        sc = s_ref[h] + bias_ref[0, h]
        e_ref[h] = jnp.exp2(sc - jnp.max(sc, axis=-1, keepdims=True)).astype(BF16)
    for hp, cols in enumerate(pair_cols):
        v2 = v_ref[0, pl.ds(k0, nk), cols]
        pair = []
        for s in range(2):
            r = jnp.dot(e_ref[2 * hp + s], v2 * keep[s] + keep[1 - s], preferred_element_type=F32)
            den = r[:, dh:dh + 1] if s == 0 else r[:, 0:1]
            pair.append(jnp.where(own[s], r / den, 0.0))
        o_ref[0, :, cols] = (pair[0] + pair[1]).astype(BF16)


def _na_attn(qkv, bias, rows):
    b, seq, three_w = qkv.shape
    width = three_w // 3
    nq, nk = NA_QROWS * GRID_W, NA_KROWS * GRID_W
    groups = rows // NA_QROWS

    def bias_idx(bi, g):
        return (jnp.where(g == 0, 0, jnp.where(g == groups - 1, 2, 1)), 0, 0, 0)

    return pl.pallas_call(
        functools.partial(_na_attn_kernel, rows=rows),
        grid=(b, groups),
        in_specs=[
            pl.BlockSpec((1, nq, width), lambda bi, g: (bi, g, 0)),
            pl.BlockSpec((1, seq, width), lambda bi, g: (bi, 0, 1)),
            pl.BlockSpec((1, seq, width), lambda bi, g: (bi, 0, 2)),
            pl.BlockSpec((1, NA_HEADS, nq, nk), bias_idx),
        ],
        out_specs=pl.BlockSpec((1, nq, width), lambda bi, g: (bi, g, 0)),
        out_shape=jax.ShapeDtypeStruct((b, seq, width), BF16),
        scratch_shapes=[pltpu.VMEM((NA_HEADS, nq, nk), F32), pltpu.VMEM((NA_HEADS, nq, nk), BF16)],
        compiler_params=_cparams(("arbitrary", "arbitrary")),
        name="na_attn",
    )(qkv, qkv, qkv, bias)


def _dft_tables():
    k = np.arange(RADIX)
    ang = 2.0 * np.pi * np.outer(k, k) / RADIX
    fr, fi = np.cos(ang), -np.sin(ang)
    ang_t = 2.0 * np.pi * np.outer(k, k) / (RADIX * RADIX)
    twr, twi = np.cos(ang_t), -np.sin(ang_t)
    f32 = lambda a: jnp.asarray(a, F32)
    return f32(fr), f32(fi), f32(twr), f32(twi)


def _embed(cr, ci):
    return jnp.concatenate([jnp.concatenate([cr, -ci], axis=1), jnp.concatenate([ci, cr], axis=1)], axis=0)


def _cmatmul(c_rows, x_lanes):
    m = c_rows.shape[0] // 2
    r = jnp.dot(c_rows, x_lanes, preferred_element_type=F32)
    return r[:m, :LANES] - r[m:, LANES:], r[:m, LANES:] + r[m:, :LANES]


def _load_pair(ref, rows):
    return jnp.concatenate([ref[0, rows, :], ref[1, rows, :]], axis=1).astype(BF16)


def _store_pair(ref, rows, re, im):
    ref[0, rows, :] = re
    ref[1, rows, :] = im


def _twiddled(fr, fi, twr, twi, scale_re=1.0, scale_im=1.0):
    return jnp.concatenate([(fr * twr - fi * twi) * scale_re, (fr * twi + fi * twr) * scale_im], axis=0).astype(BF16)


def _position_features(seq):
    t = np.linspace(0.0, 1.0, seq)[:, None]
    bands = (HY_EMB_DIM - 1) // 2
    w = (2.0 * math.pi / seq) * np.arange(seq)[:, None]
    f = np.linspace(1e-4, bands - 1, bands)[None, :]
    z = np.concatenate([t, np.cos(f * w), -np.sin(f * w)], axis=-1)
    zz = np.zeros((seq // 2, 2 * LANES), np.float32)
    zz[:, :HY_EMB_DIM] = z[:seq // 2]
    zz[:, LANES:LANES + HY_EMB_DIM] = z[seq // 2:]
    return zz


def _filt_mlp_kernel(z_ref, w1_ref, b1_ref, w2_ref, b2_ref, fr_ref, o_ref):
    hp = lax.Precision.HIGHEST
    fr = fr_ref[...]
    hid = LANES // 2
    h = jnp.sin(fr * (jnp.dot(z_ref[...], w1_ref[...], precision=hp, preferred_element_type=F32) + b1_ref[...]))
    h = jnp.sin(fr * (jnp.dot(h, w2_ref[...], precision=hp, preferred_element_type=F32) + b2_ref[...]))
    hi = h.astype(BF16).astype(F32)
    lo = h - hi
    hi_sw, lo_sw = pltpu.roll(hi, hid, 1), pltpu.roll(lo, hid, 1)
    low = lax.broadcasted_iota(jnp.int32, (1, LANES), 1) < hid
    o_ref[0, :, :LANES] = jnp.where(low, hi, lo_sw).astype(BF16)
    o_ref[0, :, LANES:] = jnp.where(low, hi, 0.0).astype(BF16)
    o_ref[1, :, :LANES] = jnp.where(low, hi_sw, lo).astype(BF16)
    o_ref[1, :, LANES:] = jnp.where(low, hi_sw, 0.0).astype(BF16)


def _filt_mlp(zz, fw1, fb1, fw2, fb2, freq):
    seq = zz.shape[0]
    hid = fw2.shape[0]
    assert 2 * hid == LANES
    w1 = jnp.zeros((2 * LANES, LANES), F32)
    w1 = w1.at[:HY_EMB_DIM, :hid].set(fw1).at[LANES:LANES + HY_EMB_DIM, hid:].set(fw1)
    w2 = jnp.zeros((LANES, LANES), F32).at[:hid, :hid].set(fw2).at[hid:, hid:].set(fw2)
    twice = lambda v: jnp.concatenate([v, v]).reshape(1, LANES)
    rows = 2048
    whole = lambda i: (0, 0)
    return pl.pallas_call(
        _filt_mlp_kernel,
        grid=(seq // rows,),
        in_specs=[pl.BlockSpec((rows, 2 * LANES), lambda i: (i, 0)),
                  pl.BlockSpec(w1.shape, whole), pl.BlockSpec((1, LANES), whole),
                  pl.BlockSpec(w2.shape, whole), pl.BlockSpec((1, LANES), whole), pl.BlockSpec((1, LANES), whole)],
        out_specs=pl.BlockSpec((2, rows, 2 * LANES), lambda i: (0, i, 0)),
        out_shape=jax.ShapeDtypeStruct((2, seq, 2 * LANES), BF16),
        compiler_params=_cparams(("arbitrary",)),
        name="filt_mlp",
    )(zz, w1, twice(fb1), w2, twice(fb2), twice(freq))


def _filt_time_kernel(ha_ref, hb_ref, perm_ref, w3_ref, b3_ref, delta_ref, k_ref, ss_ref, hsel_ref, *, seq):
    i = pl.program_id(0)
    rows = ha_ref.shape[0]
    half = seq // rows

    @pl.when(i < half)
    def _():
        hsel_ref[...] = ha_ref[...]

    @pl.when(i >= half)
    def _():
        for s in range(rows // RADIX):
            lo_row = rows - RADIX * (s + 1)
            nxt = hb_ref[0:RADIX, :] if s == 0 else ha_ref[lo_row + RADIX:lo_row + 2 * RADIX, :]
            src = jnp.concatenate([ha_ref[lo_row:lo_row + RADIX, :], nxt], axis=0)
            hsel_ref[s * RADIX:(s + 1) * RADIX, :] = jnp.dot(perm_ref[...], src, preferred_element_type=F32).astype(BF16)

    w = w3_ref[0]
    w_hi = w.astype(BF16)
    w_lo = (w - w_hi.astype(F32)).astype(BF16)
    wcat = jnp.concatenate([w_hi, w_hi, w_lo, jnp.zeros_like(w_lo)], axis=0)
    hf = jnp.dot(hsel_ref[...], wcat, preferred_element_type=F32) + b3_ref[0]
    m = i * rows + lax.broadcasted_iota(jnp.int32, (rows, 1), 0)
    pos = jnp.where(m < seq, m, 2 * seq - m).astype(F32)
    t = pos / (seq - 1)
    k = jnp.where(m == seq, 0.0, hf * jnp.exp(-t * delta_ref[...]))
    for j in range(rows // RADIX):
        k_ref[j * PITCH:j * PITCH + RADIX, :] = k[j * RADIX:(j + 1) * RADIX]
        k_ref[j * PITCH + RADIX:(j + 1) * PITCH, :] = jnp.zeros((PITCH - RADIX, k.shape[1]), F32)
    part = jnp.sum(k * k, axis=0, keepdims=True)

    @pl.when(i == 0)
    def _():
        ss_ref[...] = part

    @pl.when(i != 0)
    def _():
        ss_ref[...] += part


def _filt_time(hdn, w3d, b3d, delta2):
    seq, packed = hdn.shape
    _, hid, cols = w3d.shape
    assert packed == 4 * hid
    n2l = 2 * seq
    rows = 1024
    half = seq // rows
    prows = rows // RADIX * PITCH
    perm = np.zeros((RADIX, 2 * RADIX), np.float32)
    perm[0, RADIX] = 1.0
    perm[np.arange(1, RADIX), RADIX - np.arange(1, RADIX)] = 1.0
    return pl.pallas_call(
        functools.partial(_filt_time_kernel, seq=seq),
        grid=(n2l // rows,),
        in_specs=[pl.BlockSpec((rows, packed), lambda i: (jnp.where(i < half, i, 2 * half - 1 - i), 0)),
                  pl.BlockSpec((rows, packed), lambda i: (jnp.clip(2 * half - i, 0, half - 1), 0)),
                  pl.BlockSpec(perm.shape, lambda i: (0, 0)),
                  pl.BlockSpec((1, hid, cols), lambda i: (i // half, 0, 0)),
                  pl.BlockSpec((1, 1, cols), lambda i: (i // half, 0, 0)),
                  pl.BlockSpec((1, cols), lambda i: (0, 0))],
        out_specs=[pl.BlockSpec((prows, cols), lambda i: (i, 0)),
                   pl.BlockSpec((1, cols), lambda i: (0, 0))],
        out_shape=[jax.ShapeDtypeStruct((n2l // RADIX * PITCH, cols), F32), jax.ShapeDtypeStruct((1, cols), F32)],
        scratch_shapes=[pltpu.VMEM((rows, packed), BF16)],
        compiler_params=_cparams(("arbitrary",)),
        name="filt_time",
    )(hdn, hdn, jnp.asarray(perm, BF16), w3d, b3d, delta2)


def _filt_spec_kernel(k_ref, ss_ref, fa_ref, fr_ref, fi_ref, twr_ref, twi_ref, o_ref, a_ref):
    r = RADIX
    scale = lax.rsqrt(ss_ref[...] + 1e-12)

    def a_body(i, c):
        n2 = 2 * i
        rhs = jnp.concatenate([k_ref[pl.ds(n2, r, stride=PITCH), :], k_ref[pl.ds(n2 + 1, r, stride=PITCH), :]],
                              axis=1).astype(BF16)
        out = jnp.dot(fa_ref[...], rhs, preferred_element_type=F32)
        _store_pair(a_ref, pl.ds(n2, r, stride=PITCH), out[:r, :LANES], out[r:, :LANES])
        _store_pair(a_ref, pl.ds(n2 + 1, r, stride=PITCH), out[:r, LANES:], out[r:, LANES:])
        return c

    lax.fori_loop(0, r // 2, a_body, 0, unroll=DFT_UNROLL_STRIDED // 2)

    def b_body(k1, c):
        r0 = pl.multiple_of(k1 * PITCH, 8)
        o0 = pl.multiple_of(k1 * r, r)
        blk = _load_pair(a_ref, pl.ds(r0, r))
        sr, si = _cmatmul(_twiddled(fr_ref[...], fi_ref[...], twr_ref[pl.ds(k1, 1), :], twi_ref[pl.ds(k1, 1), :]), blk)
        o_ref[0, pl.ds(o0, r), :] = (sr * scale).astype(BF16)
        o_ref[1, pl.ds(o0, r), :] = (si * scale).astype(BF16)
        return c

    lax.fori_loop(0, r, b_body, 0, unroll=DFT_UNROLL_STRIDED)


def _filt_spec(ktime, ss, fa_real, fr, fi, twr, twi):
    prows, cols = ktime.shape
    n2l = prows // PITCH * RADIX
    whole = lambda j: (0, 0)
    return pl.pallas_call(
        _filt_spec_kernel,
        grid=(cols // LANES,),
        in_specs=[pl.BlockSpec((prows, LANES), lambda j: (0, j)),
                  pl.BlockSpec((1, LANES), lambda j: (0, j)),
                  _resident(fa_real.shape, whole), _resident(fr.shape, whole), _resident(fi.shape, whole),
                  _resident(twr.shape, whole), _resident(twi.shape, whole)],
        out_specs=pl.BlockSpec((2, n2l, LANES), lambda j: (0, 0, j)),
        out_shape=jax.ShapeDtypeStruct((2, n2l, cols), BF16),
        scratch_shapes=[pltpu.VMEM((2, prows, LANES), F32)],
        compiler_params=_cparams(("arbitrary",)),
        name="filt_spec",
    )(ktime, ss, fa_real, fr, fi, twr, twi)


def _short_conv_rows(u_ref, bi, r0, nrows, seq, w_ref, b_ref):
    edge = 16
    cur = u_ref[bi, pl.ds(r0, nrows), :].astype(F32)
    before = u_ref[bi, pl.ds(pl.multiple_of(jnp.maximum(r0 - edge, 0), edge), edge), :].astype(F32)
    after = u_ref[bi, pl.ds(pl.multiple_of(jnp.minimum(r0 + nrows, seq - edge), edge), edge), :].astype(F32)
    prev_edge = jnp.where(r0 > 0, before[edge - 1:edge, :], 0.0)
    next_edge = jnp.where(r0 + nrows < seq, after[0:1, :], 0.0)
    w0, w1, w2 = w_ref[0:1, :], w_ref[1:2, :], w_ref[2:3, :]
    out = b_ref[...] + w0 * pltpu.roll(cur, 1, 0) + w1 * cur + w2 * pltpu.roll(cur, nrows - 1, 0)
    row = lax.broadcasted_iota(jnp.int32, (8, cur.shape[1]), 0)
    first = out[0:8] + jnp.where(row == 0, w0 * (prev_edge - cur[nrows - 1:nrows]), 0.0)
    last = out[nrows - 8:] + jnp.where(row == 7, w2 * (next_edge - cur[0:1]), 0.0)
    return jnp.concatenate([first, out[8:nrows - 8], last], axis=0)


def _hy_conv_kernel(z_ref, zw_ref, zb_ref, u_ref, uw_ref, ub_ref, ks_hbm, hb_ref,
                    fa_ref, fc_ref, fr_ref, fi_ref, twr_ref, twi_ref, o_ref, a_ref, v_ref, ks_ref, ks_sem,
                    *, conv_in, ks_tile0):
    r = RADIX
    half = r // 2
    nb, seq, _ = z_ref.shape
    nchunk = seq // CONV_ROWS
    blocks = CONV_ROWS // r

    ks_col = pl.multiple_of((ks_tile0 + pl.program_id(0)) * LANES, LANES)
    ks_copy = pltpu.make_async_copy(ks_hbm.at[:, :, pl.ds(ks_col, LANES)], ks_ref, ks_sem)
    ks_copy.start()

    def load_body(c, carry):
        r0 = pl.multiple_of(c * CONV_ROWS, CONV_ROWS)
        for bi in range(nb):
            if conv_in:
                v = _short_conv_rows(z_ref, bi, r0, CONV_ROWS, seq, zw_ref, zb_ref)
                o_ref[bi, pl.ds(r0, CONV_ROWS), :] = v.astype(BF16)
            else:
                v = z_ref[bi, pl.ds(r0, CONV_ROWS), :].astype(F32)
            for j in range(blocks):
                p0 = pl.multiple_of((c * blocks + j) * PITCH, 8)
                v_ref[bi, pl.ds(p0, r), :] = v[j * r:(j + 1) * r]
        return carry

    lax.fori_loop(0, nchunk, load_body, 0)

    def pair_store(ref, rows0, rows1, out):
        _store_pair(ref, rows0, out[:r, :LANES], out[r:, :LANES])
        _store_pair(ref, rows1, out[:r, LANES:], out[r:, LANES:])

    def a_body(i, carry):
        n2 = 2 * i

        def column(n):
            return jnp.concatenate([v_ref[0, pl.ds(n, half, stride=PITCH), :],
                                    v_ref[1, pl.ds(n, half, stride=PITCH), :]], axis=0)

        rhs = jnp.concatenate([column(n2), column(n2 + 1)], axis=1).astype(BF16)
        out = jnp.dot(fa_ref[...], rhs, preferred_element_type=F32)
        pair_store(a_ref, pl.ds(n2, r, stride=PITCH), pl.ds(n2 + 1, r, stride=PITCH), out)
        return carry

    lax.fori_loop(0, r // 2, a_body, 0, unroll=DFT_UNROLL_STRIDED // 2)

    def b_forward(k1):
        r0 = pl.multiple_of(k1 * PITCH, 8)
        s0 = pl.multiple_of(k1 * r, r)
        blk = _load_pair(a_ref, pl.ds(r0, r))
        sr, si = _cmatmul(_twiddled(fr_ref[...], fi_ref[...], twr_ref[pl.ds(k1, 1), :], twi_ref[pl.ds(k1, 1), :]), blk)
        kr = ks_ref[0, pl.ds(s0, r), :].astype(F32)
        ki = ks_ref[1, pl.ds(s0, r), :].astype(F32)
        return jnp.concatenate([sr * kr - si * ki, sr * ki + si * kr], axis=0).astype(BF16)

    def b_forward_pair(i):
        return jnp.concatenate([b_forward(2 * i), b_forward(2 * i + 1)], axis=1)

    def b_inverse_pair(i, p):
        out = jnp.dot(fc_ref[...], p, preferred_element_type=F32)
        pair_store(a_ref, pl.ds(pl.multiple_of(2 * i * PITCH, 8), r), pl.ds(pl.multiple_of((2 * i + 1) * PITCH, 8), r), out)

    def b_body(i, p):
        p_next = b_forward_pair(i + 1)
        b_inverse_pair(i, p)
        return p_next

    ks_copy.wait()
    b_inverse_pair(r // 2 - 1, lax.fori_loop(0, r // 2 - 1, b_body, b_forward_pair(0), unroll=DFT_UNROLL // 2))

    inv_n = 1.0 / (r * r)

    def c_body(n2, carry):
        g = _twiddled(fr_ref[0:half, :], fi_ref[0:half, :], twr_ref[pl.ds(n2, 1), :], twi_ref[pl.ds(n2, 1), :],
                      inv_n, -inv_n)
        yr, yi = _cmatmul(g, _load_pair(a_ref, pl.ds(n2, r, stride=PITCH)))
        _store_pair(v_ref, pl.ds(n2, half, stride=PITCH), yr, yi)
        return carry

    lax.fori_loop(0, r, c_body, 0, unroll=DFT_UNROLL_STRIDED)

    def out_body(c, carry):
        r0 = pl.multiple_of(c * CONV_ROWS, CONV_ROWS)
        for bi in range(nb):
            gate = _short_conv_rows(u_ref, bi, r0, CONV_ROWS, seq, uw_ref, ub_ref)
            conv = jnp.concatenate(
                [v_ref[bi, pl.ds(pl.multiple_of((c * blocks + j) * PITCH, 8), r), :] for j in range(blocks)], axis=0)
            zp = (o_ref if conv_in else z_ref)[bi, pl.ds(r0, CONV_ROWS), :].astype(F32)
            o_ref[bi, pl.ds(r0, CONV_ROWS), :] = (gate * (conv + hb_ref[...] * zp)).astype(BF16)
        return carry

    lax.fori_loop(0, nchunk, out_body, 0)


def _hy_conv(z, z_part, u, u_part, order, short_w, short_b, kspec, hy_bias, tables, conv_in):
    nb, seq, _ = u.shape
    width = hy_bias.shape[1]
    tiles = width // LANES
    fa, fc, fr, fi, twr, twi = tables
    whole = lambda j: (0, 0)
    zcol = lambda j: (0, 0, z_part * tiles + j)
    ucol = lambda j: (0, 0, u_part * tiles + j)
    return pl.pallas_call(
        functools.partial(_hy_conv_kernel, conv_in=conv_in, ks_tile0=order * tiles),
        grid=(tiles,),
        in_specs=[
            pl.BlockSpec((nb, seq, LANES), zcol),
            pl.BlockSpec((3, LANES), lambda j: (0, z_part * tiles + j)),
            pl.BlockSpec((1, LANES), lambda j: (0, z_part * tiles + j)),
            pl.BlockSpec((nb, seq, LANES), ucol),
            pl.BlockSpec((3, LANES), lambda j: (0, u_part * tiles + j)),
            pl.BlockSpec((1, LANES), lambda j: (0, u_part * tiles + j)),
            pl.BlockSpec(memory_space=pl.ANY),
            pl.BlockSpec((1, LANES), lambda j: (0, j)),
            _resident(fa.shape, whole), _resident(fc.shape, whole), _resident(fr.shape, whole),
            _resident(fi.shape, whole), _resident(twr.shape, whole), _resident(twi.shape, whole),
        ],
        out_specs=_resident((nb, seq, LANES), lambda j: (0, 0, j)),
        out_shape=jax.ShapeDtypeStruct((nb, seq, width), BF16),
        scratch_shapes=[pltpu.VMEM((2, 2 * seq // RADIX * PITCH, LANES), F32),
                        pltpu.VMEM((nb, seq // RADIX * PITCH, LANES), F32),
                        pltpu.VMEM((2, 2 * seq, LANES), BF16),
                        pltpu.SemaphoreType.DMA(())],
        compiler_params=_cparams(("arbitrary",)),
        name=f"hy_conv{order}",
    )(z, short_w, short_b, u, short_w, short_b, kspec, hy_bias[order:order + 1], fa, fc, fr, fi, twr, twi)


def _hyena(hyu, short_w, short_b, fw1, fb1, fw2, fb2, fw3, fb3, freq, hy_bias):
    nb, seq, _ = hyu.shape
    assert nb == 2 and 2 * seq == RADIX * RADIX
    width = hy_bias.shape[1]
    hid = fw2.shape[0]
    fr, fi, twr, twi = _dft_tables()
    half = RADIX // 2
    fa_real = jnp.concatenate([fr, fi], axis=0).astype(BF16)
    fa_half = _embed(fr[:, :half], fi[:, :half]).astype(BF16)
    fc = _embed(fr, -fi).astype(BF16)

    zz = jnp.asarray(_position_features(seq))
    hdn = _filt_mlp(zz, fw1, fb1, fw2, fb2, freq).reshape(seq, 4 * hid)
    w3d = fw3.reshape(hid, 2, HY_ORDER * width).transpose(1, 0, 2)
    b3d = fb3.reshape(2, 1, HY_ORDER * width)
    max_decay = math.log(HY_DECAY_TARGET) / HY_FAST_DECAY_PCT
    min_decay = math.log(HY_DECAY_TARGET) / HY_SLOW_DECAY_PCT
    delta = np.abs(np.linspace(min_decay, max_decay, width)).astype(np.float32)
    delta2 = jnp.asarray(np.tile(delta, HY_ORDER)[None, :])
    ktime, ss = _filt_time(hdn, w3d, b3d, delta2)
    kspec = _filt_spec(ktime, ss, fa_real, fr, fi, twr, twi)

    tables = (fa_half, fc, fr, fi, twr, twi)
    sb = short_b.reshape(1, -1)
    z1 = _hy_conv(hyu, 0, hyu, 1, 0, short_w, sb, kspec, hy_bias, tables, conv_in=True)
    return _hy_conv(z1, 0, hyu, 2, 1, short_w, sb, kspec, hy_bias, tables, conv_in=False)


def kernel(x, ln1_g, ln1_b, ffn1_w1, ffn1_w3, ffn1_w2, w_in, b_gate, na_rpb, hy_short_w, hy_short_b,
           hy_filt_w1, hy_filt_b1, hy_filt_w2, hy_filt_b2, hy_filt_w3, hy_filt_b3, hy_filt_freq, hy_bias,
           w_pa, w_pb, w_out, ln2_g, ln2_b, ffn2_w1, ffn2_w3, ffn2_w2, ln3_g, ln3_b):
    assert ln1_g.shape[0] == DEPTH
    b, seq, d = x.shape
    rows = seq // GRID_W
    na_width = NA_HEADS * NA_HEAD_DIM
    n_qkv = 3 * na_width
    n_hy = hy_short_w.shape[2]
    vec = lambda a: a[0].reshape(1, -1)

    x1, qkv, hyu, gates = _dense_in(
        x.reshape(b * seq, d), ffn1_w1[0], ffn1_w3[0], ffn1_w2[0], vec(ln1_g), vec(ln1_b),
        w_in[0], vec(b_gate), n_qkv, n_hy)

    bias = _na_bias(na_rpb[0], rows)
    ya = _na_attn(qkv.reshape(b, seq, n_qkv), bias, rows)

    yb = _hyena(hyu.reshape(b, seq, n_hy), hy_short_w[0], hy_short_b[0], hy_filt_w1[0], hy_filt_b1[0],
                hy_filt_w2[0], hy_filt_b2[0], hy_filt_w3[0], hy_filt_b3[0], hy_filt_freq[0], hy_bias[0])

    out = _dense_out(
        x1, ya.reshape(b * seq, na_width), yb.reshape(b * seq, -1), gates,
        w_pa[0], w_pb[0], w_out[0], vec(ln2_g), vec(ln2_b),
        ffn2_w1[0], ffn2_w3[0], ffn2_w2[0], vec(ln3_g), vec(ln3_b))
    return out.reshape(b, seq, d)
```

```python
import functools
import math

import numpy as np
import jax
import jax.numpy as jnp
from jax import lax
from jax.experimental import pallas as pl
from jax.experimental.pallas import tpu as pltpu

F32 = jnp.float32
BF16 = jnp.bfloat16

GRID_W = 64
NA_HEADS = 8
NA_HEAD_DIM = 64
NA_KH = 8
NA_KW = 16
HY_ORDER = 2
HY_EMB_DIM = 33
HY_FAST_DECAY_PCT = 0.3
HY_SLOW_DECAY_PCT = 1.5
HY_DECAY_TARGET = 1e-2
DEPTH = 1
DN_ALPHA = (2 * DEPTH) ** 0.25
LN_EPS = 1e-5
LOG2E = math.log2(math.e)

LANES = 128
VMEM_LIMIT = 60 * 1024 * 1024

ROW_TILE = 512
FF_CHUNK = 256
PROJ_CHUNK = 512
CAST_ROWS = 64
CAST_SLOTS = 8
NA_QROWS = 4
NA_KROWS = NA_QROWS + NA_KH
NA_BIAS_GROUP = 5
RADIX = 128
PITCH = RADIX + 8
DFT_UNROLL = 42
DFT_UNROLL_STRIDED = 64
CONV_ROWS = 512
NEG = -1e30


def _cparams(sem):
    return pltpu.CompilerParams(dimension_semantics=sem, vmem_limit_bytes=VMEM_LIMIT)


def _resident(shape, index_map):
    return pl.BlockSpec(shape, index_map, pipeline_mode=pl.Buffered(1))


def _cast_weights_in(pairs, stage_ref, sem):
    ahead = CAST_SLOTS - 1
    counts = [src.shape[0] // CAST_ROWS for src, _ in pairs]
    assert all(n >= ahead for n in counts)
    bases = [sum(counts[:w]) for w in range(len(pairs))]

    def copy(w, c, g):
        src, cols = pairs[w][0], pairs[w][0].shape[1]
        slot = g % CAST_SLOTS
        return pltpu.make_async_copy(src.at[pl.ds(c * CAST_ROWS, CAST_ROWS), :],
                                     stage_ref.at[slot, :, pl.ds(0, cols)], sem.at[slot])

    for g in range(ahead):
        copy(0, g, g).start()
    for w, (src, dst) in enumerate(pairs):
        n, cols = counts[w], src.shape[1]

        def body(c, carry, w=w, n=n, cols=cols, dst=dst):
            g = bases[w] + c

            @pl.when(c + ahead < n)
            def _():
                copy(w, c + ahead, g + ahead).start()

            if w + 1 < len(pairs):
                @pl.when(c + ahead >= n)
                def _():
                    copy(w + 1, c + ahead - n, g + ahead).start()

            copy(w, c, g).wait()
            r0 = pl.multiple_of(c * CAST_ROWS, CAST_ROWS)
            dst[pl.ds(r0, CAST_ROWS), :] = stage_ref[g % CAST_SLOTS, :, 0:cols].astype(BF16)
            return carry

        lax.fori_loop(0, n, body, 0)


def _layer_norm(r, g, b):
    mu = jnp.mean(r, axis=-1, keepdims=True)
    d = r - mu
    var = jnp.mean(d * d, axis=-1, keepdims=True)
    return d * lax.rsqrt(var + LN_EPS) * g + b


def _swiglu_ln(x, xb_ref, acc_ref, w1_ref, w3_ref, w2_ref, g_ref, b_ref):
    d_ff = w1_ref.shape[1]
    xb_ref[...] = x.astype(BF16)
    for j in range(d_ff // FF_CHUNK):
        sl = slice(j * FF_CHUNK, (j + 1) * FF_CHUNK)
        a = jnp.dot(xb_ref[...], w1_ref[:, sl], preferred_element_type=F32)
        b = jnp.dot(xb_ref[...], w3_ref[:, sl], preferred_element_type=F32)
        h = (a * jax.nn.sigmoid(a) * b).astype(BF16)
        part = jnp.dot(h, w2_ref[sl, :], preferred_element_type=F32)
        if j == 0:
            acc_ref[...] = part
        else:
            acc_ref[...] += part
    return _layer_norm(DN_ALPHA * x + 0.5 * acc_ref[...], g_ref[...], b_ref[...])


def _dense_in_kernel(x_ref, w1_hbm, w3_hbm, w2_hbm, g_ref, b_ref, win_hbm, bg_ref,
                     x1_ref, qkv_ref, hyu_ref, gate_ref,
                     xb_ref, acc_ref, w1_ref, w3_ref, w2_ref, win_ref, stage_ref, sem):
    @pl.when(pl.program_id(0) == 0)
    def _():
        _cast_weights_in(((w1_hbm, w1_ref), (w3_hbm, w3_ref), (w2_hbm, w2_ref), (win_hbm, win_ref)), stage_ref, sem)

    x1 = _swiglu_ln(x_ref[...], xb_ref, acc_ref, w1_ref, w3_ref, w2_ref, g_ref, b_ref)
    x1_ref[...] = x1
    xb_ref[...] = x1.astype(BF16)
    n_qkv = qkv_ref.shape[1]
    n_hy = hyu_ref.shape[1]
    n_head = n_qkv // 3
    for j in range(win_ref.shape[1] // PROJ_CHUNK):
        c0 = j * PROJ_CHUNK
        p = jnp.dot(xb_ref[...], win_ref[:, c0:c0 + PROJ_CHUNK], preferred_element_type=F32)
        if c0 < n_qkv:
            if c0 < n_head:
                p = p * (NA_HEAD_DIM ** -0.5 * LOG2E)
            qkv_ref[:, c0:c0 + PROJ_CHUNK] = p.astype(BF16)
        elif c0 < n_qkv + n_hy:
            hyu_ref[:, c0 - n_qkv:c0 - n_qkv + PROJ_CHUNK] = p.astype(BF16)
        else:
            g0 = c0 - n_qkv - n_hy
            gate_ref[:, g0:g0 + PROJ_CHUNK] = jax.nn.sigmoid(p + bg_ref[:, g0:g0 + PROJ_CHUNK]).astype(BF16)


def _dense_in(x2d, w1, w3, w2, g, b, w_in, b_gate, n_qkv, n_hy):
    n, d = x2d.shape
    d_ff = w1.shape[1]
    n_gate = w_in.shape[1] - n_qkv - n_hy
    assert n % ROW_TILE == 0 and d_ff % FF_CHUNK == 0
    assert n_qkv % (3 * PROJ_CHUNK) == 0 and n_hy % PROJ_CHUNK == 0 and n_gate % PROJ_CHUNK == 0
    weights = (w1, w3, w2, w_in)
    assert all(w.dtype == F32 and w.shape[0] % CAST_ROWS == 0 for w in weights)
    row = lambda i: (i, 0)
    whole = lambda i: (0, 0)
    hbm = pl.BlockSpec(memory_space=pl.ANY)
    return pl.pallas_call(
        _dense_in_kernel,
        grid=(n // ROW_TILE,),
        in_specs=[
            pl.BlockSpec((ROW_TILE, d), row),
            hbm, hbm, hbm,
            _resident((1, d), whole), _resident((1, d), whole),
            hbm, _resident((1, n_gate), whole),
        ],
        out_specs=[
            pl.BlockSpec((ROW_TILE, d), row),
            pl.BlockSpec((ROW_TILE, n_qkv), row),
            pl.BlockSpec((ROW_TILE, n_hy), row),
            pl.BlockSpec((ROW_TILE, n_gate), row),
        ],
        out_shape=[
            jax.ShapeDtypeStruct((n, d), F32),
            jax.ShapeDtypeStruct((n, n_qkv), BF16),
            jax.ShapeDtypeStruct((n, n_hy), BF16),
            jax.ShapeDtypeStruct((n, n_gate), BF16),
        ],
        scratch_shapes=[pltpu.VMEM((ROW_TILE, d), BF16), pltpu.VMEM((ROW_TILE, d), F32)]
        + [pltpu.VMEM(w.shape, BF16) for w in weights]
        + [pltpu.VMEM((CAST_SLOTS, CAST_ROWS, max(w.shape[1] for w in weights)), F32),
           pltpu.SemaphoreType.DMA((CAST_SLOTS,))],
        compiler_params=_cparams(("arbitrary",)),
        name="dense_in",
    )(x2d, w1, w3, w2, g, b, w_in, b_gate)


def _dense_out_kernel(x1_ref, ya_ref, yb_ref, gate_ref, wpa_hbm, wpb_hbm, wout_hbm, g2_ref, b2_ref,
                      w1_hbm, w3_hbm, w2_hbm, g3_ref, b3_ref, o_ref,
                      xb_ref, acc_ref, wpa_ref, wpb_ref, wout_ref, w1_ref, w3_ref, w2_ref, stage_ref, sem):
    @pl.when(pl.program_id(0) == 0)
    def _():
        _cast_weights_in(((wpa_hbm, wpa_ref), (wpb_hbm, wpb_ref), (wout_hbm, wout_ref),
                          (w1_hbm, w1_ref), (w3_hbm, w3_ref), (w2_hbm, w2_ref)), stage_ref, sem)

    d = x1_ref.shape[1]
    pa = jnp.dot(ya_ref[...], wpa_ref[...], preferred_element_type=F32)
    pb = jnp.dot(yb_ref[...], wpb_ref[...], preferred_element_type=F32)
    m = gate_ref[:, :d].astype(F32) * pa + gate_ref[:, d:].astype(F32) * pb
    mix = jnp.dot(m.astype(BF16), wout_ref[...], preferred_element_type=F32)
    x2 = _layer_norm(DN_ALPHA * x1_ref[...] + mix, g2_ref[...], b2_ref[...])
    o_ref[...] = _swiglu_ln(x2, xb_ref, acc_ref, w1_ref, w3_ref, w2_ref, g3_ref, b3_ref)


def _dense_out(x1, ya, yb, gates, wpa, wpb, wout, g2, b2, w1, w3, w2, g3, b3):
    n, d = x1.shape
    weights = (wpa, wpb, wout, w1, w3, w2)
    assert all(w.dtype == F32 and w.shape[0] % CAST_ROWS == 0 for w in weights)
    row = lambda i: (i, 0)
    whole = lambda i: (0, 0)
    hbm = pl.BlockSpec(memory_space=pl.ANY)
    return pl.pallas_call(
        _dense_out_kernel,
        grid=(n // ROW_TILE,),
        in_specs=[
            pl.BlockSpec((ROW_TILE, d), row),
            pl.BlockSpec((ROW_TILE, ya.shape[1]), row),
            pl.BlockSpec((ROW_TILE, yb.shape[1]), row),
            pl.BlockSpec((ROW_TILE, gates.shape[1]), row),
            hbm, hbm, hbm,
            _resident((1, d), whole), _resident((1, d), whole),
            hbm, hbm, hbm,
            _resident((1, d), whole), _resident((1, d), whole),
        ],
        out_specs=pl.BlockSpec((ROW_TILE, d), row),
        out_shape=jax.ShapeDtypeStruct((n, d), F32),
        scratch_shapes=[pltpu.VMEM((ROW_TILE, d), BF16), pltpu.VMEM((ROW_TILE, d), F32)]
        + [pltpu.VMEM(w.shape, BF16) for w in weights]
        + [pltpu.VMEM((CAST_SLOTS, CAST_ROWS, max(w.shape[1] for w in weights)), F32),
           pltpu.SemaphoreType.DMA((CAST_SLOTS,))],
        compiler_params=_cparams(("arbitrary",)),
        name="dense_out",
    )(x1, ya, yb, gates, wpa, wpb, wout, g2, b2, w1, w3, w2, g3, b3)


def _window_start(n, k):
    return np.clip(np.arange(n) - k // 2, 0, n - k)


def _na_group_types(rows):
    last = rows - NA_QROWS
    return ((0, 0), (NA_QROWS * 2, NA_QROWS), (last, rows - NA_KROWS))


def _na_tables(rows):
    w = GRID_W
    shift = np.zeros((2 * NA_KW - 1, w, 2 * w), np.float32)
    q = np.arange(w)[:, None]
    kc = np.arange(w)[None, :]
    for dc in range(2 * NA_KW - 1):
        shift[dc, :, :w] = (kc - q + NA_KW - 1 == dc).astype(np.float32)
    cs = _window_start(w, NA_KW)
    col_ok = (kc >= cs[:, None]) & (kc < cs[:, None] + NA_KW)
    rs_all = _window_start(rows, NA_KH)
    negmask = np.full((3, NA_QROWS, w, NA_KROWS, w), NEG, np.float32)
    for t, (r0, kb) in enumerate(_na_group_types(rows)):
        for i in range(NA_QROWS):
            rs = rs_all[r0 + i]
            for j in range(NA_KROWS):
                if rs <= kb + j < rs + NA_KH:
                    negmask[t, i, :, j, :] = np.where(col_ok, 0.0, NEG)
    return shift, negmask.reshape(3, NA_QROWS * w, NA_KROWS * w)


def _na_bias_kernel(rpb_ref, shift_ref, neg_ref, o_ref, u_ref, *, rows):
    h = pl.program_id(0)
    n_dr = 2 * NA_KH - 1
    n_dc = 2 * NA_KW - 1
    w = GRID_W

    for dr0 in range(0, n_dr, NA_BIAS_GROUP):
        base = (h * n_dr + dr0) * n_dc
        accs = [None] * NA_BIAS_GROUP
        for dc in range(n_dc):
            mask = shift_ref[dc]
            for j in range(NA_BIAS_GROUP):
                term = rpb_ref[base + j * n_dc + dc] * mask
                accs[j] = term if dc == 0 else accs[j] + term
        for j in range(NA_BIAS_GROUP):
            left = accs[j] * LOG2E
            u_ref[0, dr0 + j] = left
            u_ref[1, dr0 + j] = pltpu.roll(left, w, 1)
    rs_all = _window_start(rows, NA_KH)
    for t, (r0, kb) in enumerate(_na_group_types(rows)):
        for i in range(NA_QROWS):
            rs = rs_all[r0 + i]
            for jj in range(NA_KROWS // 2):
                val = neg_ref[t, i * w:(i + 1) * w, jj * 2 * w:(jj + 1) * 2 * w]
                for s in range(2):
                    kr = kb + 2 * jj + s
                    if rs <= kr < rs + NA_KH:
                        val = val + u_ref[s, kr - (r0 + i) + NA_KH - 1]
                o_ref[t, 0, i * w:(i + 1) * w, jj * 2 * w:(jj + 1) * 2 * w] = val


def _na_bias(rpb, rows):
    shift, negmask = _na_tables(rows)
    nq, nk = NA_QROWS * GRID_W, NA_KROWS * GRID_W
    return pl.pallas_call(
        functools.partial(_na_bias_kernel, rows=rows),
        grid=(NA_HEADS,),
        in_specs=[
            pl.BlockSpec(memory_space=pltpu.SMEM),
            _resident(shift.shape, lambda h: (0, 0, 0)),
            _resident(negmask.shape, lambda h: (0, 0, 0)),
        ],
        out_specs=pl.BlockSpec((3, 1, nq, nk), lambda h: (0, h, 0, 0)),
        out_shape=jax.ShapeDtypeStruct((3, NA_HEADS, nq, nk), F32),
        scratch_shapes=[pltpu.VMEM((2, 2 * NA_KH - 1, GRID_W, 2 * GRID_W), F32)],
        compiler_params=_cparams(("arbitrary",)),
        name="na_bias",
    )(rpb.reshape(-1), jnp.asarray(shift), jnp.asarray(negmask))


def _na_attn_kernel(q_ref, k_ref, v_ref, bias_ref, o_ref, s_ref, e_ref, *, rows):
    g = pl.program_id(1)
    kb = jnp.clip(g * NA_QROWS - NA_KH // 2, 0, rows - NA_KROWS)
    k0 = pl.multiple_of(kb * GRID_W, GRID_W)
    nk = NA_KROWS * GRID_W
    dh = NA_HEAD_DIM
    lane = lax.broadcasted_iota(jnp.int32, (1, 2 * dh), 1)
    own = (lane < dh, lane >= dh)
    keep = [m.astype(BF16) for m in own]
    pair_cols = [slice(hp * 2 * dh, (hp + 1) * 2 * dh) for hp in range(NA_HEADS // 2)]
    for hp, cols in enumerate(pair_cols):
        q2 = q_ref[0, :, cols]
        k2 = k_ref[0, pl.ds(k0, nk), cols]
        for s in range(2):
            s_ref[2 * hp + s] = lax.dot_general(q2, k2 * keep[s], (((1,), (1,)), ((), ())), preferred_element_type=F32)
    for h in range(NA_HEADS):
        sc = s_ref[h] + bias_ref[0, h]
        e_ref[h] = jnp.exp2(sc - jnp.max(sc, axis=-1, keepdims=True)).astype(BF16)
    for hp, cols in enumerate(pair_cols):
        v2 = v_ref[0, pl.ds(k0, nk), cols]
        pair = []
        for s in range(2):
            r = jnp.dot(e_ref[2 * hp + s], v2 * keep[s] + keep[1 - s], preferred_element_type=F32)
            den = r[:, dh:dh + 1] if s == 0 else r[:, 0:1]
            pair.append(jnp.where(own[s], r / den, 0.0))
        o_ref[0, :, cols] = (pair[0] + pair[1]).astype(BF16)


def _na_attn(qkv, bias, rows):
    b, seq, three_w = qkv.shape
    width = three_w // 3
    nq, nk = NA_QROWS * GRID_W, NA_KROWS * GRID_W
    groups = rows // NA_QROWS

    def bias_idx(bi, g):
        return (jnp.where(g == 0, 0, jnp.where(g == groups - 1, 2, 1)), 0, 0, 0)

    return pl.pallas_call(
        functools.partial(_na_attn_kernel, rows=rows),
        grid=(b, groups),
        in_specs=[
            pl.BlockSpec((1, nq, width), lambda bi, g: (bi, g, 0)),
            pl.BlockSpec((1, seq, width), lambda bi, g: (bi, 0, 1)),
            pl.BlockSpec((1, seq, width), lambda bi, g: (bi, 0, 2)),
            pl.BlockSpec((1, NA_HEADS, nq, nk), bias_idx),
        ],
        out_specs=pl.BlockSpec((1, nq, width), lambda bi, g: (bi, g, 0)),
        out_shape=jax.ShapeDtypeStruct((b, seq, width), BF16),
        scratch_shapes=[pltpu.VMEM((NA_HEADS, nq, nk), F32), pltpu.VMEM((NA_HEADS, nq, nk), BF16)],
        compiler_params=_cparams(("arbitrary", "arbitrary")),
        name="na_attn",
    )(qkv, qkv, qkv, bias)


def _dft_tables():
    k = np.arange(RADIX)
    ang = 2.0 * np.pi * np.outer(k, k) / RADIX
    fr, fi = np.cos(ang), -np.sin(ang)
    ang_t = 2.0 * np.pi * np.outer(k, k) / (RADIX * RADIX)
    twr, twi = np.cos(ang_t), -np.sin(ang_t)
    f32 = lambda a: jnp.asarray(a, F32)
    return f32(fr), f32(fi), f32(twr), f32(twi)


def _embed(cr, ci):
    return jnp.concatenate([jnp.concatenate([cr, -ci], axis=1), jnp.concatenate([ci, cr], axis=1)], axis=0)


def _cmatmul(c_rows, x_lanes):
    m = c_rows.shape[0] // 2
    r = jnp.dot(c_rows, x_lanes, preferred_element_type=F32)
    return r[:m, :LANES] - r[m:, LANES:], r[:m, LANES:] + r[m:, :LANES]


def _load_pair(ref, rows):
    return jnp.concatenate([ref[0, rows, :], ref[1, rows, :]], axis=1).astype(BF16)


def _store_pair(ref, rows, re, im):
    ref[0, rows, :] = re
    ref[1, rows, :] = im


def _twiddled(fr, fi, twr, twi, scale_re=1.0, scale_im=1.0):
    return jnp.concatenate([(fr * twr - fi * twi) * scale_re, (fr * twi + fi * twr) * scale_im], axis=0).astype(BF16)


def _position_features(seq):
    t = np.linspace(0.0, 1.0, seq)[:, None]
    bands = (HY_EMB_DIM - 1) // 2
    w = (2.0 * math.pi / seq) * np.arange(seq)[:, None]
    f = np.linspace(1e-4, bands - 1, bands)[None, :]
    z = np.concatenate([t, np.cos(f * w), -np.sin(f * w)], axis=-1)
    zz = np.zeros((seq // 2, 2 * LANES), np.float32)
    zz[:, :HY_EMB_DIM] = z[:seq // 2]
    zz[:, LANES:LANES + HY_EMB_DIM] = z[seq // 2:]
    return zz


def _filt_mlp_kernel(z_ref, w1_ref, b1_ref, w2_ref, b2_ref, fr_ref, o_ref):
    hp = lax.Precision.HIGHEST
    fr = fr_ref[...]
    hid = LANES // 2
    h = jnp.sin(fr * (jnp.dot(z_ref[...], w1_ref[...], precision=hp, preferred_element_type=F32) + b1_ref[...]))
    h = jnp.sin(fr * (jnp.dot(h, w2_ref[...], precision=hp, preferred_element_type=F32) + b2_ref[...]))
    hi = h.astype(BF16).astype(F32)
    lo = h - hi
    hi_sw, lo_sw = pltpu.roll(hi, hid, 1), pltpu.roll(lo, hid, 1)
    low = lax.broadcasted_iota(jnp.int32, (1, LANES), 1) < hid
    o_ref[0, :, :LANES] = jnp.where(low, hi, lo_sw).astype(BF16)
    o_ref[0, :, LANES:] = jnp.where(low, hi, 0.0).astype(BF16)
    o_ref[1, :, :LANES] = jnp.where(low, hi_sw, lo).astype(BF16)
    o_ref[1, :, LANES:] = jnp.where(low, hi_sw, 0.0).astype(BF16)


def _filt_mlp(zz, fw1, fb1, fw2, fb2, freq):
    seq = zz.shape[0]
    hid = fw2.shape[0]
    assert 2 * hid == LANES
    w1 = jnp.zeros((2 * LANES, LANES), F32)
    w1 = w1.at[:HY_EMB_DIM, :hid].set(fw1).at[LANES:LANES + HY_EMB_DIM, hid:].set(fw1)
    w2 = jnp.zeros((LANES, LANES), F32).at[:hid, :hid].set(fw2).at[hid:, hid:].set(fw2)
    twice = lambda v: jnp.concatenate([v, v]).reshape(1, LANES)
    rows = 2048
    whole = lambda i: (0, 0)
    return pl.pallas_call(
        _filt_mlp_kernel,
        grid=(seq // rows,),
        in_specs=[pl.BlockSpec((rows, 2 * LANES), lambda i: (i, 0)),
                  pl.BlockSpec(w1.shape, whole), pl.BlockSpec((1, LANES), whole),
                  pl.BlockSpec(w2.shape, whole), pl.BlockSpec((1, LANES), whole), pl.BlockSpec((1, LANES), whole)],
        out_specs=pl.BlockSpec((2, rows, 2 * LANES), lambda i: (0, i, 0)),
        out_shape=jax.ShapeDtypeStruct((2, seq, 2 * LANES), BF16),
        compiler_params=_cparams(("arbitrary",)),
        name="filt_mlp",
    )(zz, w1, twice(fb1), w2, twice(fb2), twice(freq))


def _filt_time_kernel(ha_ref, hb_ref, perm_ref, w3_ref, b3_ref, delta_ref, k_ref, ss_ref, hsel_ref, *, seq):
    i = pl.program_id(0)
    rows = ha_ref.shape[0]
    half = seq // rows

    @pl.when(i < half)
    def _():
        hsel_ref[...] = ha_ref[...]

    @pl.when(i >= half)
    def _():
        for s in range(rows // RADIX):
            lo_row = rows - RADIX * (s + 1)
            nxt = hb_ref[0:RADIX, :] if s == 0 else ha_ref[lo_row + RADIX:lo_row + 2 * RADIX, :]
            src = jnp.concatenate([ha_ref[lo_row:lo_row + RADIX, :], nxt], axis=0)
            hsel_ref[s * RADIX:(s + 1) * RADIX, :] = jnp.dot(perm_ref[...], src, preferred_element_type=F32).astype(BF16)

    w = w3_ref[0]
    w_hi = w.astype(BF16)
    w_lo = (w - w_hi.astype(F32)).astype(BF16)
    wcat = jnp.concatenate([w_hi, w_hi, w_lo, jnp.zeros_like(w_lo)], axis=0)
    hf = jnp.dot(hsel_ref[...], wcat, preferred_element_type=F32) + b3_ref[0]
    m = i * rows + lax.broadcasted_iota(jnp.int32, (rows, 1), 0)
    pos = jnp.where(m < seq, m, 2 * seq - m).astype(F32)
    t = pos / (seq - 1)
    k = jnp.where(m == seq, 0.0, hf * jnp.exp(-t * delta_ref[...]))
    for j in range(rows // RADIX):
        k_ref[j * PITCH:j * PITCH + RADIX, :] = k[j * RADIX:(j + 1) * RADIX]
        k_ref[j * PITCH + RADIX:(j + 1) * PITCH, :] = jnp.zeros((PITCH - RADIX, k.shape[1]), F32)
    part = jnp.sum(k * k, axis=0, keepdims=True)

    @pl.when(i == 0)
    def _():
        ss_ref[...] = part

    @pl.when(i != 0)
    def _():
        ss_ref[...] += part


def _filt_time(hdn, w3d, b3d, delta2):
    seq, packed = hdn.shape
    _, hid, cols = w3d.shape
    assert packed == 4 * hid
    n2l = 2 * seq
    rows = 1024
    half = seq // rows
    prows = rows // RADIX * PITCH
    perm = np.zeros((RADIX, 2 * RADIX), np.float32)
    perm[0, RADIX] = 1.0
    perm[np.arange(1, RADIX), RADIX - np.arange(1, RADIX)] = 1.0
    return pl.pallas_call(
        functools.partial(_filt_time_kernel, seq=seq),
        grid=(n2l // rows,),
        in_specs=[pl.BlockSpec((rows, packed), lambda i: (jnp.where(i < half, i, 2 * half - 1 - i), 0)),
                  pl.BlockSpec((rows, packed), lambda i: (jnp.clip(2 * half - i, 0, half - 1), 0)),
                  pl.BlockSpec(perm.shape, lambda i: (0, 0)),
                  pl.BlockSpec((1, hid, cols), lambda i: (i // half, 0, 0)),
                  pl.BlockSpec((1, 1, cols), lambda i: (i // half, 0, 0)),
                  pl.BlockSpec((1, cols), lambda i: (0, 0))],
        out_specs=[pl.BlockSpec((prows, cols), lambda i: (i, 0)),
                   pl.BlockSpec((1, cols), lambda i: (0, 0))],
        out_shape=[jax.ShapeDtypeStruct((n2l // RADIX * PITCH, cols), F32), jax.ShapeDtypeStruct((1, cols), F32)],
        scratch_shapes=[pltpu.VMEM((rows, packed), BF16)],
        compiler_params=_cparams(("arbitrary",)),
        name="filt_time",
    )(hdn, hdn, jnp.asarray(perm, BF16), w3d, b3d, delta2)


def _filt_spec_kernel(k_ref, ss_ref, fa_ref, fr_ref, fi_ref, twr_ref, twi_ref, o_ref, a_ref):
    r = RADIX
    scale = lax.rsqrt(ss_ref[...] + 1e-12)

    def a_body(i, c):
        n2 = 2 * i
        rhs = jnp.concatenate([k_ref[pl.ds(n2, r, stride=PITCH), :], k_ref[pl.ds(n2 + 1, r, stride=PITCH), :]],
                              axis=1).astype(BF16)
        out = jnp.dot(fa_ref[...], rhs, preferred_element_type=F32)
        _store_pair(a_ref, pl.ds(n2, r, stride=PITCH), out[:r, :LANES], out[r:, :LANES])
        _store_pair(a_ref, pl.ds(n2 + 1, r, stride=PITCH), out[:r, LANES:], out[r:, LANES:])
        return c

    lax.fori_loop(0, r // 2, a_body, 0, unroll=DFT_UNROLL_STRIDED // 2)

    def b_body(k1, c):
        r0 = pl.multiple_of(k1 * PITCH, 8)
        o0 = pl.multiple_of(k1 * r, r)
        blk = _load_pair(a_ref, pl.ds(r0, r))
        sr, si = _cmatmul(_twiddled(fr_ref[...], fi_ref[...], twr_ref[pl.ds(k1, 1), :], twi_ref[pl.ds(k1, 1), :]), blk)
        o_ref[0, pl.ds(o0, r), :] = (sr * scale).astype(BF16)
        o_ref[1, pl.ds(o0, r), :] = (si * scale).astype(BF16)
        return c

    lax.fori_loop(0, r, b_body, 0, unroll=DFT_UNROLL_STRIDED)


def _filt_spec(ktime, ss, fa_real, fr, fi, twr, twi):
    prows, cols = ktime.shape
    n2l = prows // PITCH * RADIX
    whole = lambda j: (0, 0)
    return pl.pallas_call(
        _filt_spec_kernel,
        grid=(cols // LANES,),
        in_specs=[pl.BlockSpec((prows, LANES), lambda j: (0, j)),
                  pl.BlockSpec((1, LANES), lambda j: (0, j)),
                  _resident(fa_real.shape, whole), _resident(fr.shape, whole), _resident(fi.shape, whole),
                  _resident(twr.shape, whole), _resident(twi.shape, whole)],
        out_specs=pl.BlockSpec((2, n2l, LANES), lambda j: (0, 0, j)),
        out_shape=jax.ShapeDtypeStruct((2, n2l, cols), BF16),
        scratch_shapes=[pltpu.VMEM((2, prows, LANES), F32)],
        compiler_params=_cparams(("arbitrary",)),
        name="filt_spec",
    )(ktime, ss, fa_real, fr, fi, twr, twi)


def _short_conv_rows(u_ref, bi, r0, nrows, seq, w_ref, b_ref):
    edge = 16
    cur = u_ref[bi, pl.ds(r0, nrows), :].astype(F32)
    before = u_ref[bi, pl.ds(pl.multiple_of(jnp.maximum(r0 - edge, 0), edge), edge), :].astype(F32)
    after = u_ref[bi, pl.ds(pl.multiple_of(jnp.minimum(r0 + nrows, seq - edge), edge), edge), :].astype(F32)
    prev_edge = jnp.where(r0 > 0, before[edge - 1:edge, :], 0.0)
    next_edge = jnp.where(r0 + nrows < seq, after[0:1, :], 0.0)
    w0, w1, w2 = w_ref[0:1, :], w_ref[1:2, :], w_ref[2:3, :]
    out = b_ref[...] + w0 * pltpu.roll(cur, 1, 0) + w1 * cur + w2 * pltpu.roll(cur, nrows - 1, 0)
    row = lax.broadcasted_iota(jnp.int32, (8, cur.shape[1]), 0)
    first = out[0:8] + jnp.where(row == 0, w0 * (prev_edge - cur[nrows - 1:nrows]), 0.0)
    last = out[nrows - 8:] + jnp.where(row == 7, w2 * (next_edge - cur[0:1]), 0.0)
    return jnp.concatenate([first, out[8:nrows - 8], last], axis=0)


def _hy_conv_kernel(z_ref, zw_ref, zb_ref, hyu_hbm, uw1_ref, ub1_ref, uw2_ref, ub2_ref, ks_hbm, hb1_ref, hb2_ref,
                    fa_ref, fc_ref, fr_ref, fi_ref, twr_ref, twi_ref, o_ref,
                    a_ref, v_ref, ks_ref, u_buf, ks_sem, u_sem, *, tiles):
    r = RADIX
    half = r // 2
    nb, seq, _ = z_ref.shape
    nchunk = seq // CONV_ROWS
    blocks = CONV_ROWS // r
    tile = pl.program_id(0)

    def ks_copy(order):
        col = pl.multiple_of((order * tiles + tile) * LANES, LANES)
        return pltpu.make_async_copy(ks_hbm.at[:, :, pl.ds(col, LANES)], ks_ref, ks_sem)

    def u_copy(part):
        col = pl.multiple_of((part * tiles + tile) * LANES, LANES)
        return pltpu.make_async_copy(hyu_hbm.at[:, :, pl.ds(col, LANES)], u_buf.at[part - 1], u_sem.at[part - 1])

    ks_copy(0).start()
    u_copy(1).start()
    u_copy(2).start()

    def load_body(c, carry):
        r0 = pl.multiple_of(c * CONV_ROWS, CONV_ROWS)
        for bi in range(nb):
            v = _short_conv_rows(z_ref, bi, r0, CONV_ROWS, seq, zw_ref, zb_ref)
            o_ref[bi, pl.ds(r0, CONV_ROWS), :] = v.astype(BF16)
            for j in range(blocks):
                p0 = pl.multiple_of((c * blocks + j) * PITCH, 8)
                v_ref[bi, pl.ds(p0, r), :] = v[j * r:(j + 1) * r]
        return carry

    lax.fori_loop(0, nchunk, load_body, 0)

    def pair_store(ref, rows0, rows1, out):
        _store_pair(ref, rows0, out[:r, :LANES], out[r:, :LANES])
        _store_pair(ref, rows1, out[:r, LANES:], out[r:, LANES:])

    def a_body(i, carry):
        n2 = 2 * i

        def column(n):
            return jnp.concatenate([v_ref[0, pl.ds(n, half, stride=PITCH), :],
                                    v_ref[1, pl.ds(n, half, stride=PITCH), :]], axis=0)

        rhs = jnp.concatenate([column(n2), column(n2 + 1)], axis=1).astype(BF16)
        out = jnp.dot(fa_ref[...], rhs, preferred_element_type=F32)
        pair_store(a_ref, pl.ds(n2, r, stride=PITCH), pl.ds(n2 + 1, r, stride=PITCH), out)
        return carry

    def stage_a():
        lax.fori_loop(0, r // 2, a_body, 0, unroll=DFT_UNROLL_STRIDED // 2)

    def b_forward(k1):
        r0 = pl.multiple_of(k1 * PITCH, 8)
        s0 = pl.multiple_of(k1 * r, r)
        blk = _load_pair(a_ref, pl.ds(r0, r))
        sr, si = _cmatmul(_twiddled(fr_ref[...], fi_ref[...], twr_ref[pl.ds(k1, 1), :], twi_ref[pl.ds(k1, 1), :]), blk)
        kr = ks_ref[0, pl.ds(s0, r), :].astype(F32)
        ki = ks_ref[1, pl.ds(s0, r), :].astype(F32)
        return jnp.concatenate([sr * kr - si * ki, sr * ki + si * kr], axis=0).astype(BF16)

    def b_forward_pair(i):
        return jnp.concatenate([b_forward(2 * i), b_forward(2 * i + 1)], axis=1)

    def b_inverse_pair(i, p):
        out = jnp.dot(fc_ref[...], p, preferred_element_type=F32)
        pair_store(a_ref, pl.ds(pl.multiple_of(2 * i * PITCH, 8), r), pl.ds(pl.multiple_of((2 * i + 1) * PITCH, 8), r), out)

    def b_body(i, p):
        p_next = b_forward_pair(i + 1)
        b_inverse_pair(i, p)
        return p_next

    def stage_b():
        b_inverse_pair(r // 2 - 1, lax.fori_loop(0, r // 2 - 1, b_body, b_forward_pair(0), unroll=DFT_UNROLL // 2))

    inv_n = 1.0 / (r * r)

    def c_body(n2, carry):
        g = _twiddled(fr_ref[0:half, :], fi_ref[0:half, :], twr_ref[pl.ds(n2, 1), :], twi_ref[pl.ds(n2, 1), :],
                      inv_n, -inv_n)
        yr, yi = _cmatmul(g, _load_pair(a_ref, pl.ds(n2, r, stride=PITCH)))
        _store_pair(v_ref, pl.ds(n2, half, stride=PITCH), yr, yi)
        return carry

    def stage_c():
        lax.fori_loop(0, r, c_body, 0, unroll=DFT_UNROLL_STRIDED)

    def epilogue(order):
        u_ref = u_buf.at[order]
        uw_ref, ub_ref, hb_ref = ((uw1_ref, ub1_ref, hb1_ref), (uw2_ref, ub2_ref, hb2_ref))[order]

        def out_body(c, carry):
            r0 = pl.multiple_of(c * CONV_ROWS, CONV_ROWS)
            for bi in range(nb):
                gate = _short_conv_rows(u_ref, bi, r0, CONV_ROWS, seq, uw_ref, ub_ref)
                rows = [pl.ds(pl.multiple_of((c * blocks + j) * PITCH, 8), r) for j in range(blocks)]
                conv = jnp.concatenate([v_ref[bi, rw, :] for rw in rows], axis=0)
                zp = o_ref[bi, pl.ds(r0, CONV_ROWS), :].astype(F32)
                res = gate * (conv + hb_ref[...] * zp)
                o_ref[bi, pl.ds(r0, CONV_ROWS), :] = res.astype(BF16)
                if order == 0:
                    for j, rw in enumerate(rows):
                        v_ref[bi, rw, :] = res[j * r:(j + 1) * r]
            return carry

        lax.fori_loop(0, nchunk, out_body, 0)

    stage_a()
    ks_copy(0).wait()
    stage_b()
    ks_copy(1).start()
    stage_c()
    u_copy(1).wait()
    epilogue(0)
    stage_a()
    ks_copy(1).wait()
    stage_b()
    stage_c()
    u_copy(2).wait()
    epilogue(1)


def _hy_conv(hyu, short_w, short_b, kspec, hy_bias, tables):
    nb, seq, _ = hyu.shape
    width = hy_bias.shape[1]
    tiles = width // LANES
    fa, fc, fr, fi, twr, twi = tables
    whole = lambda j: (0, 0)
    part = lambda p, rows: pl.BlockSpec((rows, LANES), lambda j: (0, p * tiles + j))
    hbm = pl.BlockSpec(memory_space=pl.ANY)
    return pl.pallas_call(
        functools.partial(_hy_conv_kernel, tiles=tiles),
        grid=(tiles,),
        in_specs=[
            pl.BlockSpec((nb, seq, LANES), lambda j: (0, 0, j)), part(0, 3), part(0, 1),
            hbm, part(1, 3), part(1, 1), part(2, 3), part(2, 1),
            hbm, pl.BlockSpec((1, LANES), lambda j: (0, j)), pl.BlockSpec((1, LANES), lambda j: (0, j)),
            _resident(fa.shape, whole), _resident(fc.shape, whole), _resident(fr.shape, whole),
            _resident(fi.shape, whole), _resident(twr.shape, whole), _resident(twi.shape, whole),
        ],
        out_specs=_resident((nb, seq, LANES), lambda j: (0, 0, j)),
        out_shape=jax.ShapeDtypeStruct((nb, seq, width), BF16),
        scratch_shapes=[pltpu.VMEM((2, 2 * seq // RADIX * PITCH, LANES), F32),
                        pltpu.VMEM((nb, seq // RADIX * PITCH, LANES), F32),
                        pltpu.VMEM((2, 2 * seq, LANES), BF16),
                        pltpu.VMEM((HY_ORDER, nb, seq, LANES), BF16),
                        pltpu.SemaphoreType.DMA(()), pltpu.SemaphoreType.DMA((HY_ORDER,))],
        compiler_params=_cparams(("arbitrary",)),
        name="hy_conv",
    )(hyu, short_w, short_b, hyu, short_w, short_b, short_w, short_b, kspec, hy_bias[0:1], hy_bias[1:2],
      fa, fc, fr, fi, twr, twi)


def _hyena(hyu, short_w, short_b, fw1, fb1, fw2, fb2, fw3, fb3, freq, hy_bias):
    nb, seq, _ = hyu.shape
    assert nb == 2 and 2 * seq == RADIX * RADIX
    width = hy_bias.shape[1]
    hid = fw2.shape[0]
    fr, fi, twr, twi = _dft_tables()
    half = RADIX // 2
    fa_real = jnp.concatenate([fr, fi], axis=0).astype(BF16)
    fa_half = _embed(fr[:, :half], fi[:, :half]).astype(BF16)
    fc = _embed(fr, -fi).astype(BF16)

    zz = jnp.asarray(_position_features(seq))
    hdn = _filt_mlp(zz, fw1, fb1, fw2, fb2, freq).reshape(seq, 4 * hid)
    w3d = fw3.reshape(hid, 2, HY_ORDER * width).transpose(1, 0, 2)
    b3d = fb3.reshape(2, 1, HY_ORDER * width)
    max_decay = math.log(HY_DECAY_TARGET) / HY_FAST_DECAY_PCT
    min_decay = math.log(HY_DECAY_TARGET) / HY_SLOW_DECAY_PCT
    delta = np.abs(np.linspace(min_decay, max_decay, width)).astype(np.float32)
    delta2 = jnp.asarray(np.tile(delta, HY_ORDER)[None, :])
    ktime, ss = _filt_time(hdn, w3d, b3d, delta2)
    kspec = _filt_spec(ktime, ss, fa_real, fr, fi, twr, twi)

    tables = (fa_half, fc, fr, fi, twr, twi)
    sb = short_b.reshape(1, -1)
    return _hy_conv(hyu, short_w, sb, kspec, hy_bias, tables)


def kernel(x, ln1_g, ln1_b, ffn1_w1, ffn1_w3, ffn1_w2, w_in, b_gate, na_rpb, hy_short_w, hy_short_b,
           hy_filt_w1, hy_filt_b1, hy_filt_w2, hy_filt_b2, hy_filt_w3, hy_filt_b3, hy_filt_freq, hy_bias,
           w_pa, w_pb, w_out, ln2_g, ln2_b, ffn2_w1, ffn2_w3, ffn2_w2, ln3_g, ln3_b):
    assert ln1_g.shape[0] == DEPTH
    b, seq, d = x.shape
    rows = seq // GRID_W
    na_width = NA_HEADS * NA_HEAD_DIM
    n_qkv = 3 * na_width
    n_hy = hy_short_w.shape[2]
    vec = lambda a: a[0].reshape(1, -1)

    x1, qkv, hyu, gates = _dense_in(
        x.reshape(b * seq, d), ffn1_w1[0], ffn1_w3[0], ffn1_w2[0], vec(ln1_g), vec(ln1_b),
        w_in[0], vec(b_gate), n_qkv, n_hy)

    bias = _na_bias(na_rpb[0], rows)
    ya = _na_attn(qkv.reshape(b, seq, n_qkv), bias, rows)

    yb = _hyena(hyu.reshape(b, seq, n_hy), hy_short_w[0], hy_short_b[0], hy_filt_w1[0], hy_filt_b1[0],
                hy_filt_w2[0], hy_filt_b2[0], hy_filt_w3[0], hy_filt_b3[0], hy_filt_freq[0], hy_bias[0])

    out = _dense_out(
        x1, ya.reshape(b * seq, na_width), yb.reshape(b * seq, -1), gates,
        w_pa[0], w_pb[0], w_out[0], vec(ln2_g), vec(ln2_b),
        ffn2_w1[0], ffn2_w3[0], ffn2_w2[0], vec(ln3_g), vec(ln3_b))
    return out.reshape(b, seq, d)
```

```python
import functools
import math

import numpy as np
import jax
import jax.numpy as jnp
from jax import lax
from jax.experimental import pallas as pl
from jax.experimental.pallas import tpu as pltpu

F32 = jnp.float32
BF16 = jnp.bfloat16

GRID_W = 64
NA_HEADS = 8
NA_HEAD_DIM = 64
NA_KH = 8
NA_KW = 16
HY_ORDER = 2
HY_EMB_DIM = 33
HY_FAST_DECAY_PCT = 0.3
HY_SLOW_DECAY_PCT = 1.5
HY_DECAY_TARGET = 1e-2
DEPTH = 1
DN_ALPHA = (2 * DEPTH) ** 0.25
LN_EPS = 1e-5
LOG2E = math.log2(math.e)

LANES = 128
VMEM_LIMIT = 60 * 1024 * 1024

ROW_TILE = 512
FF_CHUNK = 256
PROJ_CHUNK = 512
CAST_ROWS = 64
CAST_SLOTS = 8
NA_QROWS = 4
NA_KROWS = NA_QROWS + NA_KH
NA_BIAS_GROUP = 5
RADIX = 128
PITCH = RADIX + 8
DFT_UNROLL = 42
DFT_UNROLL_STRIDED = 64
CONV_ROWS = 512
NEG = -1e30


def _cparams(sem):
    return pltpu.CompilerParams(dimension_semantics=sem, vmem_limit_bytes=VMEM_LIMIT)


def _resident(shape, index_map):
    return pl.BlockSpec(shape, index_map, pipeline_mode=pl.Buffered(1))


def _cast_weights_in(pairs, stage_ref, sem):
    ahead = CAST_SLOTS - 1
    counts = [src.shape[0] // CAST_ROWS for src, _ in pairs]
    assert all(n >= ahead for n in counts)
    bases = [sum(counts[:w]) for w in range(len(pairs))]

    def copy(w, c, g):
        src, cols = pairs[w][0], pairs[w][0].shape[1]
        slot = g % CAST_SLOTS
        return pltpu.make_async_copy(src.at[pl.ds(c * CAST_ROWS, CAST_ROWS), :],
                                     stage_ref.at[slot, :, pl.ds(0, cols)], sem.at[slot])

    for g in range(ahead):
        copy(0, g, g).start()
    for w, (src, dst) in enumerate(pairs):
        n, cols = counts[w], src.shape[1]

        def body(c, carry, w=w, n=n, cols=cols, dst=dst):
            g = bases[w] + c

            @pl.when(c + ahead < n)
            def _():
                copy(w, c + ahead, g + ahead).start()

            if w + 1 < len(pairs):
                @pl.when(c + ahead >= n)
                def _():
                    copy(w + 1, c + ahead - n, g + ahead).start()

            copy(w, c, g).wait()
            r0 = pl.multiple_of(c * CAST_ROWS, CAST_ROWS)
            dst[pl.ds(r0, CAST_ROWS), :] = stage_ref[g % CAST_SLOTS, :, 0:cols].astype(BF16)
            return carry

        lax.fori_loop(0, n, body, 0)


def _layer_norm(r, g, b):
    mu = jnp.mean(r, axis=-1, keepdims=True)
    d = r - mu
    var = jnp.mean(d * d, axis=-1, keepdims=True)
    return d * lax.rsqrt(var + LN_EPS) * g + b


def _swiglu_ln(x, xb_ref, acc_ref, w1_ref, w3_ref, w2_ref, g_ref, b_ref):
    d_ff = w1_ref.shape[1]
    xb_ref[...] = x.astype(BF16)
    for j in range(d_ff // FF_CHUNK):
        sl = slice(j * FF_CHUNK, (j + 1) * FF_CHUNK)
        a = jnp.dot(xb_ref[...], w1_ref[:, sl], preferred_element_type=F32)
        b = jnp.dot(xb_ref[...], w3_ref[:, sl], preferred_element_type=F32)
        h = (a * jax.nn.sigmoid(a) * b).astype(BF16)
        part = jnp.dot(h, w2_ref[sl, :], preferred_element_type=F32)
        if j == 0:
            acc_ref[...] = part
        else:
            acc_ref[...] += part
    return _layer_norm(DN_ALPHA * x + 0.5 * acc_ref[...], g_ref[...], b_ref[...])


def _dense_in_kernel(x_ref, w1_hbm, w3_hbm, w2_hbm, g_ref, b_ref, win_hbm, bg_ref,
                     x1_ref, qkv_ref, hyu_ref, gate_ref,
                     xb_ref, acc_ref, w1_ref, w3_ref, w2_ref, win_ref, stage_ref, sem):
    @pl.when(pl.program_id(0) == 0)
    def _():
        _cast_weights_in(((w1_hbm, w1_ref), (w3_hbm, w3_ref), (w2_hbm, w2_ref), (win_hbm, win_ref)), stage_ref, sem)

    x1 = _swiglu_ln(x_ref[...], xb_ref, acc_ref, w1_ref, w3_ref, w2_ref, g_ref, b_ref)
    x1_ref[...] = x1
    xb_ref[...] = x1.astype(BF16)
    n_qkv = qkv_ref.shape[1]
    n_hy = hyu_ref.shape[1]
    n_head = n_qkv // 3
    for j in range(win_ref.shape[1] // PROJ_CHUNK):
        c0 = j * PROJ_CHUNK
        p = jnp.dot(xb_ref[...], win_ref[:, c0:c0 + PROJ_CHUNK], preferred_element_type=F32)
        if c0 < n_qkv:
            if c0 < n_head:
                p = p * (NA_HEAD_DIM ** -0.5 * LOG2E)
            qkv_ref[:, c0:c0 + PROJ_CHUNK] = p.astype(BF16)
        elif c0 < n_qkv + n_hy:
            hyu_ref[:, c0 - n_qkv:c0 - n_qkv + PROJ_CHUNK] = p.astype(BF16)
        else:
            g0 = c0 - n_qkv - n_hy
            gate_ref[:, g0:g0 + PROJ_CHUNK] = jax.nn.sigmoid(p + bg_ref[:, g0:g0 + PROJ_CHUNK]).astype(BF16)


def _dense_in(x2d, w1, w3, w2, g, b, w_in, b_gate, n_qkv, n_hy):
    n, d = x2d.shape
    d_ff = w1.shape[1]
    n_gate = w_in.shape[1] - n_qkv - n_hy
    assert n % ROW_TILE == 0 and d_ff % FF_CHUNK == 0
    assert n_qkv % (3 * PROJ_CHUNK) == 0 and n_hy % PROJ_CHUNK == 0 and n_gate % PROJ_CHUNK == 0
    weights = (w1, w3, w2, w_in)
    assert all(w.dtype == F32 and w.shape[0] % CAST_ROWS == 0 for w in weights)
    row = lambda i: (i, 0)
    whole = lambda i: (0, 0)
    hbm = pl.BlockSpec(memory_space=pl.ANY)
    return pl.pallas_call(
        _dense_in_kernel,
        grid=(n // ROW_TILE,),
        in_specs=[
            pl.BlockSpec((ROW_TILE, d), row),
            hbm, hbm, hbm,
            _resident((1, d), whole), _resident((1, d), whole),
            hbm, _resident((1, n_gate), whole),
        ],
        out_specs=[
            pl.BlockSpec((ROW_TILE, d), row),
            pl.BlockSpec((ROW_TILE, n_qkv), row),
            pl.BlockSpec((ROW_TILE, n_hy), row),
            pl.BlockSpec((ROW_TILE, n_gate), row),
        ],
        out_shape=[
            jax.ShapeDtypeStruct((n, d), F32),
            jax.ShapeDtypeStruct((n, n_qkv), BF16),
            jax.ShapeDtypeStruct((n, n_hy), BF16),
            jax.ShapeDtypeStruct((n, n_gate), BF16),
        ],
        scratch_shapes=[pltpu.VMEM((ROW_TILE, d), BF16), pltpu.VMEM((ROW_TILE, d), F32)]
        + [pltpu.VMEM(w.shape, BF16) for w in weights]
        + [pltpu.VMEM((CAST_SLOTS, CAST_ROWS, max(w.shape[1] for w in weights)), F32),
           pltpu.SemaphoreType.DMA((CAST_SLOTS,))],
        compiler_params=_cparams(("arbitrary",)),
        name="dense_in",
    )(x2d, w1, w3, w2, g, b, w_in, b_gate)


def _dense_out_kernel(x1_ref, ya_ref, yb_ref, gate_ref, wpa_hbm, wpb_hbm, wout_hbm, g2_ref, b2_ref,
                      w1_hbm, w3_hbm, w2_hbm, g3_ref, b3_ref, o_ref,
                      xb_ref, acc_ref, wpa_ref, wpb_ref, wout_ref, w1_ref, w3_ref, w2_ref, stage_ref, sem):
    @pl.when(pl.program_id(0) == 0)
    def _():
        _cast_weights_in(((wpa_hbm, wpa_ref), (wpb_hbm, wpb_ref), (wout_hbm, wout_ref),
                          (w1_hbm, w1_ref), (w3_hbm, w3_ref), (w2_hbm, w2_ref)), stage_ref, sem)

    d = x1_ref.shape[1]
    pa = jnp.dot(ya_ref[...], wpa_ref[...], preferred_element_type=F32)
    pb = jnp.dot(yb_ref[...], wpb_ref[...], preferred_element_type=F32)
    m = gate_ref[:, :d].astype(F32) * pa + gate_ref[:, d:].astype(F32) * pb
    mix = jnp.dot(m.astype(BF16), wout_ref[...], preferred_element_type=F32)
    x2 = _layer_norm(DN_ALPHA * x1_ref[...] + mix, g2_ref[...], b2_ref[...])
    o_ref[...] = _swiglu_ln(x2, xb_ref, acc_ref, w1_ref, w3_ref, w2_ref, g3_ref, b3_ref)


def _dense_out(x1, ya, yb, gates, wpa, wpb, wout, g2, b2, w1, w3, w2, g3, b3):
    n, d = x1.shape
    weights = (wpa, wpb, wout, w1, w3, w2)
    assert all(w.dtype == F32 and w.shape[0] % CAST_ROWS == 0 for w in weights)
    row = lambda i: (i, 0)
    whole = lambda i: (0, 0)
    hbm = pl.BlockSpec(memory_space=pl.ANY)
    return pl.pallas_call(
        _dense_out_kernel,
        grid=(n // ROW_TILE,),
        in_specs=[
            pl.BlockSpec((ROW_TILE, d), row),
            pl.BlockSpec((ROW_TILE, ya.shape[1]), row),
            pl.BlockSpec((ROW_TILE, yb.shape[1]), row),
            pl.BlockSpec((ROW_TILE, gates.shape[1]), row),
            hbm, hbm, hbm,
            _resident((1, d), whole), _resident((1, d), whole),
            hbm, hbm, hbm,
            _resident((1, d), whole), _resident((1, d), whole),
        ],
        out_specs=pl.BlockSpec((ROW_TILE, d), row),
        out_shape=jax.ShapeDtypeStruct((n, d), F32),
        scratch_shapes=[pltpu.VMEM((ROW_TILE, d), BF16), pltpu.VMEM((ROW_TILE, d), F32)]
        + [pltpu.VMEM(w.shape, BF16) for w in weights]
        + [pltpu.VMEM((CAST_SLOTS, CAST_ROWS, max(w.shape[1] for w in weights)), F32),
           pltpu.SemaphoreType.DMA((CAST_SLOTS,))],
        compiler_params=_cparams(("arbitrary",)),
        name="dense_out",
    )(x1, ya, yb, gates, wpa, wpb, wout, g2, b2, w1, w3, w2, g3, b3)


def _window_start(n, k):
    return np.clip(np.arange(n) - k // 2, 0, n - k)


def _na_group_types(rows):
    last = rows - NA_QROWS
    return ((0, 0), (NA_QROWS * 2, NA_QROWS), (last, rows - NA_KROWS))


def _na_tables(rows):
    w = GRID_W
    shift = np.zeros((2 * NA_KW - 1, w, 2 * w), np.float32)
    q = np.arange(w)[:, None]
    kc = np.arange(w)[None, :]
    for dc in range(2 * NA_KW - 1):
        shift[dc, :, :w] = (kc - q + NA_KW - 1 == dc).astype(np.float32)
    cs = _window_start(w, NA_KW)
    col_ok = (kc >= cs[:, None]) & (kc < cs[:, None] + NA_KW)
    rs_all = _window_start(rows, NA_KH)
    negmask = np.full((3, NA_QROWS, w, NA_KROWS, w), NEG, np.float32)
    for t, (r0, kb) in enumerate(_na_group_types(rows)):
        for i in range(NA_QROWS):
            rs = rs_all[r0 + i]
            for j in range(NA_KROWS):
                if rs <= kb + j < rs + NA_KH:
                    negmask[t, i, :, j, :] = np.where(col_ok, 0.0, NEG)
    return shift, negmask.reshape(3, NA_QROWS * w, NA_KROWS * w)


def _na_bias_kernel(rpb_ref, shift_ref, neg_ref, o_ref, u_ref, *, rows):
    h = pl.program_id(0)
    n_dr = 2 * NA_KH - 1
    n_dc = 2 * NA_KW - 1
    w = GRID_W

    for dr0 in range(0, n_dr, NA_BIAS_GROUP):
        base = (h * n_dr + dr0) * n_dc
        accs = [None] * NA_BIAS_GROUP
        for dc in range(n_dc):
            mask = shift_ref[dc]
            for j in range(NA_BIAS_GROUP):
                term = rpb_ref[base + j * n_dc + dc] * mask
                accs[j] = term if dc == 0 else accs[j] + term
        for j in range(NA_BIAS_GROUP):
            left = accs[j] * LOG2E
            u_ref[0, dr0 + j] = left
            u_ref[1, dr0 + j] = pltpu.roll(left, w, 1)
    rs_all = _window_start(rows, NA_KH)
    for t, (r0, kb) in enumerate(_na_group_types(rows)):
        for i in range(NA_QROWS):
            rs = rs_all[r0 + i]
            for jj in range(NA_KROWS // 2):
                val = neg_ref[t, i * w:(i + 1) * w, jj * 2 * w:(jj + 1) * 2 * w]
                for s in range(2):
                    kr = kb + 2 * jj + s
                    if rs <= kr < rs + NA_KH:
                        val = val + u_ref[s, kr - (r0 + i) + NA_KH - 1]
                o_ref[t, 0, i * w:(i + 1) * w, jj * 2 * w:(jj + 1) * 2 * w] = val


def _na_bias(rpb, rows):
    shift, negmask = _na_tables(rows)
    nq, nk = NA_QROWS * GRID_W, NA_KROWS * GRID_W
    return pl.pallas_call(
        functools.partial(_na_bias_kernel, rows=rows),
        grid=(NA_HEADS,),
        in_specs=[
            pl.BlockSpec(memory_space=pltpu.SMEM),
            _resident(shift.shape, lambda h: (0, 0, 0)),
            _resident(negmask.shape, lambda h: (0, 0, 0)),
        ],
        out_specs=pl.BlockSpec((3, 1, nq, nk), lambda h: (0, h, 0, 0)),
        out_shape=jax.ShapeDtypeStruct((3, NA_HEADS, nq, nk), F32),
        scratch_shapes=[pltpu.VMEM((2, 2 * NA_KH - 1, GRID_W, 2 * GRID_W), F32)],
        compiler_params=_cparams(("arbitrary",)),
        name="na_bias",
    )(rpb.reshape(-1), jnp.asarray(shift), jnp.asarray(negmask))


def _na_attn_kernel(q_ref, k_ref, v_ref, bias_ref, o_ref, s_ref, e_ref, *, rows):
    g = pl.program_id(1)
    kb = jnp.clip(g * NA_QROWS - NA_KH // 2, 0, rows - NA_KROWS)
    k0 = pl.multiple_of(kb * GRID_W, GRID_W)
    nk = NA_KROWS * GRID_W
    dh = NA_HEAD_DIM
    lane = lax.broadcasted_iota(jnp.int32, (1, 2 * dh), 1)
    own = (lane < dh, lane >= dh)
    keep = [m.astype(BF16) for m in own]
    pair_cols = [slice(hp * 2 * dh, (hp + 1) * 2 * dh) for hp in range(NA_HEADS // 2)]
    for hp, cols in enumerate(pair_cols):
        q2 = q_ref[0, :, cols]
        k2 = k_ref[0, pl.ds(k0, nk), cols]
        for s in range(2):
            s_ref[2 * hp + s] = lax.dot_general(q2, k2 * keep[s], (((1,), (1,)), ((), ())), preferred_element_type=F32)
    for h in range(NA_HEADS):
        sc = s_ref[h] + bias_ref[0, h]
        e_ref[h] = jnp.exp2(sc - jnp.max(sc, axis=-1, keepdims=True)).astype(BF16)
    for hp, cols in enumerate(pair_cols):
        v2 = v_ref[0, pl.ds(k0, nk), cols]
        pair = []
        for s in range(2):
            r = jnp.dot(e_ref[2 * hp + s], v2 * keep[s] + keep[1 - s], preferred_element_type=F32)
            den = r[:, dh:dh + 1] if s == 0 else r[:, 0:1]
            pair.append(jnp.where(own[s], r / den, 0.0))
        o_ref[0, :, cols] = (pair[0] + pair[1]).astype(BF16)


def _na_attn(qkv, bias, rows):
    b, seq, three_w = qkv.shape
    width = three_w // 3
    nq, nk = NA_QROWS * GRID_W, NA_KROWS * GRID_W
    groups = rows // NA_QROWS

    def bias_idx(bi, g):
        return (jnp.where(g == 0, 0, jnp.where(g == groups - 1, 2, 1)), 0, 0, 0)

    return pl.pallas_call(
        functools.partial(_na_attn_kernel, rows=rows),
        grid=(b, groups),
        in_specs=[
            pl.BlockSpec((1, nq, width), lambda bi, g: (bi, g, 0)),
            pl.BlockSpec((1, seq, width), lambda bi, g: (bi, 0, 1)),
            pl.BlockSpec((1, seq, width), lambda bi, g: (bi, 0, 2)),
            pl.BlockSpec((1, NA_HEADS, nq, nk), bias_idx),
        ],
        out_specs=pl.BlockSpec((1, nq, width), lambda bi, g: (bi, g, 0)),
        out_shape=jax.ShapeDtypeStruct((b, seq, width), BF16),
        scratch_shapes=[pltpu.VMEM((NA_HEADS, nq, nk), F32), pltpu.VMEM((NA_HEADS, nq, nk), BF16)],
        compiler_params=_cparams(("arbitrary", "arbitrary")),
        name="na_attn",
    )(qkv, qkv, qkv, bias)


def _dft_tables():
    k = np.arange(RADIX)
    ang = 2.0 * np.pi * np.outer(k, k) / RADIX
    fr, fi = np.cos(ang), -np.sin(ang)
    ang_t = 2.0 * np.pi * np.outer(k, k) / (RADIX * RADIX)
    twr, twi = np.cos(ang_t), -np.sin(ang_t)
    f32 = lambda a: jnp.asarray(a, F32)
    return f32(fr), f32(fi), f32(twr), f32(twi)


def _embed(cr, ci):
    return jnp.concatenate([jnp.concatenate([cr, -ci], axis=1), jnp.concatenate([ci, cr], axis=1)], axis=0)


def _cmatmul(c_rows, x_lanes):
    m = c_rows.shape[0] // 2
    r = jnp.dot(c_rows, x_lanes, preferred_element_type=F32)
    return r[:m, :LANES] - r[m:, LANES:], r[:m, LANES:] + r[m:, :LANES]


def _load_pair(ref, rows):
    return jnp.concatenate([ref[0, rows, :], ref[1, rows, :]], axis=1).astype(BF16)


def _store_pair(ref, rows, re, im):
    ref[0, rows, :] = re
    ref[1, rows, :] = im


def _twiddled(fr, fi, twr, twi, scale_re=1.0, scale_im=1.0):
    return jnp.concatenate([(fr * twr - fi * twi) * scale_re, (fr * twi + fi * twr) * scale_im], axis=0).astype(BF16)


def _position_features(seq):
    t = np.linspace(0.0, 1.0, seq)[:, None]
    bands = (HY_EMB_DIM - 1) // 2
    w = (2.0 * math.pi / seq) * np.arange(seq)[:, None]
    f = np.linspace(1e-4, bands - 1, bands)[None, :]
    z = np.concatenate([t, np.cos(f * w), -np.sin(f * w)], axis=-1)
    zz = np.zeros((seq // 2, 2 * LANES), np.float32)
    zz[:, :HY_EMB_DIM] = z[:seq // 2]
    zz[:, LANES:LANES + HY_EMB_DIM] = z[seq // 2:]
    return zz


def _filt_mlp_kernel(z_ref, w1_ref, b1_ref, w2_ref, b2_ref, fr_ref, o_ref):
    hp = lax.Precision.HIGHEST
    fr = fr_ref[...]
    hid = LANES // 2
    h = jnp.sin(fr * (jnp.dot(z_ref[...], w1_ref[...], precision=hp, preferred_element_type=F32) + b1_ref[...]))
    h = jnp.sin(fr * (jnp.dot(h, w2_ref[...], precision=hp, preferred_element_type=F32) + b2_ref[...]))
    hi = h.astype(BF16).astype(F32)
    lo = h - hi
    hi_sw, lo_sw = pltpu.roll(hi, hid, 1), pltpu.roll(lo, hid, 1)
    low = lax.broadcasted_iota(jnp.int32, (1, LANES), 1) < hid
    o_ref[0, :, :LANES] = jnp.where(low, hi, lo_sw).astype(BF16)
    o_ref[0, :, LANES:] = jnp.where(low, hi, 0.0).astype(BF16)
    o_ref[1, :, :LANES] = jnp.where(low, hi_sw, lo).astype(BF16)
    o_ref[1, :, LANES:] = jnp.where(low, hi_sw, 0.0).astype(BF16)


def _filt_mlp_weights(fw1, fb1, fw2, fb2, freq):
    hid = fw2.shape[0]
    assert 2 * hid == LANES
    w1 = jnp.zeros((2 * LANES, LANES), F32)
    w1 = w1.at[:HY_EMB_DIM, :hid].set(fw1).at[LANES:LANES + HY_EMB_DIM, hid:].set(fw1)
    w2 = jnp.zeros((LANES, LANES), F32).at[:hid, :hid].set(fw2).at[hid:, hid:].set(fw2)
    twice = lambda v: jnp.concatenate([v, v]).reshape(1, LANES)
    return w1, twice(fb1), w2, twice(fb2), twice(freq)


def _filt_time_kernel(z_ref, w1_ref, b1_ref, w2_ref, b2_ref, fr_ref, perm_ref, w3_ref, b3_ref, delta_ref,
                      k_ref, ss_ref, hsel_ref, hdn_ref, *, seq):
    i = pl.program_id(0)
    rows = hsel_ref.shape[0]
    half = seq // rows
    qtr = half // 2

    def block(b, off, size):
        return hdn_ref[b // qtr, pl.ds(pl.multiple_of((b % qtr) * rows + off, RADIX), size), :]

    @pl.when(i < qtr)
    def _():
        _filt_mlp_kernel(z_ref, w1_ref, b1_ref, w2_ref, b2_ref, fr_ref,
                         hdn_ref.at[:, pl.ds(pl.multiple_of(i * rows, rows), rows), :])

    @pl.when(i < half)
    def _():
        hsel_ref[...] = block(i, 0, rows)

    @pl.when(i >= half)
    def _():
        ba = 2 * half - 1 - i
        bb = jnp.clip(2 * half - i, 0, half - 1)
        for s in range(rows // RADIX):
            lo_row = rows - RADIX * (s + 1)
            nxt = block(bb, 0, RADIX) if s == 0 else block(ba, lo_row + RADIX, RADIX)
            src = jnp.concatenate([block(ba, lo_row, RADIX), nxt], axis=0)
            hsel_ref[s * RADIX:(s + 1) * RADIX, :] = jnp.dot(perm_ref[...], src, preferred_element_type=F32).astype(BF16)

    w = w3_ref[0]
    w_hi = w.astype(BF16)
    w_lo = (w - w_hi.astype(F32)).astype(BF16)
    wcat = jnp.concatenate([w_hi, w_hi, w_lo, jnp.zeros_like(w_lo)], axis=0)
    hf = jnp.dot(hsel_ref[...], wcat, preferred_element_type=F32) + b3_ref[0]
    m = i * rows + lax.broadcasted_iota(jnp.int32, (rows, 1), 0)
    pos = jnp.where(m < seq, m, 2 * seq - m).astype(F32)
    t = pos / (seq - 1)
    k = jnp.where(m == seq, 0.0, hf * jnp.exp(-t * delta_ref[...]))
    for j in range(rows // RADIX):
        k_ref[j * PITCH:j * PITCH + RADIX, :] = k[j * RADIX:(j + 1) * RADIX]
        k_ref[j * PITCH + RADIX:(j + 1) * PITCH, :] = jnp.zeros((PITCH - RADIX, k.shape[1]), F32)
    part = jnp.sum(k * k, axis=0, keepdims=True)

    @pl.when(i == 0)
    def _():
        ss_ref[...] = part

    @pl.when(i != 0)
    def _():
        ss_ref[...] += part


def _filt_time(zz, mlp, w3d, b3d, delta2):
    seq = 2 * zz.shape[0]
    _, hid, cols = w3d.shape
    packed = 4 * hid
    assert packed == 2 * LANES
    n2l = 2 * seq
    rows = 1024
    half = seq // rows
    whole = lambda i: (0, 0)
    prows = rows // RADIX * PITCH
    perm = np.zeros((RADIX, 2 * RADIX), np.float32)
    perm[0, RADIX] = 1.0
    perm[np.arange(1, RADIX), RADIX - np.arange(1, RADIX)] = 1.0
    return pl.pallas_call(
        functools.partial(_filt_time_kernel, seq=seq),
        grid=(n2l // rows,),
        in_specs=[pl.BlockSpec((rows, 2 * LANES), lambda i: (jnp.minimum(i, half // 2 - 1), 0))]
                 + [pl.BlockSpec(a.shape, whole) for a in mlp]
                 + [pl.BlockSpec(perm.shape, lambda i: (0, 0)),
                  pl.BlockSpec((1, hid, cols), lambda i: (i // half, 0, 0)),
                  pl.BlockSpec((1, 1, cols), lambda i: (i // half, 0, 0)),
                  pl.BlockSpec((1, cols), lambda i: (0, 0))],
        out_specs=[pl.BlockSpec((prows, cols), lambda i: (i, 0)),
                   pl.BlockSpec((1, cols), lambda i: (0, 0))],
        out_shape=[jax.ShapeDtypeStruct((n2l // RADIX * PITCH, cols), F32), jax.ShapeDtypeStruct((1, cols), F32)],
        scratch_shapes=[pltpu.VMEM((rows, packed), BF16), pltpu.VMEM((2, seq // 2, packed), BF16)],
        compiler_params=_cparams(("arbitrary",)),
        name="filt_time",
    )(zz, *mlp, jnp.asarray(perm, BF16), w3d, b3d, delta2)


def _filt_spec_kernel(k_ref, ss_ref, fa_ref, fr_ref, fi_ref, twr_ref, twi_ref, o_ref, a_ref):
    r = RADIX
    scale = lax.rsqrt(ss_ref[...] + 1e-12)

    def a_body(i, c):
        n2 = 2 * i
        rhs = jnp.concatenate([k_ref[pl.ds(n2, r, stride=PITCH), :], k_ref[pl.ds(n2 + 1, r, stride=PITCH), :]],
                              axis=1).astype(BF16)
        out = jnp.dot(fa_ref[...], rhs, preferred_element_type=F32)
        _store_pair(a_ref, pl.ds(n2, r, stride=PITCH), out[:r, :LANES], out[r:, :LANES])
        _store_pair(a_ref, pl.ds(n2 + 1, r, stride=PITCH), out[:r, LANES:], out[r:, LANES:])
        return c

    lax.fori_loop(0, r // 2, a_body, 0, unroll=DFT_UNROLL_STRIDED // 2)

    def b_body(k1, c):
        r0 = pl.multiple_of(k1 * PITCH, 8)
        o0 = pl.multiple_of(k1 * r, r)
        blk = _load_pair(a_ref, pl.ds(r0, r))
        sr, si = _cmatmul(_twiddled(fr_ref[...], fi_ref[...], twr_ref[pl.ds(k1, 1), :], twi_ref[pl.ds(k1, 1), :]), blk)
        o_ref[0, pl.ds(o0, r), :] = (sr * scale).astype(BF16)
        o_ref[1, pl.ds(o0, r), :] = (si * scale).astype(BF16)
        return c

    lax.fori_loop(0, r, b_body, 0, unroll=DFT_UNROLL_STRIDED)


def _filt_spec(ktime, ss, fa_real, fr, fi, twr, twi):
    prows, cols = ktime.shape
    n2l = prows // PITCH * RADIX
    whole = lambda j: (0, 0)
    return pl.pallas_call(
        _filt_spec_kernel,
        grid=(cols // LANES,),
        in_specs=[pl.BlockSpec((prows, LANES), lambda j: (0, j)),
                  pl.BlockSpec((1, LANES), lambda j: (0, j)),
                  _resident(fa_real.shape, whole), _resident(fr.shape, whole), _resident(fi.shape, whole),
                  _resident(twr.shape, whole), _resident(twi.shape, whole)],
        out_specs=pl.BlockSpec((2, n2l, LANES), lambda j: (0, 0, j)),
        out_shape=jax.ShapeDtypeStruct((2, n2l, cols), BF16),
        scratch_shapes=[pltpu.VMEM((2, prows, LANES), F32)],
        compiler_params=_cparams(("arbitrary",)),
        name="filt_spec",
    )(ktime, ss, fa_real, fr, fi, twr, twi)


def _short_conv_rows(u_ref, bi, r0, nrows, seq, w_ref, b_ref):
    edge = 16
    cur = u_ref[bi, pl.ds(r0, nrows), :].astype(F32)
    before = u_ref[bi, pl.ds(pl.multiple_of(jnp.maximum(r0 - edge, 0), edge), edge), :].astype(F32)
    after = u_ref[bi, pl.ds(pl.multiple_of(jnp.minimum(r0 + nrows, seq - edge), edge), edge), :].astype(F32)
    prev_edge = jnp.where(r0 > 0, before[edge - 1:edge, :], 0.0)
    next_edge = jnp.where(r0 + nrows < seq, after[0:1, :], 0.0)
    w0, w1, w2 = w_ref[0:1, :], w_ref[1:2, :], w_ref[2:3, :]
    out = b_ref[...] + w0 * pltpu.roll(cur, 1, 0) + w1 * cur + w2 * pltpu.roll(cur, nrows - 1, 0)
    row = lax.broadcasted_iota(jnp.int32, (8, cur.shape[1]), 0)
    first = out[0:8] + jnp.where(row == 0, w0 * (prev_edge - cur[nrows - 1:nrows]), 0.0)
    last = out[nrows - 8:] + jnp.where(row == 7, w2 * (next_edge - cur[0:1]), 0.0)
    return jnp.concatenate([first, out[8:nrows - 8], last], axis=0)


def _hy_conv_kernel(z_ref, zw_ref, zb_ref, hyu_hbm, uw1_ref, ub1_ref, uw2_ref, ub2_ref, ks_hbm, hb1_ref, hb2_ref,
                    fa_ref, fc_ref, fr_ref, fi_ref, twr_ref, twi_ref, o_ref,
                    a_ref, v_ref, ks_ref, u_buf, ks_sem, u_sem, *, tiles):
    r = RADIX
    half = r // 2
    nb, seq, _ = z_ref.shape
    nchunk = seq // CONV_ROWS
    blocks = CONV_ROWS // r
    tile = pl.program_id(0)

    def ks_copy(order):
        col = pl.multiple_of((order * tiles + tile) * LANES, LANES)
        return pltpu.make_async_copy(ks_hbm.at[:, :, pl.ds(col, LANES)], ks_ref, ks_sem)

    def u_copy(part):
        col = pl.multiple_of((part * tiles + tile) * LANES, LANES)
        return pltpu.make_async_copy(hyu_hbm.at[:, :, pl.ds(col, LANES)], u_buf.at[part - 1], u_sem.at[part - 1])

    ks_copy(0).start()
    u_copy(1).start()
    u_copy(2).start()

    def load_body(c, carry):
        r0 = pl.multiple_of(c * CONV_ROWS, CONV_ROWS)
        for bi in range(nb):
            v = _short_conv_rows(z_ref, bi, r0, CONV_ROWS, seq, zw_ref, zb_ref)
            o_ref[bi, pl.ds(r0, CONV_ROWS), :] = v.astype(BF16)
            for j in range(blocks):
                p0 = pl.multiple_of((c * blocks + j) * PITCH, 8)
                v_ref[bi, pl.ds(p0, r), :] = v[j * r:(j + 1) * r]
        return carry

    lax.fori_loop(0, nchunk, load_body, 0)

    def pair_store(ref, rows0, rows1, out):
        _store_pair(ref, rows0, out[:r, :LANES], out[r:, :LANES])
        _store_pair(ref, rows1, out[:r, LANES:], out[r:, LANES:])

    def a_body(i, carry):
        n2 = 2 * i

        def column(n):
            return jnp.concatenate([v_ref[0, pl.ds(n, half, stride=PITCH), :],
                                    v_ref[1, pl.ds(n, half, stride=PITCH), :]], axis=0)

        rhs = jnp.concatenate([column(n2), column(n2 + 1)], axis=1).astype(BF16)
        out = jnp.dot(fa_ref[...], rhs, preferred_element_type=F32)
        pair_store(a_ref, pl.ds(n2, r, stride=PITCH), pl.ds(n2 + 1, r, stride=PITCH), out)
        return carry

    def stage_a():
        lax.fori_loop(0, r // 2, a_body, 0, unroll=DFT_UNROLL_STRIDED // 2)

    def b_forward(k1):
        r0 = pl.multiple_of(k1 * PITCH, 8)
        s0 = pl.multiple_of(k1 * r, r)
        blk = _load_pair(a_ref, pl.ds(r0, r))
        sr, si = _cmatmul(_twiddled(fr_ref[...], fi_ref[...], twr_ref[pl.ds(k1, 1), :], twi_ref[pl.ds(k1, 1), :]), blk)
        kr = ks_ref[0, pl.ds(s0, r), :].astype(F32)
        ki = ks_ref[1, pl.ds(s0, r), :].astype(F32)
        return jnp.concatenate([sr * kr - si * ki, sr * ki + si * kr], axis=0).astype(BF16)

    def b_forward_pair(i):
        return jnp.concatenate([b_forward(2 * i), b_forward(2 * i + 1)], axis=1)

    def b_inverse_pair(i, p):
        out = jnp.dot(fc_ref[...], p, preferred_element_type=F32)
        pair_store(a_ref, pl.ds(pl.multiple_of(2 * i * PITCH, 8), r), pl.ds(pl.multiple_of((2 * i + 1) * PITCH, 8), r), out)

    def b_body(i, p):
        p_next = b_forward_pair(i + 1)
        b_inverse_pair(i, p)
        return p_next

    def stage_b():
        b_inverse_pair(r // 2 - 1, lax.fori_loop(0, r // 2 - 1, b_body, b_forward_pair(0), unroll=DFT_UNROLL // 2))

    inv_n = 1.0 / (r * r)

    def c_body(n2, carry):
        g = _twiddled(fr_ref[0:half, :], fi_ref[0:half, :], twr_ref[pl.ds(n2, 1), :], twi_ref[pl.ds(n2, 1), :],
                      inv_n, -inv_n)
        yr, yi = _cmatmul(g, _load_pair(a_ref, pl.ds(n2, r, stride=PITCH)))
        _store_pair(v_ref, pl.ds(n2, half, stride=PITCH), yr, yi)
        return carry

    def stage_c():
        lax.fori_loop(0, r, c_body, 0, unroll=DFT_UNROLL_STRIDED)

    def epilogue(order):
        u_ref = u_buf.at[order]
        uw_ref, ub_ref, hb_ref = ((uw1_ref, ub1_ref, hb1_ref), (uw2_ref, ub2_ref, hb2_ref))[order]

        def out_body(c, carry):
            r0 = pl.multiple_of(c * CONV_ROWS, CONV_ROWS)
            for bi in range(nb):
                gate = _short_conv_rows(u_ref, bi, r0, CONV_ROWS, seq, uw_ref, ub_ref)
                rows = [pl.ds(pl.multiple_of((c * blocks + j) * PITCH, 8), r) for j in range(blocks)]
                conv = jnp.concatenate([v_ref[bi, rw, :] for rw in rows], axis=0)
                zp = o_ref[bi, pl.ds(r0, CONV_ROWS), :].astype(F32)
                res = gate * (conv + hb_ref[...] * zp)
                o_ref[bi, pl.ds(r0, CONV_ROWS), :] = res.astype(BF16)
                if order == 0:
                    for j, rw in enumerate(rows):
                        v_ref[bi, rw, :] = res[j * r:(j + 1) * r]
            return carry

        lax.fori_loop(0, nchunk, out_body, 0)

    stage_a()
    ks_copy(0).wait()
    stage_b()
    ks_copy(1).start()
    stage_c()
    u_copy(1).wait()
    epilogue(0)
    stage_a()
    ks_copy(1).wait()
    stage_b()
    stage_c()
    u_copy(2).wait()
    epilogue(1)


def _hy_conv(hyu, short_w, short_b, kspec, hy_bias, tables):
    nb, seq, _ = hyu.shape
    width = hy_bias.shape[1]
    tiles = width // LANES
    fa, fc, fr, fi, twr, twi = tables
    whole = lambda j: (0, 0)
    part = lambda p, rows: pl.BlockSpec((rows, LANES), lambda j: (0, p * tiles + j))
    hbm = pl.BlockSpec(memory_space=pl.ANY)
    return pl.pallas_call(
        functools.partial(_hy_conv_kernel, tiles=tiles),
        grid=(tiles,),
        in_specs=[
            pl.BlockSpec((nb, seq, LANES), lambda j: (0, 0, j)), part(0, 3), part(0, 1),
            hbm, part(1, 3), part(1, 1), part(2, 3), part(2, 1),
            hbm, pl.BlockSpec((1, LANES), lambda j: (0, j)), pl.BlockSpec((1, LANES), lambda j: (0, j)),
            _resident(fa.shape, whole), _resident(fc.shape, whole), _resident(fr.shape, whole),
            _resident(fi.shape, whole), _resident(twr.shape, whole), _resident(twi.shape, whole),
        ],
        out_specs=_resident((nb, seq, LANES), lambda j: (0, 0, j)),
        out_shape=jax.ShapeDtypeStruct((nb, seq, width), BF16),
        scratch_shapes=[pltpu.VMEM((2, 2 * seq // RADIX * PITCH, LANES), F32),
                        pltpu.VMEM((nb, seq // RADIX * PITCH, LANES), F32),
                        pltpu.VMEM((2, 2 * seq, LANES), BF16),
                        pltpu.VMEM((HY_ORDER, nb, seq, LANES), BF16),
                        pltpu.SemaphoreType.DMA(()), pltpu.SemaphoreType.DMA((HY_ORDER,))],
        compiler_params=_cparams(("arbitrary",)),
        name="hy_conv",
    )(hyu, short_w, short_b, hyu, short_w, short_b, short_w, short_b, kspec, hy_bias[0:1], hy_bias[1:2],
      fa, fc, fr, fi, twr, twi)


def _hyena(hyu, short_w, short_b, fw1, fb1, fw2, fb2, fw3, fb3, freq, hy_bias):
    nb, seq, _ = hyu.shape
    assert nb == 2 and 2 * seq == RADIX * RADIX
    width = hy_bias.shape[1]
    hid = fw2.shape[0]
    fr, fi, twr, twi = _dft_tables()
    half = RADIX // 2
    fa_real = jnp.concatenate([fr, fi], axis=0).astype(BF16)
    fa_half = _embed(fr[:, :half], fi[:, :half]).astype(BF16)
    fc = _embed(fr, -fi).astype(BF16)

    zz = jnp.asarray(_position_features(seq))
    mlp = _filt_mlp_weights(fw1, fb1, fw2, fb2, freq)
    w3d = fw3.reshape(hid, 2, HY_ORDER * width).transpose(1, 0, 2)
    b3d = fb3.reshape(2, 1, HY_ORDER * width)
    max_decay = math.log(HY_DECAY_TARGET) / HY_FAST_DECAY_PCT
    min_decay = math.log(HY_DECAY_TARGET) / HY_SLOW_DECAY_PCT
    delta = np.abs(np.linspace(min_decay, max_decay, width)).astype(np.float32)
    delta2 = jnp.asarray(np.tile(delta, HY_ORDER)[None, :])
    ktime, ss = _filt_time(zz, mlp, w3d, b3d, delta2)
    kspec = _filt_spec(ktime, ss, fa_real, fr, fi, twr, twi)

    tables = (fa_half, fc, fr, fi, twr, twi)
    sb = short_b.reshape(1, -1)
    return _hy_conv(hyu, short_w, sb, kspec, hy_bias, tables)


def kernel(x, ln1_g, ln1_b, ffn1_w1, ffn1_w3, ffn1_w2, w_in, b_gate, na_rpb, hy_short_w, hy_short_b,
           hy_filt_w1, hy_filt_b1, hy_filt_w2, hy_filt_b2, hy_filt_w3, hy_filt_b3, hy_filt_freq, hy_bias,
           w_pa, w_pb, w_out, ln2_g, ln2_b, ffn2_w1, ffn2_w3, ffn2_w2, ln3_g, ln3_b):
    assert ln1_g.shape[0] == DEPTH
    b, seq, d = x.shape
    rows = seq // GRID_W
    na_width = NA_HEADS * NA_HEAD_DIM
    n_qkv = 3 * na_width
    n_hy = hy_short_w.shape[2]
    vec = lambda a: a[0].reshape(1, -1)

    x1, qkv, hyu, gates = _dense_in(
        x.reshape(b * seq, d), ffn1_w1[0], ffn1_w3[0], ffn1_w2[0], vec(ln1_g), vec(ln1_b),
        w_in[0], vec(b_gate), n_qkv, n_hy)

    bias = _na_bias(na_rpb[0], rows)
    ya = _na_attn(qkv.reshape(b, seq, n_qkv), bias, rows)

    yb = _hyena(hyu.reshape(b, seq, n_hy), hy_short_w[0], hy_short_b[0], hy_filt_w1[0], hy_filt_b1[0],
                hy_filt_w2[0], hy_filt_b2[0], hy_filt_w3[0], hy_filt_b3[0], hy_filt_freq[0], hy_bias[0])

    out = _dense_out(
        x1, ya.reshape(b * seq, na_width), yb.reshape(b * seq, -1), gates,
        w_pa[0], w_pb[0], w_out[0], vec(ln2_g), vec(ln2_b),
        ffn2_w1[0], ffn2_w3[0], ffn2_w2[0], vec(ln3_g), vec(ln3_b))
    return out.reshape(b, seq, d)
```
